```python
import math
import jax, jax.numpy as jnp
from jax import lax
import numpy as np

D_MODEL = 2048
BATCH = 8
SEQ = 4096
DEPTH = 4

N_MIXERS = 4
N_HEADS = 16
HEAD_DIM = 128
MIX_WIDTH = N_HEADS * HEAD_DIM
N_MEM = 256
MEM_HEADS = 4
MEM_WIDTH = MEM_HEADS * HEAD_DIM
OUT_IN = MIX_WIDTH + MEM_WIDTH
D_FF = 5632
Q_BLOCK = 128
RMS_EPS = 1e-6
REL_BUCKETS = 32
REL_MAX_DIST = 2048
T5_INIT_SCALE = 0.2
MAX_POS_OFFSET = 1024
FOX_GATE_BIAS = 4.0
FOX_COLS = 3 * MIX_WIDTH + N_HEADS
Q_LORA = 512
KV_LORA = 512
NOPE_DIM = 128
ROPE_DIM = 64
ROPE_THETA = 10000.0
MLA_COLS = Q_LORA + KV_LORA + ROPE_DIM
DIL_GROUPS = ((128, 1), (512, 4), (2048, 16))
DIL_COLS = len(DIL_GROUPS) * 3 * MIX_WIDTH
DSA_KV_HEADS = 4
IDX_HEADS = 16
IDX_DIM = 64
TOPK_MAX = 256
DSA_COLS = MIX_WIDTH + 2 * DSA_KV_HEADS * HEAD_DIM + IDX_HEADS * IDX_DIM + IDX_DIM + IDX_HEADS

kernel_name = 'hybrid_fox_mla_dilated_dsa_trunk'

F32 = jnp.float32


def rms_norm(x, g):
    xf = x.astype(F32)
    y = xf * lax.rsqrt(jnp.mean(xf * xf, axis=-1, keepdims=True) + RMS_EPS)
    return (y * g.astype(F32)).astype(x.dtype)


def swiglu(x, w_gate, w_up, w_down):
    return (jax.nn.silu(x @ w_gate) * (x @ w_up)) @ w_down


def t5_bucket(dist):
    n = jnp.maximum(dist, 0)
    exact = REL_BUCKETS // 2
    nf = jnp.maximum(n, 1).astype(F32)
    large = exact + (jnp.log(nf / exact) / math.log(REL_MAX_DIST / exact) * (REL_BUCKETS - exact)).astype(jnp.int32)
    large = jnp.minimum(large, REL_BUCKETS - 1)
    return jnp.where(n < exact, n, large)


def rope(x, cos, sin):
    half = ROPE_DIM // 2
    x1, x2 = x[..., :half].astype(F32), x[..., half:].astype(F32)
    return jnp.concatenate([x1 * cos - x2 * sin, x1 * sin + x2 * cos], axis=-1).astype(x.dtype)


def causal_block_attention(q, k, v, scale, log_decay=None):
    B, S, H, _ = q.shape
    key_pos = jnp.arange(S)
    decay_k = None if log_decay is None else jnp.moveaxis(log_decay, 2, 1)

    def block(i):
        qb = lax.dynamic_slice_in_dim(q, i * Q_BLOCK, Q_BLOCK, axis=1)
        qpos = i * Q_BLOCK + jnp.arange(Q_BLOCK)
        s = jnp.einsum('bqhd,bkhd->bhqk', qb, k, preferred_element_type=F32) * scale
        if log_decay is not None:
            db = lax.dynamic_slice_in_dim(decay_k, i * Q_BLOCK, Q_BLOCK, axis=2)
            s = s + (db[..., None] - decay_k[:, :, None, :])
        s = jnp.where(key_pos[None, :] <= qpos[:, None], s, -jnp.inf)
        pr = jax.nn.softmax(s, axis=-1).astype(v.dtype)
        return jnp.einsum('bhqk,bkhd->bqhd', pr, v)

    o = lax.map(block, jnp.arange(S // Q_BLOCK))
    return jnp.moveaxis(o, 0, 1).reshape(B, S, H * v.shape[-1])


def fox_mixer(p, b_f, qk_g):
    B, S, _ = p.shape
    q, k, v, fg = jnp.split(p, [MIX_WIDTH, 2 * MIX_WIDTH, 3 * MIX_WIDTH], axis=-1)
    shp = (B, S, N_HEADS, HEAD_DIM)
    q = rms_norm(q.reshape(shp), qk_g[0])
    k = rms_norm(k.reshape(shp), qk_g[1])
    log_f = jax.nn.log_sigmoid(fg.astype(F32) + b_f.astype(F32))
    cum = lax.cumsum(log_f, axis=1)
    return causal_block_attention(q, k, v.reshape(shp), HEAD_DIM ** -0.5, cum)


def mla_mixer(p, positions, q_norm_g, w_uq, kv_norm_g, w_ukv, nope_g, rope_g):
    B, S, _ = p.shape
    cq, ckv, kr = jnp.split(p, [Q_LORA, Q_LORA + KV_LORA], axis=-1)
    q = (rms_norm(cq, q_norm_g) @ w_uq).reshape(B, S, N_HEADS, NOPE_DIM + ROPE_DIM)
    kv = (rms_norm(ckv, kv_norm_g) @ w_ukv).reshape(B, S, N_HEADS, NOPE_DIM + HEAD_DIM)
    half = ROPE_DIM // 2
    inv = ROPE_THETA ** (-jnp.arange(half, dtype=F32) / half)
    ang = positions.astype(F32)[..., None] * inv
    cos, sin = jnp.cos(ang), jnp.sin(ang)
    q_nope = rms_norm(q[..., :NOPE_DIM], nope_g[0])
    q_rope = rope(rms_norm(q[..., NOPE_DIM:], rope_g[0]), cos[:, :, None], sin[:, :, None])
    k_nope = rms_norm(kv[..., :NOPE_DIM], nope_g[1])
    v = kv[..., NOPE_DIM:]
    k_rope = rope(rms_norm(kr, rope_g[1]), cos, sin)
    qf = jnp.concatenate([q_nope, q_rope], axis=-1)
    kf = jnp.concatenate([k_nope, jnp.broadcast_to(k_rope[:, :, None, :], (B, S, N_HEADS, ROPE_DIM))], axis=-1)
    return causal_block_attention(qf, kf, v, (NOPE_DIM + ROPE_DIM) ** -0.5)


def dilated_group(q, k, v, dil, sub_window, t5_table):
    B, S, H, D = q.shape
    Ls = S // dil
    nb = -(-Ls // Q_BLOCK)
    Lp = nb * Q_BLOCK

    def by_stride(a):
        a = a.reshape(B, Ls, dil, H, D).transpose(0, 2, 1, 3, 4).reshape(B * dil, Ls, H, D)
        return jnp.pad(a, ((0, 0), (0, Lp - Ls), (0, 0), (0, 0)))

    front = ((0, 0), (Q_BLOCK, 0), (0, 0), (0, 0))
    qs = by_stride(q)
    ks = jnp.pad(by_stride(k), front)
    vs = jnp.pad(by_stride(v), front)
    kk = jnp.arange(2 * Q_BLOCK)
    rel = jnp.arange(Q_BLOCK)[:, None] + Q_BLOCK - kk[None, :]
    band = (rel >= 0) & (rel <= sub_window)
    bias = jnp.moveaxis(t5_table[t5_bucket(rel * dil)], -1, 0).astype(F32)
    scale = D ** -0.5

    def block(i):
        qb = lax.dynamic_slice_in_dim(qs, i * Q_BLOCK, Q_BLOCK, axis=1)
        kb = lax.dynamic_slice_in_dim(ks, i * Q_BLOCK, 2 * Q_BLOCK, axis=1)
        vb = lax.dynamic_slice_in_dim(vs, i * Q_BLOCK, 2 * Q_BLOCK, axis=1)
        s = jnp.einsum('nqhd,nkhd->nhqk', qb, kb, preferred_element_type=F32) * scale + bias
        valid = band & (kk[None, :] >= Q_BLOCK - i * Q_BLOCK)
        s = jnp.where(valid, s, -jnp.inf)
        m = jnp.max(s, axis=-1, keepdims=True)
        e = jnp.exp(s - m)
        den = jnp.sum(e, axis=-1, keepdims=True)
        o = jnp.einsum('nhqk,nkhd->nqhd', (e / den).astype(vb.dtype), vb)
        lse = (m + jnp.log(den))[..., 0]
        return o, jnp.moveaxis(lse, 1, 2)

    o, lse = lax.map(block, jnp.arange(nb))

    def back(a):
        a = jnp.moveaxis(a, 0, 1)
        a = a.reshape((B * dil, Lp) + a.shape[3:])[:, :Ls]
        a = jnp.swapaxes(a.reshape((B, dil, Ls) + a.shape[2:]), 1, 2)
        return a.reshape((B, S) + a.shape[3:])

    return back(o), back(lse)


def dilated_mixer(p, qk_g, t5_table):
    B, S, _ = p.shape
    p = p.reshape(B, S, len(DIL_GROUPS), 3, N_HEADS, HEAD_DIM)
    outs, lses = [], []
    for g, (win, dil) in enumerate(DIL_GROUPS):
        q = rms_norm(p[:, :, g, 0], qk_g[g, 0])
        k = rms_norm(p[:, :, g, 1], qk_g[g, 1])
        o, lse = dilated_group(q, k, p[:, :, g, 2], dil, win // dil, t5_table)
        outs.append(o)
        lses.append(lse)
    alpha = jax.nn.softmax(jnp.stack(lses), axis=0)
    out = jnp.einsum('gbsh,gbshd->bshd', alpha, jnp.stack(outs).astype(F32))
    return out.reshape(B, S, MIX_WIDTH).astype(p.dtype)


def dsa_mixer(p, qk_g, t5_table):
    B, S, _ = p.shape
    kvw = DSA_KV_HEADS * HEAD_DIM
    offs = np.cumsum([MIX_WIDTH, kvw, kvw, IDX_HEADS * IDX_DIM, IDX_DIM]).tolist()
    q, k, v, qi, ki, wi = jnp.split(p, offs, axis=-1)
    group = N_HEADS // DSA_KV_HEADS
    q = rms_norm(q.reshape(B, S, DSA_KV_HEADS, group, HEAD_DIM), qk_g[0])
    k = rms_norm(k.reshape(B, S, DSA_KV_HEADS, HEAD_DIM), qk_g[1])
    v = v.reshape(B, S, DSA_KV_HEADS, HEAD_DIM)
    qi = qi.reshape(B, S, IDX_HEADS, IDX_DIM)
    wi = wi.astype(F32) * IDX_HEADS ** -0.5
    n_sel = min(TOPK_MAX, S // 4)
    key_pos = jnp.arange(S)
    gather = jax.vmap(lambda a, idx: a[idx])

    def block(i):
        sl = lambda a: lax.dynamic_slice_in_dim(a, i * Q_BLOCK, Q_BLOCK, axis=1)
        qb, qib, wib = sl(q), sl(qi), sl(wi)
        qpos = i * Q_BLOCK + jnp.arange(Q_BLOCK)
        dots = jnp.einsum('bqhd,bkd->bqhk', qib, ki, preferred_element_type=F32) * IDX_DIM ** -0.5
        score = jnp.einsum('bqh,bqhk->bqk', wib, jax.nn.relu(dots))
        score = jnp.where(key_pos[None, None, :] <= qpos[None, :, None], score, -jnp.inf)
        _, idx = lax.top_k(score, n_sel)
        kg, vg = gather(k, idx), gather(v, idx)
        dist = qpos[None, :, None] - idx
        bias = t5_table[t5_bucket(dist)].reshape(B, Q_BLOCK, n_sel, DSA_KV_HEADS, group)
        s = jnp.einsum('bqgrd,bqkgd->bqgrk', qb, kg, preferred_element_type=F32) * HEAD_DIM ** -0.5
        s = s + jnp.moveaxis(bias, 2, 4).astype(F32)
        s = jnp.where((dist >= 0)[:, :, None, None, :], s, -jnp.inf)
        pr = jax.nn.softmax(s, axis=-1).astype(v.dtype)
        return jnp.einsum('bqgrk,bqkgd->bqgrd', pr, vg)

    o = lax.map(block, jnp.arange(S // Q_BLOCK))
    return jnp.moveaxis(o, 0, 1).reshape(B, S, MIX_WIDTH)


def memory_attention(qm, mem_kv, qk_g):
    B, S, _ = qm.shape
    q = rms_norm(qm.reshape(B, S, MEM_HEADS, HEAD_DIM), qk_g[0])
    k, v = jnp.split(mem_kv, 2, axis=-1)
    k = rms_norm(k.reshape(B, N_MEM, MEM_HEADS, HEAD_DIM), qk_g[1])
    v = v.reshape(B, N_MEM, MEM_HEADS, HEAD_DIM)
    s = jnp.einsum('bqhd,bkhd->bhqk', q, k, preferred_element_type=F32) * HEAD_DIM ** -0.5
    pr = jax.nn.softmax(s, axis=-1).astype(v.dtype)
    return jnp.einsum('bhqk,bkhd->bqhd', pr, v).reshape(B, S, MEM_WIDTH)


def setup_inputs(seed: int = 0) -> dict:
    key = jax.random.key(seed)
    keys = iter(jax.random.split(key, 48))

    def normal(shape, scale):
        return jax.random.normal(next(keys), shape, F32) * scale

    def gain(shape):
        return 1.0 + normal(shape, 0.05)

    n_a, n_b, n_c, n_d = [len(range(m, DEPTH, N_MIXERS)) for m in range(N_MIXERS)]
    offsets = jax.random.randint(next(keys), (BATCH, 1), 0, MAX_POS_OFFSET, dtype=jnp.int32)
    positions = offsets + jnp.arange(SEQ, dtype=jnp.int32)[None, :]
    return {
        'x': normal((BATCH, SEQ, D_MODEL), 1.0),
        'mem': normal((BATCH, N_MEM, D_MODEL), 1.0),
        'positions': positions,
        't5_table': normal((REL_BUCKETS, N_HEADS), T5_INIT_SCALE),
        'ffn_norm': gain((DEPTH, 2, D_MODEL)),
        'ffn_w_gate': normal((DEPTH, 2, D_MODEL, D_FF), D_MODEL ** -0.5),
        'ffn_w_up': normal((DEPTH, 2, D_MODEL, D_FF), D_MODEL ** -0.5),
        'ffn_w_down': normal((DEPTH, 2, D_FF, D_MODEL), D_FF ** -0.5),
        'attn_norm': gain((DEPTH, D_MODEL)),
        'mem_norm': gain((DEPTH, D_MODEL)),
        'mem_w_kv': normal((DEPTH, D_MODEL, 2 * MEM_WIDTH), D_MODEL ** -0.5),
        'mem_qk_g': gain((DEPTH, 2, HEAD_DIM)),
        'w_out': normal((DEPTH, OUT_IN, D_MODEL), OUT_IN ** -0.5),
        'a_w_in': normal((n_a, D_MODEL, FOX_COLS + MEM_WIDTH), D_MODEL ** -0.5),
        'a_b_f': FOX_GATE_BIAS + normal((n_a, N_HEADS), 0.5),
        'a_qk_g': gain((n_a, 2, HEAD_DIM)),
        'b_w_in': normal((n_b, D_MODEL, MLA_COLS + MEM_WIDTH), D_MODEL ** -0.5),
        'b_q_norm': gain((n_b, Q_LORA)),
        'b_w_uq': normal((n_b, Q_LORA, N_HEADS * (NOPE_DIM + ROPE_DIM)), Q_LORA ** -0.5),
        'b_kv_norm': gain((n_b, KV_LORA)),
        'b_w_ukv': normal((n_b, KV_LORA, N_HEADS * (NOPE_DIM + HEAD_DIM)), KV_LORA ** -0.5),
        'b_nope_g': gain((n_b, 2, NOPE_DIM)),
        'b_rope_g': gain((n_b, 2, ROPE_DIM)),
        'c_w_in': normal((n_c, D_MODEL, DIL_COLS + MEM_WIDTH), D_MODEL ** -0.5),
        'c_qk_g': gain((n_c, len(DIL_GROUPS), 2, HEAD_DIM)),
        'd_w_in': normal((n_d, D_MODEL, DSA_COLS + MEM_WIDTH), D_MODEL ** -0.5),
        'd_qk_g': gain((n_d, 2, HEAD_DIM)),
    }


def reference(x, mem, positions, t5_table, ffn_norm, ffn_w_gate, ffn_w_up, ffn_w_down, attn_norm,
              mem_norm, mem_w_kv, mem_qk_g, w_out, a_w_in, a_b_f, a_qk_g, b_w_in, b_q_norm, b_w_uq,
              b_kv_norm, b_w_ukv, b_nope_g, b_rope_g, c_w_in, c_qk_g, d_w_in, d_qk_g):
    for i in range(DEPTH):
        m, j = i % N_MIXERS, i // N_MIXERS
        x = x + 0.5 * swiglu(rms_norm(x, ffn_norm[i, 0]), ffn_w_gate[i, 0], ffn_w_up[i, 0], ffn_w_down[i, 0])
        h = rms_norm(x, attn_norm[i])
        if m == 0:
            p = h @ a_w_in[j]
            mix = fox_mixer(p[..., :FOX_COLS], a_b_f[j], a_qk_g[j])
        elif m == 1:
            p = h @ b_w_in[j]
            mix = mla_mixer(p[..., :MLA_COLS], positions, b_q_norm[j], b_w_uq[j], b_kv_norm[j],
                            b_w_ukv[j], b_nope_g[j], b_rope_g[j])
        elif m == 2:
            p = h @ c_w_in[j]
            mix = dilated_mixer(p[..., :DIL_COLS], c_qk_g[j], t5_table)
        else:
            p = h @ d_w_in[j]
            mix = dsa_mixer(p[..., :DSA_COLS], d_qk_g[j], t5_table)
        mem_kv = rms_norm(mem, mem_norm[i]) @ mem_w_kv[i]
        mo = memory_attention(p[..., -MEM_WIDTH:], mem_kv, mem_qk_g[i])
        x = x + jnp.concatenate([mix.astype(x.dtype), mo.astype(x.dtype)], axis=-1) @ w_out[i]
        x = x + 0.5 * swiglu(rms_norm(x, ffn_norm[i, 1]), ffn_w_gate[i, 1], ffn_w_up[i, 1], ffn_w_down[i, 1])
    return x
```

```python
import functools
import math

import jax
import jax.numpy as jnp
from jax import lax
from jax.experimental import pallas as pl
from jax.experimental.pallas import tpu as pltpu

F32 = jnp.float32
BF16 = jnp.bfloat16

LANE = 128
D_MODEL = 2048
N_HEADS = 16
HEAD_DIM = 128
MIX_WIDTH = N_HEADS * HEAD_DIM
MEM_HEADS = 4
MEM_WIDTH = MEM_HEADS * HEAD_DIM
D_FF = 5632
RMS_EPS = 1e-6
REL_BUCKETS = 32
REL_MAX_DIST = 2048
Q_LORA = 512
KV_LORA = 512
NOPE_DIM = 128
ROPE_DIM = 64
ROPE_THETA = 10000.0
DIL_GROUPS = ((128, 1), (512, 4), (2048, 16))
DIL_BLOCK = 128
DSA_KV_HEADS = 4
IDX_HEADS = 16
IDX_DIM = 64
TOPK_MAX = 256
VMEM_LIMIT = 56 * 1024 * 1024
NEG_BIG = -1e30


def _params(*sem):
    return pltpu.CompilerParams(dimension_semantics=sem, vmem_limit_bytes=VMEM_LIMIT)


def _tile(n, pref):
    t = min(n, pref)
    while n % t:
        t //= 2
    return t


def _ffn_kernel(x_ref, g_ref, wg_ref, wu_ref, wd_ref, o_ref, h_ref, acc_ref):
    j = pl.program_id(1)

    @pl.when(j == 0)
    def _():
        x = x_ref[...]
        ms = jnp.mean(x * x, axis=-1, keepdims=True)
        h_ref[...] = (x * lax.rsqrt(ms + RMS_EPS) * g_ref[...]).astype(BF16)
        acc_ref[...] = jnp.zeros_like(acc_ref)

    h = h_ref[...]
    g = jnp.dot(h, wg_ref[...], preferred_element_type=F32)
    u = jnp.dot(h, wu_ref[...], preferred_element_type=F32)
    a = (g * jax.nn.sigmoid(g) * u).astype(BF16)
    acc_ref[...] += jnp.dot(a, wd_ref[...], preferred_element_type=F32)

    @pl.when(j == pl.num_programs(1) - 1)
    def _():
        o_ref[...] = x_ref[...] + 0.5 * acc_ref[...]


def _ffn(x2, gain, wg, wu, wd):
    T, D = x2.shape
    F = wg.shape[1]
    tm, tf = _tile(T, 512), _tile(F, 512)
    return pl.pallas_call(
        _ffn_kernel,
        grid=(T // tm, F // tf),
        in_specs=[
            pl.BlockSpec((tm, D), lambda i, j: (i, 0)),
            pl.BlockSpec((1, D), lambda i, j: (0, 0)),
            pl.BlockSpec((D, tf), lambda i, j: (0, j)),
            pl.BlockSpec((D, tf), lambda i, j: (0, j)),
            pl.BlockSpec((tf, D), lambda i, j: (j, 0)),
        ],
        out_specs=pl.BlockSpec((tm, D), lambda i, j: (i, 0)),
        out_shape=jax.ShapeDtypeStruct((T, D), F32),
        scratch_shapes=[pltpu.VMEM((tm, D), BF16), pltpu.VMEM((tm, D), F32)],
        compiler_params=_params("parallel", "arbitrary"),
    )(x2, gain.reshape(1, D), wg, wu, wd)


def _proj_kernel(x_ref, g_ref, w_ref, eg_ref, ef_ref, o_ref, h_ref, *, epilogue):
    @pl.when(pl.program_id(3) == 0)
    def _():
        x = x_ref[...].astype(F32)
        ms = jnp.mean(x * x, axis=-1, keepdims=True)
        h_ref[...] = (x * lax.rsqrt(ms + RMS_EPS) * g_ref[...]).astype(BF16)

    acc = jnp.dot(h_ref[...], w_ref[...], preferred_element_type=F32)
    if epilogue:
        for c in range(acc.shape[1] // LANE):
            sl = slice(c * LANE, (c + 1) * LANE)
            a = acc[:, sl]
            ms = jnp.mean(a * a, axis=-1, keepdims=True)
            scale = jnp.where(ef_ref[:, sl] > 0.0, lax.rsqrt(ms + RMS_EPS), 1.0)
            o_ref[:, sl] = (a * scale * eg_ref[:, sl]).astype(o_ref.dtype)
    else:
        o_ref[...] = acc.astype(o_ref.dtype)


def _proj(x, gain, w, eg=None, ef=None, *, dil=1, kblock=0, out_dtype=BF16, tm=1024, tn=512):
    B, S, C = x.shape
    K, N = w.shape
    Ls = S // dil
    tm, tn = _tile(Ls, tm), _tile(N, tn)
    epilogue = eg is not None
    if not epilogue:
        eg = jnp.ones((1, N), F32)
        ef = jnp.zeros((1, N), F32)
    xv = x.reshape(B, Ls, dil * C)
    kpr = C // K
    out = pl.pallas_call(
        functools.partial(_proj_kernel, epilogue=epilogue),
        grid=(B, dil, Ls // tm, N // tn),
        in_specs=[
            pl.BlockSpec((None, tm, K), lambda b, r, i, j: (b, i, r * kpr + kblock)),
            pl.BlockSpec((1, K), lambda b, r, i, j: (0, 0)),
            pl.BlockSpec((K, tn), lambda b, r, i, j: (0, j)),
            pl.BlockSpec((1, tn), lambda b, r, i, j: (0, j)),
            pl.BlockSpec((1, tn), lambda b, r, i, j: (0, j)),
        ],
        out_specs=pl.BlockSpec((None, None, tm, tn), lambda b, r, i, j: (b, r, i, j)),
        out_shape=jax.ShapeDtypeStruct((B, dil, Ls, N), out_dtype),
        scratch_shapes=[pltpu.VMEM((tm, K), BF16)],
        compiler_params=_params("parallel", "parallel", "parallel", "arbitrary"),
    )(xv, gain.reshape(1, K).astype(F32), w, eg.reshape(1, N), ef.reshape(1, N))
    return out.reshape(B, S, N) if dil == 1 else out


def _out_kernel(x_ref, a_ref, b_ref, wa_ref, wb_ref, o_ref):
    acc = jnp.dot(a_ref[...], wa_ref[...], preferred_element_type=F32)
    acc += jnp.dot(b_ref[...], wb_ref[...], preferred_element_type=F32)
    o_ref[...] = x_ref[...] + acc


def _out_proj(x2, mix2, mo2, wa, wb):
    T, D = x2.shape
    tm, tn = _tile(T, 1024), _tile(D, 512)
    ka, kb = mix2.shape[1], mo2.shape[1]
    return pl.pallas_call(
        _out_kernel,
        grid=(T // tm, D // tn),
        in_specs=[
            pl.BlockSpec((tm, tn), lambda i, j: (i, j)),
            pl.BlockSpec((tm, ka), lambda i, j: (i, 0)),
            pl.BlockSpec((tm, kb), lambda i, j: (i, 0)),
            pl.BlockSpec((ka, tn), lambda i, j: (0, j)),
            pl.BlockSpec((kb, tn), lambda i, j: (0, j)),
        ],
        out_specs=pl.BlockSpec((tm, tn), lambda i, j: (i, j)),
        out_shape=jax.ShapeDtypeStruct((T, D), F32),
        compiler_params=_params("parallel", "arbitrary"),
    )(x2, mix2, mo2, wa, wb)


def _flash_kernel(*refs, nqk, decay, tq):
    q_refs, k_refs, v_ref = refs[:nqk], refs[nqk:2 * nqk], refs[2 * nqk]
    pos = 2 * nqk + 1
    if decay:
        cq_ref, ck_ref = refs[pos:pos + 2]
        pos += 2
    o_ref, m_ref, l_ref, acc_ref = refs[pos:pos + 4]
    h, i = pl.program_id(1), pl.program_id(2)

    q = q_refs[0][...] if nqk == 1 else jnp.concatenate([r[...] for r in q_refs], axis=-1)
    if decay:
        lane = lax.broadcasted_iota(jnp.int32, (tq, LANE), 1)
        cq = jnp.sum(jnp.where(lane == h, cq_ref[...], 0.0), axis=-1, keepdims=True)
    m_ref[...] = jnp.full_like(m_ref, NEG_BIG)
    l_ref[...] = jnp.zeros_like(l_ref)
    acc_ref[...] = jnp.zeros_like(acc_ref)

    def step(j, diagonal):
        off = pl.multiple_of(j * tq, tq)
        ks = [r[pl.ds(off, tq), :] for r in k_refs]
        k = ks[0] if nqk == 1 else jnp.concatenate(ks, axis=-1)
        s = lax.dot_general(q, k, (((1,), (1,)), ((), ())), preferred_element_type=F32)
        if decay:
            s = s + (cq - ck_ref[:, pl.ds(off, tq)])
        if diagonal:
            row = lax.broadcasted_iota(jnp.int32, (tq, tq), 0)
            col = lax.broadcasted_iota(jnp.int32, (tq, tq), 1)
            s = jnp.where(col <= row, s, -jnp.inf)
        m_old = m_ref[...]
        m_new = jnp.maximum(m_old, jnp.max(s, axis=-1, keepdims=True))
        p = jnp.exp(s - m_new)
        alpha = jnp.exp(m_old - m_new)
        l_ref[...] = alpha * l_ref[...] + jnp.sum(p, axis=-1, keepdims=True)
        acc_ref[...] = alpha * acc_ref[...] + jnp.dot(
            p.astype(BF16), v_ref[pl.ds(off, tq), :], preferred_element_type=F32)
        m_ref[...] = m_new

    def body(j, carry):
        step(j, False)
        return carry

    lax.fori_loop(0, i, body, 0)
    step(i, True)
    o_ref[...] = (acc_ref[...] / l_ref[...]).astype(o_ref.dtype)


def _flash(q_parts, k_parts, v_part, B, S, decay=None, tq=512):
    tq = _tile(S, tq)
    nqk = len(q_parts)
    args, specs = [], []
    for arr, off, _ in q_parts:
        args.append(arr)
        specs.append(pl.BlockSpec((None, tq, LANE), lambda b, h, i, off=off: (b, i, off + h)))
    for arr, off, per_head in k_parts + [v_part]:
        args.append(arr)
        if per_head:
            specs.append(pl.BlockSpec((None, S, LANE), lambda b, h, i, off=off: (b, 0, off + h)))
        else:
            specs.append(pl.BlockSpec((None, S, LANE), lambda b, h, i, off=off: (b, 0, off)))
    if decay is not None:
        cum, cum_t = decay
        args += [cum, cum_t]
        specs.append(pl.BlockSpec((None, tq, LANE), lambda b, h, i: (b, i, 0)))
        specs.append(pl.BlockSpec((None, None, 1, S), lambda b, h, i: (b, h, 0, 0)))
    return pl.pallas_call(
        functools.partial(_flash_kernel, nqk=nqk, decay=decay is not None, tq=tq),
        grid=(B, N_HEADS, S // tq),
        in_specs=specs,
        out_specs=pl.BlockSpec((None, tq, LANE), lambda b, h, i: (b, i, h)),
        out_shape=jax.ShapeDtypeStruct((B, S, MIX_WIDTH), BF16),
        scratch_shapes=[pltpu.VMEM((tq, 1), F32), pltpu.VMEM((tq, 1), F32),
                        pltpu.VMEM((tq, HEAD_DIM), F32)],
        compiler_params=_params("parallel", "parallel", "arbitrary"),
    )(*args)


def _gate_kernel(fg_ref, bf_ref, cum_ref, cumt_ref, carry_ref, *, ts):
    @pl.when(pl.program_id(1) == 0)
    def _():
        carry_ref[...] = jnp.zeros_like(carry_ref)

    z = fg_ref[...] + bf_ref[...]
    lf = jnp.minimum(z, 0.0) - jnp.log1p(jnp.exp(-jnp.abs(z)))
    hi = lf.astype(BF16)
    r1 = lf - hi.astype(F32)
    mid = r1.astype(BF16)
    lo = (r1 - mid.astype(F32)).astype(BF16)
    row = lax.broadcasted_iota(jnp.int32, (ts, ts), 0)
    col = lax.broadcasted_iota(jnp.int32, (ts, ts), 1)
    tri = jnp.where(col <= row, 1.0, 0.0).astype(BF16)
    cum = (jnp.dot(tri, hi, preferred_element_type=F32)
           + jnp.dot(tri, mid, preferred_element_type=F32)
           + jnp.dot(tri, lo, preferred_element_type=F32)) + carry_ref[...]
    carry_ref[...] = cum[ts - 1:ts, :]
    cum_ref[...] = cum
    cumt_ref[...] = cum.T


def _fox_gates(fg, b_f):
    B, S, _ = fg.shape
    ts = _tile(S, 512)
    bf = jnp.zeros((1, LANE), F32).at[0, :N_HEADS].set(b_f.astype(F32))
    return pl.pallas_call(
        functools.partial(_gate_kernel, ts=ts),
        grid=(B, S // ts),
        in_specs=[pl.BlockSpec((None, ts, LANE), lambda b, i: (b, i, 0)),
                  pl.BlockSpec((1, LANE), lambda b, i: (0, 0))],
        out_specs=[pl.BlockSpec((None, ts, LANE), lambda b, i: (b, i, 0)),
                   pl.BlockSpec((None, LANE, ts), lambda b, i: (b, 0, i))],
        out_shape=[jax.ShapeDtypeStruct((B, S, LANE), F32),
                   jax.ShapeDtypeStruct((B, LANE, S), F32)],
        scratch_shapes=[pltpu.VMEM((1, LANE), F32)],
        compiler_params=_params("parallel", "arbitrary"),
    )(fg, bf)


def _rope_kernel(x_ref, cos_ref, sin_ref, g_ref, o_ref):
    cos, sin, g = cos_ref[...], sin_ref[...], g_ref[...]
    for c in range(x_ref.shape[1] // LANE):
        sl = slice(c * LANE, (c + 1) * LANE)
        x = x_ref[:, sl]
        ms = jnp.sum(x * x, axis=-1, keepdims=True) * (1.0 / ROPE_DIM)
        y = x * lax.rsqrt(ms + RMS_EPS) * g
        partner = pltpu.roll(y, ROPE_DIM // 2, 1) + pltpu.roll(y, LANE - ROPE_DIM // 2, 1)
        o_ref[:, sl] = (y * cos + partner * sin).astype(o_ref.dtype)


def _rope(x, cos, sin, gain):
    B, S, N = x.shape
    ts = _tile(S, 512)
    g = jnp.zeros((1, LANE), F32).at[0, :ROPE_DIM].set(gain.astype(F32))
    return pl.pallas_call(
        _rope_kernel,
        grid=(B, S // ts),
        in_specs=[pl.BlockSpec((None, ts, N), lambda b, i: (b, i, 0)),
                  pl.BlockSpec((None, ts, LANE), lambda b, i: (b, i, 0)),
                  pl.BlockSpec((None, ts, LANE), lambda b, i: (b, i, 0)),
                  pl.BlockSpec((1, LANE), lambda b, i: (0, 0))],
        out_specs=pl.BlockSpec((None, ts, N), lambda b, i: (b, i, 0)),
        out_shape=jax.ShapeDtypeStruct((B, S, N), BF16),
        compiler_params=_params("parallel", "parallel"),
    )(x, cos, sin, g)


def _mem_kernel(q_ref, kv_ref, o_ref):
    for h in range(MEM_HEADS):
        q = q_ref[:, h * LANE:(h + 1) * LANE]
        k = kv_ref[:, h * LANE:(h + 1) * LANE]
        v = kv_ref[:, MEM_WIDTH + h * LANE:MEM_WIDTH + (h + 1) * LANE]
        s = lax.dot_general(q, k, (((1,), (1,)), ((), ())), preferred_element_type=F32)
        p = jnp.exp(s - jnp.max(s, axis=-1, keepdims=True))
        o = jnp.dot(p.astype(BF16), v, preferred_element_type=F32)
        o_ref[:, h * LANE:(h + 1) * LANE] = (o / jnp.sum(p, axis=-1, keepdims=True)).astype(o_ref.dtype)


def _mem_attention(qarr, qblock, mem_kv):
    B, S, _ = qarr.shape
    n_mem = mem_kv.shape[1]
    tq = _tile(S, 1024)
    return pl.pallas_call(
        _mem_kernel,
        grid=(B, S // tq),
        in_specs=[pl.BlockSpec((None, tq, MEM_WIDTH), lambda b, i: (b, i, qblock)),
                  pl.BlockSpec((None, n_mem, 2 * MEM_WIDTH), lambda b, i: (b, 0, 0))],
        out_specs=pl.BlockSpec((None, tq, MEM_WIDTH), lambda b, i: (b, i, 0)),
        out_shape=jax.ShapeDtypeStruct((B, S, MEM_WIDTH), BF16),
        compiler_params=_params("parallel", "parallel"),
    )(qarr, mem_kv)


def _dil_kernel(q_ref, kp_ref, kc_ref, vp_ref, vc_ref, bias_ref, o_ref, lse_ref):
    n = DIL_BLOCK
    has_prev = pl.program_id(1) > 0
    row = lax.broadcasted_iota(jnp.int32, (n, n), 0)
    col = lax.broadcasted_iota(jnp.int32, (n, n), 1)
    ok_prev = jnp.logical_and(col >= row, has_prev)
    ok_cur = col <= row
    lane = lax.broadcasted_iota(jnp.int32, (n, LANE), 1)
    lse_all = jnp.zeros((n, LANE), F32)
    dn = (((1,), (1,)), ((), ()))
    for h in range(N_HEADS):
        sl = slice(h * LANE, (h + 1) * LANE)
        q = q_ref[:, sl]
        sp = lax.dot_general(q, kp_ref[:, sl], dn, preferred_element_type=F32) + bias_ref[h, :, :n]
        sc = lax.dot_general(q, kc_ref[:, sl], dn, preferred_element_type=F32) + bias_ref[h, :, n:]
        sp = jnp.where(ok_prev, sp, -jnp.inf)
        sc = jnp.where(ok_cur, sc, -jnp.inf)
        m = jnp.maximum(jnp.max(sp, axis=-1, keepdims=True), jnp.max(sc, axis=-1, keepdims=True))
        ep, ec = jnp.exp(sp - m), jnp.exp(sc - m)
        den = jnp.sum(ep, axis=-1, keepdims=True) + jnp.sum(ec, axis=-1, keepdims=True)
        o = (jnp.dot(ep.astype(BF16), vp_ref[:, sl], preferred_element_type=F32)
             + jnp.dot(ec.astype(BF16), vc_ref[:, sl], preferred_element_type=F32))
        o_ref[:, sl] = o / den
        lse_all = jnp.where(lane == h, m + jnp.log(den), lse_all)
    lse_ref[...] = lse_all


def _dil_attention(pg, bias):
    N, Ls, _ = pg.shape
    n = DIL_BLOCK
    blk = lambda part, prev: pl.BlockSpec(
        (None, n, MIX_WIDTH),
        (lambda s, i: (s, jnp.maximum(i - 1, 0), part)) if prev else (lambda s, i: (s, i, part)))
    return pl.pallas_call(
        _dil_kernel,
        grid=(N, Ls // n),
        in_specs=[blk(0, False), blk(1, True), blk(1, False), blk(2, True), blk(2, False),
                  pl.BlockSpec((N_HEADS, n, 2 * n), lambda s, i: (0, 0, 0))],
        out_specs=[pl.BlockSpec((None, n, MIX_WIDTH), lambda s, i: (s, i, 0)),
                   pl.BlockSpec((None, n, LANE), lambda s, i: (s, i, 0))],
        out_shape=[jax.ShapeDtypeStruct((N, Ls, MIX_WIDTH), F32),
                   jax.ShapeDtypeStruct((N, Ls, LANE), F32)],
        compiler_params=_params("parallel", "arbitrary"),
    )(pg, pg, pg, pg, pg, bias)


def _dil_combine_kernel(o1_ref, o2_ref, o3_ref, l1_ref, l2_ref, l3_ref, out_ref):
    l1, l2, l3 = l1_ref[...], l2_ref[...], l3_ref[...]
    m = jnp.maximum(jnp.maximum(l1, l2), l3)
    e1, e2, e3 = jnp.exp(l1 - m), jnp.exp(l2 - m), jnp.exp(l3 - m)
    den = e1 + e2 + e3
    a1, a2, a3 = e1 / den, e2 / den, e3 / den
    for h in range(N_HEADS):
        sl = slice(h * LANE, (h + 1) * LANE)
        o = (a1[:, h:h + 1] * o1_ref[:, sl] + a2[:, h:h + 1] * o2_ref[:, sl]
             + a3[:, h:h + 1] * o3_ref[:, sl])
        out_ref[:, sl] = o.astype(out_ref.dtype)


def _dil_combine(outs, lses, B, S):
    (_, d2), (_, d3) = DIL_GROUPS[1], DIL_GROUPS[2]
    q = d3 // d2
    L3 = S // d3
    ta = _tile(L3, 256)
    W = MIX_WIDTH

    def views(a, w):
        return (a[0].reshape(B, L3, d3 * w),
                a[1].reshape(B, d2, L3, q * w),
                a[2].reshape(B, d3, L3, w))

    o1, o2, o3 = views(outs, W)
    l1, l2, l3 = views(lses, LANE)

    def specs(w):
        return [pl.BlockSpec((None, ta, w), lambda b, r, i: (b, i, r)),
                pl.BlockSpec((None, None, ta, w), lambda b, r, i: (b, r % d2, i, r // d2)),
                pl.BlockSpec((None, None, ta, w), lambda b, r, i: (b, r, i, 0))]

    out = pl.pallas_call(
        _dil_combine_kernel,
        grid=(B, d3, L3 // ta),
        in_specs=specs(W) + specs(LANE),
        out_specs=pl.BlockSpec((None, ta, W), lambda b, r, i: (b, i, r)),
        out_shape=jax.ShapeDtypeStruct((B, L3, d3 * W), BF16),
        compiler_params=_params("parallel", "parallel", "parallel"),
    )(o1, o2, o3, l1, l2, l3)
    return out.reshape(B, S, W)


def _dsa_kernel(q_ref, qi_ref, wi_ref, ki_ref, k_ref, v_ref, tb_ref, o_ref,
                sc_ref, key_ref, m_ref, l_ref, acc_ref, *, tq, tk, S, n_sel):
    i = pl.program_id(1)
    q0 = i * tq
    n_chunks = (q0 + tq + tk - 1) // tk
    dn = (((1,), (1,)), ((), ()))
    rowg = q0 + lax.broadcasted_iota(jnp.int32, (tq, tk), 0)
    colb = lax.broadcasted_iota(jnp.int32, (tq, tk), 1)

    sc_ref[...] = jnp.full_like(sc_ref, -jnp.inf)
    wi = wi_ref[...] * (IDX_HEADS ** -0.5 * IDX_DIM ** -0.5)

    def idx_body(c, carry):
        off = pl.multiple_of(c * tk, tk)
        kic = ki_ref[pl.ds(off, tk), :].astype(BF16)
        acc = jnp.zeros((tq, tk), F32)
        for h in range(IDX_HEADS):
            d = lax.dot_general(qi_ref[:, h * LANE:(h + 1) * LANE], kic, dn,
                                preferred_element_type=F32)
            acc = acc + wi[:, IDX_DIM + h:IDX_DIM + h + 1] * jnp.maximum(d, 0.0)
        sc_ref[:, pl.ds(off, tk)] = jnp.where(off + colb <= rowg, acc, -jnp.inf)
        return carry

    lax.fori_loop(0, n_chunks, idx_body, 0)

    bits = pltpu.bitcast(sc_ref[...], jnp.int32)
    key_ref[...] = bits ^ ((bits >> 31) & jnp.int32(0x7FFFFFFF))

    def count(mask):
        return jnp.sum(jnp.where(mask, 1.0, 0.0), axis=-1, keepdims=True)

    k_sel = float(n_sel)
    lo = jnp.where(count(key_ref[...] >= 0) >= k_sel, jnp.int32(0), jnp.int32(-2 ** 31))

    def bis_body(it, lo):
        cand = lo + (jnp.int32(1) << (30 - it))
        return jnp.where(count(key_ref[...] >= cand) >= k_sel, cand, lo)

    thr = lax.fori_loop(0, 31, bis_body, lo)
    keys = key_ref[...]
    need = k_sel - count(keys > thr)
    colg = lax.broadcasted_iota(jnp.int32, (tq, S), 1)
    n_bits = max(1, (S - 1).bit_length())

    def tie_body(it, jlo):
        cand = jlo + (jnp.int32(1) << (n_bits - 1 - it))
        below = jnp.logical_and(key_ref[...] == thr, colg < cand)
        return jnp.where(count(below) < need, cand, jlo)

    jsel = lax.fori_loop(0, n_bits, tie_body, jnp.zeros((tq, 1), jnp.int32))
    rowq = q0 + lax.broadcasted_iota(jnp.int32, (tq, S), 0)
    sel = jnp.logical_or(keys > thr, jnp.logical_and(keys == thr, colg <= jsel))
    sel = jnp.logical_and(sel, colg <= rowq)
    sc_ref[...] = jnp.where(sel, 0.0, -jnp.inf)

    L = tq + tk
    for h in range(N_HEADS):
        g = h // (N_HEADS // DSA_KV_HEADS)
        qh = q_ref[:, h * LANE:(h + 1) * LANE]
        m_ref[...] = jnp.full_like(m_ref, NEG_BIG)
        l_ref[...] = jnp.zeros_like(l_ref)
        acc_ref[...] = jnp.zeros_like(acc_ref)

        def att_body(c, carry, h=h, g=g, qh=qh):
            off = pl.multiple_of(c * tk, tk)
            kc = k_ref[pl.ds(off, tk), g * LANE:(g + 1) * LANE]
            vc = v_ref[pl.ds(off, tk), g * LANE:(g + 1) * LANE]
            s = lax.dot_general(qh, kc, dn, preferred_element_type=F32)
            st = pl.multiple_of(S - q0 + off, LANE)
            win = jnp.broadcast_to(tb_ref[h:h + 1, pl.ds(st, L)], (tq, L))
            bias = pltpu.roll(win, tk + 1, 1, stride=1, stride_axis=0)[:, :tk]
            s = s + bias + sc_ref[:, pl.ds(off, tk)]
            m_old = m_ref[...]
            m_new = jnp.maximum(m_old, jnp.max(s, axis=-1, keepdims=True))
            p = jnp.exp(s - m_new)
            alpha = jnp.exp(m_old - m_new)
            l_ref[...] = alpha * l_ref[...] + jnp.sum(p, axis=-1, keepdims=True)
            acc_ref[...] = alpha * acc_ref[...] + jnp.dot(p.astype(BF16), vc,
                                                          preferred_element_type=F32)
            m_ref[...] = m_new
            return carry

        lax.fori_loop(0, n_chunks, att_body, 0)
        o_ref[:, h * LANE:(h + 1) * LANE] = (acc_ref[...] / l_ref[...]).astype(o_ref.dtype)


def _dsa_attention(pm, aux, tb, B, S, n_sel):
    tq, tk = _tile(S, 256), _tile(S, 512)
    kvw = DSA_KV_HEADS * HEAD_DIM
    kblk = 2 * MIX_WIDTH // kvw
    return pl.pallas_call(
        functools.partial(_dsa_kernel, tq=tq, tk=tk, S=S, n_sel=n_sel),
        grid=(B, S // tq),
        in_specs=[pl.BlockSpec((None, tq, MIX_WIDTH), lambda b, i: (b, i, 0)),
                  pl.BlockSpec((None, tq, MIX_WIDTH), lambda b, i: (b, i, 1)),
                  pl.BlockSpec((None, tq, LANE), lambda b, i: (b, i, 0)),
                  pl.BlockSpec((None, S, LANE), lambda b, i: (b, 0, 0)),
                  pl.BlockSpec((None, S, kvw), lambda b, i: (b, 0, kblk)),
                  pl.BlockSpec((None, S, kvw), lambda b, i: (b, 0, kblk + 1)),
                  pl.BlockSpec(tb.shape, lambda b, i: (0, 0))],
        out_specs=pl.BlockSpec((None, tq, MIX_WIDTH), lambda b, i: (b, i, 0)),
        out_shape=jax.ShapeDtypeStruct((B, S, MIX_WIDTH), BF16),
        scratch_shapes=[pltpu.VMEM((tq, S), F32), pltpu.VMEM((tq, S), jnp.int32),
                        pltpu.VMEM((tq, 1), F32), pltpu.VMEM((tq, 1), F32),
                        pltpu.VMEM((tq, HEAD_DIM), F32)],
        compiler_params=_params("parallel", "arbitrary"),
    )(pm, pm, aux, aux, pm, pm, tb)


def _t5_bucket(dist):
    n = jnp.maximum(dist, 0)
    exact = REL_BUCKETS // 2
    nf = jnp.maximum(n, 1).astype(F32)
    large = exact + (jnp.log(nf / exact) / math.log(REL_MAX_DIST / exact)
                     * (REL_BUCKETS - exact)).astype(jnp.int32)
    large = jnp.minimum(large, REL_BUCKETS - 1)
    return jnp.where(n < exact, n, large)


def _dil_bias(t5_table, dil):
    n = DIL_BLOCK
    rel = jnp.arange(n)[:, None] + n - jnp.arange(2 * n)[None, :]
    return jnp.moveaxis(t5_table[_t5_bucket(rel * dil)], -1, 0).astype(F32)


def _dsa_bias_table(t5_table, S, tq, tk):
    m = jnp.arange(S + tq + tk)
    return jnp.moveaxis(t5_table[_t5_bucket(S + tq - 1 - m)], -1, 0).astype(F32)


def _headnorm_cols(spec):
    eg, ef = [], []
    for gain, count, scale in spec:
        if gain is None:
            eg.append(jnp.ones((count * LANE,), F32))
            ef.append(jnp.zeros((count * LANE,), F32))
        else:
            eg.append(jnp.tile(gain.astype(F32) * scale, count))
            ef.append(jnp.ones((count * LANE,), F32))
    return jnp.concatenate(eg), jnp.concatenate(ef)


def _pad_cols(w, width):
    return jnp.pad(w, ((0, 0), (0, width - w.shape[1])))


def _fox_layer(x, norm_g, w_in, b_f, qk_g, mem_g):
    B, S, _ = x.shape
    W = MIX_WIDTH
    w_main = jnp.concatenate([w_in[:, :3 * W], w_in[:, 3 * W + N_HEADS:]], axis=1).astype(BF16)
    w_gate = _pad_cols(w_in[:, 3 * W:3 * W + N_HEADS], LANE).astype(BF16)
    eg, ef = _headnorm_cols([(qk_g[0], N_HEADS, HEAD_DIM ** -0.5), (qk_g[1], N_HEADS, 1.0),
                             (None, N_HEADS, 1.0), (mem_g, MEM_HEADS, HEAD_DIM ** -0.5)])
    pm = _proj(x, norm_g, w_main, eg, ef)
    fg = _proj(x, norm_g, w_gate, out_dtype=F32)
    cum, cum_t = _fox_gates(fg, b_f)
    mix = _flash([(pm, 0, True)], [(pm, N_HEADS, True)], (pm, 2 * N_HEADS, True), B, S,
                 decay=(cum, cum_t.reshape(B, LANE, 1, S)))
    return mix, pm, 3 * W // MEM_WIDTH


def _mla_layer(x, positions, norm_g, w_in, q_norm, w_uq, kv_norm, w_ukv, nope_g, rope_g, mem_g):
    B, S, _ = x.shape
    scale = (NOPE_DIM + ROPE_DIM) ** -0.5
    lat = Q_LORA + KV_LORA
    w_lat = jnp.concatenate([w_in[:, :lat], _pad_cols(w_in[:, lat:lat + ROPE_DIM], LANE)],
                            axis=1).astype(BF16)
    w_mem = w_in[:, lat + ROPE_DIM:].astype(BF16)
    pl_ = _proj(x, norm_g, w_lat, out_dtype=F32, tn=lat + LANE)
    eg, ef = _headnorm_cols([(mem_g, MEM_HEADS, HEAD_DIM ** -0.5)])
    pmem = _proj(x, norm_g, w_mem, eg, ef)

    uq = w_uq.reshape(Q_LORA, N_HEADS, NOPE_DIM + ROPE_DIM)
    w_qn = uq[:, :, :NOPE_DIM].reshape(Q_LORA, MIX_WIDTH).astype(BF16)
    w_qr = jnp.pad(uq[:, :, NOPE_DIM:], ((0, 0), (0, 0), (0, LANE - ROPE_DIM))
                   ).reshape(Q_LORA, N_HEADS * LANE).astype(BF16)
    ukv = w_ukv.reshape(KV_LORA, N_HEADS, NOPE_DIM + HEAD_DIM)
    w_kv = jnp.concatenate([ukv[:, :, :NOPE_DIM].reshape(KV_LORA, MIX_WIDTH),
                            ukv[:, :, NOPE_DIM:].reshape(KV_LORA, MIX_WIDTH)], axis=1).astype(BF16)
    eg, ef = _headnorm_cols([(nope_g[0], N_HEADS, scale)])
    qn = _proj(pl_, q_norm, w_qn, eg, ef, kblock=0)
    qr_raw = _proj(pl_, q_norm, w_qr, kblock=0, out_dtype=F32)
    eg, ef = _headnorm_cols([(nope_g[1], N_HEADS, 1.0), (None, N_HEADS, 1.0)])
    kv = _proj(pl_, kv_norm, w_kv, eg, ef, kblock=1)

    half = ROPE_DIM // 2
    inv = ROPE_THETA ** (-jnp.arange(half, dtype=F32) / half)
    ang = positions.astype(F32)[..., None] * inv
    cos, sin = jnp.cos(ang), jnp.sin(ang)
    zero = jnp.zeros((B, S, LANE - ROPE_DIM), F32)
    cos_t = jnp.concatenate([cos, cos, zero], axis=-1)
    sin_t = jnp.concatenate([-sin, sin, zero], axis=-1)
    qr = _rope(qr_raw, cos_t, sin_t, rope_g[0] * scale)
    kr = _rope(pl_[:, :, lat:], cos_t, sin_t, rope_g[1])
    mix = _flash([(qn, 0, True), (qr, 0, True)], [(kv, 0, True), (kr, 0, False)],
                 (kv, N_HEADS, True), B, S)
    return mix, pmem, 0


def _dil_layer(x, norm_g, w_in, qk_g, t5_table, mem_g):
    B, S, _ = x.shape
    W = MIX_WIDTH
    outs, lses = [], []
    for gi, (win, dil) in enumerate(DIL_GROUPS):
        assert win // dil == DIL_BLOCK and (S // dil) % DIL_BLOCK == 0
        w_g = w_in[:, gi * 3 * W:(gi + 1) * 3 * W].astype(BF16)
        eg, ef = _headnorm_cols([(qk_g[gi, 0], N_HEADS, HEAD_DIM ** -0.5),
                                 (qk_g[gi, 1], N_HEADS, 1.0), (None, N_HEADS, 1.0)])
        tm = min(1024, S // dil)
        pg = _proj(x, norm_g, w_g, eg, ef, dil=dil, tm=tm)
        o, lse = _dil_attention(pg.reshape(B * dil, S // dil, 3 * W), _dil_bias(t5_table, dil))
        outs.append(o)
        lses.append(lse)
    mix = _dil_combine(outs, lses, B, S)
    eg, ef = _headnorm_cols([(mem_g, MEM_HEADS, HEAD_DIM ** -0.5)])
    pmem = _proj(x, norm_g, w_in[:, len(DIL_GROUPS) * 3 * W:].astype(BF16), eg, ef)
    return mix, pmem, 0


def _dsa_layer(x, norm_g, w_in, qk_g, t5_table, mem_g):
    B, S, _ = x.shape
    W = MIX_WIDTH
    kvw = DSA_KV_HEADS * HEAD_DIM
    o_k, o_v, o_qi = W, W + kvw, W + 2 * kvw
    o_ki = o_qi + IDX_HEADS * IDX_DIM
    o_wi = o_ki + IDX_DIM
    o_mem = o_wi + IDX_HEADS
    w_qi = jnp.pad(w_in[:, o_qi:o_ki].reshape(D_MODEL, IDX_HEADS, IDX_DIM),
                   ((0, 0), (0, 0), (0, LANE - IDX_DIM))).reshape(D_MODEL, IDX_HEADS * LANE)
    w_main = jnp.concatenate([w_in[:, :W], w_qi, w_in[:, o_k:o_qi], w_in[:, o_mem:]],
                             axis=1).astype(BF16)
    w_aux = _pad_cols(w_in[:, o_ki:o_mem], LANE).astype(BF16)
    eg, ef = _headnorm_cols([(qk_g[0], N_HEADS, HEAD_DIM ** -0.5), (None, IDX_HEADS, 1.0),
                             (qk_g[1], DSA_KV_HEADS, 1.0), (None, DSA_KV_HEADS, 1.0),
                             (mem_g, MEM_HEADS, HEAD_DIM ** -0.5)])
    pm = _proj(x, norm_g, w_main, eg, ef)
    aux = _proj(x, norm_g, w_aux, out_dtype=F32)
    tq, tk = _tile(S, 256), _tile(S, 512)
    tb = _dsa_bias_table(t5_table, S, tq, tk)
    mix = _dsa_attention(pm, aux, tb, B, S, min(TOPK_MAX, S // 4))
    return mix, pm, (2 * W + 2 * kvw) // MEM_WIDTH


def kernel(x, mem, positions, t5_table, ffn_norm, ffn_w_gate, ffn_w_up, ffn_w_down, attn_norm,
           mem_norm, mem_w_kv, mem_qk_g, w_out, a_w_in, a_b_f, a_qk_g, b_w_in, b_q_norm, b_w_uq,
           b_kv_norm, b_w_ukv, b_nope_g, b_rope_g, c_w_in, c_qk_g, d_w_in, d_qk_g):
    B, S, D = x.shape
    depth = ffn_norm.shape[0]
    n_mixers = 4

    def ffn(xc, i, k):
        return _ffn(xc.reshape(B * S, D), ffn_norm[i, k], ffn_w_gate[i, k].astype(BF16),
                    ffn_w_up[i, k].astype(BF16), ffn_w_down[i, k].astype(BF16)).reshape(B, S, D)

    for i in range(depth):
        m, j = i % n_mixers, i // n_mixers
        x = ffn(x, i, 0)
        mem_g = mem_qk_g[i]
        if m == 0:
            mix, qarr, qblock = _fox_layer(x, attn_norm[i], a_w_in[j], a_b_f[j], a_qk_g[j], mem_g[0])
        elif m == 1:
            mix, qarr, qblock = _mla_layer(x, positions, attn_norm[i], b_w_in[j], b_q_norm[j],
                                           b_w_uq[j], b_kv_norm[j], b_w_ukv[j], b_nope_g[j],
                                           b_rope_g[j], mem_g[0])
        elif m == 2:
            mix, qarr, qblock = _dil_layer(x, attn_norm[i], c_w_in[j], c_qk_g[j], t5_table, mem_g[0])
        else:
            mix, qarr, qblock = _dsa_layer(x, attn_norm[i], d_w_in[j], d_qk_g[j], t5_table, mem_g[0])
        eg, ef = _headnorm_cols([(mem_g[1], MEM_HEADS, 1.0), (None, MEM_HEADS, 1.0)])
        mem_kv = _proj(mem, mem_norm[i], mem_w_kv[i].astype(BF16), eg, ef)
        mo = _mem_attention(qarr, qblock, mem_kv)
        wo = w_out[i].astype(BF16)
        x = _out_proj(x.reshape(B * S, D), mix.reshape(B * S, MIX_WIDTH),
                      mo.reshape(B * S, MEM_WIDTH), wo[:MIX_WIDTH], wo[MIX_WIDTH:]).reshape(B, S, D)
        x = ffn(x, i, 1)
    return x
```

```python
import functools
import math

import jax
import jax.numpy as jnp
from jax import lax
from jax.experimental import pallas as pl
from jax.experimental.pallas import tpu as pltpu

F32 = jnp.float32
BF16 = jnp.bfloat16

LANE = 128
D_MODEL = 2048
N_HEADS = 16
HEAD_DIM = 128
MIX_WIDTH = N_HEADS * HEAD_DIM
MEM_HEADS = 4
MEM_WIDTH = MEM_HEADS * HEAD_DIM
D_FF = 5632
RMS_EPS = 1e-6
REL_BUCKETS = 32
REL_MAX_DIST = 2048
Q_LORA = 512
KV_LORA = 512
NOPE_DIM = 128
ROPE_DIM = 64
ROPE_THETA = 10000.0
DIL_GROUPS = ((128, 1), (512, 4), (2048, 16))
DIL_BLOCK = 128
DSA_KV_HEADS = 4
IDX_HEADS = 16
IDX_DIM = 64
TOPK_MAX = 256
VMEM_LIMIT = 56 * 1024 * 1024
NEG_BIG = -1e30
LOG2E = math.log2(math.e)


def _params(*sem):
    return pltpu.CompilerParams(dimension_semantics=sem, vmem_limit_bytes=VMEM_LIMIT)


def _tile(n, pref):
    t = min(n, pref)
    while n % t:
        t //= 2
    return t


def _ffn_kernel(x_ref, g_ref, wg_ref, wu_ref, wd_ref, o_ref, h_ref, acc_ref):
    j = pl.program_id(1)

    @pl.when(j == 0)
    def _():
        x = x_ref[...]
        ms = jnp.mean(x * x, axis=-1, keepdims=True)
        h_ref[...] = (x * lax.rsqrt(ms + RMS_EPS) * g_ref[...]).astype(BF16)
        acc_ref[...] = jnp.zeros_like(acc_ref)

    h = h_ref[...]
    g = jnp.dot(h, wg_ref[...], preferred_element_type=F32)
    u = jnp.dot(h, wu_ref[...], preferred_element_type=F32)
    a = (g * jax.nn.sigmoid(g) * u).astype(BF16)
    acc_ref[...] += jnp.dot(a, wd_ref[...], preferred_element_type=F32)

    @pl.when(j == pl.num_programs(1) - 1)
    def _():
        o_ref[...] = x_ref[...] + 0.5 * acc_ref[...]


def _ffn(x2, gain, wg, wu, wd):
    T, D = x2.shape
    F = wg.shape[1]
    tm, tf = _tile(T, 512), _tile(F, 512)
    return pl.pallas_call(
        _ffn_kernel,
        grid=(T // tm, F // tf),
        in_specs=[
            pl.BlockSpec((tm, D), lambda i, j: (i, 0)),
            pl.BlockSpec((1, D), lambda i, j: (0, 0)),
            pl.BlockSpec((D, tf), lambda i, j: (0, j)),
            pl.BlockSpec((D, tf), lambda i, j: (0, j)),
            pl.BlockSpec((tf, D), lambda i, j: (j, 0)),
        ],
        out_specs=pl.BlockSpec((tm, D), lambda i, j: (i, 0)),
        out_shape=jax.ShapeDtypeStruct((T, D), F32),
        scratch_shapes=[pltpu.VMEM((tm, D), BF16), pltpu.VMEM((tm, D), F32)],
        compiler_params=_params("parallel", "arbitrary"),
        name="ffn",
    )(x2, gain.reshape(1, D), wg, wu, wd)


def _proj_kernel(x_ref, g_ref, w_ref, eg_ref, ef_ref, o_ref, h_ref, *, epilogue):
    @pl.when(pl.program_id(3) == 0)
    def _():
        x = x_ref[...].astype(F32)
        ms = jnp.mean(x * x, axis=-1, keepdims=True)
        h_ref[...] = (x * lax.rsqrt(ms + RMS_EPS) * g_ref[...]).astype(BF16)

    acc = jnp.dot(h_ref[...], w_ref[...], preferred_element_type=F32)
    if epilogue:
        for c in range(acc.shape[1] // LANE):
            sl = slice(c * LANE, (c + 1) * LANE)
            a = acc[:, sl]
            ms = jnp.mean(a * a, axis=-1, keepdims=True)
            scale = jnp.where(ef_ref[:, sl] > 0.0, lax.rsqrt(ms + RMS_EPS), 1.0)
            o_ref[:, sl] = (a * scale * eg_ref[:, sl]).astype(o_ref.dtype)
    else:
        o_ref[...] = acc.astype(o_ref.dtype)


def _proj(x, gain, w, eg=None, ef=None, *, dil=1, kblock=0, out_dtype=BF16, tm=1024, tn=512):
    B, S, C = x.shape
    K, N = w.shape
    Ls = S // dil
    tm, tn = _tile(Ls, tm), _tile(N, tn)
    epilogue = eg is not None
    if not epilogue:
        eg = jnp.ones((1, N), F32)
        ef = jnp.zeros((1, N), F32)
    xv = x.reshape(B, Ls, dil * C)
    kpr = C // K
    out = pl.pallas_call(
        functools.partial(_proj_kernel, epilogue=epilogue),
        grid=(B, dil, Ls // tm, N // tn),
        in_specs=[
            pl.BlockSpec((None, tm, K), lambda b, r, i, j: (b, i, r * kpr + kblock)),
            pl.BlockSpec((1, K), lambda b, r, i, j: (0, 0)),
            pl.BlockSpec((K, tn), lambda b, r, i, j: (0, j)),
            pl.BlockSpec((1, tn), lambda b, r, i, j: (0, j)),
            pl.BlockSpec((1, tn), lambda b, r, i, j: (0, j)),
        ],
        out_specs=pl.BlockSpec((None, None, tm, tn), lambda b, r, i, j: (b, r, i, j)),
        out_shape=jax.ShapeDtypeStruct((B, dil, Ls, N), out_dtype),
        scratch_shapes=[pltpu.VMEM((tm, K), BF16)],
        compiler_params=_params("parallel", "parallel", "parallel", "arbitrary"),
        name="proj",
    )(xv, gain.reshape(1, K).astype(F32), w, eg.reshape(1, N), ef.reshape(1, N))
    return out.reshape(B, S, N) if dil == 1 else out


def _out_kernel(x_ref, a_ref, b_ref, wa_ref, wb_ref, o_ref):
    acc = jnp.dot(a_ref[...], wa_ref[...], preferred_element_type=F32)
    acc += jnp.dot(b_ref[...], wb_ref[...], preferred_element_type=F32)
    o_ref[...] = x_ref[...] + acc


def _out_proj(x2, mix2, mo2, wa, wb):
    T, D = x2.shape
    tm, tn = _tile(T, 1024), _tile(D, 512)
    ka, kb = mix2.shape[1], mo2.shape[1]
    return pl.pallas_call(
        _out_kernel,
        grid=(T // tm, D // tn),
        in_specs=[
            pl.BlockSpec((tm, tn), lambda i, j: (i, j)),
            pl.BlockSpec((tm, ka), lambda i, j: (i, 0)),
            pl.BlockSpec((tm, kb), lambda i, j: (i, 0)),
            pl.BlockSpec((ka, tn), lambda i, j: (0, j)),
            pl.BlockSpec((kb, tn), lambda i, j: (0, j)),
        ],
        out_specs=pl.BlockSpec((tm, tn), lambda i, j: (i, j)),
        out_shape=jax.ShapeDtypeStruct((T, D), F32),
        compiler_params=_params("parallel", "arbitrary"),
        name="out_proj",
    )(x2, mix2, mo2, wa, wb)


def _softmax_step(s, v, m_ref, acc_ref, g):
    reps = s.shape[1] // LANE
    m_old = m_ref[g]
    m_new = jnp.maximum(m_old, jnp.max(s, axis=-1, keepdims=True))
    p = jnp.exp2(s - pltpu.repeat(m_new, reps, 1)).astype(BF16)
    alpha = jnp.exp2(m_old - m_new)
    v1 = jnp.concatenate([v, jnp.ones_like(v)], axis=-1)
    acc_ref[g] = pltpu.repeat(alpha, 2, 1) * acc_ref[g] + jnp.dot(p, v1, preferred_element_type=F32)
    m_ref[g] = m_new


def _flash_kernel(*refs, nqk, decay, tq, G):
    q_refs, k_refs, v_ref = refs[:nqk], refs[nqk:2 * nqk], refs[2 * nqk]
    pos = 2 * nqk + 1
    if decay:
        ck_ref = refs[pos]
        pos += 1
    o_ref, m_ref, acc_ref = refs[pos:pos + 3]
    i = pl.program_id(2)

    def head(ref, g, rows=slice(None)):
        if ref.shape[-1] == LANE:
            return ref[rows, :]
        return ref[rows, g * LANE:(g + 1) * LANE]

    qs = []
    for g in range(G):
        parts = [head(r, g) for r in q_refs]
        qs.append(parts[0] if nqk == 1 else jnp.concatenate(parts, axis=-1))
    m_ref[...] = jnp.full_like(m_ref, NEG_BIG)
    acc_ref[...] = jnp.zeros_like(acc_ref)

    def step(j, diagonal):
        off = pl.multiple_of(j * tq, tq)
        rows = pl.ds(off, tq)
        if diagonal:
            row = lax.broadcasted_iota(jnp.int32, (tq, tq), 0)
            col = lax.broadcasted_iota(jnp.int32, (tq, tq), 1)
            visible = col <= row
        for g in range(G):
            parts = [head(r, g, rows) for r in k_refs]
            k = parts[0] if nqk == 1 else jnp.concatenate(parts, axis=-1)
            s = lax.dot_general(qs[g], k, (((1,), (1,)), ((), ())), preferred_element_type=F32)
            if decay:
                s = s - ck_ref[g, :, rows]
            if diagonal:
                s = jnp.where(visible, s, -jnp.inf)
            _softmax_step(s, head(v_ref, g, rows), m_ref, acc_ref, g)

    def body(j, carry):
        step(j, False)
        return carry

    lax.fori_loop(0, i, body, 0)
    step(i, True)
    for g in range(G):
        acc = acc_ref[g]
        o_ref[:, g * LANE:(g + 1) * LANE] = (acc[:, :LANE] / acc[:, LANE:]).astype(o_ref.dtype)


def _flash(q_parts, k_parts, v_part, B, S, decay_t=None, tq=512, G=4):
    tq = _tile(S, tq)
    nqk = len(q_parts)
    W = G * LANE
    args, specs = [], []
    for arr, off, _ in q_parts:
        args.append(arr)
        specs.append(pl.BlockSpec((None, tq, W), lambda b, h, i, off=off: (b, i, off // W + h)))
    for arr, off, per_head in k_parts + [v_part]:
        args.append(arr)
        if per_head:
            specs.append(pl.BlockSpec((None, S, W), lambda b, h, i, off=off: (b, 0, off // W + h)))
        else:
            specs.append(pl.BlockSpec((None, S, LANE), lambda b, h, i, off=off: (b, 0, off // LANE)))
    if decay_t is not None:
        args.append(decay_t)
        specs.append(pl.BlockSpec((None, G, 1, S), lambda b, h, i: (b, h, 0, 0)))
    return pl.pallas_call(
        functools.partial(_flash_kernel, nqk=nqk, decay=decay_t is not None, tq=tq, G=G),
        grid=(B, N_HEADS // G, S // tq),
        in_specs=specs,
        out_specs=pl.BlockSpec((None, tq, W), lambda b, h, i: (b, i, h)),
        out_shape=jax.ShapeDtypeStruct((B, S, MIX_WIDTH), BF16),
        scratch_shapes=[pltpu.VMEM((G, tq, LANE), F32), pltpu.VMEM((G, tq, 2 * HEAD_DIM), F32)],
        compiler_params=_params("parallel", "parallel", "arbitrary"),
        name="flash",
    )(*args)


def _gate_kernel(fg_ref, bf_ref, cumt_ref, carry_ref, *, ts):
    @pl.when(pl.program_id(1) == 0)
    def _():
        carry_ref[...] = jnp.zeros_like(carry_ref)

    z = fg_ref[...] + bf_ref[...]
    lf = jnp.minimum(z, 0.0) - jnp.log1p(jnp.exp(-jnp.abs(z)))
    hi = lf.astype(BF16)
    r1 = lf - hi.astype(F32)
    mid = r1.astype(BF16)
    lo = (r1 - mid.astype(F32)).astype(BF16)
    row = lax.broadcasted_iota(jnp.int32, (ts, ts), 0)
    col = lax.broadcasted_iota(jnp.int32, (ts, ts), 1)
    tri = jnp.where(col <= row, 1.0, 0.0).astype(BF16)
    cum = (jnp.dot(tri, hi, preferred_element_type=F32)
           + jnp.dot(tri, mid, preferred_element_type=F32)
           + jnp.dot(tri, lo, preferred_element_type=F32)) + carry_ref[...]
    carry_ref[...] = cum[ts - 1:ts, :]
    cumt_ref[...] = (cum * LOG2E).T


def _fox_gates(fg, b_f):
    B, S, _ = fg.shape
    ts = _tile(S, 512)
    bf = jnp.zeros((1, LANE), F32).at[0, :N_HEADS].set(b_f.astype(F32))
    return pl.pallas_call(
        functools.partial(_gate_kernel, ts=ts),
        grid=(B, S // ts),
        in_specs=[pl.BlockSpec((None, ts, LANE), lambda b, i: (b, i, 0)),
                  pl.BlockSpec((1, LANE), lambda b, i: (0, 0))],
        out_specs=pl.BlockSpec((None, LANE, ts), lambda b, i: (b, 0, i)),
        out_shape=jax.ShapeDtypeStruct((B, LANE, S), F32),
        scratch_shapes=[pltpu.VMEM((1, LANE), F32)],
        compiler_params=_params("parallel", "arbitrary"),
        name="fox_gates",
    )(fg, bf)


def _rope_kernel(x_ref, cos_ref, sin_ref, g_ref, o_ref):
    cos, sin, g = cos_ref[...], sin_ref[...], g_ref[...]
    for c in range(x_ref.shape[1] // LANE):
        sl = slice(c * LANE, (c + 1) * LANE)
        x = x_ref[:, sl]
        ms = jnp.sum(x * x, axis=-1, keepdims=True) * (1.0 / ROPE_DIM)
        y = x * lax.rsqrt(ms + RMS_EPS) * g
        partner = pltpu.roll(y, ROPE_DIM // 2, 1) + pltpu.roll(y, LANE - ROPE_DIM // 2, 1)
        o_ref[:, sl] = (y * cos + partner * sin).astype(o_ref.dtype)


def _rope(x, cos, sin, gain):
    B, S, N = x.shape
    ts = _tile(S, 512)
    g = jnp.zeros((1, LANE), F32).at[0, :ROPE_DIM].set(gain.astype(F32))
    return pl.pallas_call(
        _rope_kernel,
        grid=(B, S // ts),
        in_specs=[pl.BlockSpec((None, ts, N), lambda b, i: (b, i, 0)),
                  pl.BlockSpec((None, ts, LANE), lambda b, i: (b, i, 0)),
                  pl.BlockSpec((None, ts, LANE), lambda b, i: (b, i, 0)),
                  pl.BlockSpec((1, LANE), lambda b, i: (0, 0))],
        out_specs=pl.BlockSpec((None, ts, N), lambda b, i: (b, i, 0)),
        out_shape=jax.ShapeDtypeStruct((B, S, N), BF16),
        compiler_params=_params("parallel", "parallel"),
        name="rope",
    )(x, cos, sin, g)


def _mem_kernel(q_ref, kv_ref, o_ref):
    for h in range(MEM_HEADS):
        q = q_ref[:, h * LANE:(h + 1) * LANE]
        k = kv_ref[:, h * LANE:(h + 1) * LANE]
        v = kv_ref[:, MEM_WIDTH + h * LANE:MEM_WIDTH + (h + 1) * LANE]
        s = lax.dot_general(q, k, (((1,), (1,)), ((), ())), preferred_element_type=F32)
        p = jnp.exp(s - jnp.max(s, axis=-1, keepdims=True))
        o = jnp.dot(p.astype(BF16), v, preferred_element_type=F32)
        o_ref[:, h * LANE:(h + 1) * LANE] = (o / jnp.sum(p, axis=-1, keepdims=True)).astype(o_ref.dtype)


def _mem_attention(qarr, qblock, mem_kv):
    B, S, _ = qarr.shape
    n_mem = mem_kv.shape[1]
    tq = _tile(S, 1024)
    return pl.pallas_call(
        _mem_kernel,
        grid=(B, S // tq),
        in_specs=[pl.BlockSpec((None, tq, MEM_WIDTH), lambda b, i: (b, i, qblock)),
                  pl.BlockSpec((None, n_mem, 2 * MEM_WIDTH), lambda b, i: (b, 0, 0))],
        out_specs=pl.BlockSpec((None, tq, MEM_WIDTH), lambda b, i: (b, i, 0)),
        out_shape=jax.ShapeDtypeStruct((B, S, MEM_WIDTH), BF16),
        compiler_params=_params("parallel", "parallel"),
        name="mem_attn",
    )(qarr, mem_kv)


def _dil_kernel(q_ref, kp_ref, kc_ref, vp_ref, vc_ref, bias_ref, o_ref, lse_ref):
    n = DIL_BLOCK
    has_prev = pl.program_id(1) > 0
    row = lax.broadcasted_iota(jnp.int32, (n, n), 0)
    col = lax.broadcasted_iota(jnp.int32, (n, n), 1)
    ok_prev = jnp.logical_and(col >= row, has_prev)
    ok_cur = col <= row
    lane = lax.broadcasted_iota(jnp.int32, (n, LANE), 1)
    lse_all = jnp.zeros((n, LANE), F32)
    dn = (((1,), (1,)), ((), ()))
    for h in range(N_HEADS):
        sl = slice(h * LANE, (h + 1) * LANE)
        q = q_ref[:, sl]
        sp = lax.dot_general(q, kp_ref[:, sl], dn, preferred_element_type=F32) + bias_ref[h, :, :n]
        sc = lax.dot_general(q, kc_ref[:, sl], dn, preferred_element_type=F32) + bias_ref[h, :, n:]
        sp = jnp.where(ok_prev, sp, -jnp.inf)
        sc = jnp.where(ok_cur, sc, -jnp.inf)
        m = jnp.maximum(jnp.max(sp, axis=-1, keepdims=True), jnp.max(sc, axis=-1, keepdims=True))
        ep, ec = jnp.exp(sp - m), jnp.exp(sc - m)
        den = jnp.sum(ep, axis=-1, keepdims=True) + jnp.sum(ec, axis=-1, keepdims=True)
        o = (jnp.dot(ep.astype(BF16), vp_ref[:, sl], preferred_element_type=F32)
             + jnp.dot(ec.astype(BF16), vc_ref[:, sl], preferred_element_type=F32))
        o_ref[:, sl] = o / den
        lse_all = jnp.where(lane == h, m + jnp.log(den), lse_all)
    lse_ref[...] = lse_all


def _dil_attention(pg, bias):
    N, Ls, _ = pg.shape
    n = DIL_BLOCK
    blk = lambda part, prev: pl.BlockSpec(
        (None, n, MIX_WIDTH),
        (lambda s, i: (s, jnp.maximum(i - 1, 0), part)) if prev else (lambda s, i: (s, i, part)))
    return pl.pallas_call(
        _dil_kernel,
        grid=(N, Ls // n),
        in_specs=[blk(0, False), blk(1, True), blk(1, False), blk(2, True), blk(2, False),
                  pl.BlockSpec((N_HEADS, n, 2 * n), lambda s, i: (0, 0, 0))],
        out_specs=[pl.BlockSpec((None, n, MIX_WIDTH), lambda s, i: (s, i, 0)),
                   pl.BlockSpec((None, n, LANE), lambda s, i: (s, i, 0))],
        out_shape=[jax.ShapeDtypeStruct((N, Ls, MIX_WIDTH), F32),
                   jax.ShapeDtypeStruct((N, Ls, LANE), F32)],
        compiler_params=_params("parallel", "arbitrary"),
        name="dil_attn",
    )(pg, pg, pg, pg, pg, bias)


def _dil_combine_kernel(o1_ref, o2_ref, o3_ref, l1_ref, l2_ref, l3_ref, out_ref):
    l1, l2, l3 = l1_ref[...], l2_ref[...], l3_ref[...]
    m = jnp.maximum(jnp.maximum(l1, l2), l3)
    e1, e2, e3 = jnp.exp(l1 - m), jnp.exp(l2 - m), jnp.exp(l3 - m)
    den = e1 + e2 + e3
    a1, a2, a3 = e1 / den, e2 / den, e3 / den
    for h in range(N_HEADS):
        sl = slice(h * LANE, (h + 1) * LANE)
        o = (a1[:, h:h + 1] * o1_ref[:, sl] + a2[:, h:h + 1] * o2_ref[:, sl]
             + a3[:, h:h + 1] * o3_ref[:, sl])
        out_ref[:, sl] = o.astype(out_ref.dtype)


def _dil_combine(outs, lses, B, S):
    (_, d2), (_, d3) = DIL_GROUPS[1], DIL_GROUPS[2]
    q = d3 // d2
    L3 = S // d3
    ta = _tile(L3, 256)
    W = MIX_WIDTH

    def views(a, w):
        return (a[0].reshape(B, L3, d3 * w),
                a[1].reshape(B, d2, L3, q * w),
                a[2].reshape(B, d3, L3, w))

    o1, o2, o3 = views(outs, W)
    l1, l2, l3 = views(lses, LANE)

    def specs(w):
        return [pl.BlockSpec((None, ta, w), lambda b, r, i: (b, i, r)),
                pl.BlockSpec((None, None, ta, w), lambda b, r, i: (b, r % d2, i, r // d2)),
                pl.BlockSpec((None, None, ta, w), lambda b, r, i: (b, r, i, 0))]

    out = pl.pallas_call(
        _dil_combine_kernel,
        grid=(B, d3, L3 // ta),
        in_specs=specs(W) + specs(LANE),
        out_specs=pl.BlockSpec((None, ta, W), lambda b, r, i: (b, i, r)),
        out_shape=jax.ShapeDtypeStruct((B, L3, d3 * W), BF16),
        compiler_params=_params("parallel", "parallel", "parallel"),
        name="dil_combine",
    )(o1, o2, o3, l1, l2, l3)
    return out.reshape(B, S, W)


def _dsa_select_kernel(qi_ref, wi_ref, ki_ref, mask_ref, key_ref, jsel_ref, *, tq, tk, S, n_sel):
    i = pl.program_id(1)
    q0 = i * tq
    n_chunks = (q0 + tq + tk - 1) // tk
    dn = (((1,), (1,)), ((), ()))
    rowg = q0 + lax.broadcasted_iota(jnp.int32, (tq, tk), 0)
    colb = lax.broadcasted_iota(jnp.int32, (tq, tk), 1)

    wi = wi_ref[...] * (IDX_HEADS ** -0.5 * IDX_DIM ** -0.5)

    def idx_body(c, carry):
        off = pl.multiple_of(c * tk, tk)
        kic = ki_ref[pl.ds(off, tk), :].astype(BF16)
        acc = jnp.zeros((tq, tk), F32)
        for h in range(IDX_HEADS):
            d = lax.dot_general(qi_ref[:, h * LANE:(h + 1) * LANE], kic, dn,
                                preferred_element_type=F32)
            acc = acc + wi[:, IDX_DIM + h:IDX_DIM + h + 1] * jnp.maximum(d, 0.0)
        score = jnp.where(off + colb <= rowg, acc, -jnp.inf)
        bits = pltpu.bitcast(score, jnp.int32)
        key_ref[:, pl.ds(off, tk)] = bits ^ ((bits >> 31) & jnp.int32(0x7FFFFFFF))
        return carry

    lax.fori_loop(0, n_chunks, idx_body, 0)

    nb = tq // LANE
    blocks = [slice(rb * LANE, (rb + 1) * LANE) for rb in range(nb)]
    colk = lax.broadcasted_iota(jnp.int32, (LANE, tk), 1)
    rowk = [q0 + rb * LANE + lax.broadcasted_iota(jnp.int32, (LANE, tk), 0) for rb in range(nb)]

    def count(pred):
        out = []
        for rb in range(nb):
            def body(c, acc, rb=rb):
                off = pl.multiple_of(c * tk, tk)
                ones = jnp.where(pred(key_ref[blocks[rb], pl.ds(off, tk)], off, rb), 1.0, 0.0)
                for w in range(tk // LANE):
                    acc = acc + ones[:, w * LANE:(w + 1) * LANE]
                return acc

            acc = lax.fori_loop(0, n_chunks, body, jnp.zeros((LANE, LANE), F32))
            out.append(jnp.sum(acc, axis=-1, keepdims=True))
        return out

    k_sel = float(n_sel)
    lo = tuple(jnp.where(c >= k_sel, jnp.int32(0), jnp.int32(-2 ** 31))
               for c in count(lambda k, off, rb: k >= 0))

    def bis_body(it, lo):
        cand = [l + (jnp.int32(1) << (30 - it)) for l in lo]
        cnt = count(lambda k, off, rb: k >= cand[rb])
        return tuple(jnp.where(c >= k_sel, cd, l) for c, cd, l in zip(cnt, cand, lo))

    thr = lax.fori_loop(0, 31, bis_body, lo)
    n_ge = count(lambda k, off, rb: k >= thr[rb])
    need = [k_sel - c for c in count(lambda k, off, rb: k > thr[rb])]
    n_bits = max(1, (S - 1).bit_length())
    jsel_ref[...] = jnp.full_like(jsel_ref, S)
    most = functools.reduce(jnp.maximum, [jnp.max(c) for c in n_ge])

    @pl.when(most > k_sel)
    def _():
        def tie_body(it, jlo):
            cand = [j + (jnp.int32(1) << (n_bits - 1 - it)) for j in jlo]
            below = count(lambda k, off, rb: jnp.logical_and(k == thr[rb], off + colk < cand[rb]))
            return tuple(jnp.where(b < n, cd, j) for b, n, cd, j in zip(below, need, cand, jlo))

        jsel = lax.fori_loop(0, n_bits, tie_body,
                             tuple(jnp.zeros((LANE, 1), jnp.int32) for _ in range(nb)))
        for rb in range(nb):
            jsel_ref[blocks[rb], :] = jnp.broadcast_to(jsel[rb], (LANE, LANE))

    mask_ref[...] = jnp.full_like(mask_ref, -jnp.inf)

    def mask_body(c, carry):
        off = pl.multiple_of(c * tk, tk)
        col = off + colk
        for rb in range(nb):
            k = key_ref[blocks[rb], pl.ds(off, tk)]
            tie = jnp.logical_and(k == thr[rb], col <= jsel_ref[blocks[rb], :1])
            sel = jnp.logical_and(jnp.logical_or(k > thr[rb], tie), col <= rowk[rb])
            mask_ref[blocks[rb], pl.ds(off, tk)] = jnp.where(sel, 0.0, -jnp.inf).astype(mask_ref.dtype)
        return carry

    lax.fori_loop(0, n_chunks, mask_body, 0)


def _dsa_select(pm, aux, B, S, n_sel, tq=256, tk=512):
    tq, tk = _tile(S, tq), _tile(S, tk)
    return pl.pallas_call(
        functools.partial(_dsa_select_kernel, tq=tq, tk=tk, S=S, n_sel=n_sel),
        grid=(B, S // tq),
        in_specs=[pl.BlockSpec((None, tq, MIX_WIDTH), lambda b, i: (b, i, 1)),
                  pl.BlockSpec((None, tq, LANE), lambda b, i: (b, i, 0)),
                  pl.BlockSpec((None, S, LANE), lambda b, i: (b, 0, 0))],
        out_specs=pl.BlockSpec((None, tq, S), lambda b, i: (b, i, 0)),
        out_shape=jax.ShapeDtypeStruct((B, S, S), BF16),
        scratch_shapes=[pltpu.VMEM((tq, S), jnp.int32), pltpu.VMEM((tq, LANE), jnp.int32)],
        compiler_params=_params("parallel", "arbitrary"),
        name="dsa_select",
    )(pm, aux, aux)


def _dsa_attn_kernel(q_ref, k_ref, v_ref, mask_ref, gt_ref, o_ref, m_ref, acc_ref, *, tq, tk, S):
    i = pl.program_id(2)
    q0 = i * tq
    n_chunks = (q0 + tq + tk - 1) // tk
    R = q_ref.shape[1] // LANE
    nblk = tq // LANE
    dn = (((1,), (1,)), ((), ()))
    qs = [q_ref[:, r * LANE:(r + 1) * LANE] for r in range(R)]
    m_ref[...] = jnp.full_like(m_ref, NEG_BIG)
    acc_ref[...] = jnp.zeros_like(acc_ref)

    def body(c, carry):
        off = pl.multiple_of(c * tk, tk)
        kc = k_ref[pl.ds(off, tk), :]
        vc = v_ref[pl.ds(off, tk), :]
        mask = mask_ref[:, pl.ds(off, tk)].astype(F32)
        st = S - q0 + off
        for r in range(R):
            bias = jnp.concatenate(
                [gt_ref[r, :, pl.ds(pl.multiple_of(st + (nblk - 1 - a) * LANE, LANE), tk)]
                 for a in range(nblk)], axis=0)
            s = lax.dot_general(qs[r], kc, dn, preferred_element_type=F32) + bias + mask
            _softmax_step(s, vc, m_ref, acc_ref, r)
        return carry

    lax.fori_loop(0, n_chunks, body, 0)
    for r in range(R):
        acc = acc_ref[r]
        o_ref[:, r * LANE:(r + 1) * LANE] = (acc[:, :LANE] / acc[:, LANE:]).astype(o_ref.dtype)


def _dsa_attention(pm, mask, gt, B, S, tq=256, tk=512):
    tq, tk = _tile(S, tq), _tile(S, tk)
    R = N_HEADS // DSA_KV_HEADS
    kblk = 2 * MIX_WIDTH // LANE
    return pl.pallas_call(
        functools.partial(_dsa_attn_kernel, tq=tq, tk=tk, S=S),
        grid=(DSA_KV_HEADS, B, S // tq),
        in_specs=[pl.BlockSpec((None, tq, R * LANE), lambda g, b, i: (b, i, g)),
                  pl.BlockSpec((None, S, LANE), lambda g, b, i: (b, 0, kblk + g)),
                  pl.BlockSpec((None, S, LANE), lambda g, b, i: (b, 0, kblk + DSA_KV_HEADS + g)),
                  pl.BlockSpec((None, tq, S), lambda g, b, i: (b, i, 0)),
                  pl.BlockSpec((R, LANE, gt.shape[2]), lambda g, b, i: (g, 0, 0))],
        out_specs=pl.BlockSpec((None, tq, R * LANE), lambda g, b, i: (b, i, g)),
        out_shape=jax.ShapeDtypeStruct((B, S, MIX_WIDTH), BF16),
        scratch_shapes=[pltpu.VMEM((R, tq, LANE), F32), pltpu.VMEM((R, tq, 2 * HEAD_DIM), F32)],
        compiler_params=_params("parallel", "parallel", "arbitrary"),
        name="dsa_attn",
    )(pm, pm, pm, mask, gt)


def _t5_bucket(dist):
    n = jnp.maximum(dist, 0)
    exact = REL_BUCKETS // 2
    nf = jnp.maximum(n, 1).astype(F32)
    large = exact + (jnp.log(nf / exact) / math.log(REL_MAX_DIST / exact)
                     * (REL_BUCKETS - exact)).astype(jnp.int32)
    large = jnp.minimum(large, REL_BUCKETS - 1)
    return jnp.where(n < exact, n, large)


def _dil_bias(t5_table, dil):
    n = DIL_BLOCK
    rel = jnp.arange(n)[:, None] + n - jnp.arange(2 * n)[None, :]
    return jnp.moveaxis(t5_table[_t5_bucket(rel * dil)], -1, 0).astype(F32)


def _dsa_bias_table(t5_table, S, tq, tk):
    m = jnp.arange(S + tq + tk)[None, :]
    i = jnp.arange(LANE)[:, None]
    table = t5_table[_t5_bucket(S + tq - LANE - m + i)]
    return jnp.moveaxis(table, -1, 0).astype(F32) * LOG2E


def _headnorm_cols(spec):
    eg, ef = [], []
    for gain, count, scale in spec:
        if gain is None:
            eg.append(jnp.ones((count * LANE,), F32))
            ef.append(jnp.zeros((count * LANE,), F32))
        else:
            eg.append(jnp.tile(gain.astype(F32) * scale, count))
            ef.append(jnp.ones((count * LANE,), F32))
    return jnp.concatenate(eg), jnp.concatenate(ef)


def _pad_cols(w, width):
    return jnp.pad(w, ((0, 0), (0, width - w.shape[1])))


def _fox_layer(x, norm_g, w_in, b_f, qk_g, mem_g):
    B, S, _ = x.shape
    W = MIX_WIDTH
    w_main = jnp.concatenate([w_in[:, :3 * W], w_in[:, 3 * W + N_HEADS:]], axis=1).astype(BF16)
    w_gate = _pad_cols(w_in[:, 3 * W:3 * W + N_HEADS], LANE).astype(BF16)
    eg, ef = _headnorm_cols([(qk_g[0], N_HEADS, HEAD_DIM ** -0.5 * LOG2E), (qk_g[1], N_HEADS, 1.0),
                             (None, N_HEADS, 1.0), (mem_g, MEM_HEADS, HEAD_DIM ** -0.5)])
    pm = _proj(x, norm_g, w_main, eg, ef)
    fg = _proj(x, norm_g, w_gate, out_dtype=F32)
    cum_t = _fox_gates(fg, b_f)
    mix = _flash([(pm, 0, True)], [(pm, W, True)], (pm, 2 * W, True), B, S,
                 decay_t=cum_t.reshape(B, LANE, 1, S))
    return mix, pm, 3 * W // MEM_WIDTH


def _mla_layer(x, positions, norm_g, w_in, q_norm, w_uq, kv_norm, w_ukv, nope_g, rope_g, mem_g):
    B, S, _ = x.shape
    scale = (NOPE_DIM + ROPE_DIM) ** -0.5 * LOG2E
    lat = Q_LORA + KV_LORA
    w_lat = jnp.concatenate([w_in[:, :lat], _pad_cols(w_in[:, lat:lat + ROPE_DIM], LANE)],
                            axis=1).astype(BF16)
    w_mem = w_in[:, lat + ROPE_DIM:].astype(BF16)
    pl_ = _proj(x, norm_g, w_lat, out_dtype=F32, tn=lat + LANE)
    eg, ef = _headnorm_cols([(mem_g, MEM_HEADS, HEAD_DIM ** -0.5)])
    pmem = _proj(x, norm_g, w_mem, eg, ef)

    uq = w_uq.reshape(Q_LORA, N_HEADS, NOPE_DIM + ROPE_DIM)
    w_qn = uq[:, :, :NOPE_DIM].reshape(Q_LORA, MIX_WIDTH).astype(BF16)
    w_qr = jnp.pad(uq[:, :, NOPE_DIM:], ((0, 0), (0, 0), (0, LANE - ROPE_DIM))
                   ).reshape(Q_LORA, N_HEADS * LANE).astype(BF16)
    ukv = w_ukv.reshape(KV_LORA, N_HEADS, NOPE_DIM + HEAD_DIM)
    w_kv = jnp.concatenate([ukv[:, :, :NOPE_DIM].reshape(KV_LORA, MIX_WIDTH),
                            ukv[:, :, NOPE_DIM:].reshape(KV_LORA, MIX_WIDTH)], axis=1).astype(BF16)
    eg, ef = _headnorm_cols([(nope_g[0], N_HEADS, scale)])
    qn = _proj(pl_, q_norm, w_qn, eg, ef, kblock=0)
    qr_raw = _proj(pl_, q_norm, w_qr, kblock=0, out_dtype=F32)
    eg, ef = _headnorm_cols([(nope_g[1], N_HEADS, 1.0), (None, N_HEADS, 1.0)])
    kv = _proj(pl_, kv_norm, w_kv, eg, ef, kblock=1)

    half = ROPE_DIM // 2
    inv = ROPE_THETA ** (-jnp.arange(half, dtype=F32) / half)
    ang = positions.astype(F32)[..., None] * inv
    cos, sin = jnp.cos(ang), jnp.sin(ang)
    zero = jnp.zeros((B, S, LANE - ROPE_DIM), F32)
    cos_t = jnp.concatenate([cos, cos, zero], axis=-1)
    sin_t = jnp.concatenate([-sin, sin, zero], axis=-1)
    qr = _rope(qr_raw, cos_t, sin_t, rope_g[0] * scale)
    kr = _rope(pl_[:, :, lat:], cos_t, sin_t, rope_g[1])
    mix = _flash([(qn, 0, True), (qr, 0, True)], [(kv, 0, True), (kr, 0, False)],
                 (kv, MIX_WIDTH, True), B, S)
    return mix, pmem, 0


def _dil_layer(x, norm_g, w_in, qk_g, t5_table, mem_g):
    B, S, _ = x.shape
    W = MIX_WIDTH
    outs, lses = [], []
    for gi, (win, dil) in enumerate(DIL_GROUPS):
        assert win // dil == DIL_BLOCK and (S // dil) % DIL_BLOCK == 0
        w_g = w_in[:, gi * 3 * W:(gi + 1) * 3 * W].astype(BF16)
        eg, ef = _headnorm_cols([(qk_g[gi, 0], N_HEADS, HEAD_DIM ** -0.5),
                                 (qk_g[gi, 1], N_HEADS, 1.0), (None, N_HEADS, 1.0)])
        tm = min(1024, S // dil)
        pg = _proj(x, norm_g, w_g, eg, ef, dil=dil, tm=tm)
        o, lse = _dil_attention(pg.reshape(B * dil, S // dil, 3 * W), _dil_bias(t5_table, dil))
        outs.append(o)
        lses.append(lse)
    mix = _dil_combine(outs, lses, B, S)
    eg, ef = _headnorm_cols([(mem_g, MEM_HEADS, HEAD_DIM ** -0.5)])
    pmem = _proj(x, norm_g, w_in[:, len(DIL_GROUPS) * 3 * W:].astype(BF16), eg, ef)
    return mix, pmem, 0


def _dsa_layer(x, norm_g, w_in, qk_g, t5_table, mem_g):
    B, S, _ = x.shape
    W = MIX_WIDTH
    kvw = DSA_KV_HEADS * HEAD_DIM
    o_k, o_v, o_qi = W, W + kvw, W + 2 * kvw
    o_ki = o_qi + IDX_HEADS * IDX_DIM
    o_wi = o_ki + IDX_DIM
    o_mem = o_wi + IDX_HEADS
    w_qi = jnp.pad(w_in[:, o_qi:o_ki].reshape(D_MODEL, IDX_HEADS, IDX_DIM),
                   ((0, 0), (0, 0), (0, LANE - IDX_DIM))).reshape(D_MODEL, IDX_HEADS * LANE)
    w_main = jnp.concatenate([w_in[:, :W], w_qi, w_in[:, o_k:o_qi], w_in[:, o_mem:]],
                             axis=1).astype(BF16)
    w_aux = _pad_cols(w_in[:, o_ki:o_mem], LANE).astype(BF16)
    eg, ef = _headnorm_cols([(qk_g[0], N_HEADS, HEAD_DIM ** -0.5 * LOG2E), (None, IDX_HEADS, 1.0),
                             (qk_g[1], DSA_KV_HEADS, 1.0), (None, DSA_KV_HEADS, 1.0),
                             (mem_g, MEM_HEADS, HEAD_DIM ** -0.5)])
    pm = _proj(x, norm_g, w_main, eg, ef)
    aux = _proj(x, norm_g, w_aux, out_dtype=F32)
    mask = _dsa_select(pm, aux, B, S, min(TOPK_MAX, S // 4))
    tq, tk = _tile(S, 256), _tile(S, 512)
    mix = _dsa_attention(pm, mask, _dsa_bias_table(t5_table, S, tq, tk), B, S, tq, tk)
    return mix, pm, (2 * W + 2 * kvw) // MEM_WIDTH


def kernel(x, mem, positions, t5_table, ffn_norm, ffn_w_gate, ffn_w_up, ffn_w_down, attn_norm,
           mem_norm, mem_w_kv, mem_qk_g, w_out, a_w_in, a_b_f, a_qk_g, b_w_in, b_q_norm, b_w_uq,
           b_kv_norm, b_w_ukv, b_nope_g, b_rope_g, c_w_in, c_qk_g, d_w_in, d_qk_g):
    B, S, D = x.shape
    depth = ffn_norm.shape[0]
    n_mixers = 4

    def ffn(xc, i, k):
        return _ffn(xc.reshape(B * S, D), ffn_norm[i, k], ffn_w_gate[i, k].astype(BF16),
                    ffn_w_up[i, k].astype(BF16), ffn_w_down[i, k].astype(BF16)).reshape(B, S, D)

    for i in range(depth):
        m, j = i % n_mixers, i // n_mixers
        x = ffn(x, i, 0)
        mem_g = mem_qk_g[i]
        if m == 0:
            mix, qarr, qblock = _fox_layer(x, attn_norm[i], a_w_in[j], a_b_f[j], a_qk_g[j], mem_g[0])
        elif m == 1:
            mix, qarr, qblock = _mla_layer(x, positions, attn_norm[i], b_w_in[j], b_q_norm[j],
                                           b_w_uq[j], b_kv_norm[j], b_w_ukv[j], b_nope_g[j],
                                           b_rope_g[j], mem_g[0])
        elif m == 2:
            mix, qarr, qblock = _dil_layer(x, attn_norm[i], c_w_in[j], c_qk_g[j], t5_table, mem_g[0])
        else:
            mix, qarr, qblock = _dsa_layer(x, attn_norm[i], d_w_in[j], d_qk_g[j], t5_table, mem_g[0])
        eg, ef = _headnorm_cols([(mem_g[1], MEM_HEADS, 1.0), (None, MEM_HEADS, 1.0)])
        mem_kv = _proj(mem, mem_norm[i], mem_w_kv[i].astype(BF16), eg, ef)
        mo = _mem_attention(qarr, qblock, mem_kv)
        wo = w_out[i].astype(BF16)
        x = _out_proj(x.reshape(B * S, D), mix.reshape(B * S, MIX_WIDTH),
                      mo.reshape(B * S, MEM_WIDTH), wo[:MIX_WIDTH], wo[MIX_WIDTH:]).reshape(B, S, D)
        x = ffn(x, i, 1)
    return x
```

```python
import functools
import math

import jax
import jax.numpy as jnp
from jax import lax
from jax.experimental import pallas as pl
from jax.experimental.pallas import tpu as pltpu

F32 = jnp.float32
BF16 = jnp.bfloat16

LANE = 128
D_MODEL = 2048
N_HEADS = 16
HEAD_DIM = 128
MIX_WIDTH = N_HEADS * HEAD_DIM
MEM_HEADS = 4
MEM_WIDTH = MEM_HEADS * HEAD_DIM
D_FF = 5632
RMS_EPS = 1e-6
REL_BUCKETS = 32
REL_MAX_DIST = 2048
Q_LORA = 512
KV_LORA = 512
NOPE_DIM = 128
ROPE_DIM = 64
ROPE_THETA = 10000.0
DIL_GROUPS = ((128, 1), (512, 4), (2048, 16))
DIL_BLOCK = 128
DSA_KV_HEADS = 4
IDX_HEADS = 16
IDX_DIM = 64
TOPK_MAX = 256
VMEM_LIMIT = 56 * 1024 * 1024
NEG_BIG = -1e30
LOG2E = math.log2(math.e)


def _params(*sem):
    return pltpu.CompilerParams(dimension_semantics=sem, vmem_limit_bytes=VMEM_LIMIT)


def _tile(n, pref):
    t = min(n, pref)
    while n % t:
        t //= 2
    return t


def _ffn_kernel(x_ref, g_ref, wg_ref, wu_ref, wd_ref, o_ref, h_ref, acc_ref):
    j = pl.program_id(1)

    @pl.when(j == 0)
    def _():
        x = x_ref[...]
        ms = jnp.mean(x * x, axis=-1, keepdims=True)
        h_ref[...] = (x * lax.rsqrt(ms + RMS_EPS) * g_ref[...]).astype(BF16)
        acc_ref[...] = jnp.zeros_like(acc_ref)

    h = h_ref[...]
    g = jnp.dot(h, wg_ref[...], preferred_element_type=F32)
    u = jnp.dot(h, wu_ref[...], preferred_element_type=F32)
    a = (g * jax.nn.sigmoid(g) * u).astype(BF16)
    acc_ref[...] += jnp.dot(a, wd_ref[...], preferred_element_type=F32)

    @pl.when(j == pl.num_programs(1) - 1)
    def _():
        o_ref[...] = x_ref[...] + 0.5 * acc_ref[...]


def _ffn(x2, gain, wg, wu, wd):
    T, D = x2.shape
    F = wg.shape[1]
    tm, tf = _tile(T, 512), _tile(F, 512)
    return pl.pallas_call(
        _ffn_kernel,
        grid=(T // tm, F // tf),
        in_specs=[
            pl.BlockSpec((tm, D), lambda i, j: (i, 0)),
            pl.BlockSpec((1, D), lambda i, j: (0, 0)),
            pl.BlockSpec((D, tf), lambda i, j: (0, j)),
            pl.BlockSpec((D, tf), lambda i, j: (0, j)),
            pl.BlockSpec((tf, D), lambda i, j: (j, 0)),
        ],
        out_specs=pl.BlockSpec((tm, D), lambda i, j: (i, 0)),
        out_shape=jax.ShapeDtypeStruct((T, D), F32),
        scratch_shapes=[pltpu.VMEM((tm, D), BF16), pltpu.VMEM((tm, D), F32)],
        compiler_params=_params("parallel", "arbitrary"),
        name="ffn",
    )(x2, gain.reshape(1, D), wg, wu, wd)


def _proj_kernel(x_ref, g_ref, w_ref, eg_ref, ef_ref, o_ref, h_ref, *stage, epilogue, dil):
    @pl.when(pl.program_id(2) == 0)
    def _():
        x = x_ref[...].astype(F32)
        ms = jnp.mean(x * x, axis=-1, keepdims=True)
        h_ref[...] = (x * lax.rsqrt(ms + RMS_EPS) * g_ref[...]).astype(BF16)

    acc = jnp.dot(h_ref[...], w_ref[...], preferred_element_type=F32)
    n = o_ref.shape[1]
    for c in range(acc.shape[1] // LANE):
        sl = slice(c * LANE, (c + 1) * LANE)
        y = acc[:, sl]
        if epilogue:
            ms = jnp.mean(y * y, axis=-1, keepdims=True)
            scale = jnp.where(ef_ref[:, sl] > 0.0, lax.rsqrt(ms + RMS_EPS), 1.0)
            y = y * scale * eg_ref[:, sl]
        if dil == 1:
            o_ref[0, :, sl] = y.astype(o_ref.dtype)
        else:
            stage[0][c] = y
            for r in range(dil):
                o_ref[r, :, sl] = stage[0][c, pl.ds(r, n, stride=dil), :].astype(o_ref.dtype)


def _proj(x, gain, w, eg=None, ef=None, *, dil=1, kblock=0, out_dtype=BF16, tm=1024, tn=512):
    B, S, C = x.shape
    K, N = w.shape
    tm, tn = _tile(S, tm), _tile(N, tn)
    epilogue = eg is not None
    if not epilogue:
        eg = jnp.ones((1, N), F32)
        ef = jnp.zeros((1, N), F32)
    out = pl.pallas_call(
        functools.partial(_proj_kernel, epilogue=epilogue, dil=dil),
        grid=(B, S // tm, N // tn),
        in_specs=[
            pl.BlockSpec((None, tm, K), lambda b, i, j: (b, i, kblock)),
            pl.BlockSpec((1, K), lambda b, i, j: (0, 0)),
            pl.BlockSpec((K, tn), lambda b, i, j: (0, j)),
            pl.BlockSpec((1, tn), lambda b, i, j: (0, j)),
            pl.BlockSpec((1, tn), lambda b, i, j: (0, j)),
        ],
        out_specs=pl.BlockSpec((None, dil, tm // dil, tn), lambda b, i, j: (b, 0, i, j)),
        out_shape=jax.ShapeDtypeStruct((B, dil, S // dil, N), out_dtype),
        scratch_shapes=[pltpu.VMEM((tm, K), BF16)]
        + ([pltpu.VMEM((tn // LANE, tm, LANE), F32)] if dil > 1 else []),
        compiler_params=_params("parallel", "parallel", "arbitrary"),
        name="proj",
    )(x, gain.reshape(1, K).astype(F32), w, eg.reshape(1, N), ef.reshape(1, N))
    return out.reshape(B, S, N) if dil == 1 else out


def _out_kernel(x_ref, a_ref, b_ref, wa_ref, wb_ref, o_ref):
    acc = jnp.dot(a_ref[...], wa_ref[...], preferred_element_type=F32)
    acc += jnp.dot(b_ref[...], wb_ref[...], preferred_element_type=F32)
    o_ref[...] = x_ref[...] + acc


def _out_proj(x2, mix2, mo2, wa, wb):
    T, D = x2.shape
    tm, tn = _tile(T, 1024), _tile(D, 512)
    ka, kb = mix2.shape[1], mo2.shape[1]
    return pl.pallas_call(
        _out_kernel,
        grid=(T // tm, D // tn),
        in_specs=[
            pl.BlockSpec((tm, tn), lambda i, j: (i, j)),
            pl.BlockSpec((tm, ka), lambda i, j: (i, 0)),
            pl.BlockSpec((tm, kb), lambda i, j: (i, 0)),
            pl.BlockSpec((ka, tn), lambda i, j: (0, j)),
            pl.BlockSpec((kb, tn), lambda i, j: (0, j)),
        ],
        out_specs=pl.BlockSpec((tm, tn), lambda i, j: (i, j)),
        out_shape=jax.ShapeDtypeStruct((T, D), F32),
        compiler_params=_params("parallel", "arbitrary"),
        name="out_proj",
    )(x2, mix2, mo2, wa, wb)


def _lanes(x, reps):
    return x if reps == 1 else jnp.concatenate([x] * reps, axis=-1)


def _softmax_steps(scores, values, m_ref, acc_ref):
    ps, alphas = [], []
    for g, s in enumerate(scores):
        m_old = m_ref[g]
        m_new = jnp.maximum(m_old, jnp.max(s, axis=-1, keepdims=True))
        ps.append(jnp.exp2(s - _lanes(m_new, s.shape[1] // LANE)).astype(BF16))
        alphas.append(jnp.exp2(m_old - m_new))
        m_ref[g] = m_new
    for g, (p, alpha, v) in enumerate(zip(ps, alphas, values)):
        v1 = jnp.concatenate([v, jnp.ones_like(v)], axis=-1)
        acc_ref[g] = _lanes(alpha, 2) * acc_ref[g] + jnp.dot(p, v1, preferred_element_type=F32)


def _flash_kernel(*refs, nqk, decay, tq, G):
    q_refs, k_refs, v_ref = refs[:nqk], refs[nqk:2 * nqk], refs[2 * nqk]
    pos = 2 * nqk + 1
    if decay:
        ck_ref = refs[pos]
        pos += 1
    o_ref, m_ref, acc_ref = refs[pos:pos + 3]
    i = pl.program_id(2)

    def head(ref, g, rows=slice(None)):
        if ref.shape[-1] == LANE:
            return ref[rows, :]
        return ref[rows, g * LANE:(g + 1) * LANE]

    qs = []
    for g in range(G):
        parts = [head(r, g) for r in q_refs]
        qs.append(parts[0] if nqk == 1 else jnp.concatenate(parts, axis=-1))
    m_ref[...] = jnp.full_like(m_ref, NEG_BIG)
    acc_ref[...] = jnp.zeros_like(acc_ref)

    def step(j, diagonal):
        off = pl.multiple_of(j * tq, tq)
        rows = pl.ds(off, tq)
        if diagonal:
            row = lax.broadcasted_iota(jnp.int32, (tq, tq), 0)
            col = lax.broadcasted_iota(jnp.int32, (tq, tq), 1)
            visible = col <= row
        scores = []
        for g in range(G):
            parts = [head(r, g, rows) for r in k_refs]
            k = parts[0] if nqk == 1 else jnp.concatenate(parts, axis=-1)
            s = lax.dot_general(qs[g], k, (((1,), (1,)), ((), ())), preferred_element_type=F32)
            if decay:
                s = s - ck_ref[g, :, rows]
            if diagonal:
                s = jnp.where(visible, s, -jnp.inf)
            scores.append(s)
        _softmax_steps(scores, [head(v_ref, g, rows) for g in range(G)], m_ref, acc_ref)

    def body(j, carry):
        step(j, False)
        return carry

    lax.fori_loop(0, i, body, 0)
    step(i, True)
    for g in range(G):
        acc = acc_ref[g]
        o_ref[:, g * LANE:(g + 1) * LANE] = (acc[:, :LANE] / acc[:, LANE:]).astype(o_ref.dtype)


def _flash(q_parts, k_parts, v_part, B, S, decay_t=None, tq=512, G=4):
    tq = _tile(S, tq)
    nqk = len(q_parts)
    W = G * LANE
    args, specs = [], []
    for arr, off, _ in q_parts:
        args.append(arr)
        specs.append(pl.BlockSpec((None, tq, W), lambda b, h, i, off=off: (b, i, off // W + h)))
    for arr, off, per_head in k_parts + [v_part]:
        args.append(arr)
        if per_head:
            specs.append(pl.BlockSpec((None, S, W), lambda b, h, i, off=off: (b, 0, off // W + h)))
        else:
            specs.append(pl.BlockSpec((None, S, LANE), lambda b, h, i, off=off: (b, 0, off // LANE)))
    if decay_t is not None:
        args.append(decay_t)
        specs.append(pl.BlockSpec((None, G, 1, S), lambda b, h, i: (b, h, 0, 0)))
    return pl.pallas_call(
        functools.partial(_flash_kernel, nqk=nqk, decay=decay_t is not None, tq=tq, G=G),
        grid=(B, N_HEADS // G, S // tq),
        in_specs=specs,
        out_specs=pl.BlockSpec((None, tq, W), lambda b, h, i: (b, i, h)),
        out_shape=jax.ShapeDtypeStruct((B, S, MIX_WIDTH), BF16),
        scratch_shapes=[pltpu.VMEM((G, tq, LANE), F32), pltpu.VMEM((G, tq, 2 * HEAD_DIM), F32)],
        compiler_params=_params("parallel", "parallel", "arbitrary"),
        name="flash",
    )(*args)


def _gate_kernel(fg_ref, bf_ref, cumt_ref, carry_ref, *, ts):
    @pl.when(pl.program_id(1) == 0)
    def _():
        carry_ref[...] = jnp.zeros_like(carry_ref)

    z = fg_ref[...] + bf_ref[...]
    lf = jnp.minimum(z, 0.0) - jnp.log1p(jnp.exp(-jnp.abs(z)))
    hi = lf.astype(BF16)
    r1 = lf - hi.astype(F32)
    mid = r1.astype(BF16)
    lo = (r1 - mid.astype(F32)).astype(BF16)
    row = lax.broadcasted_iota(jnp.int32, (ts, ts), 0)
    col = lax.broadcasted_iota(jnp.int32, (ts, ts), 1)
    tri = jnp.where(col <= row, 1.0, 0.0).astype(BF16)
    cum = (jnp.dot(tri, hi, preferred_element_type=F32)
           + jnp.dot(tri, mid, preferred_element_type=F32)
           + jnp.dot(tri, lo, preferred_element_type=F32)) + carry_ref[...]
    carry_ref[...] = cum[ts - 1:ts, :]
    cumt_ref[...] = (cum * LOG2E).T


def _fox_gates(fg, b_f):
    B, S, _ = fg.shape
    ts = _tile(S, 512)
    bf = jnp.zeros((1, LANE), F32).at[0, :N_HEADS].set(b_f.astype(F32))
    return pl.pallas_call(
        functools.partial(_gate_kernel, ts=ts),
        grid=(B, S // ts),
        in_specs=[pl.BlockSpec((None, ts, LANE), lambda b, i: (b, i, 0)),
                  pl.BlockSpec((1, LANE), lambda b, i: (0, 0))],
        out_specs=pl.BlockSpec((None, LANE, ts), lambda b, i: (b, 0, i)),
        out_shape=jax.ShapeDtypeStruct((B, LANE, S), F32),
        scratch_shapes=[pltpu.VMEM((1, LANE), F32)],
        compiler_params=_params("parallel", "arbitrary"),
        name="fox_gates",
    )(fg, bf)


def _rope_kernel(x_ref, cos_ref, sin_ref, g_ref, o_ref):
    cos, sin, g = cos_ref[...], sin_ref[...], g_ref[...]
    for c in range(x_ref.shape[1] // LANE):
        sl = slice(c * LANE, (c + 1) * LANE)
        x = x_ref[:, sl]
        ms = jnp.sum(x * x, axis=-1, keepdims=True) * (1.0 / ROPE_DIM)
        y = x * lax.rsqrt(ms + RMS_EPS) * g
        partner = pltpu.roll(y, ROPE_DIM // 2, 1) + pltpu.roll(y, LANE - ROPE_DIM // 2, 1)
        o_ref[:, sl] = (y * cos + partner * sin).astype(o_ref.dtype)


def _rope(x, cos, sin, gain):
    B, S, N = x.shape
    ts = _tile(S, 512)
    g = jnp.zeros((1, LANE), F32).at[0, :ROPE_DIM].set(gain.astype(F32))
    return pl.pallas_call(
        _rope_kernel,
        grid=(B, S // ts),
        in_specs=[pl.BlockSpec((None, ts, N), lambda b, i: (b, i, 0)),
                  pl.BlockSpec((None, ts, LANE), lambda b, i: (b, i, 0)),
                  pl.BlockSpec((None, ts, LANE), lambda b, i: (b, i, 0)),
                  pl.BlockSpec((1, LANE), lambda b, i: (0, 0))],
        out_specs=pl.BlockSpec((None, ts, N), lambda b, i: (b, i, 0)),
        out_shape=jax.ShapeDtypeStruct((B, S, N), BF16),
        compiler_params=_params("parallel", "parallel"),
        name="rope",
    )(x, cos, sin, g)


def _mem_kernel(q_ref, kv_ref, o_ref):
    for h in range(MEM_HEADS):
        q = q_ref[:, h * LANE:(h + 1) * LANE]
        k = kv_ref[:, h * LANE:(h + 1) * LANE]
        v = kv_ref[:, MEM_WIDTH + h * LANE:MEM_WIDTH + (h + 1) * LANE]
        s = lax.dot_general(q, k, (((1,), (1,)), ((), ())), preferred_element_type=F32)
        p = jnp.exp(s - jnp.max(s, axis=-1, keepdims=True))
        o = jnp.dot(p.astype(BF16), v, preferred_element_type=F32)
        o_ref[:, h * LANE:(h + 1) * LANE] = (o / jnp.sum(p, axis=-1, keepdims=True)).astype(o_ref.dtype)


def _mem_attention(qarr, qblock, mem_kv):
    B, S, _ = qarr.shape
    n_mem = mem_kv.shape[1]
    tq = _tile(S, 1024)
    return pl.pallas_call(
        _mem_kernel,
        grid=(B, S // tq),
        in_specs=[pl.BlockSpec((None, tq, MEM_WIDTH), lambda b, i: (b, i, qblock)),
                  pl.BlockSpec((None, n_mem, 2 * MEM_WIDTH), lambda b, i: (b, 0, 0))],
        out_specs=pl.BlockSpec((None, tq, MEM_WIDTH), lambda b, i: (b, i, 0)),
        out_shape=jax.ShapeDtypeStruct((B, S, MEM_WIDTH), BF16),
        compiler_params=_params("parallel", "parallel"),
        name="mem_attn",
    )(qarr, mem_kv)


def _dil_kernel(q_ref, kp_ref, kc_ref, vp_ref, vc_ref, bias_ref, o_ref, lse_ref):
    n = DIL_BLOCK
    has_prev = pl.program_id(1) > 0
    row = lax.broadcasted_iota(jnp.int32, (n, 2 * n), 0)
    col = lax.broadcasted_iota(jnp.int32, (n, 2 * n), 1)
    valid = jnp.logical_and(jnp.logical_and(col >= row, col <= row + n),
                            jnp.logical_or(col >= n, has_prev))
    lane = lax.broadcasted_iota(jnp.int32, (n, LANE), 1)
    dn = (((1,), (1,)), ((), ()))
    heads = [slice(h * LANE, (h + 1) * LANE) for h in range(N_HEADS)]
    scores = []
    for h, sl in enumerate(heads):
        k2 = jnp.concatenate([kp_ref[:, sl], kc_ref[:, sl]], axis=0)
        s = lax.dot_general(q_ref[:, sl], k2, dn, preferred_element_type=F32) + bias_ref[h]
        scores.append(jnp.where(valid, s, -jnp.inf))
    ms = [jnp.max(s, axis=-1, keepdims=True) for s in scores]
    ps = [jnp.exp2(s - m).astype(BF16) for s, m in zip(scores, ms)]
    lse_all = jnp.zeros((n, LANE), F32)
    for h, sl in enumerate(heads):
        v2 = jnp.concatenate([vp_ref[:, sl], vc_ref[:, sl]], axis=0)
        pv = jnp.dot(ps[h], jnp.concatenate([v2, jnp.ones_like(v2)], axis=-1),
                     preferred_element_type=F32)
        den = pv[:, LANE:]
        o_ref[h] = pv[:, :LANE] / den
        lse_all = jnp.where(lane == h, ms[h] + jnp.log2(den), lse_all)
    lse_ref[...] = lse_all


def _dil_attention(pg, bias):
    N, Ls, _ = pg.shape
    n = DIL_BLOCK
    blk = lambda part, prev: pl.BlockSpec(
        (None, n, MIX_WIDTH),
        (lambda s, i: (s, jnp.maximum(i - 1, 0), part)) if prev else (lambda s, i: (s, i, part)))
    return pl.pallas_call(
        _dil_kernel,
        grid=(N, Ls // n),
        in_specs=[blk(0, False), blk(1, True), blk(1, False), blk(2, True), blk(2, False),
                  pl.BlockSpec((N_HEADS, n, 2 * n), lambda s, i: (0, 0, 0))],
        out_specs=[pl.BlockSpec((None, N_HEADS, n, LANE), lambda s, i: (s, 0, i, 0)),
                   pl.BlockSpec((None, n, LANE), lambda s, i: (s, i, 0))],
        out_shape=[jax.ShapeDtypeStruct((N, N_HEADS, Ls, LANE), F32),
                   jax.ShapeDtypeStruct((N, Ls, LANE), F32)],
        compiler_params=_params("parallel", "arbitrary"),
        name="dil_attn",
    )(pg, pg, pg, pg, pg, bias)


def _dil_combine_kernel(o1_ref, o2_ref, o3_ref, l1_ref, l2_ref, l3_ref, out_ref, stage_ref, *, d2, d3):
    n = l3_ref.shape[1]
    q = d3 // d2
    for r in range(d3):
        rows = [(pl.ds(r, n, stride=d3),), (r % d2, pl.ds(r // d2, n, stride=q)), (r, slice(None))]
        ls = [l_ref[idx + (slice(None),)]
              for l_ref, idx in zip((l1_ref, l2_ref, l3_ref), rows)]
        m = functools.reduce(jnp.maximum, ls)
        es = [jnp.exp2(l - m) for l in ls]
        inv = 1.0 / functools.reduce(jnp.add, es)
        ws = [e * inv for e in es]
        for h in range(N_HEADS):
            o = (ws[0][:, h:h + 1] * o1_ref[h, pl.ds(r, n, stride=d3), :]
                 + ws[1][:, h:h + 1] * o2_ref[r % d2, h, pl.ds(r // d2, n, stride=q), :]
                 + ws[2][:, h:h + 1] * o3_ref[r, h, :, :])
            stage_ref[h, pl.ds(r, n, stride=d3), :] = o
    for h in range(N_HEADS):
        out_ref[:, h * LANE:(h + 1) * LANE] = stage_ref[h].astype(out_ref.dtype)


def _dil_combine(outs, lses, B, S, tm=256):
    (_, d1), (_, d2), (_, d3) = DIL_GROUPS
    assert d1 == 1 and d3 % d2 == 0
    tm = _tile(S, tm)
    H = N_HEADS
    o_specs = [pl.BlockSpec((None, H, tm, LANE), lambda b, i: (b, 0, i, 0)),
               pl.BlockSpec((None, d2, H, tm // d2, LANE), lambda b, i: (b, 0, 0, i, 0)),
               pl.BlockSpec((None, d3, H, tm // d3, LANE), lambda b, i: (b, 0, 0, i, 0))]
    l_specs = [pl.BlockSpec((None, tm, LANE), lambda b, i: (b, i, 0)),
               pl.BlockSpec((None, d2, tm // d2, LANE), lambda b, i: (b, 0, i, 0)),
               pl.BlockSpec((None, d3, tm // d3, LANE), lambda b, i: (b, 0, i, 0))]
    return pl.pallas_call(
        functools.partial(_dil_combine_kernel, d2=d2, d3=d3),
        grid=(B, S // tm),
        in_specs=o_specs + l_specs,
        out_specs=pl.BlockSpec((None, tm, MIX_WIDTH), lambda b, i: (b, i, 0)),
        out_shape=jax.ShapeDtypeStruct((B, S, MIX_WIDTH), BF16),
        scratch_shapes=[pltpu.VMEM((H, tm, LANE), F32)],
        compiler_params=_params("parallel", "parallel"),
        name="dil_combine",
    )(outs[0].reshape(B, H, S, LANE), outs[1].reshape(B, d2, H, S // d2, LANE),
      outs[2].reshape(B, d3, H, S // d3, LANE),
      lses[0].reshape(B, S, LANE), lses[1].reshape(B, d2, S // d2, LANE),
      lses[2].reshape(B, d3, S // d3, LANE))


def _dsa_select_kernel(qi_ref, wi_ref, ki_ref, mask_ref, key_ref, jsel_ref, *, tq, tk, S, n_sel):
    i = pl.program_id(1)
    q0 = i * tq
    n_chunks = (q0 + tq + tk - 1) // tk
    dn = (((1,), (1,)), ((), ()))
    rowg = q0 + lax.broadcasted_iota(jnp.int32, (tq, tk), 0)
    colb = lax.broadcasted_iota(jnp.int32, (tq, tk), 1)

    wi = wi_ref[...] * (IDX_HEADS ** -0.5 * IDX_DIM ** -0.5)

    def idx_body(c, carry):
        off = pl.multiple_of(c * tk, tk)
        kic = ki_ref[pl.ds(off, tk), :].astype(BF16)
        acc = jnp.zeros((tq, tk), F32)
        for h in range(IDX_HEADS):
            d = lax.dot_general(qi_ref[:, h * LANE:(h + 1) * LANE], kic, dn,
                                preferred_element_type=F32)
            acc = acc + wi[:, IDX_DIM + h:IDX_DIM + h + 1] * jnp.maximum(d, 0.0)
        score = jnp.where(off + colb <= rowg, acc, -jnp.inf)
        bits = pltpu.bitcast(score, jnp.int32)
        key_ref[:, pl.ds(off, tk)] = bits ^ ((bits >> 31) & jnp.int32(0x7FFFFFFF))
        return carry

    lax.fori_loop(0, n_chunks, idx_body, 0)

    nb = tq // LANE
    blocks = [slice(rb * LANE, (rb + 1) * LANE) for rb in range(nb)]
    colk = lax.broadcasted_iota(jnp.int32, (LANE, tk), 1)
    rowk = [q0 + rb * LANE + lax.broadcasted_iota(jnp.int32, (LANE, tk), 0) for rb in range(nb)]

    def count(pred):
        out = []
        for rb in range(nb):
            def body(c, acc, rb=rb):
                off = pl.multiple_of(c * tk, tk)
                ones = jnp.where(pred(key_ref[blocks[rb], pl.ds(off, tk)], off, rb), 1.0, 0.0)
                for w in range(tk // LANE):
                    acc = acc + ones[:, w * LANE:(w + 1) * LANE]
                return acc

            acc = lax.fori_loop(0, n_chunks, body, jnp.zeros((LANE, LANE), F32))
            out.append(jnp.sum(acc, axis=-1, keepdims=True))
        return out

    k_sel = float(n_sel)
    lo = tuple(jnp.where(c >= k_sel, jnp.int32(0), jnp.int32(-2 ** 31))
               for c in count(lambda k, off, rb: k >= 0))

    def bis_body(it, lo):
        cand = [l + (jnp.int32(1) << (30 - it)) for l in lo]
        cnt = count(lambda k, off, rb: k >= cand[rb])
        return tuple(jnp.where(c >= k_sel, cd, l) for c, cd, l in zip(cnt, cand, lo))

    thr = lax.fori_loop(0, 31, bis_body, lo)
    n_ge = count(lambda k, off, rb: k >= thr[rb])
    need = [k_sel - c for c in count(lambda k, off, rb: k > thr[rb])]
    n_bits = max(1, (S - 1).bit_length())
    jsel_ref[...] = jnp.full_like(jsel_ref, S)
    most = functools.reduce(jnp.maximum, [jnp.max(c) for c in n_ge])

    @pl.when(most > k_sel)
    def _():
        def tie_body(it, jlo):
            cand = [j + (jnp.int32(1) << (n_bits - 1 - it)) for j in jlo]
            below = count(lambda k, off, rb: jnp.logical_and(k == thr[rb], off + colk < cand[rb]))
            return tuple(jnp.where(b < n, cd, j) for b, n, cd, j in zip(below, need, cand, jlo))

        jsel = lax.fori_loop(0, n_bits, tie_body,
                             tuple(jnp.zeros((LANE, 1), jnp.int32) for _ in range(nb)))
        for rb in range(nb):
            jsel_ref[blocks[rb], :] = jnp.broadcast_to(jsel[rb], (LANE, LANE))

    mask_ref[...] = jnp.full_like(mask_ref, -jnp.inf)

    def mask_body(c, carry):
        off = pl.multiple_of(c * tk, tk)
        col = off + colk
        for rb in range(nb):
            k = key_ref[blocks[rb], pl.ds(off, tk)]
            tie = jnp.logical_and(k == thr[rb], col <= jsel_ref[blocks[rb], :1])
            sel = jnp.logical_and(jnp.logical_or(k > thr[rb], tie), col <= rowk[rb])
            mask_ref[blocks[rb], pl.ds(off, tk)] = jnp.where(sel, 0.0, -jnp.inf).astype(mask_ref.dtype)
        return carry

    lax.fori_loop(0, n_chunks, mask_body, 0)


def _dsa_select(pm, aux, B, S, n_sel, tq=256, tk=512):
    tq, tk = _tile(S, tq), _tile(S, tk)
    return pl.pallas_call(
        functools.partial(_dsa_select_kernel, tq=tq, tk=tk, S=S, n_sel=n_sel),
        grid=(B, S // tq),
        in_specs=[pl.BlockSpec((None, tq, MIX_WIDTH), lambda b, i: (b, i, 1)),
                  pl.BlockSpec((None, tq, LANE), lambda b, i: (b, i, 0)),
                  pl.BlockSpec((None, S, LANE), lambda b, i: (b, 0, 0))],
        out_specs=pl.BlockSpec((None, tq, S), lambda b, i: (b, i, 0)),
        out_shape=jax.ShapeDtypeStruct((B, S, S), BF16),
        scratch_shapes=[pltpu.VMEM((tq, S), jnp.int32), pltpu.VMEM((tq, LANE), jnp.int32)],
        compiler_params=_params("parallel", "arbitrary"),
        name="dsa_select",
    )(pm, aux, aux)


def _dsa_attn_kernel(q_ref, k_ref, v_ref, mask_ref, gt_ref, o_ref, m_ref, acc_ref, *, tq, tk, S):
    i = pl.program_id(2)
    q0 = i * tq
    n_chunks = (q0 + tq + tk - 1) // tk
    R = q_ref.shape[1] // LANE
    nblk = tq // LANE
    dn = (((1,), (1,)), ((), ()))
    qs = [q_ref[:, r * LANE:(r + 1) * LANE] for r in range(R)]
    m_ref[...] = jnp.full_like(m_ref, NEG_BIG)
    acc_ref[...] = jnp.zeros_like(acc_ref)

    def body(c, carry):
        off = pl.multiple_of(c * tk, tk)
        kc = k_ref[pl.ds(off, tk), :]
        vc = v_ref[pl.ds(off, tk), :]
        mask = mask_ref[:, pl.ds(off, tk)].astype(F32)
        st = S - q0 + off
        scores = []
        for r in range(R):
            bias = jnp.concatenate(
                [gt_ref[r, :, pl.ds(pl.multiple_of(st + (nblk - 1 - a) * LANE, LANE), tk)]
                 for a in range(nblk)], axis=0)
            scores.append(lax.dot_general(qs[r], kc, dn, preferred_element_type=F32) + bias + mask)
        _softmax_steps(scores, [vc] * R, m_ref, acc_ref)
        return carry

    lax.fori_loop(0, n_chunks, body, 0)
    for r in range(R):
        acc = acc_ref[r]
        o_ref[:, r * LANE:(r + 1) * LANE] = (acc[:, :LANE] / acc[:, LANE:]).astype(o_ref.dtype)


def _dsa_attention(pm, mask, gt, B, S, tq=256, tk=512):
    tq, tk = _tile(S, tq), _tile(S, tk)
    R = N_HEADS // DSA_KV_HEADS
    kblk = 2 * MIX_WIDTH // LANE
    return pl.pallas_call(
        functools.partial(_dsa_attn_kernel, tq=tq, tk=tk, S=S),
        grid=(DSA_KV_HEADS, B, S // tq),
        in_specs=[pl.BlockSpec((None, tq, R * LANE), lambda g, b, i: (b, i, g)),
                  pl.BlockSpec((None, S, LANE), lambda g, b, i: (b, 0, kblk + g)),
                  pl.BlockSpec((None, S, LANE), lambda g, b, i: (b, 0, kblk + DSA_KV_HEADS + g)),
                  pl.BlockSpec((None, tq, S), lambda g, b, i: (b, i, 0)),
                  pl.BlockSpec((R, LANE, gt.shape[2]), lambda g, b, i: (g, 0, 0))],
        out_specs=pl.BlockSpec((None, tq, R * LANE), lambda g, b, i: (b, i, g)),
        out_shape=jax.ShapeDtypeStruct((B, S, MIX_WIDTH), BF16),
        scratch_shapes=[pltpu.VMEM((R, tq, LANE), F32), pltpu.VMEM((R, tq, 2 * HEAD_DIM), F32)],
        compiler_params=_params("parallel", "parallel", "arbitrary"),
        name="dsa_attn",
    )(pm, pm, pm, mask, gt)


def _t5_bucket(dist):
    n = jnp.maximum(dist, 0)
    exact = REL_BUCKETS // 2
    nf = jnp.maximum(n, 1).astype(F32)
    large = exact + (jnp.log(nf / exact) / math.log(REL_MAX_DIST / exact)
                     * (REL_BUCKETS - exact)).astype(jnp.int32)
    large = jnp.minimum(large, REL_BUCKETS - 1)
    return jnp.where(n < exact, n, large)


def _t5_bias(t5_table, dist):
    onehot = jax.nn.one_hot(_t5_bucket(dist), REL_BUCKETS, dtype=F32)
    table = jnp.einsum('...b,bh->h...', onehot, t5_table.astype(F32), precision=lax.Precision.HIGHEST)
    return table * LOG2E


def _dil_bias(t5_table, dil):
    n = DIL_BLOCK
    rel = jnp.arange(n)[:, None] + n - jnp.arange(2 * n)[None, :]
    return _t5_bias(t5_table, rel * dil)


def _dsa_bias_table(t5_table, S, tq, tk):
    lt = S + tq + tk
    tb = _t5_bias(t5_table, S + tq - 1 - jnp.arange(lt + LANE))
    return jnp.stack([tb[:, LANE - 1 - i:LANE - 1 - i + lt] for i in range(LANE)], axis=1)


def _headnorm_cols(spec):
    eg, ef = [], []
    for gain, count, scale in spec:
        if gain is None:
            eg.append(jnp.ones((count * LANE,), F32))
            ef.append(jnp.zeros((count * LANE,), F32))
        else:
            eg.append(jnp.tile(gain.astype(F32) * scale, count))
            ef.append(jnp.ones((count * LANE,), F32))
    return jnp.concatenate(eg), jnp.concatenate(ef)


def _pad_cols(w, width):
    return jnp.pad(w, ((0, 0), (0, width - w.shape[1])))


def _fox_layer(x, norm_g, w_in, b_f, qk_g, mem_g):
    B, S, _ = x.shape
    W = MIX_WIDTH
    w_main = jnp.concatenate([w_in[:, :3 * W], w_in[:, 3 * W + N_HEADS:]], axis=1).astype(BF16)
    w_gate = _pad_cols(w_in[:, 3 * W:3 * W + N_HEADS], LANE).astype(BF16)
    eg, ef = _headnorm_cols([(qk_g[0], N_HEADS, HEAD_DIM ** -0.5 * LOG2E), (qk_g[1], N_HEADS, 1.0),
                             (None, N_HEADS, 1.0), (mem_g, MEM_HEADS, HEAD_DIM ** -0.5)])
    pm = _proj(x, norm_g, w_main, eg, ef)
    fg = _proj(x, norm_g, w_gate, out_dtype=F32)
    cum_t = _fox_gates(fg, b_f)
    mix = _flash([(pm, 0, True)], [(pm, W, True)], (pm, 2 * W, True), B, S,
                 decay_t=cum_t.reshape(B, LANE, 1, S))
    return mix, pm, 3 * W // MEM_WIDTH


def _mla_layer(x, positions, norm_g, w_in, q_norm, w_uq, kv_norm, w_ukv, nope_g, rope_g, mem_g):
    B, S, _ = x.shape
    scale = (NOPE_DIM + ROPE_DIM) ** -0.5 * LOG2E
    lat = Q_LORA + KV_LORA
    w_lat = jnp.concatenate([w_in[:, :lat], _pad_cols(w_in[:, lat:lat + ROPE_DIM], LANE)],
                            axis=1).astype(BF16)
    w_mem = w_in[:, lat + ROPE_DIM:].astype(BF16)
    pl_ = _proj(x, norm_g, w_lat, out_dtype=F32, tn=lat + LANE)
    eg, ef = _headnorm_cols([(mem_g, MEM_HEADS, HEAD_DIM ** -0.5)])
    pmem = _proj(x, norm_g, w_mem, eg, ef)

    uq = w_uq.reshape(Q_LORA, N_HEADS, NOPE_DIM + ROPE_DIM)
    w_qn = uq[:, :, :NOPE_DIM].reshape(Q_LORA, MIX_WIDTH).astype(BF16)
    w_qr = jnp.pad(uq[:, :, NOPE_DIM:], ((0, 0), (0, 0), (0, LANE - ROPE_DIM))
                   ).reshape(Q_LORA, N_HEADS * LANE).astype(BF16)
    ukv = w_ukv.reshape(KV_LORA, N_HEADS, NOPE_DIM + HEAD_DIM)
    w_kv = jnp.concatenate([ukv[:, :, :NOPE_DIM].reshape(KV_LORA, MIX_WIDTH),
                            ukv[:, :, NOPE_DIM:].reshape(KV_LORA, MIX_WIDTH)], axis=1).astype(BF16)
    eg, ef = _headnorm_cols([(nope_g[0], N_HEADS, scale)])
    qn = _proj(pl_, q_norm, w_qn, eg, ef, kblock=0)
    qr_raw = _proj(pl_, q_norm, w_qr, kblock=0, out_dtype=F32)
    eg, ef = _headnorm_cols([(nope_g[1], N_HEADS, 1.0), (None, N_HEADS, 1.0)])
    kv = _proj(pl_, kv_norm, w_kv, eg, ef, kblock=1)

    half = ROPE_DIM // 2
    inv = ROPE_THETA ** (-jnp.arange(half, dtype=F32) / half)
    ang = positions.astype(F32)[..., None] * inv
    cos, sin = jnp.cos(ang), jnp.sin(ang)
    zero = jnp.zeros((B, S, LANE - ROPE_DIM), F32)
    cos_t = jnp.concatenate([cos, cos, zero], axis=-1)
    sin_t = jnp.concatenate([-sin, sin, zero], axis=-1)
    qr = _rope(qr_raw, cos_t, sin_t, rope_g[0] * scale)
    kr = _rope(pl_[:, :, lat:], cos_t, sin_t, rope_g[1])
    mix = _flash([(qn, 0, True), (qr, 0, True)], [(kv, 0, True), (kr, 0, False)],
                 (kv, MIX_WIDTH, True), B, S)
    return mix, pmem, 0


def _dil_layer(x, norm_g, w_in, qk_g, t5_table, mem_g):
    B, S, _ = x.shape
    W = MIX_WIDTH
    outs, lses = [], []
    for gi, (win, dil) in enumerate(DIL_GROUPS):
        assert win // dil == DIL_BLOCK and (S // dil) % DIL_BLOCK == 0
        w_g = w_in[:, gi * 3 * W:(gi + 1) * 3 * W].astype(BF16)
        eg, ef = _headnorm_cols([(qk_g[gi, 0], N_HEADS, HEAD_DIM ** -0.5 * LOG2E),
                                 (qk_g[gi, 1], N_HEADS, 1.0), (None, N_HEADS, 1.0)])
        pg = _proj(x, norm_g, w_g, eg, ef, dil=dil)
        o, lse = _dil_attention(pg.reshape(B * dil, S // dil, 3 * W), _dil_bias(t5_table, dil))
        outs.append(o)
        lses.append(lse)
    mix = _dil_combine(outs, lses, B, S)
    eg, ef = _headnorm_cols([(mem_g, MEM_HEADS, HEAD_DIM ** -0.5)])
    pmem = _proj(x, norm_g, w_in[:, len(DIL_GROUPS) * 3 * W:].astype(BF16), eg, ef)
    return mix, pmem, 0


def _dsa_layer(x, norm_g, w_in, qk_g, t5_table, mem_g):
    B, S, _ = x.shape
    W = MIX_WIDTH
    kvw = DSA_KV_HEADS * HEAD_DIM
    o_k, o_v, o_qi = W, W + kvw, W + 2 * kvw
    o_ki = o_qi + IDX_HEADS * IDX_DIM
    o_wi = o_ki + IDX_DIM
    o_mem = o_wi + IDX_HEADS
    w_qi = jnp.pad(w_in[:, o_qi:o_ki].reshape(D_MODEL, IDX_HEADS, IDX_DIM),
                   ((0, 0), (0, 0), (0, LANE - IDX_DIM))).reshape(D_MODEL, IDX_HEADS * LANE)
    w_main = jnp.concatenate([w_in[:, :W], w_qi, w_in[:, o_k:o_qi], w_in[:, o_mem:]],
                             axis=1).astype(BF16)
    w_aux = _pad_cols(w_in[:, o_ki:o_mem], LANE).astype(BF16)
    eg, ef = _headnorm_cols([(qk_g[0], N_HEADS, HEAD_DIM ** -0.5 * LOG2E), (None, IDX_HEADS, 1.0),
                             (qk_g[1], DSA_KV_HEADS, 1.0), (None, DSA_KV_HEADS, 1.0),
                             (mem_g, MEM_HEADS, HEAD_DIM ** -0.5)])
    pm = _proj(x, norm_g, w_main, eg, ef)
    aux = _proj(x, norm_g, w_aux, out_dtype=F32)
    mask = _dsa_select(pm, aux, B, S, min(TOPK_MAX, S // 4))
    tq, tk = _tile(S, 256), _tile(S, 512)
    mix = _dsa_attention(pm, mask, _dsa_bias_table(t5_table, S, tq, tk), B, S, tq, tk)
    return mix, pm, (2 * W + 2 * kvw) // MEM_WIDTH


def kernel(x, mem, positions, t5_table, ffn_norm, ffn_w_gate, ffn_w_up, ffn_w_down, attn_norm,
           mem_norm, mem_w_kv, mem_qk_g, w_out, a_w_in, a_b_f, a_qk_g, b_w_in, b_q_norm, b_w_uq,
           b_kv_norm, b_w_ukv, b_nope_g, b_rope_g, c_w_in, c_qk_g, d_w_in, d_qk_g):
    B, S, D = x.shape
    depth = ffn_norm.shape[0]
    n_mixers = 4

    def ffn(xc, i, k):
        return _ffn(xc.reshape(B * S, D), ffn_norm[i, k], ffn_w_gate[i, k].astype(BF16),
                    ffn_w_up[i, k].astype(BF16), ffn_w_down[i, k].astype(BF16)).reshape(B, S, D)

    for i in range(depth):
        m, j = i % n_mixers, i // n_mixers
        x = ffn(x, i, 0)
        mem_g = mem_qk_g[i]
        if m == 0:
            mix, qarr, qblock = _fox_layer(x, attn_norm[i], a_w_in[j], a_b_f[j], a_qk_g[j], mem_g[0])
        elif m == 1:
            mix, qarr, qblock = _mla_layer(x, positions, attn_norm[i], b_w_in[j], b_q_norm[j],
                                           b_w_uq[j], b_kv_norm[j], b_w_ukv[j], b_nope_g[j],
                                           b_rope_g[j], mem_g[0])
        elif m == 2:
            mix, qarr, qblock = _dil_layer(x, attn_norm[i], c_w_in[j], c_qk_g[j], t5_table, mem_g[0])
        else:
            mix, qarr, qblock = _dsa_layer(x, attn_norm[i], d_w_in[j], d_qk_g[j], t5_table, mem_g[0])
        eg, ef = _headnorm_cols([(mem_g[1], MEM_HEADS, 1.0), (None, MEM_HEADS, 1.0)])
        mem_kv = _proj(mem, mem_norm[i], mem_w_kv[i].astype(BF16), eg, ef)
        mo = _mem_attention(qarr, qblock, mem_kv)
        wo = w_out[i].astype(BF16)
        x = _out_proj(x.reshape(B * S, D), mix.reshape(B * S, MIX_WIDTH),
                      mo.reshape(B * S, MEM_WIDTH), wo[:MIX_WIDTH], wo[MIX_WIDTH:]).reshape(B, S, D)
        x = ffn(x, i, 1)
    return x
```

```python
import functools
import math

import jax
import jax.numpy as jnp
from jax import lax
from jax.experimental import pallas as pl
from jax.experimental.pallas import tpu as pltpu

F32 = jnp.float32
BF16 = jnp.bfloat16

LANE = 128
D_MODEL = 2048
N_HEADS = 16
HEAD_DIM = 128
MIX_WIDTH = N_HEADS * HEAD_DIM
MEM_HEADS = 4
MEM_WIDTH = MEM_HEADS * HEAD_DIM
D_FF = 5632
RMS_EPS = 1e-6
REL_BUCKETS = 32
REL_MAX_DIST = 2048
Q_LORA = 512
KV_LORA = 512
NOPE_DIM = 128
ROPE_DIM = 64
ROPE_THETA = 10000.0
DIL_GROUPS = ((128, 1), (512, 4), (2048, 16))
DIL_BLOCK = 128
DSA_KV_HEADS = 4
IDX_HEADS = 16
IDX_DIM = 64
TOPK_MAX = 256
VMEM_LIMIT = 56 * 1024 * 1024
NEG_BIG = -1e30
LOG2E = math.log2(math.e)


def _params(*sem):
    return pltpu.CompilerParams(dimension_semantics=sem, vmem_limit_bytes=VMEM_LIMIT)


def _tile(n, pref):
    t = min(n, pref)
    while n % t:
        t //= 2
    return t


def _ffn_kernel(x_ref, g_ref, wg_ref, wu_ref, wd_ref, o_ref, h_ref, acc_ref):
    j = pl.program_id(1)
    last = pl.num_programs(1) - 1
    tm = x_ref.shape[0]
    halves = [slice(0, tm // 2), slice(tm // 2, tm)]

    def down(h):
        g = jnp.dot(h, wg_ref[...], preferred_element_type=F32)
        u = jnp.dot(h, wu_ref[...], preferred_element_type=F32)
        a = (g * jax.nn.sigmoid(g) * u).astype(BF16)
        return jnp.dot(a, wd_ref[...], preferred_element_type=F32)

    @pl.when(j == 0)
    def _():
        for rows in halves:
            x = x_ref[rows, :]
            ms = jnp.mean(x * x, axis=-1, keepdims=True)
            h = (x * lax.rsqrt(ms + RMS_EPS) * g_ref[...]).astype(BF16)
            h_ref[rows, :] = h
            acc_ref[rows, :] = down(h)

    @pl.when(jnp.logical_and(j > 0, j < last))
    def _():
        acc_ref[...] += down(h_ref[...])

    @pl.when(jnp.logical_and(j > 0, j == last))
    def _():
        for rows in halves:
            o_ref[rows, :] = x_ref[rows, :] + 0.5 * (acc_ref[rows, :] + down(h_ref[rows, :]))


def _ffn(x2, gain, wg, wu, wd):
    T, D = x2.shape
    F = wg.shape[1]
    tm, tf = _tile(T, 512), _tile(F, 512)
    assert F // tf >= 2 and tm % 16 == 0
    return pl.pallas_call(
        _ffn_kernel,
        grid=(T // tm, F // tf),
        in_specs=[
            pl.BlockSpec((tm, D), lambda i, j: (i, 0)),
            pl.BlockSpec((1, D), lambda i, j: (0, 0)),
            pl.BlockSpec((D, tf), lambda i, j: (0, j)),
            pl.BlockSpec((D, tf), lambda i, j: (0, j)),
            pl.BlockSpec((tf, D), lambda i, j: (j, 0)),
        ],
        out_specs=pl.BlockSpec((tm, D), lambda i, j: (i, 0)),
        out_shape=jax.ShapeDtypeStruct((T, D), F32),
        scratch_shapes=[pltpu.VMEM((tm, D), BF16), pltpu.VMEM((tm, D), F32)],
        compiler_params=_params("parallel", "arbitrary"),
        name="ffn",
    )(x2, gain.reshape(1, D), wg, wu, wd)


def _proj_kernel(x_ref, g_ref, w_ref, eg_ref, ef_ref, o_ref, h_ref, *stage, epilogue, dil):
    @pl.when(pl.program_id(2) == 0)
    def _():
        x = x_ref[...].astype(F32)
        ms = jnp.mean(x * x, axis=-1, keepdims=True)
        h_ref[...] = (x * lax.rsqrt(ms + RMS_EPS) * g_ref[...]).astype(BF16)

    tm = h_ref.shape[0]
    parts = 2 if tm % (2 * 8 * dil) == 0 else 1
    rows_per, n = tm // parts, tm // parts // dil
    for p in range(parts):
        rows = slice(p * rows_per, (p + 1) * rows_per)
        acc = jnp.dot(h_ref[rows, :], w_ref[...], preferred_element_type=F32)
        for c in range(acc.shape[1] // LANE):
            sl = slice(c * LANE, (c + 1) * LANE)
            y = acc[:, sl]
            if epilogue:
                ms = jnp.mean(y * y, axis=-1, keepdims=True)
                scale = jnp.where(ef_ref[:, sl] > 0.0, lax.rsqrt(ms + RMS_EPS), 1.0)
                y = y * scale * eg_ref[:, sl]
            if dil == 1:
                o_ref[0, rows, sl] = y.astype(o_ref.dtype)
            else:
                stage[0][c, rows, :] = y
                for r in range(dil):
                    o_ref[r, p * n:(p + 1) * n, sl] = stage[0][
                        c, pl.ds(p * rows_per + r, n, stride=dil), :].astype(o_ref.dtype)


def _proj(x, gain, w, eg=None, ef=None, *, dil=1, kblock=0, out_dtype=BF16, tm=1024, tn=512):
    B, S, C = x.shape
    K, N = w.shape
    tm, tn = _tile(S, tm), _tile(N, tn)
    epilogue = eg is not None
    if not epilogue:
        eg = jnp.ones((1, N), F32)
        ef = jnp.zeros((1, N), F32)
    out = pl.pallas_call(
        functools.partial(_proj_kernel, epilogue=epilogue, dil=dil),
        grid=(B, S // tm, N // tn),
        in_specs=[
            pl.BlockSpec((None, tm, K), lambda b, i, j: (b, i, kblock)),
            pl.BlockSpec((1, K), lambda b, i, j: (0, 0)),
            pl.BlockSpec((K, tn), lambda b, i, j: (0, j)),
            pl.BlockSpec((1, tn), lambda b, i, j: (0, j)),
            pl.BlockSpec((1, tn), lambda b, i, j: (0, j)),
        ],
        out_specs=pl.BlockSpec((None, dil, tm // dil, tn), lambda b, i, j: (b, 0, i, j)),
        out_shape=jax.ShapeDtypeStruct((B, dil, S // dil, N), out_dtype),
        scratch_shapes=[pltpu.VMEM((tm, K), BF16)]
        + ([pltpu.VMEM((tn // LANE, tm, LANE), F32)] if dil > 1 else []),
        compiler_params=_params("parallel", "parallel", "arbitrary"),
        name="proj",
    )(x, gain.reshape(1, K).astype(F32), w, eg.reshape(1, N), ef.reshape(1, N))
    return out.reshape(B, S, N) if dil == 1 else out


def _out_kernel(x_ref, a_ref, b_ref, wa_ref, wb_ref, o_ref):
    acc = jnp.dot(a_ref[...], wa_ref[...], preferred_element_type=F32)
    acc += jnp.dot(b_ref[...], wb_ref[...], preferred_element_type=F32)
    o_ref[...] = x_ref[...] + acc


def _out_proj(x2, mix2, mo2, wa, wb):
    T, D = x2.shape
    tm, tn = _tile(T, 1024), _tile(D, 512)
    ka, kb = mix2.shape[1], mo2.shape[1]
    return pl.pallas_call(
        _out_kernel,
        grid=(T // tm, D // tn),
        in_specs=[
            pl.BlockSpec((tm, tn), lambda i, j: (i, j)),
            pl.BlockSpec((tm, ka), lambda i, j: (i, 0)),
            pl.BlockSpec((tm, kb), lambda i, j: (i, 0)),
            pl.BlockSpec((ka, tn), lambda i, j: (0, j)),
            pl.BlockSpec((kb, tn), lambda i, j: (0, j)),
        ],
        out_specs=pl.BlockSpec((tm, tn), lambda i, j: (i, j)),
        out_shape=jax.ShapeDtypeStruct((T, D), F32),
        compiler_params=_params("parallel", "arbitrary"),
        name="out_proj",
    )(x2, mix2, mo2, wa, wb)


def _lanes(x, reps):
    return x if reps == 1 else jnp.concatenate([x] * reps, axis=-1)


def _softmax_steps(scores, values, m_ref, acc_ref):
    ps, alphas = [], []
    for g, s in enumerate(scores):
        m_old = m_ref[g]
        m_new = jnp.maximum(m_old, jnp.max(s, axis=-1, keepdims=True))
        ps.append(jnp.exp2(s - _lanes(m_new, s.shape[1] // LANE)).astype(BF16))
        alphas.append(jnp.exp2(m_old - m_new))
        m_ref[g] = m_new
    for g, (p, alpha, v) in enumerate(zip(ps, alphas, values)):
        v1 = jnp.concatenate([v, jnp.ones_like(v)], axis=-1)
        acc_ref[g] = _lanes(alpha, 2) * acc_ref[g] + jnp.dot(p, v1, preferred_element_type=F32)


def _flash_kernel(*refs, nqk, decay, tq, G):
    q_refs, k_refs, v_ref = refs[:nqk], refs[nqk:2 * nqk], refs[2 * nqk]
    pos = 2 * nqk + 1
    if decay:
        ck_ref = refs[pos]
        pos += 1
    o_ref, m_ref, acc_ref = refs[pos:pos + 3]
    i = pl.program_id(2)

    def head(ref, g, rows=slice(None)):
        if ref.shape[-1] == LANE:
            return ref[rows, :]
        return ref[rows, g * LANE:(g + 1) * LANE]

    qs = []
    for g in range(G):
        parts = [head(r, g) for r in q_refs]
        qs.append(parts[0] if nqk == 1 else jnp.concatenate(parts, axis=-1))
    m_ref[...] = jnp.full_like(m_ref, NEG_BIG)
    acc_ref[...] = jnp.zeros_like(acc_ref)

    def step(j, diagonal):
        off = pl.multiple_of(j * tq, tq)
        rows = pl.ds(off, tq)
        if diagonal:
            row = lax.broadcasted_iota(jnp.int32, (tq, tq), 0)
            col = lax.broadcasted_iota(jnp.int32, (tq, tq), 1)
            visible = col <= row
        scores = []
        for g in range(G):
            parts = [head(r, g, rows) for r in k_refs]
            k = parts[0] if nqk == 1 else jnp.concatenate(parts, axis=-1)
            s = lax.dot_general(qs[g], k, (((1,), (1,)), ((), ())), preferred_element_type=F32)
            if decay:
                s = s - ck_ref[g, :, rows]
            if diagonal:
                s = jnp.where(visible, s, -jnp.inf)
            scores.append(s)
        _softmax_steps(scores, [head(v_ref, g, rows) for g in range(G)], m_ref, acc_ref)

    def body(j, carry):
        step(j, False)
        return carry

    lax.fori_loop(0, i, body, 0)
    step(i, True)
    for g in range(G):
        acc = acc_ref[g]
        o_ref[:, g * LANE:(g + 1) * LANE] = (acc[:, :LANE] / acc[:, LANE:]).astype(o_ref.dtype)


def _flash(q_parts, k_parts, v_part, B, S, decay_t=None, tq=512, G=4):
    tq = _tile(S, tq)
    nqk = len(q_parts)
    W = G * LANE
    args, specs = [], []
    for arr, off, _ in q_parts:
        args.append(arr)
        specs.append(pl.BlockSpec((None, tq, W), lambda b, h, i, off=off: (b, i, off // W + h)))
    for arr, off, per_head in k_parts + [v_part]:
        args.append(arr)
        if per_head:
            specs.append(pl.BlockSpec((None, S, W), lambda b, h, i, off=off: (b, 0, off // W + h)))
        else:
            specs.append(pl.BlockSpec((None, S, LANE), lambda b, h, i, off=off: (b, 0, off // LANE)))
    if decay_t is not None:
        args.append(decay_t)
        specs.append(pl.BlockSpec((None, G, 1, S), lambda b, h, i: (b, h, 0, 0)))
    return pl.pallas_call(
        functools.partial(_flash_kernel, nqk=nqk, decay=decay_t is not None, tq=tq, G=G),
        grid=(B, N_HEADS // G, S // tq),
        in_specs=specs,
        out_specs=pl.BlockSpec((None, tq, W), lambda b, h, i: (b, i, h)),
        out_shape=jax.ShapeDtypeStruct((B, S, MIX_WIDTH), BF16),
        scratch_shapes=[pltpu.VMEM((G, tq, LANE), F32), pltpu.VMEM((G, tq, 2 * HEAD_DIM), F32)],
        compiler_params=_params("parallel", "parallel", "arbitrary"),
        name="flash",
    )(*args)


def _gate_kernel(fg_ref, bf_ref, cumt_ref, carry_ref, *, ts):
    @pl.when(pl.program_id(1) == 0)
    def _():
        carry_ref[...] = jnp.zeros_like(carry_ref)

    z = fg_ref[...] + bf_ref[...]
    lf = jnp.minimum(z, 0.0) - jnp.log1p(jnp.exp(-jnp.abs(z)))
    hi = lf.astype(BF16)
    r1 = lf - hi.astype(F32)
    mid = r1.astype(BF16)
    lo = (r1 - mid.astype(F32)).astype(BF16)
    row = lax.broadcasted_iota(jnp.int32, (ts, ts), 0)
    col = lax.broadcasted_iota(jnp.int32, (ts, ts), 1)
    tri = jnp.where(col <= row, 1.0, 0.0).astype(BF16)
    cum = (jnp.dot(tri, hi, preferred_element_type=F32)
           + jnp.dot(tri, mid, preferred_element_type=F32)
           + jnp.dot(tri, lo, preferred_element_type=F32)) + carry_ref[...]
    carry_ref[...] = cum[ts - 1:ts, :]
    cumt_ref[...] = (cum * LOG2E).T


def _fox_gates(fg, b_f):
    B, S, _ = fg.shape
    ts = _tile(S, 512)
    bf = jnp.zeros((1, LANE), F32).at[0, :N_HEADS].set(b_f.astype(F32))
    return pl.pallas_call(
        functools.partial(_gate_kernel, ts=ts),
        grid=(B, S // ts),
        in_specs=[pl.BlockSpec((None, ts, LANE), lambda b, i: (b, i, 0)),
                  pl.BlockSpec((1, LANE), lambda b, i: (0, 0))],
        out_specs=pl.BlockSpec((None, LANE, ts), lambda b, i: (b, 0, i)),
        out_shape=jax.ShapeDtypeStruct((B, LANE, S), F32),
        scratch_shapes=[pltpu.VMEM((1, LANE), F32)],
        compiler_params=_params("parallel", "arbitrary"),
        name="fox_gates",
    )(fg, bf)


def _rope_kernel(x_ref, cos_ref, sin_ref, g_ref, o_ref):
    cos, sin, g = cos_ref[...], sin_ref[...], g_ref[...]
    for c in range(x_ref.shape[1] // LANE):
        sl = slice(c * LANE, (c + 1) * LANE)
        x = x_ref[:, sl]
        ms = jnp.sum(x * x, axis=-1, keepdims=True) * (1.0 / ROPE_DIM)
        y = x * lax.rsqrt(ms + RMS_EPS) * g
        partner = pltpu.roll(y, ROPE_DIM // 2, 1) + pltpu.roll(y, LANE - ROPE_DIM // 2, 1)
        o_ref[:, sl] = (y * cos + partner * sin).astype(o_ref.dtype)


def _rope(x, cos, sin, gain):
    B, S, N = x.shape
    ts = _tile(S, 512)
    g = jnp.zeros((1, LANE), F32).at[0, :ROPE_DIM].set(gain.astype(F32))
    return pl.pallas_call(
        _rope_kernel,
        grid=(B, S // ts),
        in_specs=[pl.BlockSpec((None, ts, N), lambda b, i: (b, i, 0)),
                  pl.BlockSpec((None, ts, LANE), lambda b, i: (b, i, 0)),
                  pl.BlockSpec((None, ts, LANE), lambda b, i: (b, i, 0)),
                  pl.BlockSpec((1, LANE), lambda b, i: (0, 0))],
        out_specs=pl.BlockSpec((None, ts, N), lambda b, i: (b, i, 0)),
        out_shape=jax.ShapeDtypeStruct((B, S, N), BF16),
        compiler_params=_params("parallel", "parallel"),
        name="rope",
    )(x, cos, sin, g)


def _mem_kernel(q_ref, kv_ref, o_ref):
    for h in range(MEM_HEADS):
        q = q_ref[:, h * LANE:(h + 1) * LANE]
        k = kv_ref[:, h * LANE:(h + 1) * LANE]
        v = kv_ref[:, MEM_WIDTH + h * LANE:MEM_WIDTH + (h + 1) * LANE]
        s = lax.dot_general(q, k, (((1,), (1,)), ((), ())), preferred_element_type=F32)
        p = jnp.exp(s - jnp.max(s, axis=-1, keepdims=True))
        o = jnp.dot(p.astype(BF16), v, preferred_element_type=F32)
        o_ref[:, h * LANE:(h + 1) * LANE] = (o / jnp.sum(p, axis=-1, keepdims=True)).astype(o_ref.dtype)


def _mem_attention(qarr, qblock, mem_kv):
    B, S, _ = qarr.shape
    n_mem = mem_kv.shape[1]
    tq = _tile(S, 1024)
    return pl.pallas_call(
        _mem_kernel,
        grid=(B, S // tq),
        in_specs=[pl.BlockSpec((None, tq, MEM_WIDTH), lambda b, i: (b, i, qblock)),
                  pl.BlockSpec((None, n_mem, 2 * MEM_WIDTH), lambda b, i: (b, 0, 0))],
        out_specs=pl.BlockSpec((None, tq, MEM_WIDTH), lambda b, i: (b, i, 0)),
        out_shape=jax.ShapeDtypeStruct((B, S, MEM_WIDTH), BF16),
        compiler_params=_params("parallel", "parallel"),
        name="mem_attn",
    )(qarr, mem_kv)


def _dil_kernel(q_ref, kp_ref, kc_ref, vp_ref, vc_ref, bias_ref, o_ref, lse_ref):
    n = DIL_BLOCK
    has_prev = pl.program_id(1) > 0
    row = lax.broadcasted_iota(jnp.int32, (n, 2 * n), 0)
    col = lax.broadcasted_iota(jnp.int32, (n, 2 * n), 1)
    valid = jnp.logical_and(jnp.logical_and(col >= row, col <= row + n),
                            jnp.logical_or(col >= n, has_prev))
    lane = lax.broadcasted_iota(jnp.int32, (n, LANE), 1)
    dn = (((1,), (1,)), ((), ()))
    heads = [slice(h * LANE, (h + 1) * LANE) for h in range(N_HEADS)]
    scores = []
    for h, sl in enumerate(heads):
        k2 = jnp.concatenate([kp_ref[:, sl], kc_ref[:, sl]], axis=0)
        s = lax.dot_general(q_ref[:, sl], k2, dn, preferred_element_type=F32) + bias_ref[h]
        scores.append(jnp.where(valid, s, -jnp.inf))
    ms = [jnp.max(s, axis=-1, keepdims=True) for s in scores]
    ps = [jnp.exp2(s - m).astype(BF16) for s, m in zip(scores, ms)]
    lse_all = jnp.zeros((n, LANE), F32)
    for h, sl in enumerate(heads):
        v2 = jnp.concatenate([vp_ref[:, sl], vc_ref[:, sl]], axis=0)
        pv = jnp.dot(ps[h], jnp.concatenate([v2, jnp.ones_like(v2)], axis=-1),
                     preferred_element_type=F32)
        den = pv[:, LANE:]
        o_ref[h] = pv[:, :LANE] / den
        lse_all = jnp.where(lane == h, ms[h] + jnp.log2(den), lse_all)
    lse_ref[...] = lse_all


def _dil_attention(pg, bias):
    N, Ls, _ = pg.shape
    n = DIL_BLOCK
    blk = lambda part, prev: pl.BlockSpec(
        (None, n, MIX_WIDTH),
        (lambda s, i: (s, jnp.maximum(i - 1, 0), part)) if prev else (lambda s, i: (s, i, part)))
    return pl.pallas_call(
        _dil_kernel,
        grid=(N, Ls // n),
        in_specs=[blk(0, False), blk(1, True), blk(1, False), blk(2, True), blk(2, False),
                  pl.BlockSpec((N_HEADS, n, 2 * n), lambda s, i: (0, 0, 0))],
        out_specs=[pl.BlockSpec((None, N_HEADS, n, LANE), lambda s, i: (s, 0, i, 0)),
                   pl.BlockSpec((None, n, LANE), lambda s, i: (s, i, 0))],
        out_shape=[jax.ShapeDtypeStruct((N, N_HEADS, Ls, LANE), F32),
                   jax.ShapeDtypeStruct((N, Ls, LANE), F32)],
        compiler_params=_params("parallel", "arbitrary"),
        name="dil_attn",
    )(pg, pg, pg, pg, pg, bias)


def _dil_combine_kernel(o1_ref, o2_ref, o3_ref, l1_ref, l2_ref, l3_ref, out_ref, stage_ref, *, d2, d3):
    n = l3_ref.shape[1]
    q = d3 // d2
    for r in range(d3):
        rows = [(pl.ds(r, n, stride=d3),), (r % d2, pl.ds(r // d2, n, stride=q)), (r, slice(None))]
        ls = [l_ref[idx + (slice(None),)]
              for l_ref, idx in zip((l1_ref, l2_ref, l3_ref), rows)]
        m = functools.reduce(jnp.maximum, ls)
        es = [jnp.exp2(l - m) for l in ls]
        inv = 1.0 / functools.reduce(jnp.add, es)
        ws = [e * inv for e in es]
        for h in range(N_HEADS):
            o = (ws[0][:, h:h + 1] * o1_ref[h, pl.ds(r, n, stride=d3), :]
                 + ws[1][:, h:h + 1] * o2_ref[r % d2, h, pl.ds(r // d2, n, stride=q), :]
                 + ws[2][:, h:h + 1] * o3_ref[r, h, :, :])
            stage_ref[h, pl.ds(r, n, stride=d3), :] = o
    for h in range(N_HEADS):
        out_ref[:, h * LANE:(h + 1) * LANE] = stage_ref[h].astype(out_ref.dtype)


def _dil_combine(outs, lses, B, S, tm=256):
    (_, d1), (_, d2), (_, d3) = DIL_GROUPS
    assert d1 == 1 and d3 % d2 == 0
    tm = _tile(S, tm)
    H = N_HEADS
    o_specs = [pl.BlockSpec((None, H, tm, LANE), lambda b, i: (b, 0, i, 0)),
               pl.BlockSpec((None, d2, H, tm // d2, LANE), lambda b, i: (b, 0, 0, i, 0)),
               pl.BlockSpec((None, d3, H, tm // d3, LANE), lambda b, i: (b, 0, 0, i, 0))]
    l_specs = [pl.BlockSpec((None, tm, LANE), lambda b, i: (b, i, 0)),
               pl.BlockSpec((None, d2, tm // d2, LANE), lambda b, i: (b, 0, i, 0)),
               pl.BlockSpec((None, d3, tm // d3, LANE), lambda b, i: (b, 0, i, 0))]
    return pl.pallas_call(
        functools.partial(_dil_combine_kernel, d2=d2, d3=d3),
        grid=(B, S // tm),
        in_specs=o_specs + l_specs,
        out_specs=pl.BlockSpec((None, tm, MIX_WIDTH), lambda b, i: (b, i, 0)),
        out_shape=jax.ShapeDtypeStruct((B, S, MIX_WIDTH), BF16),
        scratch_shapes=[pltpu.VMEM((H, tm, LANE), F32)],
        compiler_params=_params("parallel", "parallel"),
        name="dil_combine",
    )(outs[0].reshape(B, H, S, LANE), outs[1].reshape(B, d2, H, S // d2, LANE),
      outs[2].reshape(B, d3, H, S // d3, LANE),
      lses[0].reshape(B, S, LANE), lses[1].reshape(B, d2, S // d2, LANE),
      lses[2].reshape(B, d3, S // d3, LANE))


def _dsa_select_kernel(qi_ref, wi_ref, ki_ref, mask_ref, key_ref, jsel_ref, *, tq, tk, S, n_sel):
    i = pl.program_id(1)
    q0 = i * tq
    n_chunks = (q0 + tq + tk - 1) // tk
    dn = (((1,), (1,)), ((), ()))
    rowg = q0 + lax.broadcasted_iota(jnp.int32, (tq, tk), 0)
    colb = lax.broadcasted_iota(jnp.int32, (tq, tk), 1)

    wi = wi_ref[...] * (IDX_HEADS ** -0.5 * IDX_DIM ** -0.5)

    def idx_body(c, carry):
        off = pl.multiple_of(c * tk, tk)
        kic = ki_ref[pl.ds(off, tk), :].astype(BF16)
        acc = jnp.zeros((tq, tk), F32)
        for h in range(IDX_HEADS):
            d = lax.dot_general(qi_ref[:, h * LANE:(h + 1) * LANE], kic, dn,
                                preferred_element_type=F32)
            acc = acc + wi[:, IDX_DIM + h:IDX_DIM + h + 1] * jnp.maximum(d, 0.0)
        score = jnp.where(off + colb <= rowg, acc, -jnp.inf)
        bits = pltpu.bitcast(score, jnp.int32)
        key_ref[:, pl.ds(off, tk)] = bits ^ ((bits >> 31) & jnp.int32(0x7FFFFFFF))
        return carry

    lax.fori_loop(0, n_chunks, idx_body, 0)

    nb = tq // LANE
    blocks = [slice(rb * LANE, (rb + 1) * LANE) for rb in range(nb)]
    colk = lax.broadcasted_iota(jnp.int32, (LANE, tk), 1)
    rowk = [q0 + rb * LANE + lax.broadcasted_iota(jnp.int32, (LANE, tk), 0) for rb in range(nb)]

    def count(pred):
        out = []
        for rb in range(nb):
            def body(c, acc, rb=rb):
                off = pl.multiple_of(c * tk, tk)
                ones = jnp.where(pred(key_ref[blocks[rb], pl.ds(off, tk)], off, rb), 1.0, 0.0)
                for w in range(tk // LANE):
                    acc = acc + ones[:, w * LANE:(w + 1) * LANE]
                return acc

            acc = lax.fori_loop(0, n_chunks, body, jnp.zeros((LANE, LANE), F32))
            out.append(jnp.sum(acc, axis=-1, keepdims=True))
        return out

    k_sel = float(n_sel)
    lo = tuple(jnp.where(c >= k_sel, jnp.int32(0), jnp.int32(-2 ** 31))
               for c in count(lambda k, off, rb: k >= 0))

    def bis_body(it, lo):
        cand = [l + (jnp.int32(1) << (30 - it)) for l in lo]
        cnt = count(lambda k, off, rb: k >= cand[rb])
        return tuple(jnp.where(c >= k_sel, cd, l) for c, cd, l in zip(cnt, cand, lo))

    thr = lax.fori_loop(0, 31, bis_body, lo)
    n_ge = count(lambda k, off, rb: k >= thr[rb])
    need = [k_sel - c for c in count(lambda k, off, rb: k > thr[rb])]
    n_bits = max(1, (S - 1).bit_length())
    jsel_ref[...] = jnp.full_like(jsel_ref, S)
    most = functools.reduce(jnp.maximum, [jnp.max(c) for c in n_ge])

    @pl.when(most > k_sel)
    def _():
        def tie_body(it, jlo):
            cand = [j + (jnp.int32(1) << (n_bits - 1 - it)) for j in jlo]
            below = count(lambda k, off, rb: jnp.logical_and(k == thr[rb], off + colk < cand[rb]))
            return tuple(jnp.where(b < n, cd, j) for b, n, cd, j in zip(below, need, cand, jlo))

        jsel = lax.fori_loop(0, n_bits, tie_body,
                             tuple(jnp.zeros((LANE, 1), jnp.int32) for _ in range(nb)))
        for rb in range(nb):
            jsel_ref[blocks[rb], :] = jnp.broadcast_to(jsel[rb], (LANE, LANE))

    mask_ref[...] = jnp.full_like(mask_ref, -jnp.inf)

    def mask_body(c, carry):
        off = pl.multiple_of(c * tk, tk)
        col = off + colk
        for rb in range(nb):
            k = key_ref[blocks[rb], pl.ds(off, tk)]
            tie = jnp.logical_and(k == thr[rb], col <= jsel_ref[blocks[rb], :1])
            sel = jnp.logical_and(jnp.logical_or(k > thr[rb], tie), col <= rowk[rb])
            mask_ref[blocks[rb], pl.ds(off, tk)] = jnp.where(sel, 0.0, -jnp.inf).astype(mask_ref.dtype)
        return carry

    lax.fori_loop(0, n_chunks, mask_body, 0)


def _dsa_select(pm, aux, B, S, n_sel, tq=256, tk=512):
    tq, tk = _tile(S, tq), _tile(S, tk)
    return pl.pallas_call(
        functools.partial(_dsa_select_kernel, tq=tq, tk=tk, S=S, n_sel=n_sel),
        grid=(B, S // tq),
        in_specs=[pl.BlockSpec((None, tq, MIX_WIDTH), lambda b, i: (b, i, 1)),
                  pl.BlockSpec((None, tq, LANE), lambda b, i: (b, i, 0)),
                  pl.BlockSpec((None, S, LANE), lambda b, i: (b, 0, 0))],
        out_specs=pl.BlockSpec((None, tq, S), lambda b, i: (b, i, 0)),
        out_shape=jax.ShapeDtypeStruct((B, S, S), BF16),
        scratch_shapes=[pltpu.VMEM((tq, S), jnp.int32), pltpu.VMEM((tq, LANE), jnp.int32)],
        compiler_params=_params("parallel", "arbitrary"),
        name="dsa_select",
    )(pm, aux, aux)


def _dsa_attn_kernel(q_ref, k_ref, v_ref, mask_ref, gt_ref, o_ref, m_ref, acc_ref, *, tq, tk, S):
    i = pl.program_id(2)
    q0 = i * tq
    n_chunks = (q0 + tq + tk - 1) // tk
    R = q_ref.shape[1] // LANE
    nblk = tq // LANE
    dn = (((1,), (1,)), ((), ()))
    qs = [q_ref[:, r * LANE:(r + 1) * LANE] for r in range(R)]
    m_ref[...] = jnp.full_like(m_ref, NEG_BIG)
    acc_ref[...] = jnp.zeros_like(acc_ref)

    def body(c, carry):
        off = pl.multiple_of(c * tk, tk)
        kc = k_ref[pl.ds(off, tk), :]
        vc = v_ref[pl.ds(off, tk), :]
        mask = mask_ref[:, pl.ds(off, tk)].astype(F32)
        st = S - q0 + off
        scores = []
        for r in range(R):
            bias = jnp.concatenate(
                [gt_ref[r, :, pl.ds(pl.multiple_of(st + (nblk - 1 - a) * LANE, LANE), tk)]
                 for a in range(nblk)], axis=0)
            scores.append(lax.dot_general(qs[r], kc, dn, preferred_element_type=F32) + bias + mask)
        _softmax_steps(scores, [vc] * R, m_ref, acc_ref)
        return carry

    lax.fori_loop(0, n_chunks, body, 0)
    for r in range(R):
        acc = acc_ref[r]
        o_ref[:, r * LANE:(r + 1) * LANE] = (acc[:, :LANE] / acc[:, LANE:]).astype(o_ref.dtype)


def _dsa_attention(pm, mask, gt, B, S, tq=256, tk=512):
    tq, tk = _tile(S, tq), _tile(S, tk)
    R = N_HEADS // DSA_KV_HEADS
    kblk = 2 * MIX_WIDTH // LANE
    return pl.pallas_call(
        functools.partial(_dsa_attn_kernel, tq=tq, tk=tk, S=S),
        grid=(DSA_KV_HEADS, B, S // tq),
        in_specs=[pl.BlockSpec((None, tq, R * LANE), lambda g, b, i: (b, i, g)),
                  pl.BlockSpec((None, S, LANE), lambda g, b, i: (b, 0, kblk + g)),
                  pl.BlockSpec((None, S, LANE), lambda g, b, i: (b, 0, kblk + DSA_KV_HEADS + g)),
                  pl.BlockSpec((None, tq, S), lambda g, b, i: (b, i, 0)),
                  pl.BlockSpec((R, LANE, gt.shape[2]), lambda g, b, i: (g, 0, 0))],
        out_specs=pl.BlockSpec((None, tq, R * LANE), lambda g, b, i: (b, i, g)),
        out_shape=jax.ShapeDtypeStruct((B, S, MIX_WIDTH), BF16),
        scratch_shapes=[pltpu.VMEM((R, tq, LANE), F32), pltpu.VMEM((R, tq, 2 * HEAD_DIM), F32)],
        compiler_params=_params("parallel", "parallel", "arbitrary"),
        name="dsa_attn",
    )(pm, pm, pm, mask, gt)


def _t5_bucket(dist):
    n = jnp.maximum(dist, 0)
    exact = REL_BUCKETS // 2
    nf = jnp.maximum(n, 1).astype(F32)
    large = exact + (jnp.log(nf / exact) / math.log(REL_MAX_DIST / exact)
                     * (REL_BUCKETS - exact)).astype(jnp.int32)
    large = jnp.minimum(large, REL_BUCKETS - 1)
    return jnp.where(n < exact, n, large)


def _t5_bias(t5_table, dist):
    onehot = jax.nn.one_hot(_t5_bucket(dist), REL_BUCKETS, dtype=F32)
    table = jnp.einsum('...b,bh->h...', onehot, t5_table.astype(F32), precision=lax.Precision.HIGHEST)
    return table * LOG2E


def _dil_bias(t5_table, dil):
    n = DIL_BLOCK
    rel = jnp.arange(n)[:, None] + n - jnp.arange(2 * n)[None, :]
    return _t5_bias(t5_table, rel * dil)


def _dsa_bias_table(t5_table, S, tq, tk):
    lt = S + tq + tk
    tb = _t5_bias(t5_table, S + tq - 1 - jnp.arange(lt + LANE))
    return jnp.stack([tb[:, LANE - 1 - i:LANE - 1 - i + lt] for i in range(LANE)], axis=1)


def _headnorm_cols(spec):
    eg, ef = [], []
    for gain, count, scale in spec:
        if gain is None:
            eg.append(jnp.ones((count * LANE,), F32))
            ef.append(jnp.zeros((count * LANE,), F32))
        else:
            eg.append(jnp.tile(gain.astype(F32) * scale, count))
            ef.append(jnp.ones((count * LANE,), F32))
    return jnp.concatenate(eg), jnp.concatenate(ef)


def _pad_cols(w, width):
    return jnp.pad(w, ((0, 0), (0, width - w.shape[1])))


def _fox_layer(x, norm_g, w_in, b_f, qk_g, mem_g):
    B, S, _ = x.shape
    W = MIX_WIDTH
    w_main = jnp.concatenate([w_in[:, :3 * W], w_in[:, 3 * W + N_HEADS:]], axis=1).astype(BF16)
    w_gate = _pad_cols(w_in[:, 3 * W:3 * W + N_HEADS], LANE).astype(BF16)
    eg, ef = _headnorm_cols([(qk_g[0], N_HEADS, HEAD_DIM ** -0.5 * LOG2E), (qk_g[1], N_HEADS, 1.0),
                             (None, N_HEADS, 1.0), (mem_g, MEM_HEADS, HEAD_DIM ** -0.5)])
    pm = _proj(x, norm_g, w_main, eg, ef)
    fg = _proj(x, norm_g, w_gate, out_dtype=F32)
    cum_t = _fox_gates(fg, b_f)
    mix = _flash([(pm, 0, True)], [(pm, W, True)], (pm, 2 * W, True), B, S,
                 decay_t=cum_t.reshape(B, LANE, 1, S))
    return mix, pm, 3 * W // MEM_WIDTH


def _mla_layer(x, positions, norm_g, w_in, q_norm, w_uq, kv_norm, w_ukv, nope_g, rope_g, mem_g):
    B, S, _ = x.shape
    scale = (NOPE_DIM + ROPE_DIM) ** -0.5 * LOG2E
    lat = Q_LORA + KV_LORA
    w_lat = jnp.concatenate([w_in[:, :lat], _pad_cols(w_in[:, lat:lat + ROPE_DIM], LANE)],
                            axis=1).astype(BF16)
    w_mem = w_in[:, lat + ROPE_DIM:].astype(BF16)
    pl_ = _proj(x, norm_g, w_lat, out_dtype=F32, tn=lat + LANE)
    eg, ef = _headnorm_cols([(mem_g, MEM_HEADS, HEAD_DIM ** -0.5)])
    pmem = _proj(x, norm_g, w_mem, eg, ef)

    uq = w_uq.reshape(Q_LORA, N_HEADS, NOPE_DIM + ROPE_DIM)
    w_qn = uq[:, :, :NOPE_DIM].reshape(Q_LORA, MIX_WIDTH).astype(BF16)
    w_qr = jnp.pad(uq[:, :, NOPE_DIM:], ((0, 0), (0, 0), (0, LANE - ROPE_DIM))
                   ).reshape(Q_LORA, N_HEADS * LANE).astype(BF16)
    ukv = w_ukv.reshape(KV_LORA, N_HEADS, NOPE_DIM + HEAD_DIM)
    w_kv = jnp.concatenate([ukv[:, :, :NOPE_DIM].reshape(KV_LORA, MIX_WIDTH),
                            ukv[:, :, NOPE_DIM:].reshape(KV_LORA, MIX_WIDTH)], axis=1).astype(BF16)
    eg, ef = _headnorm_cols([(nope_g[0], N_HEADS, scale)])
    qn = _proj(pl_, q_norm, w_qn, eg, ef, kblock=0)
    qr_raw = _proj(pl_, q_norm, w_qr, kblock=0, out_dtype=F32)
    eg, ef = _headnorm_cols([(nope_g[1], N_HEADS, 1.0), (None, N_HEADS, 1.0)])
    kv = _proj(pl_, kv_norm, w_kv, eg, ef, kblock=1)

    half = ROPE_DIM // 2
    inv = ROPE_THETA ** (-jnp.arange(half, dtype=F32) / half)
    ang = positions.astype(F32)[..., None] * inv
    cos, sin = jnp.cos(ang), jnp.sin(ang)
    zero = jnp.zeros((B, S, LANE - ROPE_DIM), F32)
    cos_t = jnp.concatenate([cos, cos, zero], axis=-1)
    sin_t = jnp.concatenate([-sin, sin, zero], axis=-1)
    qr = _rope(qr_raw, cos_t, sin_t, rope_g[0] * scale)
    kr = _rope(pl_[:, :, lat:], cos_t, sin_t, rope_g[1])
    mix = _flash([(qn, 0, True), (qr, 0, True)], [(kv, 0, True), (kr, 0, False)],
                 (kv, MIX_WIDTH, True), B, S)
    return mix, pmem, 0


def _dil_layer(x, norm_g, w_in, qk_g, t5_table, mem_g):
    B, S, _ = x.shape
    W = MIX_WIDTH
    outs, lses = [], []
    for gi, (win, dil) in enumerate(DIL_GROUPS):
        assert win // dil == DIL_BLOCK and (S // dil) % DIL_BLOCK == 0
        w_g = w_in[:, gi * 3 * W:(gi + 1) * 3 * W].astype(BF16)
        eg, ef = _headnorm_cols([(qk_g[gi, 0], N_HEADS, HEAD_DIM ** -0.5 * LOG2E),
                                 (qk_g[gi, 1], N_HEADS, 1.0), (None, N_HEADS, 1.0)])
        pg = _proj(x, norm_g, w_g, eg, ef, dil=dil)
        o, lse = _dil_attention(pg.reshape(B * dil, S // dil, 3 * W), _dil_bias(t5_table, dil))
        outs.append(o)
        lses.append(lse)
    mix = _dil_combine(outs, lses, B, S)
    eg, ef = _headnorm_cols([(mem_g, MEM_HEADS, HEAD_DIM ** -0.5)])
    pmem = _proj(x, norm_g, w_in[:, len(DIL_GROUPS) * 3 * W:].astype(BF16), eg, ef)
    return mix, pmem, 0


def _dsa_layer(x, norm_g, w_in, qk_g, t5_table, mem_g):
    B, S, _ = x.shape
    W = MIX_WIDTH
    kvw = DSA_KV_HEADS * HEAD_DIM
    o_k, o_v, o_qi = W, W + kvw, W + 2 * kvw
    o_ki = o_qi + IDX_HEADS * IDX_DIM
    o_wi = o_ki + IDX_DIM
    o_mem = o_wi + IDX_HEADS
    w_qi = jnp.pad(w_in[:, o_qi:o_ki].reshape(D_MODEL, IDX_HEADS, IDX_DIM),
                   ((0, 0), (0, 0), (0, LANE - IDX_DIM))).reshape(D_MODEL, IDX_HEADS * LANE)
    w_main = jnp.concatenate([w_in[:, :W], w_qi, w_in[:, o_k:o_qi], w_in[:, o_mem:]],
                             axis=1).astype(BF16)
    w_aux = _pad_cols(w_in[:, o_ki:o_mem], LANE).astype(BF16)
    eg, ef = _headnorm_cols([(qk_g[0], N_HEADS, HEAD_DIM ** -0.5 * LOG2E), (None, IDX_HEADS, 1.0),
                             (qk_g[1], DSA_KV_HEADS, 1.0), (None, DSA_KV_HEADS, 1.0),
                             (mem_g, MEM_HEADS, HEAD_DIM ** -0.5)])
    pm = _proj(x, norm_g, w_main, eg, ef)
    aux = _proj(x, norm_g, w_aux, out_dtype=F32)
    mask = _dsa_select(pm, aux, B, S, min(TOPK_MAX, S // 4))
    tq, tk = _tile(S, 512), _tile(S, 512)
    mix = _dsa_attention(pm, mask, _dsa_bias_table(t5_table, S, tq, tk), B, S, tq, tk)
    return mix, pm, (2 * W + 2 * kvw) // MEM_WIDTH


def kernel(x, mem, positions, t5_table, ffn_norm, ffn_w_gate, ffn_w_up, ffn_w_down, attn_norm,
           mem_norm, mem_w_kv, mem_qk_g, w_out, a_w_in, a_b_f, a_qk_g, b_w_in, b_q_norm, b_w_uq,
           b_kv_norm, b_w_ukv, b_nope_g, b_rope_g, c_w_in, c_qk_g, d_w_in, d_qk_g):
    B, S, D = x.shape
    depth = ffn_norm.shape[0]
    n_mixers = 4

    def ffn(xc, i, k):
        return _ffn(xc.reshape(B * S, D), ffn_norm[i, k], ffn_w_gate[i, k].astype(BF16),
                    ffn_w_up[i, k].astype(BF16), ffn_w_down[i, k].astype(BF16)).reshape(B, S, D)

    for i in range(depth):
        m, j = i % n_mixers, i // n_mixers
        x = ffn(x, i, 0)
        mem_g = mem_qk_g[i]
        if m == 0:
            mix, qarr, qblock = _fox_layer(x, attn_norm[i], a_w_in[j], a_b_f[j], a_qk_g[j], mem_g[0])
        elif m == 1:
            mix, qarr, qblock = _mla_layer(x, positions, attn_norm[i], b_w_in[j], b_q_norm[j],
                                           b_w_uq[j], b_kv_norm[j], b_w_ukv[j], b_nope_g[j],
                                           b_rope_g[j], mem_g[0])
        elif m == 2:
            mix, qarr, qblock = _dil_layer(x, attn_norm[i], c_w_in[j], c_qk_g[j], t5_table, mem_g[0])
        else:
            mix, qarr, qblock = _dsa_layer(x, attn_norm[i], d_w_in[j], d_qk_g[j], t5_table, mem_g[0])
        eg, ef = _headnorm_cols([(mem_g[1], MEM_HEADS, 1.0), (None, MEM_HEADS, 1.0)])
        mem_kv = _proj(mem, mem_norm[i], mem_w_kv[i].astype(BF16), eg, ef)
        mo = _mem_attention(qarr, qblock, mem_kv)
        wo = w_out[i].astype(BF16)
        x = _out_proj(x.reshape(B * S, D), mix.reshape(B * S, MIX_WIDTH),
                      mo.reshape(B * S, MEM_WIDTH), wo[:MIX_WIDTH], wo[MIX_WIDTH:]).reshape(B, S, D)
        x = ffn(x, i, 1)
    return x
```

```python
import functools
import math

import jax
import jax.numpy as jnp
from jax import lax
from jax.experimental import pallas as pl
from jax.experimental.pallas import tpu as pltpu

F32 = jnp.float32
BF16 = jnp.bfloat16

LANE = 128
D_MODEL = 2048
N_HEADS = 16
HEAD_DIM = 128
MIX_WIDTH = N_HEADS * HEAD_DIM
MEM_HEADS = 4
MEM_WIDTH = MEM_HEADS * HEAD_DIM
D_FF = 5632
RMS_EPS = 1e-6
REL_BUCKETS = 32
REL_MAX_DIST = 2048
Q_LORA = 512
KV_LORA = 512
NOPE_DIM = 128
ROPE_DIM = 64
ROPE_THETA = 10000.0
DIL_GROUPS = ((128, 1), (512, 4), (2048, 16))
DIL_BLOCK = 128
DSA_KV_HEADS = 4
IDX_HEADS = 16
IDX_DIM = 64
TOPK_MAX = 256
VMEM_LIMIT = 56 * 1024 * 1024
FFN_VMEM_LIMIT = 62 * 1024 * 1024
NEG_BIG = -1e30
LOG2E = math.log2(math.e)


def _params(*sem):
    return pltpu.CompilerParams(dimension_semantics=sem, vmem_limit_bytes=VMEM_LIMIT)


def _tile(n, pref):
    t = min(n, pref)
    while n % t:
        t //= 2
    return t


def _ffn_kernel(x_ref, g_ref, wg_ref, wu_ref, wd_ref, o_ref, h_ref):
    _ffn_body(x_ref, g_ref, wg_ref, wu_ref, wd_ref, o_ref, h_ref, ())


def _ffn_kernel_with_next(x_ref, g_ref, wg_ref, wu_ref, wd_ref, ng_ref, o_ref, hn_ref, h_ref):
    _ffn_body(x_ref, g_ref, wg_ref, wu_ref, wd_ref, o_ref, h_ref, (ng_ref, hn_ref))


def _ffn_body(x_ref, g_ref, wg_ref, wu_ref, wd_ref, o_ref, h_ref, next_refs):
    j = pl.program_id(1)
    last = pl.num_programs(1) - 1
    tm = x_ref.shape[0]
    halves = [slice(0, tm // 2), slice(tm // 2, tm)]

    def down(h):
        g = jnp.dot(h, wg_ref[...], preferred_element_type=F32)
        u = jnp.dot(h, wu_ref[...], preferred_element_type=F32)
        a = (g * jax.nn.sigmoid(g) * u).astype(BF16)
        return jnp.dot(a, wd_ref[...], preferred_element_type=F32)

    @pl.when(j == 0)
    def _():
        for rows in halves:
            x = x_ref[rows, :]
            ms = jnp.mean(x * x, axis=-1, keepdims=True)
            h = (x * lax.rsqrt(ms + RMS_EPS) * g_ref[...]).astype(BF16)
            h_ref[rows, :] = h
            o_ref[rows, :] = down(h)

    @pl.when(jnp.logical_and(j > 0, j < last))
    def _():
        o_ref[...] += down(h_ref[...])

    @pl.when(jnp.logical_and(j > 0, j == last))
    def _():
        for rows in halves:
            y = x_ref[rows, :] + 0.5 * (o_ref[rows, :] + down(h_ref[rows, :]))
            o_ref[rows, :] = y
            if next_refs:
                ng_ref, hn_ref = next_refs
                ms = jnp.mean(y * y, axis=-1, keepdims=True)
                hn_ref[rows, :] = (y * lax.rsqrt(ms + RMS_EPS) * ng_ref[...]).astype(BF16)


def _ffn(x2, gain, wg, wu, wd, next_gain=None):
    T, D = x2.shape
    F = wg.shape[1]
    tm, tf = _tile(T, 1024), _tile(F, 512)
    assert F // tf >= 2 and tm % 16 == 0
    row_spec = pl.BlockSpec((tm, D), lambda i, j: (i, 0))
    vec_spec = pl.BlockSpec((1, D), lambda i, j: (0, 0))
    in_specs = [row_spec, vec_spec,
                pl.BlockSpec((D, tf), lambda i, j: (0, j)),
                pl.BlockSpec((D, tf), lambda i, j: (0, j)),
                pl.BlockSpec((tf, D), lambda i, j: (j, 0))]
    args = [x2, gain.reshape(1, D), wg, wu, wd]
    out_specs, out_shape = row_spec, jax.ShapeDtypeStruct((T, D), F32)
    kern = _ffn_kernel
    if next_gain is not None:
        in_specs.append(vec_spec)
        args.append(next_gain.reshape(1, D).astype(F32))
        out_specs = [row_spec, row_spec]
        out_shape = [out_shape, jax.ShapeDtypeStruct((T, D), BF16)]
        kern = _ffn_kernel_with_next
    return pl.pallas_call(
        kern,
        grid=(T // tm, F // tf),
        in_specs=in_specs,
        out_specs=out_specs,
        out_shape=out_shape,
        scratch_shapes=[pltpu.VMEM((tm, D), BF16)],
        compiler_params=pltpu.CompilerParams(dimension_semantics=("parallel", "arbitrary"),
                                             vmem_limit_bytes=FFN_VMEM_LIMIT),
        name="ffn",
    )(*args)


def _proj_kernel(x_ref, g_ref, w_ref, eg_ref, ef_ref, o_ref, *scratch, epilogue, dil, prenormed):
    scratch = list(scratch)
    if prenormed:
        h_ref = x_ref
    else:
        h_ref = scratch.pop(0)

        @pl.when(pl.program_id(2) == 0)
        def _():
            x = x_ref[...].astype(F32)
            ms = jnp.mean(x * x, axis=-1, keepdims=True)
            h_ref[...] = (x * lax.rsqrt(ms + RMS_EPS) * g_ref[...]).astype(BF16)

    stage = scratch
    tm = h_ref.shape[0]
    parts = 2 if tm % (2 * 8 * dil) == 0 else 1
    rows_per, n = tm // parts, tm // parts // dil
    for p in range(parts):
        rows = slice(p * rows_per, (p + 1) * rows_per)
        acc = jnp.dot(h_ref[rows, :], w_ref[...], preferred_element_type=F32)
        for c in range(acc.shape[1] // LANE):
            sl = slice(c * LANE, (c + 1) * LANE)
            y = acc[:, sl]
            if epilogue:
                ms = jnp.mean(y * y, axis=-1, keepdims=True)
                scale = jnp.where(ef_ref[:, sl] > 0.0, lax.rsqrt(ms + RMS_EPS), 1.0)
                y = y * scale * eg_ref[:, sl]
            if dil == 1:
                o_ref[0, rows, sl] = y.astype(o_ref.dtype)
            else:
                stage[0][c, rows, :] = y
                for r in range(dil):
                    o_ref[r, p * n:(p + 1) * n, sl] = stage[0][
                        c, pl.ds(p * rows_per + r, n, stride=dil), :].astype(o_ref.dtype)


def _proj(x, gain, w, eg=None, ef=None, *, dil=1, kblock=0, out_dtype=BF16, tm=1024, tn=512):
    B, S, C = x.shape
    K, N = w.shape
    tm, tn = _tile(S, tm), _tile(N, tn)
    epilogue = eg is not None
    if not epilogue:
        eg = jnp.ones((1, N), F32)
        ef = jnp.zeros((1, N), F32)
    prenormed = gain is None
    if prenormed:
        assert x.dtype == BF16
        gain = jnp.ones((K,), F32)
    out = pl.pallas_call(
        functools.partial(_proj_kernel, epilogue=epilogue, dil=dil, prenormed=prenormed),
        grid=(B, S // tm, N // tn),
        in_specs=[
            pl.BlockSpec((None, tm, K), lambda b, i, j: (b, i, kblock)),
            pl.BlockSpec((1, K), lambda b, i, j: (0, 0)),
            pl.BlockSpec((K, tn), lambda b, i, j: (0, j)),
            pl.BlockSpec((1, tn), lambda b, i, j: (0, j)),
            pl.BlockSpec((1, tn), lambda b, i, j: (0, j)),
        ],
        out_specs=pl.BlockSpec((None, dil, tm // dil, tn), lambda b, i, j: (b, 0, i, j)),
        out_shape=jax.ShapeDtypeStruct((B, dil, S // dil, N), out_dtype),
        scratch_shapes=([] if prenormed else [pltpu.VMEM((tm, K), BF16)])
        + ([pltpu.VMEM((tn // LANE, tm, LANE), F32)] if dil > 1 else []),
        compiler_params=_params("parallel", "parallel", "arbitrary"),
        name="proj",
    )(x, gain.reshape(1, K).astype(F32), w, eg.reshape(1, N), ef.reshape(1, N))
    return out.reshape(B, S, N) if dil == 1 else out


def _out_kernel(x_ref, a_ref, b_ref, wa_ref, wb_ref, o_ref):
    acc = jnp.dot(a_ref[...], wa_ref[...], preferred_element_type=F32)
    acc += jnp.dot(b_ref[...], wb_ref[...], preferred_element_type=F32)
    o_ref[...] = x_ref[...] + acc


def _out_proj(x2, mix2, mo2, wa, wb):
    T, D = x2.shape
    tm, tn = _tile(T, 1024), _tile(D, 512)
    ka, kb = mix2.shape[1], mo2.shape[1]
    return pl.pallas_call(
        _out_kernel,
        grid=(T // tm, D // tn),
        in_specs=[
            pl.BlockSpec((tm, tn), lambda i, j: (i, j)),
            pl.BlockSpec((tm, ka), lambda i, j: (i, 0)),
            pl.BlockSpec((tm, kb), lambda i, j: (i, 0)),
            pl.BlockSpec((ka, tn), lambda i, j: (0, j)),
            pl.BlockSpec((kb, tn), lambda i, j: (0, j)),
        ],
        out_specs=pl.BlockSpec((tm, tn), lambda i, j: (i, j)),
        out_shape=jax.ShapeDtypeStruct((T, D), F32),
        compiler_params=_params("parallel", "arbitrary"),
        name="out_proj",
    )(x2, mix2, mo2, wa, wb)


def _lanes(x, reps):
    return x if reps == 1 else jnp.concatenate([x] * reps, axis=-1)


def _softmax_steps(scores, values, m_ref, acc_ref):
    ps, alphas = [], []
    for g, s in enumerate(scores):
        m_old = m_ref[g]
        m_new = jnp.maximum(m_old, jnp.max(s, axis=-1, keepdims=True))
        ps.append(jnp.exp2(s - _lanes(m_new, s.shape[1] // LANE)).astype(BF16))
        alphas.append(jnp.exp2(m_old - m_new))
        m_ref[g] = m_new
    for g, (p, alpha, v) in enumerate(zip(ps, alphas, values)):
        v1 = jnp.concatenate([v, jnp.ones_like(v)], axis=-1)
        acc_ref[g] = _lanes(alpha, 2) * acc_ref[g] + jnp.dot(p, v1, preferred_element_type=F32)


def _flash_kernel(*refs, nqk, decay, tq, G):
    q_refs, k_refs, v_ref = refs[:nqk], refs[nqk:2 * nqk], refs[2 * nqk]
    pos = 2 * nqk + 1
    if decay:
        ck_ref = refs[pos]
        pos += 1
    o_ref, m_ref, acc_ref = refs[pos:pos + 3]
    i = pl.program_id(2)

    def head(ref, g, rows=slice(None)):
        if ref.shape[-1] == LANE:
            return ref[rows, :]
        return ref[rows, g * LANE:(g + 1) * LANE]

    qs = []
    for g in range(G):
        parts = [head(r, g) for r in q_refs]
        qs.append(parts[0] if nqk == 1 else jnp.concatenate(parts, axis=-1))
    m_ref[...] = jnp.full_like(m_ref, NEG_BIG)
    acc_ref[...] = jnp.zeros_like(acc_ref)

    def step(j, diagonal):
        off = pl.multiple_of(j * tq, tq)
        rows = pl.ds(off, tq)
        if diagonal:
            row = lax.broadcasted_iota(jnp.int32, (tq, tq), 0)
            col = lax.broadcasted_iota(jnp.int32, (tq, tq), 1)
            visible = col <= row
        scores = []
        for g in range(G):
            parts = [head(r, g, rows) for r in k_refs]
            k = parts[0] if nqk == 1 else jnp.concatenate(parts, axis=-1)
            s = lax.dot_general(qs[g], k, (((1,), (1,)), ((), ())), preferred_element_type=F32)
            if decay:
                s = s - ck_ref[g, :, rows]
            if diagonal:
                s = jnp.where(visible, s, -jnp.inf)
            scores.append(s)
        _softmax_steps(scores, [head(v_ref, g, rows) for g in range(G)], m_ref, acc_ref)

    def body(j, carry):
        step(j, False)
        return carry

    lax.fori_loop(0, i, body, 0)
    step(i, True)
    for g in range(G):
        acc = acc_ref[g]
        o_ref[:, g * LANE:(g + 1) * LANE] = (acc[:, :LANE] / acc[:, LANE:]).astype(o_ref.dtype)


def _flash(q_parts, k_parts, v_part, B, S, decay_t=None, tq=512, G=4):
    tq = _tile(S, tq)
    nqk = len(q_parts)
    W = G * LANE
    args, specs = [], []
    for arr, off, _ in q_parts:
        args.append(arr)
        specs.append(pl.BlockSpec((None, tq, W), lambda b, h, i, off=off: (b, i, off // W + h)))
    for arr, off, per_head in k_parts + [v_part]:
        args.append(arr)
        if per_head:
            specs.append(pl.BlockSpec((None, S, W), lambda b, h, i, off=off: (b, 0, off // W + h)))
        else:
            specs.append(pl.BlockSpec((None, S, LANE), lambda b, h, i, off=off: (b, 0, off // LANE)))
    if decay_t is not None:
        args.append(decay_t)
        specs.append(pl.BlockSpec((None, G, 1, S), lambda b, h, i: (b, h, 0, 0)))
    return pl.pallas_call(
        functools.partial(_flash_kernel, nqk=nqk, decay=decay_t is not None, tq=tq, G=G),
        grid=(B, N_HEADS // G, S // tq),
        in_specs=specs,
        out_specs=pl.BlockSpec((None, tq, W), lambda b, h, i: (b, i, h)),
        out_shape=jax.ShapeDtypeStruct((B, S, MIX_WIDTH), BF16),
        scratch_shapes=[pltpu.VMEM((G, tq, LANE), F32), pltpu.VMEM((G, tq, 2 * HEAD_DIM), F32)],
        compiler_params=_params("parallel", "parallel", "arbitrary"),
        name="flash",
    )(*args)


def _gate_kernel(fg_ref, bf_ref, cumt_ref, carry_ref, *, ts):
    @pl.when(pl.program_id(1) == 0)
    def _():
        carry_ref[...] = jnp.zeros_like(carry_ref)

    z = fg_ref[...] + bf_ref[...]
    lf = jnp.minimum(z, 0.0) - jnp.log1p(jnp.exp(-jnp.abs(z)))
    hi = lf.astype(BF16)
    r1 = lf - hi.astype(F32)
    mid = r1.astype(BF16)
    lo = (r1 - mid.astype(F32)).astype(BF16)
    row = lax.broadcasted_iota(jnp.int32, (ts, ts), 0)
    col = lax.broadcasted_iota(jnp.int32, (ts, ts), 1)
    tri = jnp.where(col <= row, 1.0, 0.0).astype(BF16)
    cum = (jnp.dot(tri, hi, preferred_element_type=F32)
           + jnp.dot(tri, mid, preferred_element_type=F32)
           + jnp.dot(tri, lo, preferred_element_type=F32)) + carry_ref[...]
    carry_ref[...] = cum[ts - 1:ts, :]
    cumt_ref[...] = (cum * LOG2E).T


def _fox_gates(fg, b_f):
    B, S, _ = fg.shape
    ts = _tile(S, 512)
    bf = jnp.zeros((1, LANE), F32).at[0, :N_HEADS].set(b_f.astype(F32))
    return pl.pallas_call(
        functools.partial(_gate_kernel, ts=ts),
        grid=(B, S // ts),
        in_specs=[pl.BlockSpec((None, ts, LANE), lambda b, i: (b, i, 0)),
                  pl.BlockSpec((1, LANE), lambda b, i: (0, 0))],
        out_specs=pl.BlockSpec((None, LANE, ts), lambda b, i: (b, 0, i)),
        out_shape=jax.ShapeDtypeStruct((B, LANE, S), F32),
        scratch_shapes=[pltpu.VMEM((1, LANE), F32)],
        compiler_params=_params("parallel", "arbitrary"),
        name="fox_gates",
    )(fg, bf)


def _rope_kernel(x_ref, cos_ref, sin_ref, g_ref, o_ref):
    cos, sin, g = cos_ref[...], sin_ref[...], g_ref[...]
    for c in range(x_ref.shape[1] // LANE):
        sl = slice(c * LANE, (c + 1) * LANE)
        x = x_ref[:, sl]
        ms = jnp.sum(x * x, axis=-1, keepdims=True) * (1.0 / ROPE_DIM)
        y = x * lax.rsqrt(ms + RMS_EPS) * g
        partner = pltpu.roll(y, ROPE_DIM // 2, 1) + pltpu.roll(y, LANE - ROPE_DIM // 2, 1)
        o_ref[:, sl] = (y * cos + partner * sin).astype(o_ref.dtype)


def _rope(x, cos, sin, gain):
    B, S, N = x.shape
    ts = _tile(S, 512)
    g = jnp.zeros((1, LANE), F32).at[0, :ROPE_DIM].set(gain.astype(F32))
    return pl.pallas_call(
        _rope_kernel,
        grid=(B, S // ts),
        in_specs=[pl.BlockSpec((None, ts, N), lambda b, i: (b, i, 0)),
                  pl.BlockSpec((None, ts, LANE), lambda b, i: (b, i, 0)),
                  pl.BlockSpec((None, ts, LANE), lambda b, i: (b, i, 0)),
                  pl.BlockSpec((1, LANE), lambda b, i: (0, 0))],
        out_specs=pl.BlockSpec((None, ts, N), lambda b, i: (b, i, 0)),
        out_shape=jax.ShapeDtypeStruct((B, S, N), BF16),
        compiler_params=_params("parallel", "parallel"),
        name="rope",
    )(x, cos, sin, g)


def _mem_kernel(q_ref, kv_ref, o_ref):
    for h in range(MEM_HEADS):
        q = q_ref[:, h * LANE:(h + 1) * LANE]
        k = kv_ref[:, h * LANE:(h + 1) * LANE]
        v = kv_ref[:, MEM_WIDTH + h * LANE:MEM_WIDTH + (h + 1) * LANE]
        s = lax.dot_general(q, k, (((1,), (1,)), ((), ())), preferred_element_type=F32)
        p = jnp.exp(s - jnp.max(s, axis=-1, keepdims=True))
        o = jnp.dot(p.astype(BF16), v, preferred_element_type=F32)
        o_ref[:, h * LANE:(h + 1) * LANE] = (o / jnp.sum(p, axis=-1, keepdims=True)).astype(o_ref.dtype)


def _mem_attention(qarr, qblock, mem_kv):
    B, S, _ = qarr.shape
    n_mem = mem_kv.shape[1]
    tq = _tile(S, 1024)
    return pl.pallas_call(
        _mem_kernel,
        grid=(B, S // tq),
        in_specs=[pl.BlockSpec((None, tq, MEM_WIDTH), lambda b, i: (b, i, qblock)),
                  pl.BlockSpec((None, n_mem, 2 * MEM_WIDTH), lambda b, i: (b, 0, 0))],
        out_specs=pl.BlockSpec((None, tq, MEM_WIDTH), lambda b, i: (b, i, 0)),
        out_shape=jax.ShapeDtypeStruct((B, S, MEM_WIDTH), BF16),
        compiler_params=_params("parallel", "parallel"),
        name="mem_attn",
    )(qarr, mem_kv)


def _dil_kernel(q_ref, kp_ref, kc_ref, vp_ref, vc_ref, bias_ref, o_ref, lse_ref):
    n = DIL_BLOCK
    has_prev = pl.program_id(1) > 0
    row = lax.broadcasted_iota(jnp.int32, (n, 2 * n), 0)
    col = lax.broadcasted_iota(jnp.int32, (n, 2 * n), 1)
    valid = jnp.logical_and(jnp.logical_and(col >= row, col <= row + n),
                            jnp.logical_or(col >= n, has_prev))
    lane = lax.broadcasted_iota(jnp.int32, (n, LANE), 1)
    dn = (((1,), (1,)), ((), ()))
    heads = [slice(h * LANE, (h + 1) * LANE) for h in range(N_HEADS)]
    scores = []
    for h, sl in enumerate(heads):
        k2 = jnp.concatenate([kp_ref[:, sl], kc_ref[:, sl]], axis=0)
        s = lax.dot_general(q_ref[:, sl], k2, dn, preferred_element_type=F32) + bias_ref[h]
        scores.append(jnp.where(valid, s, -jnp.inf))
    ms = [jnp.max(s, axis=-1, keepdims=True) for s in scores]
    ps = [jnp.exp2(s - m).astype(BF16) for s, m in zip(scores, ms)]
    lse_all = jnp.zeros((n, LANE), F32)
    for h, sl in enumerate(heads):
        v2 = jnp.concatenate([vp_ref[:, sl], vc_ref[:, sl]], axis=0)
        pv = jnp.dot(ps[h], jnp.concatenate([v2, jnp.ones_like(v2)], axis=-1),
                     preferred_element_type=F32)
        den = pv[:, LANE:]
        o_ref[h] = pv[:, :LANE] / den
        lse_all = jnp.where(lane == h, ms[h] + jnp.log2(den), lse_all)
    lse_ref[...] = lse_all


def _dil_attention(pg, bias):
    N, Ls, _ = pg.shape
    n = DIL_BLOCK
    blk = lambda part, prev: pl.BlockSpec(
        (None, n, MIX_WIDTH),
        (lambda s, i: (s, jnp.maximum(i - 1, 0), part)) if prev else (lambda s, i: (s, i, part)))
    return pl.pallas_call(
        _dil_kernel,
        grid=(N, Ls // n),
        in_specs=[blk(0, False), blk(1, True), blk(1, False), blk(2, True), blk(2, False),
                  pl.BlockSpec((N_HEADS, n, 2 * n), lambda s, i: (0, 0, 0))],
        out_specs=[pl.BlockSpec((None, N_HEADS, n, LANE), lambda s, i: (s, 0, i, 0)),
                   pl.BlockSpec((None, n, LANE), lambda s, i: (s, i, 0))],
        out_shape=[jax.ShapeDtypeStruct((N, N_HEADS, Ls, LANE), F32),
                   jax.ShapeDtypeStruct((N, Ls, LANE), F32)],
        compiler_params=_params("parallel", "arbitrary"),
        name="dil_attn",
    )(pg, pg, pg, pg, pg, bias)


def _dil_combine_kernel(o1_ref, o2_ref, o3_ref, l1_ref, l2_ref, l3_ref, out_ref, stage_ref, *, d2, d3):
    n = l3_ref.shape[1]
    q = d3 // d2
    for r in range(d3):
        rows = [(pl.ds(r, n, stride=d3),), (r % d2, pl.ds(r // d2, n, stride=q)), (r, slice(None))]
        ls = [l_ref[idx + (slice(None),)]
              for l_ref, idx in zip((l1_ref, l2_ref, l3_ref), rows)]
        m = functools.reduce(jnp.maximum, ls)
        es = [jnp.exp2(l - m) for l in ls]
        inv = 1.0 / functools.reduce(jnp.add, es)
        ws = [e * inv for e in es]
        for h in range(N_HEADS):
            o = (ws[0][:, h:h + 1] * o1_ref[h, pl.ds(r, n, stride=d3), :]
                 + ws[1][:, h:h + 1] * o2_ref[r % d2, h, pl.ds(r // d2, n, stride=q), :]
                 + ws[2][:, h:h + 1] * o3_ref[r, h, :, :])
            stage_ref[h, pl.ds(r, n, stride=d3), :] = o
    for h in range(N_HEADS):
        out_ref[:, h * LANE:(h + 1) * LANE] = stage_ref[h].astype(out_ref.dtype)


def _dil_combine(outs, lses, B, S, tm=256):
    (_, d1), (_, d2), (_, d3) = DIL_GROUPS
    assert d1 == 1 and d3 % d2 == 0
    tm = _tile(S, tm)
    H = N_HEADS
    o_specs = [pl.BlockSpec((None, H, tm, LANE), lambda b, i: (b, 0, i, 0)),
               pl.BlockSpec((None, d2, H, tm // d2, LANE), lambda b, i: (b, 0, 0, i, 0)),
               pl.BlockSpec((None, d3, H, tm // d3, LANE), lambda b, i: (b, 0, 0, i, 0))]
    l_specs = [pl.BlockSpec((None, tm, LANE), lambda b, i: (b, i, 0)),
               pl.BlockSpec((None, d2, tm // d2, LANE), lambda b, i: (b, 0, i, 0)),
               pl.BlockSpec((None, d3, tm // d3, LANE), lambda b, i: (b, 0, i, 0))]
    return pl.pallas_call(
        functools.partial(_dil_combine_kernel, d2=d2, d3=d3),
        grid=(B, S // tm),
        in_specs=o_specs + l_specs,
        out_specs=pl.BlockSpec((None, tm, MIX_WIDTH), lambda b, i: (b, i, 0)),
        out_shape=jax.ShapeDtypeStruct((B, S, MIX_WIDTH), BF16),
        scratch_shapes=[pltpu.VMEM((H, tm, LANE), F32)],
        compiler_params=_params("parallel", "parallel"),
        name="dil_combine",
    )(outs[0].reshape(B, H, S, LANE), outs[1].reshape(B, d2, H, S // d2, LANE),
      outs[2].reshape(B, d3, H, S // d3, LANE),
      lses[0].reshape(B, S, LANE), lses[1].reshape(B, d2, S // d2, LANE),
      lses[2].reshape(B, d3, S // d3, LANE))


def _dsa_select_kernel(qi_ref, wi_ref, ki_ref, mask_ref, key_ref, jsel_ref, *, tq, tk, S, n_sel):
    i = pl.program_id(1)
    q0 = i * tq
    n_chunks = (q0 + tq + tk - 1) // tk
    dn = (((1,), (1,)), ((), ()))
    rowg = q0 + lax.broadcasted_iota(jnp.int32, (tq, tk), 0)
    colb = lax.broadcasted_iota(jnp.int32, (tq, tk), 1)

    wi = wi_ref[...] * (IDX_HEADS ** -0.5 * IDX_DIM ** -0.5)

    def idx_body(c, carry):
        off = pl.multiple_of(c * tk, tk)
        kic = ki_ref[pl.ds(off, tk), :].astype(BF16)
        acc = jnp.zeros((tq, tk), F32)
        for h in range(IDX_HEADS):
            d = lax.dot_general(qi_ref[:, h * LANE:(h + 1) * LANE], kic, dn,
                                preferred_element_type=F32)
            acc = acc + wi[:, IDX_DIM + h:IDX_DIM + h + 1] * jnp.maximum(d, 0.0)
        score = jnp.where(off + colb <= rowg, acc, -jnp.inf)
        bits = pltpu.bitcast(score, jnp.int32)
        key_ref[:, pl.ds(off, tk)] = bits ^ ((bits >> 31) & jnp.int32(0x7FFFFFFF))
        return carry

    lax.fori_loop(0, n_chunks, idx_body, 0)

    nb = tq // LANE
    blocks = [slice(rb * LANE, (rb + 1) * LANE) for rb in range(nb)]
    colk = lax.broadcasted_iota(jnp.int32, (LANE, tk), 1)
    rowk = [q0 + rb * LANE + lax.broadcasted_iota(jnp.int32, (LANE, tk), 0) for rb in range(nb)]

    def count(pred):
        out = []
        for rb in range(nb):
            def body(c, acc, rb=rb):
                off = pl.multiple_of(c * tk, tk)
                ones = jnp.where(pred(key_ref[blocks[rb], pl.ds(off, tk)], off, rb), 1.0, 0.0)
                for w in range(tk // LANE):
                    acc = acc + ones[:, w * LANE:(w + 1) * LANE]
                return acc

            acc = lax.fori_loop(0, n_chunks, body, jnp.zeros((LANE, LANE), F32))
            out.append(jnp.sum(acc, axis=-1, keepdims=True))
        return out

    k_sel = float(n_sel)
    lo = tuple(jnp.where(c >= k_sel, jnp.int32(0), jnp.int32(-2 ** 31))
               for c in count(lambda k, off, rb: k >= 0))

    def bis_body(it, lo):
        cand = [l + (jnp.int32(1) << (30 - it)) for l in lo]
        cnt = count(lambda k, off, rb: k >= cand[rb])
        return tuple(jnp.where(c >= k_sel, cd, l) for c, cd, l in zip(cnt, cand, lo))

    thr = lax.fori_loop(0, 31, bis_body, lo)
    n_ge = count(lambda k, off, rb: k >= thr[rb])
    need = [k_sel - c for c in count(lambda k, off, rb: k > thr[rb])]
    n_bits = max(1, (S - 1).bit_length())
    jsel_ref[...] = jnp.full_like(jsel_ref, S)
    most = functools.reduce(jnp.maximum, [jnp.max(c) for c in n_ge])

    @pl.when(most > k_sel)
    def _():
        def tie_body(it, jlo):
            cand = [j + (jnp.int32(1) << (n_bits - 1 - it)) for j in jlo]
            below = count(lambda k, off, rb: jnp.logical_and(k == thr[rb], off + colk < cand[rb]))
            return tuple(jnp.where(b < n, cd, j) for b, n, cd, j in zip(below, need, cand, jlo))

        jsel = lax.fori_loop(0, n_bits, tie_body,
                             tuple(jnp.zeros((LANE, 1), jnp.int32) for _ in range(nb)))
        for rb in range(nb):
            jsel_ref[blocks[rb], :] = jnp.broadcast_to(jsel[rb], (LANE, LANE))

    mask_ref[...] = jnp.full_like(mask_ref, -jnp.inf)

    def mask_body(c, carry):
        off = pl.multiple_of(c * tk, tk)
        col = off + colk
        for rb in range(nb):
            k = key_ref[blocks[rb], pl.ds(off, tk)]
            tie = jnp.logical_and(k == thr[rb], col <= jsel_ref[blocks[rb], :1])
            sel = jnp.logical_and(jnp.logical_or(k > thr[rb], tie), col <= rowk[rb])
            mask_ref[blocks[rb], pl.ds(off, tk)] = jnp.where(sel, 0.0, -jnp.inf).astype(mask_ref.dtype)
        return carry

    lax.fori_loop(0, n_chunks, mask_body, 0)


def _dsa_select(pm, aux, B, S, n_sel, tq=256, tk=512):
    tq, tk = _tile(S, tq), _tile(S, tk)
    return pl.pallas_call(
        functools.partial(_dsa_select_kernel, tq=tq, tk=tk, S=S, n_sel=n_sel),
        grid=(B, S // tq),
        in_specs=[pl.BlockSpec((None, tq, MIX_WIDTH), lambda b, i: (b, i, 1)),
                  pl.BlockSpec((None, tq, LANE), lambda b, i: (b, i, 0)),
                  pl.BlockSpec((None, S, LANE), lambda b, i: (b, 0, 0))],
        out_specs=pl.BlockSpec((None, tq, S), lambda b, i: (b, i, 0)),
        out_shape=jax.ShapeDtypeStruct((B, S, S), BF16),
        scratch_shapes=[pltpu.VMEM((tq, S), jnp.int32), pltpu.VMEM((tq, LANE), jnp.int32)],
        compiler_params=_params("parallel", "arbitrary"),
        name="dsa_select",
    )(pm, aux, aux)


def _dsa_attn_kernel(q_ref, k_ref, v_ref, mask_ref, gt_ref, o_ref, m_ref, acc_ref, *, tq, tk, S):
    i = pl.program_id(2)
    q0 = i * tq
    n_chunks = (q0 + tq + tk - 1) // tk
    R = q_ref.shape[1] // LANE
    nblk = tq // LANE
    dn = (((1,), (1,)), ((), ()))
    qs = [q_ref[:, r * LANE:(r + 1) * LANE] for r in range(R)]
    m_ref[...] = jnp.full_like(m_ref, NEG_BIG)
    acc_ref[...] = jnp.zeros_like(acc_ref)

    def body(c, carry):
        off = pl.multiple_of(c * tk, tk)
        kc = k_ref[pl.ds(off, tk), :]
        vc = v_ref[pl.ds(off, tk), :]
        mask = mask_ref[:, pl.ds(off, tk)].astype(F32)
        st = S - q0 + off
        scores = []
        for r in range(R):
            bias = jnp.concatenate(
                [gt_ref[r, :, pl.ds(pl.multiple_of(st + (nblk - 1 - a) * LANE, LANE), tk)]
                 for a in range(nblk)], axis=0)
            scores.append(lax.dot_general(qs[r], kc, dn, preferred_element_type=F32) + bias + mask)
        _softmax_steps(scores, [vc] * R, m_ref, acc_ref)
        return carry

    lax.fori_loop(0, n_chunks, body, 0)
    for r in range(R):
        acc = acc_ref[r]
        o_ref[:, r * LANE:(r + 1) * LANE] = (acc[:, :LANE] / acc[:, LANE:]).astype(o_ref.dtype)


def _dsa_attention(pm, mask, gt, B, S, tq=256, tk=512):
    tq, tk = _tile(S, tq), _tile(S, tk)
    R = N_HEADS // DSA_KV_HEADS
    kblk = 2 * MIX_WIDTH // LANE
    return pl.pallas_call(
        functools.partial(_dsa_attn_kernel, tq=tq, tk=tk, S=S),
        grid=(DSA_KV_HEADS, B, S // tq),
        in_specs=[pl.BlockSpec((None, tq, R * LANE), lambda g, b, i: (b, i, g)),
                  pl.BlockSpec((None, S, LANE), lambda g, b, i: (b, 0, kblk + g)),
                  pl.BlockSpec((None, S, LANE), lambda g, b, i: (b, 0, kblk + DSA_KV_HEADS + g)),
                  pl.BlockSpec((None, tq, S), lambda g, b, i: (b, i, 0)),
                  pl.BlockSpec((R, LANE, gt.shape[2]), lambda g, b, i: (g, 0, 0))],
        out_specs=pl.BlockSpec((None, tq, R * LANE), lambda g, b, i: (b, i, g)),
        out_shape=jax.ShapeDtypeStruct((B, S, MIX_WIDTH), BF16),
        scratch_shapes=[pltpu.VMEM((R, tq, LANE), F32), pltpu.VMEM((R, tq, 2 * HEAD_DIM), F32)],
        compiler_params=_params("parallel", "parallel", "arbitrary"),
        name="dsa_attn",
    )(pm, pm, pm, mask, gt)


def _t5_bucket(dist):
    n = jnp.maximum(dist, 0)
    exact = REL_BUCKETS // 2
    nf = jnp.maximum(n, 1).astype(F32)
    large = exact + (jnp.log(nf / exact) / math.log(REL_MAX_DIST / exact)
                     * (REL_BUCKETS - exact)).astype(jnp.int32)
    large = jnp.minimum(large, REL_BUCKETS - 1)
    return jnp.where(n < exact, n, large)


def _t5_bias(t5_table, dist):
    onehot = jax.nn.one_hot(_t5_bucket(dist), REL_BUCKETS, dtype=F32)
    table = jnp.einsum('...b,bh->h...', onehot, t5_table.astype(F32), precision=lax.Precision.HIGHEST)
    return table * LOG2E


def _dil_bias(t5_table, dil):
    n = DIL_BLOCK
    rel = jnp.arange(n)[:, None] + n - jnp.arange(2 * n)[None, :]
    return _t5_bias(t5_table, rel * dil)


def _dsa_bias_table(t5_table, S, tq, tk):
    lt = S + tq + tk
    tb = _t5_bias(t5_table, S + tq - 1 - jnp.arange(lt + LANE))
    return jnp.stack([tb[:, LANE - 1 - i:LANE - 1 - i + lt] for i in range(LANE)], axis=1)


def _headnorm_cols(spec):
    eg, ef = [], []
    for gain, count, scale in spec:
        if gain is None:
            eg.append(jnp.ones((count * LANE,), F32))
            ef.append(jnp.zeros((count * LANE,), F32))
        else:
            eg.append(jnp.tile(gain.astype(F32) * scale, count))
            ef.append(jnp.ones((count * LANE,), F32))
    return jnp.concatenate(eg), jnp.concatenate(ef)


def _pad_cols(w, width):
    return jnp.pad(w, ((0, 0), (0, width - w.shape[1])))


def _fox_layer(x, norm_g, w_in, b_f, qk_g, mem_g):
    B, S, _ = x.shape
    W = MIX_WIDTH
    w_main = jnp.concatenate([w_in[:, :3 * W], w_in[:, 3 * W + N_HEADS:]], axis=1).astype(BF16)
    w_gate = _pad_cols(w_in[:, 3 * W:3 * W + N_HEADS], LANE).astype(BF16)
    eg, ef = _headnorm_cols([(qk_g[0], N_HEADS, HEAD_DIM ** -0.5 * LOG2E), (qk_g[1], N_HEADS, 1.0),
                             (None, N_HEADS, 1.0), (mem_g, MEM_HEADS, HEAD_DIM ** -0.5)])
    pm = _proj(x, norm_g, w_main, eg, ef)
    fg = _proj(x, norm_g, w_gate, out_dtype=F32)
    cum_t = _fox_gates(fg, b_f)
    mix = _flash([(pm, 0, True)], [(pm, W, True)], (pm, 2 * W, True), B, S,
                 decay_t=cum_t.reshape(B, LANE, 1, S))
    return mix, pm, 3 * W // MEM_WIDTH


def _mla_layer(x, positions, norm_g, w_in, q_norm, w_uq, kv_norm, w_ukv, nope_g, rope_g, mem_g):
    B, S, _ = x.shape
    scale = (NOPE_DIM + ROPE_DIM) ** -0.5 * LOG2E
    lat = Q_LORA + KV_LORA
    w_lat = jnp.concatenate([w_in[:, :lat], _pad_cols(w_in[:, lat:lat + ROPE_DIM], LANE)],
                            axis=1).astype(BF16)
    w_mem = w_in[:, lat + ROPE_DIM:].astype(BF16)
    pl_ = _proj(x, norm_g, w_lat, out_dtype=F32, tn=lat + LANE)
    eg, ef = _headnorm_cols([(mem_g, MEM_HEADS, HEAD_DIM ** -0.5)])
    pmem = _proj(x, norm_g, w_mem, eg, ef)

    uq = w_uq.reshape(Q_LORA, N_HEADS, NOPE_DIM + ROPE_DIM)
    w_qn = uq[:, :, :NOPE_DIM].reshape(Q_LORA, MIX_WIDTH).astype(BF16)
    w_qr = jnp.pad(uq[:, :, NOPE_DIM:], ((0, 0), (0, 0), (0, LANE - ROPE_DIM))
                   ).reshape(Q_LORA, N_HEADS * LANE).astype(BF16)
    ukv = w_ukv.reshape(KV_LORA, N_HEADS, NOPE_DIM + HEAD_DIM)
    w_kv = jnp.concatenate([ukv[:, :, :NOPE_DIM].reshape(KV_LORA, MIX_WIDTH),
                            ukv[:, :, NOPE_DIM:].reshape(KV_LORA, MIX_WIDTH)], axis=1).astype(BF16)
    eg, ef = _headnorm_cols([(nope_g[0], N_HEADS, scale)])
    qn = _proj(pl_, q_norm, w_qn, eg, ef, kblock=0)
    qr_raw = _proj(pl_, q_norm, w_qr, kblock=0, out_dtype=F32)
    eg, ef = _headnorm_cols([(nope_g[1], N_HEADS, 1.0), (None, N_HEADS, 1.0)])
    kv = _proj(pl_, kv_norm, w_kv, eg, ef, kblock=1)

    half = ROPE_DIM // 2
    inv = ROPE_THETA ** (-jnp.arange(half, dtype=F32) / half)
    ang = positions.astype(F32)[..., None] * inv
    cos, sin = jnp.cos(ang), jnp.sin(ang)
    zero = jnp.zeros((B, S, LANE - ROPE_DIM), F32)
    cos_t = jnp.concatenate([cos, cos, zero], axis=-1)
    sin_t = jnp.concatenate([-sin, sin, zero], axis=-1)
    qr = _rope(qr_raw, cos_t, sin_t, rope_g[0] * scale)
    kr = _rope(pl_[:, :, lat:], cos_t, sin_t, rope_g[1])
    mix = _flash([(qn, 0, True), (qr, 0, True)], [(kv, 0, True), (kr, 0, False)],
                 (kv, MIX_WIDTH, True), B, S)
    return mix, pmem, 0


def _dil_layer(x, norm_g, w_in, qk_g, t5_table, mem_g):
    B, S, _ = x.shape
    W = MIX_WIDTH
    outs, lses = [], []
    for gi, (win, dil) in enumerate(DIL_GROUPS):
        assert win // dil == DIL_BLOCK and (S // dil) % DIL_BLOCK == 0
        w_g = w_in[:, gi * 3 * W:(gi + 1) * 3 * W].astype(BF16)
        eg, ef = _headnorm_cols([(qk_g[gi, 0], N_HEADS, HEAD_DIM ** -0.5 * LOG2E),
                                 (qk_g[gi, 1], N_HEADS, 1.0), (None, N_HEADS, 1.0)])
        pg = _proj(x, norm_g, w_g, eg, ef, dil=dil, tn=1024)
        o, lse = _dil_attention(pg.reshape(B * dil, S // dil, 3 * W), _dil_bias(t5_table, dil))
        outs.append(o)
        lses.append(lse)
    mix = _dil_combine(outs, lses, B, S)
    eg, ef = _headnorm_cols([(mem_g, MEM_HEADS, HEAD_DIM ** -0.5)])
    pmem = _proj(x, norm_g, w_in[:, len(DIL_GROUPS) * 3 * W:].astype(BF16), eg, ef)
    return mix, pmem, 0


def _dsa_layer(x, norm_g, w_in, qk_g, t5_table, mem_g):
    B, S, _ = x.shape
    W = MIX_WIDTH
    kvw = DSA_KV_HEADS * HEAD_DIM
    o_k, o_v, o_qi = W, W + kvw, W + 2 * kvw
    o_ki = o_qi + IDX_HEADS * IDX_DIM
    o_wi = o_ki + IDX_DIM
    o_mem = o_wi + IDX_HEADS
    w_qi = jnp.pad(w_in[:, o_qi:o_ki].reshape(D_MODEL, IDX_HEADS, IDX_DIM),
                   ((0, 0), (0, 0), (0, LANE - IDX_DIM))).reshape(D_MODEL, IDX_HEADS * LANE)
    w_main = jnp.concatenate([w_in[:, :W], w_qi, w_in[:, o_k:o_qi], w_in[:, o_mem:]],
                             axis=1).astype(BF16)
    w_aux = _pad_cols(w_in[:, o_ki:o_mem], LANE).astype(BF16)
    eg, ef = _headnorm_cols([(qk_g[0], N_HEADS, HEAD_DIM ** -0.5 * LOG2E), (None, IDX_HEADS, 1.0),
                             (qk_g[1], DSA_KV_HEADS, 1.0), (None, DSA_KV_HEADS, 1.0),
                             (mem_g, MEM_HEADS, HEAD_DIM ** -0.5)])
    pm = _proj(x, norm_g, w_main, eg, ef)
    aux = _proj(x, norm_g, w_aux, out_dtype=F32)
    mask = _dsa_select(pm, aux, B, S, min(TOPK_MAX, S // 4))
    tq, tk = _tile(S, 512), _tile(S, 512)
    mix = _dsa_attention(pm, mask, _dsa_bias_table(t5_table, S, tq, tk), B, S, tq, tk)
    return mix, pm, (2 * W + 2 * kvw) // MEM_WIDTH


def kernel(x, mem, positions, t5_table, ffn_norm, ffn_w_gate, ffn_w_up, ffn_w_down, attn_norm,
           mem_norm, mem_w_kv, mem_qk_g, w_out, a_w_in, a_b_f, a_qk_g, b_w_in, b_q_norm, b_w_uq,
           b_kv_norm, b_w_ukv, b_nope_g, b_rope_g, c_w_in, c_qk_g, d_w_in, d_qk_g):
    B, S, D = x.shape
    depth = ffn_norm.shape[0]
    n_mixers = 4

    def ffn(xc, i, k, next_gain=None):
        return _ffn(xc.reshape(B * S, D), ffn_norm[i, k], ffn_w_gate[i, k].astype(BF16),
                    ffn_w_up[i, k].astype(BF16), ffn_w_down[i, k].astype(BF16), next_gain)

    for i in range(depth):
        m, j = i % n_mixers, i // n_mixers
        x, h = ffn(x, i, 0, attn_norm[i])
        x, h = x.reshape(B, S, D), h.reshape(B, S, D)
        mem_g = mem_qk_g[i]
        if m == 0:
            mix, qarr, qblock = _fox_layer(h, None, a_w_in[j], a_b_f[j], a_qk_g[j], mem_g[0])
        elif m == 1:
            mix, qarr, qblock = _mla_layer(h, positions, None, b_w_in[j], b_q_norm[j],
                                           b_w_uq[j], b_kv_norm[j], b_w_ukv[j], b_nope_g[j],
                                           b_rope_g[j], mem_g[0])
        elif m == 2:
            mix, qarr, qblock = _dil_layer(h, None, c_w_in[j], c_qk_g[j], t5_table, mem_g[0])
        else:
            mix, qarr, qblock = _dsa_layer(h, None, d_w_in[j], d_qk_g[j], t5_table, mem_g[0])
        eg, ef = _headnorm_cols([(mem_g[1], MEM_HEADS, 1.0), (None, MEM_HEADS, 1.0)])
        mem_kv = _proj(mem, mem_norm[i], mem_w_kv[i].astype(BF16), eg, ef)
        mo = _mem_attention(qarr, qblock, mem_kv)
        wo = w_out[i].astype(BF16)
        x = _out_proj(x.reshape(B * S, D), mix.reshape(B * S, MIX_WIDTH),
                      mo.reshape(B * S, MEM_WIDTH), wo[:MIX_WIDTH], wo[MIX_WIDTH:]).reshape(B, S, D)
        x = ffn(x, i, 1).reshape(B, S, D)
    return x
```

```python
import functools
import math

import jax
import jax.numpy as jnp
from jax import lax
from jax.experimental import pallas as pl
from jax.experimental.pallas import tpu as pltpu

F32 = jnp.float32
BF16 = jnp.bfloat16

LANE = 128
D_MODEL = 2048
N_HEADS = 16
HEAD_DIM = 128
MIX_WIDTH = N_HEADS * HEAD_DIM
MEM_HEADS = 4
MEM_WIDTH = MEM_HEADS * HEAD_DIM
D_FF = 5632
RMS_EPS = 1e-6
REL_BUCKETS = 32
REL_MAX_DIST = 2048
Q_LORA = 512
KV_LORA = 512
NOPE_DIM = 128
ROPE_DIM = 64
ROPE_THETA = 10000.0
DIL_GROUPS = ((128, 1), (512, 4), (2048, 16))
DIL_BLOCK = 128
DSA_KV_HEADS = 4
IDX_HEADS = 16
IDX_DIM = 64
TOPK_MAX = 256
VMEM_LIMIT = 56 * 1024 * 1024
FFN_VMEM_LIMIT = 62 * 1024 * 1024
NEG_BIG = -1e30
LOG2E = math.log2(math.e)


def _params(*sem):
    return pltpu.CompilerParams(dimension_semantics=sem, vmem_limit_bytes=VMEM_LIMIT)


def _tile(n, pref):
    t = min(n, pref)
    while n % t:
        t //= 2
    return t


def _ffn_kernel(x_ref, g_ref, wg_ref, wu_ref, wd_ref, o_ref, h_ref):
    _ffn_body(x_ref, g_ref, wg_ref, wu_ref, wd_ref, o_ref, h_ref, ())


def _ffn_kernel_with_next(x_ref, g_ref, wg_ref, wu_ref, wd_ref, ng_ref, o_ref, hn_ref, h_ref):
    _ffn_body(x_ref, g_ref, wg_ref, wu_ref, wd_ref, o_ref, h_ref, (ng_ref, hn_ref))


def _ffn_body(x_ref, g_ref, wg_ref, wu_ref, wd_ref, o_ref, h_ref, next_refs):
    j = pl.program_id(1)
    last = pl.num_programs(1) - 1
    tm = x_ref.shape[0]
    halves = [slice(0, tm // 2), slice(tm // 2, tm)]

    def down(h):
        g = jnp.dot(h, wg_ref[...], preferred_element_type=F32)
        u = jnp.dot(h, wu_ref[...], preferred_element_type=F32)
        a = (g * jax.nn.sigmoid(g) * u).astype(BF16)
        return jnp.dot(a, wd_ref[...], preferred_element_type=F32)

    @pl.when(j == 0)
    def _():
        for rows in halves:
            x = x_ref[rows, :]
            ms = jnp.mean(x * x, axis=-1, keepdims=True)
            h = (x * lax.rsqrt(ms + RMS_EPS) * g_ref[...]).astype(BF16)
            h_ref[rows, :] = h
            o_ref[rows, :] = down(h)

    @pl.when(jnp.logical_and(j > 0, j < last))
    def _():
        o_ref[...] += down(h_ref[...])

    @pl.when(jnp.logical_and(j > 0, j == last))
    def _():
        for rows in halves:
            y = x_ref[rows, :] + 0.5 * (o_ref[rows, :] + down(h_ref[rows, :]))
            o_ref[rows, :] = y
            if next_refs:
                ng_ref, hn_ref = next_refs
                ms = jnp.mean(y * y, axis=-1, keepdims=True)
                hn_ref[rows, :] = (y * lax.rsqrt(ms + RMS_EPS) * ng_ref[...]).astype(BF16)


def _ffn(x2, gain, wg, wu, wd, next_gain=None):
    T, D = x2.shape
    F = wg.shape[1]
    tm, tf = _tile(T, 1024), _tile(F, 512)
    assert F // tf >= 2 and tm % 16 == 0
    row_spec = pl.BlockSpec((tm, D), lambda i, j: (i, 0))
    vec_spec = pl.BlockSpec((1, D), lambda i, j: (0, 0))
    in_specs = [row_spec, vec_spec,
                pl.BlockSpec((D, tf), lambda i, j: (0, j)),
                pl.BlockSpec((D, tf), lambda i, j: (0, j)),
                pl.BlockSpec((tf, D), lambda i, j: (j, 0))]
    args = [x2, gain.reshape(1, D), wg, wu, wd]
    out_specs, out_shape = row_spec, jax.ShapeDtypeStruct((T, D), F32)
    kern = _ffn_kernel
    if next_gain is not None:
        in_specs.append(vec_spec)
        args.append(next_gain.reshape(1, D).astype(F32))
        out_specs = [row_spec, row_spec]
        out_shape = [out_shape, jax.ShapeDtypeStruct((T, D), BF16)]
        kern = _ffn_kernel_with_next
    return pl.pallas_call(
        kern,
        grid=(T // tm, F // tf),
        in_specs=in_specs,
        out_specs=out_specs,
        out_shape=out_shape,
        scratch_shapes=[pltpu.VMEM((tm, D), BF16)],
        compiler_params=pltpu.CompilerParams(dimension_semantics=("parallel", "arbitrary"),
                                             vmem_limit_bytes=FFN_VMEM_LIMIT),
        name="ffn",
    )(*args)


def _proj_kernel(x_ref, g_ref, w_ref, eg_ref, ef_ref, o_ref, *scratch, epilogue, dil, prenormed):
    scratch = list(scratch)
    if prenormed:
        h_ref = x_ref
    else:
        h_ref = scratch.pop(0)

        @pl.when(pl.program_id(2) == 0)
        def _():
            x = x_ref[...].astype(F32)
            ms = jnp.mean(x * x, axis=-1, keepdims=True)
            h_ref[...] = (x * lax.rsqrt(ms + RMS_EPS) * g_ref[...]).astype(BF16)

    stage = scratch
    tm = h_ref.shape[0]
    parts = 2 if tm % (2 * 8 * dil) == 0 else 1
    rows_per, n = tm // parts, tm // parts // dil
    for p in range(parts):
        rows = slice(p * rows_per, (p + 1) * rows_per)
        acc = jnp.dot(h_ref[rows, :], w_ref[...], preferred_element_type=F32)
        for c in range(acc.shape[1] // LANE):
            sl = slice(c * LANE, (c + 1) * LANE)
            y = acc[:, sl]
            if epilogue:
                ms = jnp.mean(y * y, axis=-1, keepdims=True)
                scale = jnp.where(ef_ref[:, sl] > 0.0, lax.rsqrt(ms + RMS_EPS), 1.0)
                y = y * scale * eg_ref[:, sl]
            if dil == 1:
                o_ref[0, rows, sl] = y.astype(o_ref.dtype)
            else:
                stage[0][c, rows, :] = y
                for r in range(dil):
                    o_ref[r, p * n:(p + 1) * n, sl] = stage[0][
                        c, pl.ds(p * rows_per + r, n, stride=dil), :].astype(o_ref.dtype)


def _proj(x, gain, w, eg=None, ef=None, *, dil=1, kblock=0, out_dtype=BF16, tm=1024, tn=512):
    B, S, C = x.shape
    K, N = w.shape
    tm, tn = _tile(S, tm), _tile(N, tn)
    epilogue = eg is not None
    if not epilogue:
        eg = jnp.ones((1, N), F32)
        ef = jnp.zeros((1, N), F32)
    prenormed = gain is None
    if prenormed:
        assert x.dtype == BF16
        gain = jnp.ones((K,), F32)
    out = pl.pallas_call(
        functools.partial(_proj_kernel, epilogue=epilogue, dil=dil, prenormed=prenormed),
        grid=(B, S // tm, N // tn),
        in_specs=[
            pl.BlockSpec((None, tm, K), lambda b, i, j: (b, i, kblock)),
            pl.BlockSpec((1, K), lambda b, i, j: (0, 0)),
            pl.BlockSpec((K, tn), lambda b, i, j: (0, j)),
            pl.BlockSpec((1, tn), lambda b, i, j: (0, j)),
            pl.BlockSpec((1, tn), lambda b, i, j: (0, j)),
        ],
        out_specs=pl.BlockSpec((None, dil, tm // dil, tn), lambda b, i, j: (b, 0, i, j)),
        out_shape=jax.ShapeDtypeStruct((B, dil, S // dil, N), out_dtype),
        scratch_shapes=([] if prenormed else [pltpu.VMEM((tm, K), BF16)])
        + ([pltpu.VMEM((tn // LANE, tm, LANE), F32)] if dil > 1 else []),
        compiler_params=_params("parallel", "parallel", "arbitrary"),
        name="proj",
    )(x, gain.reshape(1, K).astype(F32), w, eg.reshape(1, N), ef.reshape(1, N))
    return out.reshape(B, S, N) if dil == 1 else out


def _out_kernel(x_ref, a_ref, b_ref, wa_ref, wb_ref, o_ref):
    acc = jnp.dot(a_ref[...], wa_ref[...], preferred_element_type=F32)
    acc += jnp.dot(b_ref[...], wb_ref[...], preferred_element_type=F32)
    o_ref[...] = x_ref[...] + acc


def _out_proj(x2, mix2, mo2, wa, wb):
    T, D = x2.shape
    tm, tn = _tile(T, 1024), _tile(D, 1024)
    ka, kb = mix2.shape[1], mo2.shape[1]
    return pl.pallas_call(
        _out_kernel,
        grid=(T // tm, D // tn),
        in_specs=[
            pl.BlockSpec((tm, tn), lambda i, j: (i, j)),
            pl.BlockSpec((tm, ka), lambda i, j: (i, 0)),
            pl.BlockSpec((tm, kb), lambda i, j: (i, 0)),
            pl.BlockSpec((ka, tn), lambda i, j: (0, j)),
            pl.BlockSpec((kb, tn), lambda i, j: (0, j)),
        ],
        out_specs=pl.BlockSpec((tm, tn), lambda i, j: (i, j)),
        out_shape=jax.ShapeDtypeStruct((T, D), F32),
        compiler_params=_params("parallel", "arbitrary"),
        name="out_proj",
    )(x2, mix2, mo2, wa, wb)


def _lanes(x, reps):
    return x if reps == 1 else jnp.concatenate([x] * reps, axis=-1)


def _online_softmax(n_chunks, produce, values, s_ref, m_ref, acc_ref, finish=None):
    G = s_ref.shape[1]

    def consume(j, scores):
        ps, alphas = [], []
        for g, s in enumerate(scores):
            m_old = m_ref[g]
            m_new = jnp.maximum(m_old, jnp.max(s, axis=-1, keepdims=True))
            ps.append(jnp.exp2(s - _lanes(m_new, s.shape[1] // LANE)).astype(BF16))
            alphas.append(jnp.exp2(m_old - m_new))
            m_ref[g] = m_new
        return ps, alphas

    def accumulate(j, ps, alphas):
        for g, (p, alpha, v) in enumerate(zip(ps, alphas, values(j))):
            v1 = jnp.concatenate([v, jnp.ones_like(v)], axis=-1)
            acc_ref[g] = _lanes(alpha, 2) * acc_ref[g] + jnp.dot(p, v1, preferred_element_type=F32)

    m_ref[...] = jnp.full_like(m_ref, NEG_BIG)
    acc_ref[...] = jnp.zeros_like(acc_ref)
    for g, s in enumerate(produce(0)):
        s_ref[0, g] = s

    def step(j, slot):
        cur = [s_ref[slot, g] for g in range(G)]
        nxt = produce(j + 1)
        ps, alphas = consume(j, cur)
        for g, s in enumerate(nxt):
            s_ref[1 - slot, g] = s
        accumulate(j, ps, alphas)

    def body(jj, carry):
        step(2 * jj, 0)
        step(2 * jj + 1, 1)
        return carry

    last = n_chunks - 1
    lax.fori_loop(0, last // 2, body, 0)

    @pl.when(last % 2 == 1)
    def _():
        step(last - 1, 0)

    cur = [s_ref[last % 2, g] for g in range(G)]
    if finish is not None:
        cur = [finish(s) for s in cur]
    accumulate(last, *consume(last, cur))


def _flash_kernel(*refs, nqk, decay, tq, G):
    q_refs, k_refs, v_ref = refs[:nqk], refs[nqk:2 * nqk], refs[2 * nqk]
    pos = 2 * nqk + 1
    if decay:
        ck_ref = refs[pos]
        pos += 1
    o_ref, s_ref, m_ref, acc_ref = refs[pos:pos + 4]
    i = pl.program_id(2)

    def head(ref, g, rows=slice(None)):
        if ref.shape[-1] == LANE:
            return ref[rows, :]
        return ref[rows, g * LANE:(g + 1) * LANE]

    qs = []
    for g in range(G):
        parts = [head(r, g) for r in q_refs]
        qs.append(parts[0] if nqk == 1 else jnp.concatenate(parts, axis=-1))

    def chunk(j):
        return pl.ds(pl.multiple_of(j * tq, tq), tq)

    def produce(j):
        rows = chunk(j)
        scores = []
        for g in range(G):
            parts = [head(r, g, rows) for r in k_refs]
            k = parts[0] if nqk == 1 else jnp.concatenate(parts, axis=-1)
            s = lax.dot_general(qs[g], k, (((1,), (1,)), ((), ())), preferred_element_type=F32)
            scores.append(s - ck_ref[g, :, rows] if decay else s)
        return scores

    def values(j):
        return [head(v_ref, g, chunk(j)) for g in range(G)]

    def causal(s):
        row = lax.broadcasted_iota(jnp.int32, (tq, tq), 0)
        col = lax.broadcasted_iota(jnp.int32, (tq, tq), 1)
        return jnp.where(col <= row, s, -jnp.inf)

    _online_softmax(i + 1, produce, values, s_ref, m_ref, acc_ref, finish=causal)
    for g in range(G):
        acc = acc_ref[g]
        o_ref[:, g * LANE:(g + 1) * LANE] = (acc[:, :LANE] / acc[:, LANE:]).astype(o_ref.dtype)


def _flash(q_parts, k_parts, v_part, B, S, decay_t=None, tq=512, G=4):
    tq = _tile(S, tq)
    nqk = len(q_parts)
    W = G * LANE
    args, specs = [], []
    for arr, off, _ in q_parts:
        args.append(arr)
        specs.append(pl.BlockSpec((None, tq, W), lambda b, h, i, off=off: (b, i, off // W + h)))
    for arr, off, per_head in k_parts + [v_part]:
        args.append(arr)
        if per_head:
            specs.append(pl.BlockSpec((None, S, W), lambda b, h, i, off=off: (b, 0, off // W + h)))
        else:
            specs.append(pl.BlockSpec((None, S, LANE), lambda b, h, i, off=off: (b, 0, off // LANE)))
    if decay_t is not None:
        args.append(decay_t)
        specs.append(pl.BlockSpec((None, G, 1, S), lambda b, h, i: (b, h, 0, 0)))
    return pl.pallas_call(
        functools.partial(_flash_kernel, nqk=nqk, decay=decay_t is not None, tq=tq, G=G),
        grid=(B, N_HEADS // G, S // tq),
        in_specs=specs,
        out_specs=pl.BlockSpec((None, tq, W), lambda b, h, i: (b, i, h)),
        out_shape=jax.ShapeDtypeStruct((B, S, MIX_WIDTH), BF16),
        scratch_shapes=[pltpu.VMEM((2, G, tq, tq), F32), pltpu.VMEM((G, tq, LANE), F32),
                        pltpu.VMEM((G, tq, 2 * HEAD_DIM), F32)],
        compiler_params=_params("parallel", "parallel", "arbitrary"),
        name="flash",
    )(*args)


def _gate_kernel(fg_ref, bf_ref, cumt_ref, carry_ref, *, ts):
    @pl.when(pl.program_id(1) == 0)
    def _():
        carry_ref[...] = jnp.zeros_like(carry_ref)

    z = fg_ref[...] + bf_ref[...]
    lf = jnp.minimum(z, 0.0) - jnp.log1p(jnp.exp(-jnp.abs(z)))
    hi = lf.astype(BF16)
    r1 = lf - hi.astype(F32)
    mid = r1.astype(BF16)
    lo = (r1 - mid.astype(F32)).astype(BF16)
    row = lax.broadcasted_iota(jnp.int32, (ts, ts), 0)
    col = lax.broadcasted_iota(jnp.int32, (ts, ts), 1)
    tri = jnp.where(col <= row, 1.0, 0.0).astype(BF16)
    cum = (jnp.dot(tri, hi, preferred_element_type=F32)
           + jnp.dot(tri, mid, preferred_element_type=F32)
           + jnp.dot(tri, lo, preferred_element_type=F32)) + carry_ref[...]
    carry_ref[...] = cum[ts - 1:ts, :]
    cumt_ref[...] = (cum * LOG2E).T


def _fox_gates(fg, b_f):
    B, S, _ = fg.shape
    ts = _tile(S, 512)
    bf = jnp.zeros((1, LANE), F32).at[0, :N_HEADS].set(b_f.astype(F32))
    return pl.pallas_call(
        functools.partial(_gate_kernel, ts=ts),
        grid=(B, S // ts),
        in_specs=[pl.BlockSpec((None, ts, LANE), lambda b, i: (b, i, 0)),
                  pl.BlockSpec((1, LANE), lambda b, i: (0, 0))],
        out_specs=pl.BlockSpec((None, LANE, ts), lambda b, i: (b, 0, i)),
        out_shape=jax.ShapeDtypeStruct((B, LANE, S), F32),
        scratch_shapes=[pltpu.VMEM((1, LANE), F32)],
        compiler_params=_params("parallel", "arbitrary"),
        name="fox_gates",
    )(fg, bf)


def _rope_kernel(x_ref, cos_ref, sin_ref, g_ref, o_ref):
    cos, sin, g = cos_ref[...], sin_ref[...], g_ref[...]
    for c in range(x_ref.shape[1] // LANE):
        sl = slice(c * LANE, (c + 1) * LANE)
        x = x_ref[:, sl]
        ms = jnp.sum(x * x, axis=-1, keepdims=True) * (1.0 / ROPE_DIM)
        y = x * lax.rsqrt(ms + RMS_EPS) * g
        partner = pltpu.roll(y, ROPE_DIM // 2, 1) + pltpu.roll(y, LANE - ROPE_DIM // 2, 1)
        o_ref[:, sl] = (y * cos + partner * sin).astype(o_ref.dtype)


def _rope(x, cos, sin, gain):
    B, S, N = x.shape
    ts = _tile(S, 512)
    g = jnp.zeros((1, LANE), F32).at[0, :ROPE_DIM].set(gain.astype(F32))
    return pl.pallas_call(
        _rope_kernel,
        grid=(B, S // ts),
        in_specs=[pl.BlockSpec((None, ts, N), lambda b, i: (b, i, 0)),
                  pl.BlockSpec((None, ts, LANE), lambda b, i: (b, i, 0)),
                  pl.BlockSpec((None, ts, LANE), lambda b, i: (b, i, 0)),
                  pl.BlockSpec((1, LANE), lambda b, i: (0, 0))],
        out_specs=pl.BlockSpec((None, ts, N), lambda b, i: (b, i, 0)),
        out_shape=jax.ShapeDtypeStruct((B, S, N), BF16),
        compiler_params=_params("parallel", "parallel"),
        name="rope",
    )(x, cos, sin, g)


def _mem_kernel(q_ref, kv_ref, o_ref):
    for h in range(MEM_HEADS):
        q = q_ref[:, h * LANE:(h + 1) * LANE]
        k = kv_ref[:, h * LANE:(h + 1) * LANE]
        v = kv_ref[:, MEM_WIDTH + h * LANE:MEM_WIDTH + (h + 1) * LANE]
        s = lax.dot_general(q, k, (((1,), (1,)), ((), ())), preferred_element_type=F32)
        p = jnp.exp(s - jnp.max(s, axis=-1, keepdims=True))
        o = jnp.dot(p.astype(BF16), v, preferred_element_type=F32)
        o_ref[:, h * LANE:(h + 1) * LANE] = (o / jnp.sum(p, axis=-1, keepdims=True)).astype(o_ref.dtype)


def _mem_attention(qarr, qblock, mem_kv):
    B, S, _ = qarr.shape
    n_mem = mem_kv.shape[1]
    tq = _tile(S, 1024)
    return pl.pallas_call(
        _mem_kernel,
        grid=(B, S // tq),
        in_specs=[pl.BlockSpec((None, tq, MEM_WIDTH), lambda b, i: (b, i, qblock)),
                  pl.BlockSpec((None, n_mem, 2 * MEM_WIDTH), lambda b, i: (b, 0, 0))],
        out_specs=pl.BlockSpec((None, tq, MEM_WIDTH), lambda b, i: (b, i, 0)),
        out_shape=jax.ShapeDtypeStruct((B, S, MEM_WIDTH), BF16),
        compiler_params=_params("parallel", "parallel"),
        name="mem_attn",
    )(qarr, mem_kv)


def _dil_kernel(q_ref, kp_ref, kc_ref, vp_ref, vc_ref, bias_ref, o_ref, lse_ref):
    n = DIL_BLOCK
    has_prev = pl.program_id(1) > 0
    row = lax.broadcasted_iota(jnp.int32, (n, 2 * n), 0)
    col = lax.broadcasted_iota(jnp.int32, (n, 2 * n), 1)
    valid = jnp.logical_and(jnp.logical_and(col >= row, col <= row + n),
                            jnp.logical_or(col >= n, has_prev))
    lane = lax.broadcasted_iota(jnp.int32, (n, LANE), 1)
    dn = (((1,), (1,)), ((), ()))
    heads = [slice(h * LANE, (h + 1) * LANE) for h in range(N_HEADS)]
    scores = []
    for h, sl in enumerate(heads):
        k2 = jnp.concatenate([kp_ref[:, sl], kc_ref[:, sl]], axis=0)
        s = lax.dot_general(q_ref[:, sl], k2, dn, preferred_element_type=F32) + bias_ref[h]
        scores.append(jnp.where(valid, s, -jnp.inf))
    ms = [jnp.max(s, axis=-1, keepdims=True) for s in scores]
    ps = [jnp.exp2(s - m).astype(BF16) for s, m in zip(scores, ms)]
    lse_all = jnp.zeros((n, LANE), F32)
    for h, sl in enumerate(heads):
        v2 = jnp.concatenate([vp_ref[:, sl], vc_ref[:, sl]], axis=0)
        pv = jnp.dot(ps[h], jnp.concatenate([v2, jnp.ones_like(v2)], axis=-1),
                     preferred_element_type=F32)
        den = pv[:, LANE:]
        o_ref[h] = pv[:, :LANE] / den
        lse_all = jnp.where(lane == h, ms[h] + jnp.log2(den), lse_all)
    lse_ref[...] = lse_all


def _dil_attention(pg, bias):
    N, Ls, _ = pg.shape
    n = DIL_BLOCK
    blk = lambda part, prev: pl.BlockSpec(
        (None, n, MIX_WIDTH),
        (lambda s, i: (s, jnp.maximum(i - 1, 0), part)) if prev else (lambda s, i: (s, i, part)))
    return pl.pallas_call(
        _dil_kernel,
        grid=(N, Ls // n),
        in_specs=[blk(0, False), blk(1, True), blk(1, False), blk(2, True), blk(2, False),
                  pl.BlockSpec((N_HEADS, n, 2 * n), lambda s, i: (0, 0, 0))],
        out_specs=[pl.BlockSpec((None, N_HEADS, n, LANE), lambda s, i: (s, 0, i, 0)),
                   pl.BlockSpec((None, n, LANE), lambda s, i: (s, i, 0))],
        out_shape=[jax.ShapeDtypeStruct((N, N_HEADS, Ls, LANE), F32),
                   jax.ShapeDtypeStruct((N, Ls, LANE), F32)],
        compiler_params=_params("parallel", "arbitrary"),
        name="dil_attn",
    )(pg, pg, pg, pg, pg, bias)


def _dil_combine_kernel(o1_ref, o2_ref, o3_ref, l1_ref, l2_ref, l3_ref, out_ref, stage_ref, *, d2, d3):
    n = l3_ref.shape[1]
    q = d3 // d2
    for r in range(d3):
        rows = [(pl.ds(r, n, stride=d3),), (r % d2, pl.ds(r // d2, n, stride=q)), (r, slice(None))]
        ls = [l_ref[idx + (slice(None),)]
              for l_ref, idx in zip((l1_ref, l2_ref, l3_ref), rows)]
        m = functools.reduce(jnp.maximum, ls)
        es = [jnp.exp2(l - m) for l in ls]
        inv = 1.0 / functools.reduce(jnp.add, es)
        ws = [e * inv for e in es]
        for h in range(N_HEADS):
            o = (ws[0][:, h:h + 1] * o1_ref[h, pl.ds(r, n, stride=d3), :]
                 + ws[1][:, h:h + 1] * o2_ref[r % d2, h, pl.ds(r // d2, n, stride=q), :]
                 + ws[2][:, h:h + 1] * o3_ref[r, h, :, :])
            stage_ref[h, pl.ds(r, n, stride=d3), :] = o
    for h in range(N_HEADS):
        out_ref[:, h * LANE:(h + 1) * LANE] = stage_ref[h].astype(out_ref.dtype)


def _dil_combine(outs, lses, B, S, tm=256):
    (_, d1), (_, d2), (_, d3) = DIL_GROUPS
    assert d1 == 1 and d3 % d2 == 0
    tm = _tile(S, tm)
    H = N_HEADS
    o_specs = [pl.BlockSpec((None, H, tm, LANE), lambda b, i: (b, 0, i, 0)),
               pl.BlockSpec((None, d2, H, tm // d2, LANE), lambda b, i: (b, 0, 0, i, 0)),
               pl.BlockSpec((None, d3, H, tm // d3, LANE), lambda b, i: (b, 0, 0, i, 0))]
    l_specs = [pl.BlockSpec((None, tm, LANE), lambda b, i: (b, i, 0)),
               pl.BlockSpec((None, d2, tm // d2, LANE), lambda b, i: (b, 0, i, 0)),
               pl.BlockSpec((None, d3, tm // d3, LANE), lambda b, i: (b, 0, i, 0))]
    return pl.pallas_call(
        functools.partial(_dil_combine_kernel, d2=d2, d3=d3),
        grid=(B, S // tm),
        in_specs=o_specs + l_specs,
        out_specs=pl.BlockSpec((None, tm, MIX_WIDTH), lambda b, i: (b, i, 0)),
        out_shape=jax.ShapeDtypeStruct((B, S, MIX_WIDTH), BF16),
        scratch_shapes=[pltpu.VMEM((H, tm, LANE), F32)],
        compiler_params=_params("parallel", "parallel"),
        name="dil_combine",
    )(outs[0].reshape(B, H, S, LANE), outs[1].reshape(B, d2, H, S // d2, LANE),
      outs[2].reshape(B, d3, H, S // d3, LANE),
      lses[0].reshape(B, S, LANE), lses[1].reshape(B, d2, S // d2, LANE),
      lses[2].reshape(B, d3, S // d3, LANE))


def _dsa_select_kernel(qi_ref, wi_ref, ki_ref, mask_ref, key_ref, jsel_ref, *, tq, tk, S, n_sel):
    i = pl.program_id(1)
    q0 = i * tq
    n_chunks = (q0 + tq + tk - 1) // tk
    dn = (((1,), (1,)), ((), ()))
    rowg = q0 + lax.broadcasted_iota(jnp.int32, (tq, tk), 0)
    colb = lax.broadcasted_iota(jnp.int32, (tq, tk), 1)

    wi = wi_ref[...] * (IDX_HEADS ** -0.5 * IDX_DIM ** -0.5)

    def idx_body(c, carry):
        off = pl.multiple_of(c * tk, tk)
        kic = ki_ref[pl.ds(off, tk), :].astype(BF16)
        acc = jnp.zeros((tq, tk), F32)
        for h in range(IDX_HEADS):
            d = lax.dot_general(qi_ref[:, h * LANE:(h + 1) * LANE], kic, dn,
                                preferred_element_type=F32)
            acc = acc + wi[:, IDX_DIM + h:IDX_DIM + h + 1] * jnp.maximum(d, 0.0)
        score = jnp.where(off + colb <= rowg, acc, -jnp.inf)
        bits = pltpu.bitcast(score, jnp.int32)
        key_ref[:, pl.ds(off, tk)] = bits ^ ((bits >> 31) & jnp.int32(0x7FFFFFFF))
        return carry

    lax.fori_loop(0, n_chunks, idx_body, 0)

    nb = tq // LANE
    blocks = [slice(rb * LANE, (rb + 1) * LANE) for rb in range(nb)]
    colk = lax.broadcasted_iota(jnp.int32, (LANE, tk), 1)
    rowk = [q0 + rb * LANE + lax.broadcasted_iota(jnp.int32, (LANE, tk), 0) for rb in range(nb)]

    def count(pred):
        out = []
        for rb in range(nb):
            def body(c, acc, rb=rb):
                off = pl.multiple_of(c * tk, tk)
                ones = jnp.where(pred(key_ref[blocks[rb], pl.ds(off, tk)], off, rb), 1.0, 0.0)
                for w in range(tk // LANE):
                    acc = acc + ones[:, w * LANE:(w + 1) * LANE]
                return acc

            acc = lax.fori_loop(0, n_chunks, body, jnp.zeros((LANE, LANE), F32))
            out.append(jnp.sum(acc, axis=-1, keepdims=True))
        return out

    k_sel = float(n_sel)
    lo = tuple(jnp.where(c >= k_sel, jnp.int32(0), jnp.int32(-2 ** 31))
               for c in count(lambda k, off, rb: k >= 0))

    def bis_body(it, lo):
        cand = [l + (jnp.int32(1) << (30 - it)) for l in lo]
        cnt = count(lambda k, off, rb: k >= cand[rb])
        return tuple(jnp.where(c >= k_sel, cd, l) for c, cd, l in zip(cnt, cand, lo))

    thr = lax.fori_loop(0, 31, bis_body, lo)
    n_ge = count(lambda k, off, rb: k >= thr[rb])
    need = [k_sel - c for c in count(lambda k, off, rb: k > thr[rb])]
    n_bits = max(1, (S - 1).bit_length())
    jsel_ref[...] = jnp.full_like(jsel_ref, S)
    most = functools.reduce(jnp.maximum, [jnp.max(c) for c in n_ge])

    @pl.when(most > k_sel)
    def _():
        def tie_body(it, jlo):
            cand = [j + (jnp.int32(1) << (n_bits - 1 - it)) for j in jlo]
            below = count(lambda k, off, rb: jnp.logical_and(k == thr[rb], off + colk < cand[rb]))
            return tuple(jnp.where(b < n, cd, j) for b, n, cd, j in zip(below, need, cand, jlo))

        jsel = lax.fori_loop(0, n_bits, tie_body,
                             tuple(jnp.zeros((LANE, 1), jnp.int32) for _ in range(nb)))
        for rb in range(nb):
            jsel_ref[blocks[rb], :] = jnp.broadcast_to(jsel[rb], (LANE, LANE))

    mask_ref[...] = jnp.full_like(mask_ref, -jnp.inf)

    def mask_body(c, carry):
        off = pl.multiple_of(c * tk, tk)
        col = off + colk
        for rb in range(nb):
            k = key_ref[blocks[rb], pl.ds(off, tk)]
            tie = jnp.logical_and(k == thr[rb], col <= jsel_ref[blocks[rb], :1])
            sel = jnp.logical_and(jnp.logical_or(k > thr[rb], tie), col <= rowk[rb])
            mask_ref[blocks[rb], pl.ds(off, tk)] = jnp.where(sel, 0.0, -jnp.inf).astype(mask_ref.dtype)
        return carry

    lax.fori_loop(0, n_chunks, mask_body, 0)


def _dsa_select(pm, aux, B, S, n_sel, tq=256, tk=512):
    tq, tk = _tile(S, tq), _tile(S, tk)
    return pl.pallas_call(
        functools.partial(_dsa_select_kernel, tq=tq, tk=tk, S=S, n_sel=n_sel),
        grid=(B, S // tq),
        in_specs=[pl.BlockSpec((None, tq, MIX_WIDTH), lambda b, i: (b, i, 1)),
                  pl.BlockSpec((None, tq, LANE), lambda b, i: (b, i, 0)),
                  pl.BlockSpec((None, S, LANE), lambda b, i: (b, 0, 0))],
        out_specs=pl.BlockSpec((None, tq, S), lambda b, i: (b, i, 0)),
        out_shape=jax.ShapeDtypeStruct((B, S, S), BF16),
        scratch_shapes=[pltpu.VMEM((tq, S), jnp.int32), pltpu.VMEM((tq, LANE), jnp.int32)],
        compiler_params=_params("parallel", "arbitrary"),
        name="dsa_select",
    )(pm, aux, aux)


def _dsa_attn_kernel(q_ref, k_ref, v_ref, mask_ref, gt_ref, o_ref, s_ref, m_ref, acc_ref, *, tq, tk, S):
    i = pl.program_id(2)
    q0 = i * tq
    n_chunks = (q0 + tq + tk - 1) // tk
    R = q_ref.shape[1] // LANE
    nblk = tq // LANE
    dn = (((1,), (1,)), ((), ()))
    qs = [q_ref[:, r * LANE:(r + 1) * LANE] for r in range(R)]

    def produce(c):
        off = pl.multiple_of(c * tk, tk)
        kc = k_ref[pl.ds(off, tk), :]
        mask = mask_ref[:, pl.ds(off, tk)].astype(F32)
        st = S - q0 + off
        scores = []
        for r in range(R):
            bias = jnp.concatenate(
                [gt_ref[r, :, pl.ds(pl.multiple_of(st + (nblk - 1 - a) * LANE, LANE), tk)]
                 for a in range(nblk)], axis=0)
            scores.append(lax.dot_general(qs[r], kc, dn, preferred_element_type=F32) + bias + mask)
        return scores

    def values(c):
        return [v_ref[pl.ds(pl.multiple_of(c * tk, tk), tk), :]] * R

    _online_softmax(n_chunks, produce, values, s_ref, m_ref, acc_ref)
    for r in range(R):
        acc = acc_ref[r]
        o_ref[:, r * LANE:(r + 1) * LANE] = (acc[:, :LANE] / acc[:, LANE:]).astype(o_ref.dtype)


def _dsa_attention(pm, mask, gt, B, S, tq=256, tk=512):
    tq, tk = _tile(S, tq), _tile(S, tk)
    R = N_HEADS // DSA_KV_HEADS
    kblk = 2 * MIX_WIDTH // LANE
    return pl.pallas_call(
        functools.partial(_dsa_attn_kernel, tq=tq, tk=tk, S=S),
        grid=(DSA_KV_HEADS, B, S // tq),
        in_specs=[pl.BlockSpec((None, tq, R * LANE), lambda g, b, i: (b, i, g)),
                  pl.BlockSpec((None, S, LANE), lambda g, b, i: (b, 0, kblk + g)),
                  pl.BlockSpec((None, S, LANE), lambda g, b, i: (b, 0, kblk + DSA_KV_HEADS + g)),
                  pl.BlockSpec((None, tq, S), lambda g, b, i: (b, i, 0)),
                  pl.BlockSpec((R, LANE, gt.shape[2]), lambda g, b, i: (g, 0, 0))],
        out_specs=pl.BlockSpec((None, tq, R * LANE), lambda g, b, i: (b, i, g)),
        out_shape=jax.ShapeDtypeStruct((B, S, MIX_WIDTH), BF16),
        scratch_shapes=[pltpu.VMEM((2, R, tq, tk), F32), pltpu.VMEM((R, tq, LANE), F32),
                        pltpu.VMEM((R, tq, 2 * HEAD_DIM), F32)],
        compiler_params=_params("parallel", "parallel", "arbitrary"),
        name="dsa_attn",
    )(pm, pm, pm, mask, gt)


def _t5_bucket(dist):
    n = jnp.maximum(dist, 0)
    exact = REL_BUCKETS // 2
    nf = jnp.maximum(n, 1).astype(F32)
    large = exact + (jnp.log(nf / exact) / math.log(REL_MAX_DIST / exact)
                     * (REL_BUCKETS - exact)).astype(jnp.int32)
    large = jnp.minimum(large, REL_BUCKETS - 1)
    return jnp.where(n < exact, n, large)


def _t5_bias(t5_table, dist):
    onehot = jax.nn.one_hot(_t5_bucket(dist), REL_BUCKETS, dtype=F32)
    table = jnp.einsum('...b,bh->h...', onehot, t5_table.astype(F32), precision=lax.Precision.HIGHEST)
    return table * LOG2E


def _dil_bias(t5_table, dil):
    n = DIL_BLOCK
    rel = jnp.arange(n)[:, None] + n - jnp.arange(2 * n)[None, :]
    return _t5_bias(t5_table, rel * dil)


def _dsa_bias_table(t5_table, S, tq, tk):
    lt = S + tq + tk
    tb = _t5_bias(t5_table, S + tq - 1 - jnp.arange(lt + LANE))
    return jnp.stack([tb[:, LANE - 1 - i:LANE - 1 - i + lt] for i in range(LANE)], axis=1)


def _headnorm_cols(spec):
    eg, ef = [], []
    for gain, count, scale in spec:
        if gain is None:
            eg.append(jnp.ones((count * LANE,), F32))
            ef.append(jnp.zeros((count * LANE,), F32))
        else:
            eg.append(jnp.tile(gain.astype(F32) * scale, count))
            ef.append(jnp.ones((count * LANE,), F32))
    return jnp.concatenate(eg), jnp.concatenate(ef)


def _pad_cols(w, width):
    return jnp.pad(w, ((0, 0), (0, width - w.shape[1])))


def _fox_layer(x, norm_g, w_in, b_f, qk_g, mem_g):
    B, S, _ = x.shape
    W = MIX_WIDTH
    w_main = jnp.concatenate([w_in[:, :3 * W], w_in[:, 3 * W + N_HEADS:]], axis=1).astype(BF16)
    w_gate = _pad_cols(w_in[:, 3 * W:3 * W + N_HEADS], LANE).astype(BF16)
    eg, ef = _headnorm_cols([(qk_g[0], N_HEADS, HEAD_DIM ** -0.5 * LOG2E), (qk_g[1], N_HEADS, 1.0),
                             (None, N_HEADS, 1.0), (mem_g, MEM_HEADS, HEAD_DIM ** -0.5)])
    pm = _proj(x, norm_g, w_main, eg, ef)
    fg = _proj(x, norm_g, w_gate, out_dtype=F32)
    cum_t = _fox_gates(fg, b_f)
    mix = _flash([(pm, 0, True)], [(pm, W, True)], (pm, 2 * W, True), B, S,
                 decay_t=cum_t.reshape(B, LANE, 1, S))
    return mix, pm, 3 * W // MEM_WIDTH


def _mla_layer(x, positions, norm_g, w_in, q_norm, w_uq, kv_norm, w_ukv, nope_g, rope_g, mem_g):
    B, S, _ = x.shape
    scale = (NOPE_DIM + ROPE_DIM) ** -0.5 * LOG2E
    lat = Q_LORA + KV_LORA
    w_lat = jnp.concatenate([w_in[:, :lat], _pad_cols(w_in[:, lat:lat + ROPE_DIM], LANE)],
                            axis=1).astype(BF16)
    w_mem = w_in[:, lat + ROPE_DIM:].astype(BF16)
    pl_ = _proj(x, norm_g, w_lat, out_dtype=F32, tn=lat + LANE)
    eg, ef = _headnorm_cols([(mem_g, MEM_HEADS, HEAD_DIM ** -0.5)])
    pmem = _proj(x, norm_g, w_mem, eg, ef)

    uq = w_uq.reshape(Q_LORA, N_HEADS, NOPE_DIM + ROPE_DIM)
    w_qn = uq[:, :, :NOPE_DIM].reshape(Q_LORA, MIX_WIDTH).astype(BF16)
    w_qr = jnp.pad(uq[:, :, NOPE_DIM:], ((0, 0), (0, 0), (0, LANE - ROPE_DIM))
                   ).reshape(Q_LORA, N_HEADS * LANE).astype(BF16)
    ukv = w_ukv.reshape(KV_LORA, N_HEADS, NOPE_DIM + HEAD_DIM)
    w_kv = jnp.concatenate([ukv[:, :, :NOPE_DIM].reshape(KV_LORA, MIX_WIDTH),
                            ukv[:, :, NOPE_DIM:].reshape(KV_LORA, MIX_WIDTH)], axis=1).astype(BF16)
    eg, ef = _headnorm_cols([(nope_g[0], N_HEADS, scale)])
    qn = _proj(pl_, q_norm, w_qn, eg, ef, kblock=0)
    qr_raw = _proj(pl_, q_norm, w_qr, kblock=0, out_dtype=F32)
    eg, ef = _headnorm_cols([(nope_g[1], N_HEADS, 1.0), (None, N_HEADS, 1.0)])
    kv = _proj(pl_, kv_norm, w_kv, eg, ef, kblock=1)

    half = ROPE_DIM // 2
    inv = ROPE_THETA ** (-jnp.arange(half, dtype=F32) / half)
    ang = positions.astype(F32)[..., None] * inv
    cos, sin = jnp.cos(ang), jnp.sin(ang)
    zero = jnp.zeros((B, S, LANE - ROPE_DIM), F32)
    cos_t = jnp.concatenate([cos, cos, zero], axis=-1)
    sin_t = jnp.concatenate([-sin, sin, zero], axis=-1)
    qr = _rope(qr_raw, cos_t, sin_t, rope_g[0] * scale)
    kr = _rope(pl_[:, :, lat:], cos_t, sin_t, rope_g[1])
    mix = _flash([(qn, 0, True), (qr, 0, True)], [(kv, 0, True), (kr, 0, False)],
                 (kv, MIX_WIDTH, True), B, S)
    return mix, pmem, 0


def _dil_layer(x, norm_g, w_in, qk_g, t5_table, mem_g):
    B, S, _ = x.shape
    W = MIX_WIDTH
    outs, lses = [], []
    for gi, (win, dil) in enumerate(DIL_GROUPS):
        assert win // dil == DIL_BLOCK and (S // dil) % DIL_BLOCK == 0
        w_g = w_in[:, gi * 3 * W:(gi + 1) * 3 * W].astype(BF16)
        eg, ef = _headnorm_cols([(qk_g[gi, 0], N_HEADS, HEAD_DIM ** -0.5 * LOG2E),
                                 (qk_g[gi, 1], N_HEADS, 1.0), (None, N_HEADS, 1.0)])
        pg = _proj(x, norm_g, w_g, eg, ef, dil=dil, tn=1024)
        o, lse = _dil_attention(pg.reshape(B * dil, S // dil, 3 * W), _dil_bias(t5_table, dil))
        outs.append(o)
        lses.append(lse)
    mix = _dil_combine(outs, lses, B, S)
    eg, ef = _headnorm_cols([(mem_g, MEM_HEADS, HEAD_DIM ** -0.5)])
    pmem = _proj(x, norm_g, w_in[:, len(DIL_GROUPS) * 3 * W:].astype(BF16), eg, ef)
    return mix, pmem, 0


def _dsa_layer(x, norm_g, w_in, qk_g, t5_table, mem_g):
    B, S, _ = x.shape
    W = MIX_WIDTH
    kvw = DSA_KV_HEADS * HEAD_DIM
    o_k, o_v, o_qi = W, W + kvw, W + 2 * kvw
    o_ki = o_qi + IDX_HEADS * IDX_DIM
    o_wi = o_ki + IDX_DIM
    o_mem = o_wi + IDX_HEADS
    w_qi = jnp.pad(w_in[:, o_qi:o_ki].reshape(D_MODEL, IDX_HEADS, IDX_DIM),
                   ((0, 0), (0, 0), (0, LANE - IDX_DIM))).reshape(D_MODEL, IDX_HEADS * LANE)
    w_main = jnp.concatenate([w_in[:, :W], w_qi, w_in[:, o_k:o_qi], w_in[:, o_mem:]],
                             axis=1).astype(BF16)
    w_aux = _pad_cols(w_in[:, o_ki:o_mem], LANE).astype(BF16)
    eg, ef = _headnorm_cols([(qk_g[0], N_HEADS, HEAD_DIM ** -0.5 * LOG2E), (None, IDX_HEADS, 1.0),
                             (qk_g[1], DSA_KV_HEADS, 1.0), (None, DSA_KV_HEADS, 1.0),
                             (mem_g, MEM_HEADS, HEAD_DIM ** -0.5)])
    pm = _proj(x, norm_g, w_main, eg, ef)
    aux = _proj(x, norm_g, w_aux, out_dtype=F32)
    mask = _dsa_select(pm, aux, B, S, min(TOPK_MAX, S // 4))
    tq, tk = _tile(S, 512), _tile(S, 512)
    mix = _dsa_attention(pm, mask, _dsa_bias_table(t5_table, S, tq, tk), B, S, tq, tk)
    return mix, pm, (2 * W + 2 * kvw) // MEM_WIDTH


def kernel(x, mem, positions, t5_table, ffn_norm, ffn_w_gate, ffn_w_up, ffn_w_down, attn_norm,
           mem_norm, mem_w_kv, mem_qk_g, w_out, a_w_in, a_b_f, a_qk_g, b_w_in, b_q_norm, b_w_uq,
           b_kv_norm, b_w_ukv, b_nope_g, b_rope_g, c_w_in, c_qk_g, d_w_in, d_qk_g):
    B, S, D = x.shape
    depth = ffn_norm.shape[0]
    n_mixers = 4

    def ffn(xc, i, k, next_gain=None):
        return _ffn(xc.reshape(B * S, D), ffn_norm[i, k], ffn_w_gate[i, k].astype(BF16),
                    ffn_w_up[i, k].astype(BF16), ffn_w_down[i, k].astype(BF16), next_gain)

    for i in range(depth):
        m, j = i % n_mixers, i // n_mixers
        x, h = ffn(x, i, 0, attn_norm[i])
        x, h = x.reshape(B, S, D), h.reshape(B, S, D)
        mem_g = mem_qk_g[i]
        if m == 0:
            mix, qarr, qblock = _fox_layer(h, None, a_w_in[j], a_b_f[j], a_qk_g[j], mem_g[0])
        elif m == 1:
            mix, qarr, qblock = _mla_layer(h, positions, None, b_w_in[j], b_q_norm[j],
                                           b_w_uq[j], b_kv_norm[j], b_w_ukv[j], b_nope_g[j],
                                           b_rope_g[j], mem_g[0])
        elif m == 2:
            mix, qarr, qblock = _dil_layer(h, None, c_w_in[j], c_qk_g[j], t5_table, mem_g[0])
        else:
            mix, qarr, qblock = _dsa_layer(h, None, d_w_in[j], d_qk_g[j], t5_table, mem_g[0])
        eg, ef = _headnorm_cols([(mem_g[1], MEM_HEADS, 1.0), (None, MEM_HEADS, 1.0)])
        mem_kv = _proj(mem, mem_norm[i], mem_w_kv[i].astype(BF16), eg, ef)
        mo = _mem_attention(qarr, qblock, mem_kv)
        wo = w_out[i].astype(BF16)
        x = _out_proj(x.reshape(B * S, D), mix.reshape(B * S, MIX_WIDTH),
                      mo.reshape(B * S, MEM_WIDTH), wo[:MIX_WIDTH], wo[MIX_WIDTH:]).reshape(B, S, D)
        x = ffn(x, i, 1).reshape(B, S, D)
    return x
```

```python
import functools
import math

import jax
import jax.numpy as jnp
from jax import lax
from jax.experimental import pallas as pl
from jax.experimental.pallas import tpu as pltpu

F32 = jnp.float32
BF16 = jnp.bfloat16

LANE = 128
D_MODEL = 2048
N_HEADS = 16
HEAD_DIM = 128
MIX_WIDTH = N_HEADS * HEAD_DIM
MEM_HEADS = 4
MEM_WIDTH = MEM_HEADS * HEAD_DIM
D_FF = 5632
RMS_EPS = 1e-6
REL_BUCKETS = 32
REL_MAX_DIST = 2048
Q_LORA = 512
KV_LORA = 512
NOPE_DIM = 128
ROPE_DIM = 64
ROPE_THETA = 10000.0
DIL_GROUPS = ((128, 1), (512, 4), (2048, 16))
DIL_BLOCK = 128
DSA_KV_HEADS = 4
IDX_HEADS = 16
IDX_DIM = 64
TOPK_MAX = 256
VMEM_LIMIT = 56 * 1024 * 1024
FFN_VMEM_LIMIT = 62 * 1024 * 1024
NEG_BIG = -1e30
LOG2E = math.log2(math.e)


def _params(*sem):
    return pltpu.CompilerParams(dimension_semantics=sem, vmem_limit_bytes=VMEM_LIMIT)


def _tile(n, pref):
    t = min(n, pref)
    while n % t:
        t //= 2
    return t


def _ffn_kernel(x_ref, g_ref, wg_ref, wu_ref, wd_ref, o_ref, h_ref):
    _ffn_body(x_ref, g_ref, wg_ref, wu_ref, wd_ref, o_ref, h_ref, ())


def _ffn_kernel_with_next(x_ref, g_ref, wg_ref, wu_ref, wd_ref, ng_ref, o_ref, hn_ref, h_ref):
    _ffn_body(x_ref, g_ref, wg_ref, wu_ref, wd_ref, o_ref, h_ref, (ng_ref, hn_ref))


def _ffn_body(x_ref, g_ref, wg_ref, wu_ref, wd_ref, o_ref, h_ref, next_refs):
    j = pl.program_id(1)
    last = pl.num_programs(1) - 1
    tm = x_ref.shape[0]
    halves = [slice(0, tm // 2), slice(tm // 2, tm)]

    def down(h):
        g = jnp.dot(h, wg_ref[...], preferred_element_type=F32)
        u = jnp.dot(h, wu_ref[...], preferred_element_type=F32)
        a = (g * jax.nn.sigmoid(g) * u).astype(BF16)
        return jnp.dot(a, wd_ref[...], preferred_element_type=F32)

    @pl.when(j == 0)
    def _():
        for rows in halves:
            x = x_ref[rows, :]
            ms = jnp.mean(x * x, axis=-1, keepdims=True)
            h = (x * lax.rsqrt(ms + RMS_EPS) * g_ref[...]).astype(BF16)
            h_ref[rows, :] = h
            o_ref[rows, :] = down(h)

    @pl.when(jnp.logical_and(j > 0, j < last))
    def _():
        o_ref[...] += down(h_ref[...])

    @pl.when(jnp.logical_and(j > 0, j == last))
    def _():
        for rows in halves:
            y = x_ref[rows, :] + 0.5 * (o_ref[rows, :] + down(h_ref[rows, :]))
            o_ref[rows, :] = y
            if next_refs:
                ng_ref, hn_ref = next_refs
                ms = jnp.mean(y * y, axis=-1, keepdims=True)
                hn_ref[rows, :] = (y * lax.rsqrt(ms + RMS_EPS) * ng_ref[...]).astype(BF16)


def _ffn(x2, gain, wg, wu, wd, next_gain=None):
    T, D = x2.shape
    F = wg.shape[1]
    tm, tf = _tile(T, 1024), _tile(F, 512)
    assert F // tf >= 2 and tm % 16 == 0
    row_spec = pl.BlockSpec((tm, D), lambda i, j: (i, 0))
    vec_spec = pl.BlockSpec((1, D), lambda i, j: (0, 0))
    in_specs = [row_spec, vec_spec,
                pl.BlockSpec((D, tf), lambda i, j: (0, j)),
                pl.BlockSpec((D, tf), lambda i, j: (0, j)),
                pl.BlockSpec((tf, D), lambda i, j: (j, 0))]
    args = [x2, gain.reshape(1, D), wg, wu, wd]
    out_specs, out_shape = row_spec, jax.ShapeDtypeStruct((T, D), F32)
    kern = _ffn_kernel
    if next_gain is not None:
        in_specs.append(vec_spec)
        args.append(next_gain.reshape(1, D).astype(F32))
        out_specs = [row_spec, row_spec]
        out_shape = [out_shape, jax.ShapeDtypeStruct((T, D), BF16)]
        kern = _ffn_kernel_with_next
    return pl.pallas_call(
        kern,
        grid=(T // tm, F // tf),
        in_specs=in_specs,
        out_specs=out_specs,
        out_shape=out_shape,
        scratch_shapes=[pltpu.VMEM((tm, D), BF16)],
        compiler_params=pltpu.CompilerParams(dimension_semantics=("parallel", "arbitrary"),
                                             vmem_limit_bytes=FFN_VMEM_LIMIT),
        name="ffn",
    )(*args)


def _proj_kernel(x_ref, g_ref, w_ref, eg_ref, ef_ref, o_ref, *scratch, epilogue, dil, prenormed):
    scratch = list(scratch)
    if prenormed:
        h_ref = x_ref
    else:
        h_ref = scratch.pop(0)

        @pl.when(pl.program_id(2) == 0)
        def _():
            x = x_ref[...].astype(F32)
            ms = jnp.mean(x * x, axis=-1, keepdims=True)
            h_ref[...] = (x * lax.rsqrt(ms + RMS_EPS) * g_ref[...]).astype(BF16)

    stage = scratch
    tm = h_ref.shape[0]
    parts = 2 if tm % (2 * 8 * dil) == 0 else 1
    rows_per, n = tm // parts, tm // parts // dil
    for p in range(parts):
        rows = slice(p * rows_per, (p + 1) * rows_per)
        acc = jnp.dot(h_ref[rows, :], w_ref[...], preferred_element_type=F32)
        for c in range(acc.shape[1] // LANE):
            sl = slice(c * LANE, (c + 1) * LANE)
            y = acc[:, sl]
            if epilogue:
                ms = jnp.mean(y * y, axis=-1, keepdims=True)
                scale = jnp.where(ef_ref[:, sl] > 0.0, lax.rsqrt(ms + RMS_EPS), 1.0)
                y = y * scale * eg_ref[:, sl]
            if dil == 1:
                o_ref[0, rows, sl] = y.astype(o_ref.dtype)
            else:
                stage[0][c, rows, :] = y
                for r in range(dil):
                    o_ref[r, p * n:(p + 1) * n, sl] = stage[0][
                        c, pl.ds(p * rows_per + r, n, stride=dil), :].astype(o_ref.dtype)


def _proj(x, gain, w, eg=None, ef=None, *, dil=1, kblock=0, out_dtype=BF16, tm=1024, tn=512):
    B, S, C = x.shape
    K, N = w.shape
    tm, tn = _tile(S, tm), _tile(N, tn)
    epilogue = eg is not None
    if not epilogue:
        eg = jnp.ones((1, N), F32)
        ef = jnp.zeros((1, N), F32)
    prenormed = gain is None
    if prenormed:
        assert x.dtype == BF16
        gain = jnp.ones((K,), F32)
    out = pl.pallas_call(
        functools.partial(_proj_kernel, epilogue=epilogue, dil=dil, prenormed=prenormed),
        grid=(B, S // tm, N // tn),
        in_specs=[
            pl.BlockSpec((None, tm, K), lambda b, i, j: (b, i, kblock)),
            pl.BlockSpec((1, K), lambda b, i, j: (0, 0)),
            pl.BlockSpec((K, tn), lambda b, i, j: (0, j)),
            pl.BlockSpec((1, tn), lambda b, i, j: (0, j)),
            pl.BlockSpec((1, tn), lambda b, i, j: (0, j)),
        ],
        out_specs=pl.BlockSpec((None, dil, tm // dil, tn), lambda b, i, j: (b, 0, i, j)),
        out_shape=jax.ShapeDtypeStruct((B, dil, S // dil, N), out_dtype),
        scratch_shapes=([] if prenormed else [pltpu.VMEM((tm, K), BF16)])
        + ([pltpu.VMEM((tn // LANE, tm, LANE), F32)] if dil > 1 else []),
        compiler_params=_params("parallel", "parallel", "arbitrary"),
        name="proj",
    )(x, gain.reshape(1, K).astype(F32), w, eg.reshape(1, N), ef.reshape(1, N))
    return out.reshape(B, S, N) if dil == 1 else out


def _out_kernel(x_ref, a_ref, b_ref, wa_ref, wb_ref, o_ref):
    acc = jnp.dot(a_ref[...], wa_ref[...], preferred_element_type=F32)
    acc += jnp.dot(b_ref[...], wb_ref[...], preferred_element_type=F32)
    o_ref[...] = x_ref[...] + acc


def _out_proj(x2, mix2, mo2, wa, wb):
    T, D = x2.shape
    tm, tn = _tile(T, 1024), _tile(D, 1024)
    ka, kb = mix2.shape[1], mo2.shape[1]
    return pl.pallas_call(
        _out_kernel,
        grid=(T // tm, D // tn),
        in_specs=[
            pl.BlockSpec((tm, tn), lambda i, j: (i, j)),
            pl.BlockSpec((tm, ka), lambda i, j: (i, 0)),
            pl.BlockSpec((tm, kb), lambda i, j: (i, 0)),
            pl.BlockSpec((ka, tn), lambda i, j: (0, j)),
            pl.BlockSpec((kb, tn), lambda i, j: (0, j)),
        ],
        out_specs=pl.BlockSpec((tm, tn), lambda i, j: (i, j)),
        out_shape=jax.ShapeDtypeStruct((T, D), F32),
        compiler_params=_params("parallel", "arbitrary"),
        name="out_proj",
    )(x2, mix2, mo2, wa, wb)


def _lanes(x, reps):
    return x if reps == 1 else jnp.concatenate([x] * reps, axis=-1)


def _online_softmax(n_chunks, produce, values, s_ref, m_ref, acc_ref, finish=None):
    G = s_ref.shape[1]

    def consume(j, scores):
        ps, alphas = [], []
        for g, s in enumerate(scores):
            m_old = m_ref[g]
            m_new = jnp.maximum(m_old, jnp.max(s, axis=-1, keepdims=True))
            ps.append(jnp.exp2(s - _lanes(m_new, s.shape[1] // LANE)).astype(BF16))
            alphas.append(jnp.exp2(m_old - m_new))
            m_ref[g] = m_new
        return ps, alphas

    def accumulate(j, ps, alphas):
        for g, (p, alpha, v) in enumerate(zip(ps, alphas, values(j))):
            v1 = jnp.concatenate([v, jnp.ones_like(v)], axis=-1)
            acc_ref[g] = _lanes(alpha, 2) * acc_ref[g] + jnp.dot(p, v1, preferred_element_type=F32)

    m_ref[...] = jnp.full_like(m_ref, NEG_BIG)
    acc_ref[...] = jnp.zeros_like(acc_ref)
    for g, s in enumerate(produce(0)):
        s_ref[0, g] = s

    def step(j, slot):
        cur = [s_ref[slot, g] for g in range(G)]
        nxt = produce(j + 1)
        ps, alphas = consume(j, cur)
        for g, s in enumerate(nxt):
            s_ref[1 - slot, g] = s
        accumulate(j, ps, alphas)

    def body(jj, carry):
        step(2 * jj, 0)
        step(2 * jj + 1, 1)
        return carry

    last = n_chunks - 1
    lax.fori_loop(0, last // 2, body, 0)

    @pl.when(last % 2 == 1)
    def _():
        step(last - 1, 0)

    cur = [s_ref[last % 2, g] for g in range(G)]
    if finish is not None:
        cur = [finish(s) for s in cur]
    accumulate(last, *consume(last, cur))


def _flash_kernel(*refs, nqk, decay, tq, G):
    q_refs, k_refs, v_ref = refs[:nqk], refs[nqk:2 * nqk], refs[2 * nqk]
    pos = 2 * nqk + 1
    if decay:
        ck_ref = refs[pos]
        pos += 1
    o_ref, s_ref, m_ref, acc_ref = refs[pos:pos + 4]
    i = pl.program_id(2)

    def head(ref, g, rows=slice(None)):
        if ref.shape[-1] == LANE:
            return ref[rows, :]
        return ref[rows, g * LANE:(g + 1) * LANE]

    qs = []
    for g in range(G):
        parts = [head(r, g) for r in q_refs]
        qs.append(parts[0] if nqk == 1 else jnp.concatenate(parts, axis=-1))

    def chunk(j):
        return pl.ds(pl.multiple_of(j * tq, tq), tq)

    def produce(j):
        rows = chunk(j)
        scores = []
        for g in range(G):
            parts = [head(r, g, rows) for r in k_refs]
            k = parts[0] if nqk == 1 else jnp.concatenate(parts, axis=-1)
            s = lax.dot_general(qs[g], k, (((1,), (1,)), ((), ())), preferred_element_type=F32)
            scores.append(s - ck_ref[g, :, rows] if decay else s)
        return scores

    def values(j):
        return [head(v_ref, g, chunk(j)) for g in range(G)]

    def causal(s):
        row = lax.broadcasted_iota(jnp.int32, (tq, tq), 0)
        col = lax.broadcasted_iota(jnp.int32, (tq, tq), 1)
        return jnp.where(col <= row, s, -jnp.inf)

    _online_softmax(i + 1, produce, values, s_ref, m_ref, acc_ref, finish=causal)
    for g in range(G):
        acc = acc_ref[g]
        o_ref[:, g * LANE:(g + 1) * LANE] = (acc[:, :LANE] / acc[:, LANE:]).astype(o_ref.dtype)


def _flash(q_parts, k_parts, v_part, B, S, decay_t=None, tq=512, G=4):
    tq = _tile(S, tq)
    nqk = len(q_parts)
    W = G * LANE
    args, specs = [], []
    for arr, off, _ in q_parts:
        args.append(arr)
        specs.append(pl.BlockSpec((None, tq, W), lambda b, h, i, off=off: (b, i, off // W + h)))
    for arr, off, per_head in k_parts + [v_part]:
        args.append(arr)
        if per_head:
            specs.append(pl.BlockSpec((None, S, W), lambda b, h, i, off=off: (b, 0, off // W + h)))
        else:
            specs.append(pl.BlockSpec((None, S, LANE), lambda b, h, i, off=off: (b, 0, off // LANE)))
    if decay_t is not None:
        args.append(decay_t)
        specs.append(pl.BlockSpec((None, G, 1, S), lambda b, h, i: (b, h, 0, 0)))
    return pl.pallas_call(
        functools.partial(_flash_kernel, nqk=nqk, decay=decay_t is not None, tq=tq, G=G),
        grid=(B, N_HEADS // G, S // tq),
        in_specs=specs,
        out_specs=pl.BlockSpec((None, tq, W), lambda b, h, i: (b, i, h)),
        out_shape=jax.ShapeDtypeStruct((B, S, MIX_WIDTH), BF16),
        scratch_shapes=[pltpu.VMEM((2, G, tq, tq), F32), pltpu.VMEM((G, tq, LANE), F32),
                        pltpu.VMEM((G, tq, 2 * HEAD_DIM), F32)],
        compiler_params=_params("parallel", "parallel", "arbitrary"),
        name="flash",
    )(*args)


def _gate_kernel(fg_ref, bf_ref, cumt_ref, carry_ref, *, ts):
    @pl.when(pl.program_id(1) == 0)
    def _():
        carry_ref[...] = jnp.zeros_like(carry_ref)

    z = fg_ref[...] + bf_ref[...]
    lf = jnp.minimum(z, 0.0) - jnp.log1p(jnp.exp(-jnp.abs(z)))
    hi = lf.astype(BF16)
    r1 = lf - hi.astype(F32)
    mid = r1.astype(BF16)
    lo = (r1 - mid.astype(F32)).astype(BF16)
    row = lax.broadcasted_iota(jnp.int32, (ts, ts), 0)
    col = lax.broadcasted_iota(jnp.int32, (ts, ts), 1)
    tri = jnp.where(col <= row, 1.0, 0.0).astype(BF16)
    cum = (jnp.dot(tri, hi, preferred_element_type=F32)
           + jnp.dot(tri, mid, preferred_element_type=F32)
           + jnp.dot(tri, lo, preferred_element_type=F32)) + carry_ref[...]
    carry_ref[...] = cum[ts - 1:ts, :]
    cumt_ref[...] = (cum * LOG2E).T


def _fox_gates(fg, b_f):
    B, S, _ = fg.shape
    ts = _tile(S, 512)
    bf = jnp.zeros((1, LANE), F32).at[0, :N_HEADS].set(b_f.astype(F32))
    return pl.pallas_call(
        functools.partial(_gate_kernel, ts=ts),
        grid=(B, S // ts),
        in_specs=[pl.BlockSpec((None, ts, LANE), lambda b, i: (b, i, 0)),
                  pl.BlockSpec((1, LANE), lambda b, i: (0, 0))],
        out_specs=pl.BlockSpec((None, LANE, ts), lambda b, i: (b, 0, i)),
        out_shape=jax.ShapeDtypeStruct((B, LANE, S), F32),
        scratch_shapes=[pltpu.VMEM((1, LANE), F32)],
        compiler_params=_params("parallel", "arbitrary"),
        name="fox_gates",
    )(fg, bf)


def _rope_kernel(x_ref, cos_ref, sin_ref, g_ref, o_ref):
    cos, sin, g = cos_ref[...], sin_ref[...], g_ref[...]
    for c in range(x_ref.shape[1] // LANE):
        sl = slice(c * LANE, (c + 1) * LANE)
        x = x_ref[:, sl]
        ms = jnp.sum(x * x, axis=-1, keepdims=True) * (1.0 / ROPE_DIM)
        y = x * lax.rsqrt(ms + RMS_EPS) * g
        partner = pltpu.roll(y, ROPE_DIM // 2, 1) + pltpu.roll(y, LANE - ROPE_DIM // 2, 1)
        o_ref[:, sl] = (y * cos + partner * sin).astype(o_ref.dtype)


def _rope(x, cos, sin, gain):
    B, S, N = x.shape
    ts = _tile(S, 512)
    g = jnp.zeros((1, LANE), F32).at[0, :ROPE_DIM].set(gain.astype(F32))
    return pl.pallas_call(
        _rope_kernel,
        grid=(B, S // ts),
        in_specs=[pl.BlockSpec((None, ts, N), lambda b, i: (b, i, 0)),
                  pl.BlockSpec((None, ts, LANE), lambda b, i: (b, i, 0)),
                  pl.BlockSpec((None, ts, LANE), lambda b, i: (b, i, 0)),
                  pl.BlockSpec((1, LANE), lambda b, i: (0, 0))],
        out_specs=pl.BlockSpec((None, ts, N), lambda b, i: (b, i, 0)),
        out_shape=jax.ShapeDtypeStruct((B, S, N), BF16),
        compiler_params=_params("parallel", "parallel"),
        name="rope",
    )(x, cos, sin, g)


def _mem_kernel(q_ref, kv_ref, o_ref):
    for h in range(MEM_HEADS):
        q = q_ref[:, h * LANE:(h + 1) * LANE]
        k = kv_ref[:, h * LANE:(h + 1) * LANE]
        v = kv_ref[:, MEM_WIDTH + h * LANE:MEM_WIDTH + (h + 1) * LANE]
        s = lax.dot_general(q, k, (((1,), (1,)), ((), ())), preferred_element_type=F32)
        p = jnp.exp(s - jnp.max(s, axis=-1, keepdims=True))
        o = jnp.dot(p.astype(BF16), v, preferred_element_type=F32)
        o_ref[:, h * LANE:(h + 1) * LANE] = (o / jnp.sum(p, axis=-1, keepdims=True)).astype(o_ref.dtype)


def _mem_attention(qarr, qblock, mem_kv):
    B, S, _ = qarr.shape
    n_mem = mem_kv.shape[1]
    tq = _tile(S, 1024)
    return pl.pallas_call(
        _mem_kernel,
        grid=(B, S // tq),
        in_specs=[pl.BlockSpec((None, tq, MEM_WIDTH), lambda b, i: (b, i, qblock)),
                  pl.BlockSpec((None, n_mem, 2 * MEM_WIDTH), lambda b, i: (b, 0, 0))],
        out_specs=pl.BlockSpec((None, tq, MEM_WIDTH), lambda b, i: (b, i, 0)),
        out_shape=jax.ShapeDtypeStruct((B, S, MEM_WIDTH), BF16),
        compiler_params=_params("parallel", "parallel"),
        name="mem_attn",
    )(qarr, mem_kv)


def _dil_kernel(q_ref, kp_ref, kc_ref, vp_ref, vc_ref, bias_ref, o_ref, lse_ref):
    n = DIL_BLOCK
    has_prev = pl.program_id(1) > 0
    row = lax.broadcasted_iota(jnp.int32, (n, 2 * n), 0)
    col = lax.broadcasted_iota(jnp.int32, (n, 2 * n), 1)
    valid = jnp.logical_and(jnp.logical_and(col >= row, col <= row + n),
                            jnp.logical_or(col >= n, has_prev))
    lane = lax.broadcasted_iota(jnp.int32, (n, LANE), 1)
    dn = (((1,), (1,)), ((), ()))
    heads = [slice(h * LANE, (h + 1) * LANE) for h in range(N_HEADS)]
    scores = []
    for h, sl in enumerate(heads):
        k2 = jnp.concatenate([kp_ref[:, sl], kc_ref[:, sl]], axis=0)
        s = lax.dot_general(q_ref[:, sl], k2, dn, preferred_element_type=F32) + bias_ref[h]
        scores.append(jnp.where(valid, s, -jnp.inf))
    ms = [jnp.max(s, axis=-1, keepdims=True) for s in scores]
    ps = [jnp.exp2(s - m).astype(BF16) for s, m in zip(scores, ms)]
    lse_all = jnp.zeros((n, LANE), F32)
    for h, sl in enumerate(heads):
        v2 = jnp.concatenate([vp_ref[:, sl], vc_ref[:, sl]], axis=0)
        pv = jnp.dot(ps[h], jnp.concatenate([v2, jnp.ones_like(v2)], axis=-1),
                     preferred_element_type=F32)
        den = pv[:, LANE:]
        o_ref[h] = pv[:, :LANE] / den
        lse_all = jnp.where(lane == h, ms[h] + jnp.log2(den), lse_all)
    lse_ref[...] = lse_all


def _dil_attention(pg, bias):
    N, Ls, _ = pg.shape
    n = DIL_BLOCK
    blk = lambda part, prev: pl.BlockSpec(
        (None, n, MIX_WIDTH),
        (lambda s, i: (s, jnp.maximum(i - 1, 0), part)) if prev else (lambda s, i: (s, i, part)))
    return pl.pallas_call(
        _dil_kernel,
        grid=(N, Ls // n),
        in_specs=[blk(0, False), blk(1, True), blk(1, False), blk(2, True), blk(2, False),
                  pl.BlockSpec((N_HEADS, n, 2 * n), lambda s, i: (0, 0, 0))],
        out_specs=[pl.BlockSpec((None, N_HEADS, n, LANE), lambda s, i: (s, 0, i, 0)),
                   pl.BlockSpec((None, n, LANE), lambda s, i: (s, i, 0))],
        out_shape=[jax.ShapeDtypeStruct((N, N_HEADS, Ls, LANE), F32),
                   jax.ShapeDtypeStruct((N, Ls, LANE), F32)],
        compiler_params=_params("parallel", "arbitrary"),
        name="dil_attn",
    )(pg, pg, pg, pg, pg, bias)


def _dil_combine_kernel(o1_ref, o2_ref, o3_ref, l1_ref, l2_ref, l3_ref, out_ref, stage_ref, *, d2, d3):
    n = l3_ref.shape[1]
    q = d3 // d2
    for r in range(d3):
        rows = [(pl.ds(r, n, stride=d3),), (r % d2, pl.ds(r // d2, n, stride=q)), (r, slice(None))]
        ls = [l_ref[idx + (slice(None),)]
              for l_ref, idx in zip((l1_ref, l2_ref, l3_ref), rows)]
        m = functools.reduce(jnp.maximum, ls)
        es = [jnp.exp2(l - m) for l in ls]
        inv = 1.0 / functools.reduce(jnp.add, es)
        ws = [e * inv for e in es]
        for h in range(N_HEADS):
            o = (ws[0][:, h:h + 1] * o1_ref[h, pl.ds(r, n, stride=d3), :]
                 + ws[1][:, h:h + 1] * o2_ref[r % d2, h, pl.ds(r // d2, n, stride=q), :]
                 + ws[2][:, h:h + 1] * o3_ref[r, h, :, :])
            stage_ref[h, pl.ds(r, n, stride=d3), :] = o
    for h in range(N_HEADS):
        out_ref[:, h * LANE:(h + 1) * LANE] = stage_ref[h].astype(out_ref.dtype)


def _dil_combine(outs, lses, B, S, tm=256):
    (_, d1), (_, d2), (_, d3) = DIL_GROUPS
    assert d1 == 1 and d3 % d2 == 0
    tm = _tile(S, tm)
    H = N_HEADS
    o_specs = [pl.BlockSpec((None, H, tm, LANE), lambda b, i: (b, 0, i, 0)),
               pl.BlockSpec((None, d2, H, tm // d2, LANE), lambda b, i: (b, 0, 0, i, 0)),
               pl.BlockSpec((None, d3, H, tm // d3, LANE), lambda b, i: (b, 0, 0, i, 0))]
    l_specs = [pl.BlockSpec((None, tm, LANE), lambda b, i: (b, i, 0)),
               pl.BlockSpec((None, d2, tm // d2, LANE), lambda b, i: (b, 0, i, 0)),
               pl.BlockSpec((None, d3, tm // d3, LANE), lambda b, i: (b, 0, i, 0))]
    return pl.pallas_call(
        functools.partial(_dil_combine_kernel, d2=d2, d3=d3),
        grid=(B, S // tm),
        in_specs=o_specs + l_specs,
        out_specs=pl.BlockSpec((None, tm, MIX_WIDTH), lambda b, i: (b, i, 0)),
        out_shape=jax.ShapeDtypeStruct((B, S, MIX_WIDTH), BF16),
        scratch_shapes=[pltpu.VMEM((H, tm, LANE), F32)],
        compiler_params=_params("parallel", "parallel"),
        name="dil_combine",
    )(outs[0].reshape(B, H, S, LANE), outs[1].reshape(B, d2, H, S // d2, LANE),
      outs[2].reshape(B, d3, H, S // d3, LANE),
      lses[0].reshape(B, S, LANE), lses[1].reshape(B, d2, S // d2, LANE),
      lses[2].reshape(B, d3, S // d3, LANE))


def _dsa_topk_kernel(qi_ref, wi_ref, ki_ref, mask_ref, key_ref, jsel_ref, *, tq, tk, S, n_sel):
    i = pl.program_id(1)
    q0 = i * tq
    n_chunks = (q0 + tq + tk - 1) // tk
    dn = (((1,), (1,)), ((), ()))
    krow = lax.broadcasted_iota(jnp.int32, (tk, tq), 0)
    qpos = q0 + lax.broadcasted_iota(jnp.int32, (tk, tq), 1)

    wi_t = (wi_ref[...] * (IDX_HEADS ** -0.5 * IDX_DIM ** -0.5)).T

    def idx_body(c, carry):
        off = pl.multiple_of(c * tk, tk)
        kic = ki_ref[pl.ds(off, tk), :].astype(BF16)
        acc = jnp.zeros((tk, tq), F32)
        for h in range(IDX_HEADS):
            d = lax.dot_general(kic, qi_ref[:, h * LANE:(h + 1) * LANE], dn,
                                preferred_element_type=F32)
            acc = acc + wi_t[IDX_DIM + h:IDX_DIM + h + 1, :] * jnp.maximum(d, 0.0)
        score = jnp.where(off + krow <= qpos, acc, -jnp.inf)
        bits = pltpu.bitcast(score, jnp.int32)
        key_ref[pl.ds(off, tk), :] = bits ^ ((bits >> 31) & jnp.int32(0x7FFFFFFF))
        return carry

    lax.fori_loop(0, n_chunks, idx_body, 0)

    def count(pred):
        sub = 64
        def body(c, acc):
            off = pl.multiple_of(c * tk, tk)
            ones = jnp.where(pred(key_ref[pl.ds(off, tk), :], off), 1.0, 0.0)
            return acc + jnp.sum(ones.reshape(tk // sub, sub, tq), axis=0)

        acc = lax.fori_loop(0, n_chunks, body, jnp.zeros((sub, tq), F32))
        return jnp.sum(acc, axis=0, keepdims=True)

    k_sel = float(n_sel)
    c0 = count(lambda k, off: k >= 0)
    lo0 = jnp.where(c0 >= k_sel, jnp.int32(0), jnp.int32(-2 ** 31))
    n0 = jnp.where(c0 >= k_sel, c0, (n_chunks * tk).astype(F32))

    def bis_cond(state):
        it, _, n_lo = state
        return jnp.logical_and(it < 31, jnp.max(jnp.abs(n_lo - k_sel)) > 0.0)

    def bis_body(state):
        it, lo, n_lo = state
        cand = lo + (jnp.int32(1) << (30 - it))
        cnt = count(lambda k, off: k >= cand)
        take = cnt >= k_sel
        return it + 1, jnp.where(take, cand, lo), jnp.where(take, cnt, n_lo)

    _, thr, n_ge = lax.while_loop(bis_cond, bis_body, (jnp.int32(0), lo0, n0))
    n_bits = max(1, (S - 1).bit_length())
    jsel_ref[...] = jnp.full_like(jsel_ref, S)

    @pl.when(jnp.max(n_ge) > k_sel)
    def _():
        need = k_sel - count(lambda k, off: k > thr)

        def tie_body(it, jlo):
            cand = jlo + (jnp.int32(1) << (n_bits - 1 - it))
            below = count(lambda k, off: jnp.logical_and(k == thr, off + krow < cand))
            return jnp.where(below < need, cand, jlo)

        jsel = lax.fori_loop(0, n_bits, tie_body, jnp.zeros((1, tq), jnp.int32))
        jsel_ref[...] = jnp.broadcast_to(jsel, jsel_ref.shape)

    jsel = jsel_ref[:1, :]
    mask_ref[...] = jnp.full_like(mask_ref, -jnp.inf)

    def mask_body(c, carry):
        off = pl.multiple_of(c * tk, tk)
        k = key_ref[pl.ds(off, tk), :]
        kpos = off + krow
        tie = jnp.logical_and(k == thr, kpos <= jsel)
        sel = jnp.logical_and(jnp.logical_or(k > thr, tie), kpos <= qpos)
        mask_ref[:, pl.ds(off, tk)] = jnp.where(sel, 0.0, -jnp.inf).T.astype(mask_ref.dtype)
        return carry

    lax.fori_loop(0, n_chunks, mask_body, 0)


def _dsa_select(pm, aux, B, S, n_sel, tq=256, tk=512):
    tq, tk = _tile(S, tq), _tile(S, tk)
    return pl.pallas_call(
        functools.partial(_dsa_topk_kernel, tq=tq, tk=tk, S=S, n_sel=n_sel),
        grid=(B, S // tq),
        in_specs=[pl.BlockSpec((None, tq, MIX_WIDTH), lambda b, i: (b, i, 1)),
                  pl.BlockSpec((None, tq, LANE), lambda b, i: (b, i, 0)),
                  pl.BlockSpec((None, S, LANE), lambda b, i: (b, 0, 0))],
        out_specs=pl.BlockSpec((None, tq, S), lambda b, i: (b, i, 0)),
        out_shape=jax.ShapeDtypeStruct((B, S, S), BF16),
        scratch_shapes=[pltpu.VMEM((S, tq), jnp.int32), pltpu.VMEM((8, tq), jnp.int32)],
        compiler_params=_params("parallel", "arbitrary"),
        name="dsa_select",
    )(pm, aux, aux)


def _dsa_attn_kernel(q_ref, k_ref, v_ref, mask_ref, gt_ref, o_ref, s_ref, m_ref, acc_ref, *, tq, tk, S):
    i = pl.program_id(2)
    q0 = i * tq
    n_chunks = (q0 + tq + tk - 1) // tk
    R = q_ref.shape[1] // LANE
    nblk = tq // LANE
    dn = (((1,), (1,)), ((), ()))
    qs = [q_ref[:, r * LANE:(r + 1) * LANE] for r in range(R)]

    def produce(c):
        off = pl.multiple_of(c * tk, tk)
        kc = k_ref[pl.ds(off, tk), :]
        mask = mask_ref[:, pl.ds(off, tk)].astype(F32)
        st = S - q0 + off
        scores = []
        for r in range(R):
            bias = jnp.concatenate(
                [gt_ref[r, :, pl.ds(pl.multiple_of(st + (nblk - 1 - a) * LANE, LANE), tk)]
                 for a in range(nblk)], axis=0)
            scores.append(lax.dot_general(qs[r], kc, dn, preferred_element_type=F32) + bias + mask)
        return scores

    def values(c):
        return [v_ref[pl.ds(pl.multiple_of(c * tk, tk), tk), :]] * R

    _online_softmax(n_chunks, produce, values, s_ref, m_ref, acc_ref)
    for r in range(R):
        acc = acc_ref[r]
        o_ref[:, r * LANE:(r + 1) * LANE] = (acc[:, :LANE] / acc[:, LANE:]).astype(o_ref.dtype)


def _dsa_attention(pm, mask, gt, B, S, tq=256, tk=512):
    tq, tk = _tile(S, tq), _tile(S, tk)
    R = N_HEADS // DSA_KV_HEADS
    kblk = 2 * MIX_WIDTH // LANE
    return pl.pallas_call(
        functools.partial(_dsa_attn_kernel, tq=tq, tk=tk, S=S),
        grid=(DSA_KV_HEADS, B, S // tq),
        in_specs=[pl.BlockSpec((None, tq, R * LANE), lambda g, b, i: (b, i, g)),
                  pl.BlockSpec((None, S, LANE), lambda g, b, i: (b, 0, kblk + g)),
                  pl.BlockSpec((None, S, LANE), lambda g, b, i: (b, 0, kblk + DSA_KV_HEADS + g)),
                  pl.BlockSpec((None, tq, S), lambda g, b, i: (b, i, 0)),
                  pl.BlockSpec((R, LANE, gt.shape[2]), lambda g, b, i: (g, 0, 0))],
        out_specs=pl.BlockSpec((None, tq, R * LANE), lambda g, b, i: (b, i, g)),
        out_shape=jax.ShapeDtypeStruct((B, S, MIX_WIDTH), BF16),
        scratch_shapes=[pltpu.VMEM((2, R, tq, tk), F32), pltpu.VMEM((R, tq, LANE), F32),
                        pltpu.VMEM((R, tq, 2 * HEAD_DIM), F32)],
        compiler_params=_params("parallel", "parallel", "arbitrary"),
        name="dsa_attn",
    )(pm, pm, pm, mask, gt)


def _t5_bucket(dist):
    n = jnp.maximum(dist, 0)
    exact = REL_BUCKETS // 2
    nf = jnp.maximum(n, 1).astype(F32)
    large = exact + (jnp.log(nf / exact) / math.log(REL_MAX_DIST / exact)
                     * (REL_BUCKETS - exact)).astype(jnp.int32)
    large = jnp.minimum(large, REL_BUCKETS - 1)
    return jnp.where(n < exact, n, large)


def _t5_bias(t5_table, dist):
    onehot = jax.nn.one_hot(_t5_bucket(dist), REL_BUCKETS, dtype=F32)
    table = jnp.einsum('...b,bh->h...', onehot, t5_table.astype(F32), precision=lax.Precision.HIGHEST)
    return table * LOG2E


def _dil_bias(t5_table, dil):
    n = DIL_BLOCK
    rel = jnp.arange(n)[:, None] + n - jnp.arange(2 * n)[None, :]
    return _t5_bias(t5_table, rel * dil)


def _dsa_bias_table(t5_table, S, tq, tk):
    lt = S + tq + tk
    tb = _t5_bias(t5_table, S + tq - 1 - jnp.arange(lt + LANE))
    return jnp.stack([tb[:, LANE - 1 - i:LANE - 1 - i + lt] for i in range(LANE)], axis=1)


def _headnorm_cols(spec):
    eg, ef = [], []
    for gain, count, scale in spec:
        if gain is None:
            eg.append(jnp.ones((count * LANE,), F32))
            ef.append(jnp.zeros((count * LANE,), F32))
        else:
            eg.append(jnp.tile(gain.astype(F32) * scale, count))
            ef.append(jnp.ones((count * LANE,), F32))
    return jnp.concatenate(eg), jnp.concatenate(ef)


def _pad_cols(w, width):
    return jnp.pad(w, ((0, 0), (0, width - w.shape[1])))


def _fox_layer(x, norm_g, w_in, b_f, qk_g, mem_g):
    B, S, _ = x.shape
    W = MIX_WIDTH
    w_main = jnp.concatenate([w_in[:, :3 * W], w_in[:, 3 * W + N_HEADS:]], axis=1).astype(BF16)
    w_gate = _pad_cols(w_in[:, 3 * W:3 * W + N_HEADS], LANE).astype(BF16)
    eg, ef = _headnorm_cols([(qk_g[0], N_HEADS, HEAD_DIM ** -0.5 * LOG2E), (qk_g[1], N_HEADS, 1.0),
                             (None, N_HEADS, 1.0), (mem_g, MEM_HEADS, HEAD_DIM ** -0.5)])
    pm = _proj(x, norm_g, w_main, eg, ef)
    fg = _proj(x, norm_g, w_gate, out_dtype=F32)
    cum_t = _fox_gates(fg, b_f)
    mix = _flash([(pm, 0, True)], [(pm, W, True)], (pm, 2 * W, True), B, S,
                 decay_t=cum_t.reshape(B, LANE, 1, S))
    return mix, pm, 3 * W // MEM_WIDTH


def _mla_layer(x, positions, norm_g, w_in, q_norm, w_uq, kv_norm, w_ukv, nope_g, rope_g, mem_g):
    B, S, _ = x.shape
    scale = (NOPE_DIM + ROPE_DIM) ** -0.5 * LOG2E
    lat = Q_LORA + KV_LORA
    w_lat = jnp.concatenate([w_in[:, :lat], _pad_cols(w_in[:, lat:lat + ROPE_DIM], LANE)],
                            axis=1).astype(BF16)
    w_mem = w_in[:, lat + ROPE_DIM:].astype(BF16)
    pl_ = _proj(x, norm_g, w_lat, out_dtype=F32, tn=lat + LANE)
    eg, ef = _headnorm_cols([(mem_g, MEM_HEADS, HEAD_DIM ** -0.5)])
    pmem = _proj(x, norm_g, w_mem, eg, ef)

    uq = w_uq.reshape(Q_LORA, N_HEADS, NOPE_DIM + ROPE_DIM)
    w_qn = uq[:, :, :NOPE_DIM].reshape(Q_LORA, MIX_WIDTH).astype(BF16)
    w_qr = jnp.pad(uq[:, :, NOPE_DIM:], ((0, 0), (0, 0), (0, LANE - ROPE_DIM))
                   ).reshape(Q_LORA, N_HEADS * LANE).astype(BF16)
    ukv = w_ukv.reshape(KV_LORA, N_HEADS, NOPE_DIM + HEAD_DIM)
    w_kv = jnp.concatenate([ukv[:, :, :NOPE_DIM].reshape(KV_LORA, MIX_WIDTH),
                            ukv[:, :, NOPE_DIM:].reshape(KV_LORA, MIX_WIDTH)], axis=1).astype(BF16)
    eg, ef = _headnorm_cols([(nope_g[0], N_HEADS, scale)])
    qn = _proj(pl_, q_norm, w_qn, eg, ef, kblock=0)
    qr_raw = _proj(pl_, q_norm, w_qr, kblock=0, out_dtype=F32)
    eg, ef = _headnorm_cols([(nope_g[1], N_HEADS, 1.0), (None, N_HEADS, 1.0)])
    kv = _proj(pl_, kv_norm, w_kv, eg, ef, kblock=1)

    half = ROPE_DIM // 2
    inv = ROPE_THETA ** (-jnp.arange(half, dtype=F32) / half)
    ang = positions.astype(F32)[..., None] * inv
    cos, sin = jnp.cos(ang), jnp.sin(ang)
    zero = jnp.zeros((B, S, LANE - ROPE_DIM), F32)
    cos_t = jnp.concatenate([cos, cos, zero], axis=-1)
    sin_t = jnp.concatenate([-sin, sin, zero], axis=-1)
    qr = _rope(qr_raw, cos_t, sin_t, rope_g[0] * scale)
    kr = _rope(pl_[:, :, lat:], cos_t, sin_t, rope_g[1])
    mix = _flash([(qn, 0, True), (qr, 0, True)], [(kv, 0, True), (kr, 0, False)],
                 (kv, MIX_WIDTH, True), B, S)
    return mix, pmem, 0


def _dil_layer(x, norm_g, w_in, qk_g, t5_table, mem_g):
    B, S, _ = x.shape
    W = MIX_WIDTH
    outs, lses = [], []
    for gi, (win, dil) in enumerate(DIL_GROUPS):
        assert win // dil == DIL_BLOCK and (S // dil) % DIL_BLOCK == 0
        w_g = w_in[:, gi * 3 * W:(gi + 1) * 3 * W].astype(BF16)
        eg, ef = _headnorm_cols([(qk_g[gi, 0], N_HEADS, HEAD_DIM ** -0.5 * LOG2E),
                                 (qk_g[gi, 1], N_HEADS, 1.0), (None, N_HEADS, 1.0)])
        pg = _proj(x, norm_g, w_g, eg, ef, dil=dil, tn=1024)
        o, lse = _dil_attention(pg.reshape(B * dil, S // dil, 3 * W), _dil_bias(t5_table, dil))
        outs.append(o)
        lses.append(lse)
    mix = _dil_combine(outs, lses, B, S)
    eg, ef = _headnorm_cols([(mem_g, MEM_HEADS, HEAD_DIM ** -0.5)])
    pmem = _proj(x, norm_g, w_in[:, len(DIL_GROUPS) * 3 * W:].astype(BF16), eg, ef)
    return mix, pmem, 0


def _dsa_layer(x, norm_g, w_in, qk_g, t5_table, mem_g):
    B, S, _ = x.shape
    W = MIX_WIDTH
    kvw = DSA_KV_HEADS * HEAD_DIM
    o_k, o_v, o_qi = W, W + kvw, W + 2 * kvw
    o_ki = o_qi + IDX_HEADS * IDX_DIM
    o_wi = o_ki + IDX_DIM
    o_mem = o_wi + IDX_HEADS
    w_qi = jnp.pad(w_in[:, o_qi:o_ki].reshape(D_MODEL, IDX_HEADS, IDX_DIM),
                   ((0, 0), (0, 0), (0, LANE - IDX_DIM))).reshape(D_MODEL, IDX_HEADS * LANE)
    w_main = jnp.concatenate([w_in[:, :W], w_qi, w_in[:, o_k:o_qi], w_in[:, o_mem:]],
                             axis=1).astype(BF16)
    w_aux = _pad_cols(w_in[:, o_ki:o_mem], LANE).astype(BF16)
    eg, ef = _headnorm_cols([(qk_g[0], N_HEADS, HEAD_DIM ** -0.5 * LOG2E), (None, IDX_HEADS, 1.0),
                             (qk_g[1], DSA_KV_HEADS, 1.0), (None, DSA_KV_HEADS, 1.0),
                             (mem_g, MEM_HEADS, HEAD_DIM ** -0.5)])
    pm = _proj(x, norm_g, w_main, eg, ef)
    aux = _proj(x, norm_g, w_aux, out_dtype=F32)
    mask = _dsa_select(pm, aux, B, S, min(TOPK_MAX, S // 4), tq=512)
    tq, tk = _tile(S, 512), _tile(S, 512)
    mix = _dsa_attention(pm, mask, _dsa_bias_table(t5_table, S, tq, tk), B, S, tq, tk)
    return mix, pm, (2 * W + 2 * kvw) // MEM_WIDTH


def kernel(x, mem, positions, t5_table, ffn_norm, ffn_w_gate, ffn_w_up, ffn_w_down, attn_norm,
           mem_norm, mem_w_kv, mem_qk_g, w_out, a_w_in, a_b_f, a_qk_g, b_w_in, b_q_norm, b_w_uq,
           b_kv_norm, b_w_ukv, b_nope_g, b_rope_g, c_w_in, c_qk_g, d_w_in, d_qk_g):
    B, S, D = x.shape
    depth = ffn_norm.shape[0]
    n_mixers = 4

    def ffn(xc, i, k, next_gain=None):
        return _ffn(xc.reshape(B * S, D), ffn_norm[i, k], ffn_w_gate[i, k].astype(BF16),
                    ffn_w_up[i, k].astype(BF16), ffn_w_down[i, k].astype(BF16), next_gain)

    for i in range(depth):
        m, j = i % n_mixers, i // n_mixers
        x, h = ffn(x, i, 0, attn_norm[i])
        x, h = x.reshape(B, S, D), h.reshape(B, S, D)
        mem_g = mem_qk_g[i]
        if m == 0:
            mix, qarr, qblock = _fox_layer(h, None, a_w_in[j], a_b_f[j], a_qk_g[j], mem_g[0])
        elif m == 1:
            mix, qarr, qblock = _mla_layer(h, positions, None, b_w_in[j], b_q_norm[j],
                                           b_w_uq[j], b_kv_norm[j], b_w_ukv[j], b_nope_g[j],
                                           b_rope_g[j], mem_g[0])
        elif m == 2:
            mix, qarr, qblock = _dil_layer(h, None, c_w_in[j], c_qk_g[j], t5_table, mem_g[0])
        else:
            mix, qarr, qblock = _dsa_layer(h, None, d_w_in[j], d_qk_g[j], t5_table, mem_g[0])
        eg, ef = _headnorm_cols([(mem_g[1], MEM_HEADS, 1.0), (None, MEM_HEADS, 1.0)])
        mem_kv = _proj(mem, mem_norm[i], mem_w_kv[i].astype(BF16), eg, ef)
        mo = _mem_attention(qarr, qblock, mem_kv)
        wo = w_out[i].astype(BF16)
        x = _out_proj(x.reshape(B * S, D), mix.reshape(B * S, MIX_WIDTH),
                      mo.reshape(B * S, MEM_WIDTH), wo[:MIX_WIDTH], wo[MIX_WIDTH:]).reshape(B, S, D)
        x = ffn(x, i, 1).reshape(B, S, D)
    return x
```

```python
import functools
import math

import jax
import jax.numpy as jnp
from jax import lax
from jax.experimental import pallas as pl
from jax.experimental.pallas import tpu as pltpu

F32 = jnp.float32
BF16 = jnp.bfloat16

LANE = 128
D_MODEL = 2048
N_HEADS = 16
HEAD_DIM = 128
MIX_WIDTH = N_HEADS * HEAD_DIM
MEM_HEADS = 4
MEM_WIDTH = MEM_HEADS * HEAD_DIM
D_FF = 5632
RMS_EPS = 1e-6
REL_BUCKETS = 32
REL_MAX_DIST = 2048
Q_LORA = 512
KV_LORA = 512
NOPE_DIM = 128
ROPE_DIM = 64
ROPE_THETA = 10000.0
DIL_GROUPS = ((128, 1), (512, 4), (2048, 16))
DIL_BLOCK = 128
DSA_KV_HEADS = 4
IDX_HEADS = 16
IDX_DIM = 64
TOPK_MAX = 256
VMEM_LIMIT = 56 * 1024 * 1024
FFN_VMEM_LIMIT = 62 * 1024 * 1024
NEG_BIG = -1e30
LOG2E = math.log2(math.e)


def _params(*sem):
    return pltpu.CompilerParams(dimension_semantics=sem, vmem_limit_bytes=VMEM_LIMIT)


def _tile(n, pref):
    t = min(n, pref)
    while n % t:
        t //= 2
    return t


def _ffn_kernel(x_ref, g_ref, wg_ref, wu_ref, wd_ref, o_ref, h_ref):
    _ffn_body(x_ref, g_ref, wg_ref, wu_ref, wd_ref, o_ref, h_ref, ())


def _ffn_kernel_with_next(x_ref, g_ref, wg_ref, wu_ref, wd_ref, ng_ref, o_ref, hn_ref, h_ref):
    _ffn_body(x_ref, g_ref, wg_ref, wu_ref, wd_ref, o_ref, h_ref, (ng_ref, hn_ref))


def _ffn_body(x_ref, g_ref, wg_ref, wu_ref, wd_ref, o_ref, h_ref, next_refs):
    j = pl.program_id(1)
    last = pl.num_programs(1) - 1
    tm = x_ref.shape[0]
    halves = [slice(0, tm // 2), slice(tm // 2, tm)]

    def down(h):
        g = jnp.dot(h, wg_ref[...], preferred_element_type=F32)
        u = jnp.dot(h, wu_ref[...], preferred_element_type=F32)
        a = (g * jax.nn.sigmoid(g) * u).astype(BF16)
        return jnp.dot(a, wd_ref[...], preferred_element_type=F32)

    @pl.when(j == 0)
    def _():
        for rows in halves:
            x = x_ref[rows, :]
            ms = jnp.mean(x * x, axis=-1, keepdims=True)
            h = (x * lax.rsqrt(ms + RMS_EPS) * g_ref[...]).astype(BF16)
            h_ref[rows, :] = h
            o_ref[rows, :] = down(h)

    @pl.when(jnp.logical_and(j > 0, j < last))
    def _():
        o_ref[...] += down(h_ref[...])

    @pl.when(jnp.logical_and(j > 0, j == last))
    def _():
        for rows in halves:
            y = x_ref[rows, :] + 0.5 * (o_ref[rows, :] + down(h_ref[rows, :]))
            o_ref[rows, :] = y
            if next_refs:
                ng_ref, hn_ref = next_refs
                ms = jnp.mean(y * y, axis=-1, keepdims=True)
                hn_ref[rows, :] = (y * lax.rsqrt(ms + RMS_EPS) * ng_ref[...]).astype(BF16)


def _ffn(x2, gain, wg, wu, wd, next_gain=None):
    T, D = x2.shape
    F = wg.shape[1]
    tm, tf = _tile(T, 1024), _tile(F, 512)
    assert F // tf >= 2 and tm % 16 == 0
    row_spec = pl.BlockSpec((tm, D), lambda i, j: (i, 0))
    vec_spec = pl.BlockSpec((1, D), lambda i, j: (0, 0))
    in_specs = [row_spec, vec_spec,
                pl.BlockSpec((D, tf), lambda i, j: (0, j)),
                pl.BlockSpec((D, tf), lambda i, j: (0, j)),
                pl.BlockSpec((tf, D), lambda i, j: (j, 0))]
    args = [x2, gain.reshape(1, D), wg, wu, wd]
    out_specs, out_shape = row_spec, jax.ShapeDtypeStruct((T, D), F32)
    kern = _ffn_kernel
    if next_gain is not None:
        in_specs.append(vec_spec)
        args.append(next_gain.reshape(1, D).astype(F32))
        out_specs = [row_spec, row_spec]
        out_shape = [out_shape, jax.ShapeDtypeStruct((T, D), BF16)]
        kern = _ffn_kernel_with_next
    return pl.pallas_call(
        kern,
        grid=(T // tm, F // tf),
        in_specs=in_specs,
        out_specs=out_specs,
        out_shape=out_shape,
        scratch_shapes=[pltpu.VMEM((tm, D), BF16)],
        compiler_params=pltpu.CompilerParams(dimension_semantics=("parallel", "arbitrary"),
                                             vmem_limit_bytes=FFN_VMEM_LIMIT),
        name="ffn",
    )(*args)


def _proj_kernel(x_ref, g_ref, w_ref, eg_ref, ef_ref, o_ref, *scratch, epilogue, dil, prenormed):
    scratch = list(scratch)
    if prenormed:
        h_ref = x_ref
    else:
        h_ref = scratch.pop(0)

        @pl.when(pl.program_id(2) == 0)
        def _():
            x = x_ref[...].astype(F32)
            ms = jnp.mean(x * x, axis=-1, keepdims=True)
            h_ref[...] = (x * lax.rsqrt(ms + RMS_EPS) * g_ref[...]).astype(BF16)

    stage = scratch
    tm = h_ref.shape[0]
    parts = 2 if tm % (2 * 8 * dil) == 0 else 1
    rows_per, n = tm // parts, tm // parts // dil
    for p in range(parts):
        rows = slice(p * rows_per, (p + 1) * rows_per)
        acc = jnp.dot(h_ref[rows, :], w_ref[...], preferred_element_type=F32)
        for c in range(acc.shape[1] // LANE):
            sl = slice(c * LANE, (c + 1) * LANE)
            y = acc[:, sl]
            if epilogue:
                ms = jnp.mean(y * y, axis=-1, keepdims=True)
                scale = jnp.where(ef_ref[:, sl] > 0.0, lax.rsqrt(ms + RMS_EPS), 1.0)
                y = y * scale * eg_ref[:, sl]
            if dil == 1:
                o_ref[0, rows, sl] = y.astype(o_ref.dtype)
            else:
                stage[0][c, rows, :] = y
                for r in range(dil):
                    o_ref[r, p * n:(p + 1) * n, sl] = stage[0][
                        c, pl.ds(p * rows_per + r, n, stride=dil), :].astype(o_ref.dtype)


def _proj(x, gain, w, eg=None, ef=None, *, dil=1, kblock=0, out_dtype=BF16, tm=1024, tn=512):
    B, S, C = x.shape
    K, N = w.shape
    tm, tn = _tile(S, tm), _tile(N, tn)
    epilogue = eg is not None
    if not epilogue:
        eg = jnp.ones((1, N), F32)
        ef = jnp.zeros((1, N), F32)
    prenormed = gain is None
    if prenormed:
        assert x.dtype == BF16
        gain = jnp.ones((K,), F32)
    out = pl.pallas_call(
        functools.partial(_proj_kernel, epilogue=epilogue, dil=dil, prenormed=prenormed),
        grid=(B, S // tm, N // tn),
        in_specs=[
            pl.BlockSpec((None, tm, K), lambda b, i, j: (b, i, kblock)),
            pl.BlockSpec((1, K), lambda b, i, j: (0, 0)),
            pl.BlockSpec((K, tn), lambda b, i, j: (0, j)),
            pl.BlockSpec((1, tn), lambda b, i, j: (0, j)),
            pl.BlockSpec((1, tn), lambda b, i, j: (0, j)),
        ],
        out_specs=pl.BlockSpec((None, dil, tm // dil, tn), lambda b, i, j: (b, 0, i, j)),
        out_shape=jax.ShapeDtypeStruct((B, dil, S // dil, N), out_dtype),
        scratch_shapes=([] if prenormed else [pltpu.VMEM((tm, K), BF16)])
        + ([pltpu.VMEM((tn // LANE, tm, LANE), F32)] if dil > 1 else []),
        compiler_params=_params("parallel", "parallel", "arbitrary"),
        name="proj",
    )(x, gain.reshape(1, K).astype(F32), w, eg.reshape(1, N), ef.reshape(1, N))
    return out.reshape(B, S, N) if dil == 1 else out


def _out_kernel(x_ref, a_ref, b_ref, wa_ref, wb_ref, o_ref):
    acc = jnp.dot(a_ref[...], wa_ref[...], preferred_element_type=F32)
    acc += jnp.dot(b_ref[...], wb_ref[...], preferred_element_type=F32)
    o_ref[...] = x_ref[...] + acc


def _out_proj(x2, mix2, mo2, wa, wb):
    T, D = x2.shape
    tm, tn = _tile(T, 1024), _tile(D, 1024)
    ka, kb = mix2.shape[1], mo2.shape[1]
    return pl.pallas_call(
        _out_kernel,
        grid=(T // tm, D // tn),
        in_specs=[
            pl.BlockSpec((tm, tn), lambda i, j: (i, j)),
            pl.BlockSpec((tm, ka), lambda i, j: (i, 0)),
            pl.BlockSpec((tm, kb), lambda i, j: (i, 0)),
            pl.BlockSpec((ka, tn), lambda i, j: (0, j)),
            pl.BlockSpec((kb, tn), lambda i, j: (0, j)),
        ],
        out_specs=pl.BlockSpec((tm, tn), lambda i, j: (i, j)),
        out_shape=jax.ShapeDtypeStruct((T, D), F32),
        compiler_params=_params("parallel", "arbitrary"),
        name="out_proj",
    )(x2, mix2, mo2, wa, wb)


def _lanes(x, reps):
    return x if reps == 1 else jnp.concatenate([x] * reps, axis=-1)


def _online_softmax(n_chunks, produce, values, s_ref, m_ref, acc_ref, finish=None):
    G = s_ref.shape[1]

    def consume(j, scores):
        ps, alphas = [], []
        for g, s in enumerate(scores):
            m_old = m_ref[g]
            m_new = jnp.maximum(m_old, jnp.max(s, axis=-1, keepdims=True))
            ps.append(jnp.exp2(s - _lanes(m_new, s.shape[1] // LANE)).astype(BF16))
            alphas.append(jnp.exp2(m_old - m_new))
            m_ref[g] = m_new
        return ps, alphas

    def accumulate(j, ps, alphas):
        for g, (p, alpha, v) in enumerate(zip(ps, alphas, values(j))):
            v1 = jnp.concatenate([v, jnp.ones_like(v)], axis=-1)
            acc_ref[g] = _lanes(alpha, 2) * acc_ref[g] + jnp.dot(p, v1, preferred_element_type=F32)

    m_ref[...] = jnp.full_like(m_ref, NEG_BIG)
    acc_ref[...] = jnp.zeros_like(acc_ref)
    for g, s in enumerate(produce(0)):
        s_ref[0, g] = s

    def step(j, slot):
        cur = [s_ref[slot, g] for g in range(G)]
        nxt = produce(j + 1)
        ps, alphas = consume(j, cur)
        for g, s in enumerate(nxt):
            s_ref[1 - slot, g] = s
        accumulate(j, ps, alphas)

    def body(jj, carry):
        step(2 * jj, 0)
        step(2 * jj + 1, 1)
        return carry

    last = n_chunks - 1
    lax.fori_loop(0, last // 2, body, 0)

    @pl.when(last % 2 == 1)
    def _():
        step(last - 1, 0)

    cur = [s_ref[last % 2, g] for g in range(G)]
    if finish is not None:
        cur = [finish(s) for s in cur]
    accumulate(last, *consume(last, cur))


def _flash_kernel(*refs, nqk, decay, tq, G):
    q_refs, k_refs, v_ref = refs[:nqk], refs[nqk:2 * nqk], refs[2 * nqk]
    pos = 2 * nqk + 1
    if decay:
        ck_ref = refs[pos]
        pos += 1
    o_ref, s_ref, m_ref, acc_ref = refs[pos:pos + 4]
    i = pl.program_id(2)

    def head(ref, g, rows=slice(None)):
        if ref.shape[-1] == LANE:
            return ref[rows, :]
        return ref[rows, g * LANE:(g + 1) * LANE]

    qs = []
    for g in range(G):
        parts = [head(r, g) for r in q_refs]
        qs.append(parts[0] if nqk == 1 else jnp.concatenate(parts, axis=-1))

    def chunk(j):
        return pl.ds(pl.multiple_of(j * tq, tq), tq)

    def produce(j):
        rows = chunk(j)
        scores = []
        for g in range(G):
            parts = [head(r, g, rows) for r in k_refs]
            k = parts[0] if nqk == 1 else jnp.concatenate(parts, axis=-1)
            s = lax.dot_general(qs[g], k, (((1,), (1,)), ((), ())), preferred_element_type=F32)
            scores.append(s - ck_ref[g, :, rows] if decay else s)
        return scores

    def values(j):
        return [head(v_ref, g, chunk(j)) for g in range(G)]

    def causal(s):
        row = lax.broadcasted_iota(jnp.int32, (tq, tq), 0)
        col = lax.broadcasted_iota(jnp.int32, (tq, tq), 1)
        return jnp.where(col <= row, s, -jnp.inf)

    _online_softmax(i + 1, produce, values, s_ref, m_ref, acc_ref, finish=causal)
    for g in range(G):
        acc = acc_ref[g]
        o_ref[:, g * LANE:(g + 1) * LANE] = (acc[:, :LANE] / acc[:, LANE:]).astype(o_ref.dtype)


def _flash(q_parts, k_parts, v_part, B, S, decay_t=None, tq=512, G=4):
    tq = _tile(S, tq)
    nqk = len(q_parts)
    W = G * LANE
    args, specs = [], []
    for arr, off, _ in q_parts:
        args.append(arr)
        specs.append(pl.BlockSpec((None, tq, W), lambda b, h, i, off=off: (b, i, off // W + h)))
    for arr, off, per_head in k_parts + [v_part]:
        args.append(arr)
        if per_head:
            specs.append(pl.BlockSpec((None, S, W), lambda b, h, i, off=off: (b, 0, off // W + h)))
        else:
            specs.append(pl.BlockSpec((None, S, LANE), lambda b, h, i, off=off: (b, 0, off // LANE)))
    if decay_t is not None:
        args.append(decay_t)
        specs.append(pl.BlockSpec((None, G, 1, S), lambda b, h, i: (b, h, 0, 0)))
    return pl.pallas_call(
        functools.partial(_flash_kernel, nqk=nqk, decay=decay_t is not None, tq=tq, G=G),
        grid=(B, N_HEADS // G, S // tq),
        in_specs=specs,
        out_specs=pl.BlockSpec((None, tq, W), lambda b, h, i: (b, i, h)),
        out_shape=jax.ShapeDtypeStruct((B, S, MIX_WIDTH), BF16),
        scratch_shapes=[pltpu.VMEM((2, G, tq, tq), F32), pltpu.VMEM((G, tq, LANE), F32),
                        pltpu.VMEM((G, tq, 2 * HEAD_DIM), F32)],
        compiler_params=_params("parallel", "parallel", "arbitrary"),
        name="flash",
    )(*args)


def _gate_kernel(fg_ref, bf_ref, cumt_ref, carry_ref, *, ts):
    @pl.when(pl.program_id(1) == 0)
    def _():
        carry_ref[...] = jnp.zeros_like(carry_ref)

    z = fg_ref[...] + bf_ref[...]
    lf = jnp.minimum(z, 0.0) - jnp.log1p(jnp.exp(-jnp.abs(z)))
    hi = lf.astype(BF16)
    r1 = lf - hi.astype(F32)
    mid = r1.astype(BF16)
    lo = (r1 - mid.astype(F32)).astype(BF16)
    row = lax.broadcasted_iota(jnp.int32, (ts, ts), 0)
    col = lax.broadcasted_iota(jnp.int32, (ts, ts), 1)
    tri = jnp.where(col <= row, 1.0, 0.0).astype(BF16)
    cum = (jnp.dot(tri, hi, preferred_element_type=F32)
           + jnp.dot(tri, mid, preferred_element_type=F32)
           + jnp.dot(tri, lo, preferred_element_type=F32)) + carry_ref[...]
    carry_ref[...] = cum[ts - 1:ts, :]
    cumt_ref[...] = (cum * LOG2E).T


def _fox_gates(fg, b_f):
    B, S, _ = fg.shape
    ts = _tile(S, 512)
    bf = jnp.zeros((1, LANE), F32).at[0, :N_HEADS].set(b_f.astype(F32))
    return pl.pallas_call(
        functools.partial(_gate_kernel, ts=ts),
        grid=(B, S // ts),
        in_specs=[pl.BlockSpec((None, ts, LANE), lambda b, i: (b, i, 0)),
                  pl.BlockSpec((1, LANE), lambda b, i: (0, 0))],
        out_specs=pl.BlockSpec((None, LANE, ts), lambda b, i: (b, 0, i)),
        out_shape=jax.ShapeDtypeStruct((B, LANE, S), F32),
        scratch_shapes=[pltpu.VMEM((1, LANE), F32)],
        compiler_params=_params("parallel", "arbitrary"),
        name="fox_gates",
    )(fg, bf)


def _rope_kernel(x_ref, cos_ref, sin_ref, g_ref, o_ref):
    cos, sin, g = cos_ref[...], sin_ref[...], g_ref[...]
    for c in range(x_ref.shape[1] // LANE):
        sl = slice(c * LANE, (c + 1) * LANE)
        x = x_ref[:, sl]
        ms = jnp.sum(x * x, axis=-1, keepdims=True) * (1.0 / ROPE_DIM)
        y = x * lax.rsqrt(ms + RMS_EPS) * g
        partner = pltpu.roll(y, ROPE_DIM // 2, 1) + pltpu.roll(y, LANE - ROPE_DIM // 2, 1)
        o_ref[:, sl] = (y * cos + partner * sin).astype(o_ref.dtype)


def _rope(x, cos, sin, gain):
    B, S, N = x.shape
    ts = _tile(S, 512)
    g = jnp.zeros((1, LANE), F32).at[0, :ROPE_DIM].set(gain.astype(F32))
    return pl.pallas_call(
        _rope_kernel,
        grid=(B, S // ts),
        in_specs=[pl.BlockSpec((None, ts, N), lambda b, i: (b, i, 0)),
                  pl.BlockSpec((None, ts, LANE), lambda b, i: (b, i, 0)),
                  pl.BlockSpec((None, ts, LANE), lambda b, i: (b, i, 0)),
                  pl.BlockSpec((1, LANE), lambda b, i: (0, 0))],
        out_specs=pl.BlockSpec((None, ts, N), lambda b, i: (b, i, 0)),
        out_shape=jax.ShapeDtypeStruct((B, S, N), BF16),
        compiler_params=_params("parallel", "parallel"),
        name="rope",
    )(x, cos, sin, g)


def _mem_kernel(q_ref, kv_ref, o_ref):
    for h in range(MEM_HEADS):
        q = q_ref[:, h * LANE:(h + 1) * LANE]
        k = kv_ref[:, h * LANE:(h + 1) * LANE]
        v = kv_ref[:, MEM_WIDTH + h * LANE:MEM_WIDTH + (h + 1) * LANE]
        s = lax.dot_general(q, k, (((1,), (1,)), ((), ())), preferred_element_type=F32)
        p = jnp.exp(s - jnp.max(s, axis=-1, keepdims=True))
        o = jnp.dot(p.astype(BF16), v, preferred_element_type=F32)
        o_ref[:, h * LANE:(h + 1) * LANE] = (o / jnp.sum(p, axis=-1, keepdims=True)).astype(o_ref.dtype)


def _mem_attention(qarr, qblock, mem_kv):
    B, S, _ = qarr.shape
    n_mem = mem_kv.shape[1]
    tq = _tile(S, 1024)
    return pl.pallas_call(
        _mem_kernel,
        grid=(B, S // tq),
        in_specs=[pl.BlockSpec((None, tq, MEM_WIDTH), lambda b, i: (b, i, qblock)),
                  pl.BlockSpec((None, n_mem, 2 * MEM_WIDTH), lambda b, i: (b, 0, 0))],
        out_specs=pl.BlockSpec((None, tq, MEM_WIDTH), lambda b, i: (b, i, 0)),
        out_shape=jax.ShapeDtypeStruct((B, S, MEM_WIDTH), BF16),
        compiler_params=_params("parallel", "parallel"),
        name="mem_attn",
    )(qarr, mem_kv)


def _dil_kernel(q_ref, kp_ref, kc_ref, vp_ref, vc_ref, bias_ref, o_ref, lse_ref):
    n = DIL_BLOCK
    nsub = q_ref.shape[0] // n
    row = lax.broadcasted_iota(jnp.int32, (n, 2 * n), 0)
    col = lax.broadcasted_iota(jnp.int32, (n, 2 * n), 1)
    band = jnp.logical_and(col >= row, col <= row + n)
    first = jnp.logical_and(band, jnp.logical_or(col >= n, pl.program_id(1) > 0))
    lane = lax.broadcasted_iota(jnp.int32, (n, LANE), 1)
    dn = (((1,), (1,)), ((), ()))
    heads = [slice(h * LANE, (h + 1) * LANE) for h in range(N_HEADS)]

    def window(prev_ref, cur_ref, u, sl):
        if u == 0:
            return jnp.concatenate([prev_ref[:, sl], cur_ref[:n, sl]], axis=0)
        return cur_ref[(u - 1) * n:(u + 1) * n, sl]

    for u in range(nsub):
        rows = slice(u * n, (u + 1) * n)
        valid = first if u == 0 else band
        scores = []
        for h, sl in enumerate(heads):
            s = lax.dot_general(q_ref[rows, sl], window(kp_ref, kc_ref, u, sl), dn,
                                preferred_element_type=F32) + bias_ref[h]
            scores.append(jnp.where(valid, s, -jnp.inf))
        ms = [jnp.max(s, axis=-1, keepdims=True) for s in scores]
        ps = [jnp.exp2(s - m).astype(BF16) for s, m in zip(scores, ms)]
        lse_all = jnp.zeros((n, LANE), F32)
        for h, sl in enumerate(heads):
            v2 = window(vp_ref, vc_ref, u, sl)
            pv = jnp.dot(ps[h], jnp.concatenate([v2, jnp.ones_like(v2)], axis=-1),
                         preferred_element_type=F32)
            den = pv[:, LANE:]
            o_ref[h, rows, :] = pv[:, :LANE] / den
            lse_all = jnp.where(lane == h, ms[h] + jnp.log2(den), lse_all)
        lse_ref[rows, :] = lse_all


def _dil_attention(pg, bias):
    N, Ls, _ = pg.shape
    n = DIL_BLOCK
    nsub = 2 if Ls % (2 * n) == 0 else 1
    tr = nsub * n

    def cur(part):
        return pl.BlockSpec((None, tr, MIX_WIDTH), lambda s, i: (s, i, part))

    def prev(part):
        return pl.BlockSpec((None, n, MIX_WIDTH), lambda s, i: (s, jnp.maximum(i * nsub - 1, 0), part))

    return pl.pallas_call(
        _dil_kernel,
        grid=(N, Ls // tr),
        in_specs=[cur(0), prev(1), cur(1), prev(2), cur(2),
                  pl.BlockSpec((N_HEADS, n, 2 * n), lambda s, i: (0, 0, 0))],
        out_specs=[pl.BlockSpec((None, N_HEADS, tr, LANE), lambda s, i: (s, 0, i, 0)),
                   pl.BlockSpec((None, tr, LANE), lambda s, i: (s, i, 0))],
        out_shape=[jax.ShapeDtypeStruct((N, N_HEADS, Ls, LANE), F32),
                   jax.ShapeDtypeStruct((N, Ls, LANE), F32)],
        compiler_params=_params("parallel", "arbitrary"),
        name="dil_attn",
    )(pg, pg, pg, pg, pg, bias)


def _dil_combine_kernel(o1_ref, o2_ref, o3_ref, l1_ref, l2_ref, l3_ref, out_ref, stage_ref, *, d2, d3):
    n = l3_ref.shape[1]
    q = d3 // d2
    for r in range(d3):
        rows = [(pl.ds(r, n, stride=d3),), (r % d2, pl.ds(r // d2, n, stride=q)), (r, slice(None))]
        ls = [l_ref[idx + (slice(None),)]
              for l_ref, idx in zip((l1_ref, l2_ref, l3_ref), rows)]
        m = functools.reduce(jnp.maximum, ls)
        es = [jnp.exp2(l - m) for l in ls]
        inv = 1.0 / functools.reduce(jnp.add, es)
        ws = [e * inv for e in es]
        for h in range(N_HEADS):
            o = (ws[0][:, h:h + 1] * o1_ref[h, pl.ds(r, n, stride=d3), :]
                 + ws[1][:, h:h + 1] * o2_ref[r % d2, h, pl.ds(r // d2, n, stride=q), :]
                 + ws[2][:, h:h + 1] * o3_ref[r, h, :, :])
            stage_ref[h, pl.ds(r, n, stride=d3), :] = o
    for h in range(N_HEADS):
        out_ref[:, h * LANE:(h + 1) * LANE] = stage_ref[h].astype(out_ref.dtype)


def _dil_combine(outs, lses, B, S, tm=256):
    (_, d1), (_, d2), (_, d3) = DIL_GROUPS
    assert d1 == 1 and d3 % d2 == 0
    tm = _tile(S, tm)
    H = N_HEADS
    o_specs = [pl.BlockSpec((None, H, tm, LANE), lambda b, i: (b, 0, i, 0)),
               pl.BlockSpec((None, d2, H, tm // d2, LANE), lambda b, i: (b, 0, 0, i, 0)),
               pl.BlockSpec((None, d3, H, tm // d3, LANE), lambda b, i: (b, 0, 0, i, 0))]
    l_specs = [pl.BlockSpec((None, tm, LANE), lambda b, i: (b, i, 0)),
               pl.BlockSpec((None, d2, tm // d2, LANE), lambda b, i: (b, 0, i, 0)),
               pl.BlockSpec((None, d3, tm // d3, LANE), lambda b, i: (b, 0, i, 0))]
    return pl.pallas_call(
        functools.partial(_dil_combine_kernel, d2=d2, d3=d3),
        grid=(B, S // tm),
        in_specs=o_specs + l_specs,
        out_specs=pl.BlockSpec((None, tm, MIX_WIDTH), lambda b, i: (b, i, 0)),
        out_shape=jax.ShapeDtypeStruct((B, S, MIX_WIDTH), BF16),
        scratch_shapes=[pltpu.VMEM((H, tm, LANE), F32)],
        compiler_params=_params("parallel", "parallel"),
        name="dil_combine",
    )(outs[0].reshape(B, H, S, LANE), outs[1].reshape(B, d2, H, S // d2, LANE),
      outs[2].reshape(B, d3, H, S // d3, LANE),
      lses[0].reshape(B, S, LANE), lses[1].reshape(B, d2, S // d2, LANE),
      lses[2].reshape(B, d3, S // d3, LANE))


def _dsa_topk_kernel(qi_ref, wi_ref, ki_ref, mask_ref, key_ref, jsel_ref, *, tq, tk, S, n_sel):
    i = pl.program_id(1)
    q0 = i * tq
    n_chunks = (q0 + tq + tk - 1) // tk
    dn = (((1,), (1,)), ((), ()))
    krow = lax.broadcasted_iota(jnp.int32, (tk, tq), 0)
    qpos = q0 + lax.broadcasted_iota(jnp.int32, (tk, tq), 1)

    wi_t = (wi_ref[...] * (IDX_HEADS ** -0.5 * IDX_DIM ** -0.5)).T

    def idx_body(c, carry):
        off = pl.multiple_of(c * tk, tk)
        kic = ki_ref[pl.ds(off, tk), :].astype(BF16)
        acc = jnp.zeros((tk, tq), F32)
        for h in range(IDX_HEADS):
            d = lax.dot_general(kic, qi_ref[:, h * LANE:(h + 1) * LANE], dn,
                                preferred_element_type=F32)
            acc = acc + wi_t[IDX_DIM + h:IDX_DIM + h + 1, :] * jnp.maximum(d, 0.0)
        score = jnp.where(off + krow <= qpos, acc, -jnp.inf)
        bits = pltpu.bitcast(score, jnp.int32)
        key_ref[pl.ds(off, tk), :] = bits ^ ((bits >> 31) & jnp.int32(0x7FFFFFFF))
        return carry

    lax.fori_loop(0, n_chunks, idx_body, 0)

    def count(pred):
        sub = 64
        def body(c, acc):
            off = pl.multiple_of(c * tk, tk)
            ones = jnp.where(pred(key_ref[pl.ds(off, tk), :], off), 1.0, 0.0)
            return acc + jnp.sum(ones.reshape(tk // sub, sub, tq), axis=0)

        acc = lax.fori_loop(0, n_chunks, body, jnp.zeros((sub, tq), F32))
        return jnp.sum(acc, axis=0, keepdims=True)

    k_sel = float(n_sel)
    c0 = count(lambda k, off: k >= 0)
    lo0 = jnp.where(c0 >= k_sel, jnp.int32(0), jnp.int32(-2 ** 31))
    n0 = jnp.where(c0 >= k_sel, c0, (n_chunks * tk).astype(F32))

    def bis_cond(state):
        it, _, n_lo = state
        return jnp.logical_and(it < 31, jnp.max(jnp.abs(n_lo - k_sel)) > 0.0)

    def bis_body(state):
        it, lo, n_lo = state
        cand = lo + (jnp.int32(1) << (30 - it))
        cnt = count(lambda k, off: k >= cand)
        take = cnt >= k_sel
        return it + 1, jnp.where(take, cand, lo), jnp.where(take, cnt, n_lo)

    _, thr, n_ge = lax.while_loop(bis_cond, bis_body, (jnp.int32(0), lo0, n0))
    n_bits = max(1, (S - 1).bit_length())
    jsel_ref[...] = jnp.full_like(jsel_ref, S)

    @pl.when(jnp.max(n_ge) > k_sel)
    def _():
        need = k_sel - count(lambda k, off: k > thr)

        def tie_body(it, jlo):
            cand = jlo + (jnp.int32(1) << (n_bits - 1 - it))
            below = count(lambda k, off: jnp.logical_and(k == thr, off + krow < cand))
            return jnp.where(below < need, cand, jlo)

        jsel = lax.fori_loop(0, n_bits, tie_body, jnp.zeros((1, tq), jnp.int32))
        jsel_ref[...] = jnp.broadcast_to(jsel, jsel_ref.shape)

    jsel = jsel_ref[:1, :]
    mask_ref[...] = jnp.full_like(mask_ref, -jnp.inf)

    def mask_body(c, carry):
        off = pl.multiple_of(c * tk, tk)
        k = key_ref[pl.ds(off, tk), :]
        kpos = off + krow
        tie = jnp.logical_and(k == thr, kpos <= jsel)
        sel = jnp.logical_and(jnp.logical_or(k > thr, tie), kpos <= qpos)
        mask_ref[:, pl.ds(off, tk)] = jnp.where(sel, 0.0, -jnp.inf).T.astype(mask_ref.dtype)
        return carry

    lax.fori_loop(0, n_chunks, mask_body, 0)


def _dsa_select(pm, aux, B, S, n_sel, tq=256, tk=512):
    tq, tk = _tile(S, tq), _tile(S, tk)
    return pl.pallas_call(
        functools.partial(_dsa_topk_kernel, tq=tq, tk=tk, S=S, n_sel=n_sel),
        grid=(B, S // tq),
        in_specs=[pl.BlockSpec((None, tq, MIX_WIDTH), lambda b, i: (b, i, 1)),
                  pl.BlockSpec((None, tq, LANE), lambda b, i: (b, i, 0)),
                  pl.BlockSpec((None, S, LANE), lambda b, i: (b, 0, 0))],
        out_specs=pl.BlockSpec((None, tq, S), lambda b, i: (b, i, 0)),
        out_shape=jax.ShapeDtypeStruct((B, S, S), BF16),
        scratch_shapes=[pltpu.VMEM((S, tq), jnp.int32), pltpu.VMEM((8, tq), jnp.int32)],
        compiler_params=_params("parallel", "arbitrary"),
        name="dsa_select",
    )(pm, aux, aux)


def _dsa_attn_kernel(q_ref, k_ref, v_ref, mask_ref, gt_ref, o_ref, s_ref, m_ref, acc_ref, *, tq, tk, S):
    i = pl.program_id(2)
    q0 = i * tq
    n_chunks = (q0 + tq + tk - 1) // tk
    R = q_ref.shape[1] // LANE
    nblk = tq // LANE
    dn = (((1,), (1,)), ((), ()))
    qs = [q_ref[:, r * LANE:(r + 1) * LANE] for r in range(R)]

    def produce(c):
        off = pl.multiple_of(c * tk, tk)
        kc = k_ref[pl.ds(off, tk), :]
        mask = mask_ref[:, pl.ds(off, tk)].astype(F32)
        st = S - q0 + off
        scores = []
        for r in range(R):
            bias = jnp.concatenate(
                [gt_ref[r, :, pl.ds(pl.multiple_of(st + (nblk - 1 - a) * LANE, LANE), tk)]
                 for a in range(nblk)], axis=0)
            scores.append(lax.dot_general(qs[r], kc, dn, preferred_element_type=F32) + bias + mask)
        return scores

    def values(c):
        return [v_ref[pl.ds(pl.multiple_of(c * tk, tk), tk), :]] * R

    _online_softmax(n_chunks, produce, values, s_ref, m_ref, acc_ref)
    for r in range(R):
        acc = acc_ref[r]
        o_ref[:, r * LANE:(r + 1) * LANE] = (acc[:, :LANE] / acc[:, LANE:]).astype(o_ref.dtype)


def _dsa_attention(pm, mask, gt, B, S, tq=256, tk=512):
    tq, tk = _tile(S, tq), _tile(S, tk)
    R = N_HEADS // DSA_KV_HEADS
    kblk = 2 * MIX_WIDTH // LANE
    return pl.pallas_call(
        functools.partial(_dsa_attn_kernel, tq=tq, tk=tk, S=S),
        grid=(DSA_KV_HEADS, B, S // tq),
        in_specs=[pl.BlockSpec((None, tq, R * LANE), lambda g, b, i: (b, i, g)),
                  pl.BlockSpec((None, S, LANE), lambda g, b, i: (b, 0, kblk + g)),
                  pl.BlockSpec((None, S, LANE), lambda g, b, i: (b, 0, kblk + DSA_KV_HEADS + g)),
                  pl.BlockSpec((None, tq, S), lambda g, b, i: (b, i, 0)),
                  pl.BlockSpec((R, LANE, gt.shape[2]), lambda g, b, i: (g, 0, 0))],
        out_specs=pl.BlockSpec((None, tq, R * LANE), lambda g, b, i: (b, i, g)),
        out_shape=jax.ShapeDtypeStruct((B, S, MIX_WIDTH), BF16),
        scratch_shapes=[pltpu.VMEM((2, R, tq, tk), F32), pltpu.VMEM((R, tq, LANE), F32),
                        pltpu.VMEM((R, tq, 2 * HEAD_DIM), F32)],
        compiler_params=_params("parallel", "parallel", "arbitrary"),
        name="dsa_attn",
    )(pm, pm, pm, mask, gt)


def _t5_bucket(dist):
    n = jnp.maximum(dist, 0)
    exact = REL_BUCKETS // 2
    nf = jnp.maximum(n, 1).astype(F32)
    large = exact + (jnp.log(nf / exact) / math.log(REL_MAX_DIST / exact)
                     * (REL_BUCKETS - exact)).astype(jnp.int32)
    large = jnp.minimum(large, REL_BUCKETS - 1)
    return jnp.where(n < exact, n, large)


def _t5_bias(t5_table, dist):
    onehot = jax.nn.one_hot(_t5_bucket(dist), REL_BUCKETS, dtype=F32)
    table = jnp.einsum('...b,bh->h...', onehot, t5_table.astype(F32), precision=lax.Precision.HIGHEST)
    return table * LOG2E


def _dil_bias(t5_table, dil):
    n = DIL_BLOCK
    rel = jnp.arange(n)[:, None] + n - jnp.arange(2 * n)[None, :]
    return _t5_bias(t5_table, rel * dil)


def _dsa_bias_table(t5_table, S, tq, tk):
    lt = S + tq + tk
    tb = _t5_bias(t5_table, S + tq - 1 - jnp.arange(lt + LANE))
    return jnp.stack([tb[:, LANE - 1 - i:LANE - 1 - i + lt] for i in range(LANE)], axis=1)


def _headnorm_cols(spec):
    eg, ef = [], []
    for gain, count, scale in spec:
        if gain is None:
            eg.append(jnp.ones((count * LANE,), F32))
            ef.append(jnp.zeros((count * LANE,), F32))
        else:
            eg.append(jnp.tile(gain.astype(F32) * scale, count))
            ef.append(jnp.ones((count * LANE,), F32))
    return jnp.concatenate(eg), jnp.concatenate(ef)


def _pad_cols(w, width):
    return jnp.pad(w, ((0, 0), (0, width - w.shape[1])))


def _fox_layer(x, norm_g, w_in, b_f, qk_g, mem_g):
    B, S, _ = x.shape
    W = MIX_WIDTH
    w_main = jnp.concatenate([w_in[:, :3 * W], w_in[:, 3 * W + N_HEADS:]], axis=1).astype(BF16)
    w_gate = _pad_cols(w_in[:, 3 * W:3 * W + N_HEADS], LANE).astype(BF16)
    eg, ef = _headnorm_cols([(qk_g[0], N_HEADS, HEAD_DIM ** -0.5 * LOG2E), (qk_g[1], N_HEADS, 1.0),
                             (None, N_HEADS, 1.0), (mem_g, MEM_HEADS, HEAD_DIM ** -0.5)])
    pm = _proj(x, norm_g, w_main, eg, ef)
    fg = _proj(x, norm_g, w_gate, out_dtype=F32)
    cum_t = _fox_gates(fg, b_f)
    mix = _flash([(pm, 0, True)], [(pm, W, True)], (pm, 2 * W, True), B, S,
                 decay_t=cum_t.reshape(B, LANE, 1, S))
    return mix, pm, 3 * W // MEM_WIDTH


def _mla_layer(x, positions, norm_g, w_in, q_norm, w_uq, kv_norm, w_ukv, nope_g, rope_g, mem_g):
    B, S, _ = x.shape
    scale = (NOPE_DIM + ROPE_DIM) ** -0.5 * LOG2E
    lat = Q_LORA + KV_LORA
    w_lat = jnp.concatenate([w_in[:, :lat], _pad_cols(w_in[:, lat:lat + ROPE_DIM], LANE)],
                            axis=1).astype(BF16)
    w_mem = w_in[:, lat + ROPE_DIM:].astype(BF16)
    pl_ = _proj(x, norm_g, w_lat, out_dtype=F32, tn=lat + LANE)
    eg, ef = _headnorm_cols([(mem_g, MEM_HEADS, HEAD_DIM ** -0.5)])
    pmem = _proj(x, norm_g, w_mem, eg, ef)

    uq = w_uq.reshape(Q_LORA, N_HEADS, NOPE_DIM + ROPE_DIM)
    w_qn = uq[:, :, :NOPE_DIM].reshape(Q_LORA, MIX_WIDTH).astype(BF16)
    w_qr = jnp.pad(uq[:, :, NOPE_DIM:], ((0, 0), (0, 0), (0, LANE - ROPE_DIM))
                   ).reshape(Q_LORA, N_HEADS * LANE).astype(BF16)
    ukv = w_ukv.reshape(KV_LORA, N_HEADS, NOPE_DIM + HEAD_DIM)
    w_kv = jnp.concatenate([ukv[:, :, :NOPE_DIM].reshape(KV_LORA, MIX_WIDTH),
                            ukv[:, :, NOPE_DIM:].reshape(KV_LORA, MIX_WIDTH)], axis=1).astype(BF16)
    eg, ef = _headnorm_cols([(nope_g[0], N_HEADS, scale)])
    qn = _proj(pl_, q_norm, w_qn, eg, ef, kblock=0, tn=2048)
    qr_raw = _proj(pl_, q_norm, w_qr, kblock=0, out_dtype=F32, tn=1024)
    eg, ef = _headnorm_cols([(nope_g[1], N_HEADS, 1.0), (None, N_HEADS, 1.0)])
    kv = _proj(pl_, kv_norm, w_kv, eg, ef, kblock=1, tn=2048)

    half = ROPE_DIM // 2
    inv = ROPE_THETA ** (-jnp.arange(half, dtype=F32) / half)
    ang = positions.astype(F32)[..., None] * inv
    cos, sin = jnp.cos(ang), jnp.sin(ang)
    zero = jnp.zeros((B, S, LANE - ROPE_DIM), F32)
    cos_t = jnp.concatenate([cos, cos, zero], axis=-1)
    sin_t = jnp.concatenate([-sin, sin, zero], axis=-1)
    qr = _rope(qr_raw, cos_t, sin_t, rope_g[0] * scale)
    kr = _rope(pl_[:, :, lat:], cos_t, sin_t, rope_g[1])
    mix = _flash([(qn, 0, True), (qr, 0, True)], [(kv, 0, True), (kr, 0, False)],
                 (kv, MIX_WIDTH, True), B, S)
    return mix, pmem, 0


def _dil_layer(x, norm_g, w_in, qk_g, t5_table, mem_g):
    B, S, _ = x.shape
    W = MIX_WIDTH
    outs, lses = [], []
    for gi, (win, dil) in enumerate(DIL_GROUPS):
        assert win // dil == DIL_BLOCK and (S // dil) % DIL_BLOCK == 0
        w_g = w_in[:, gi * 3 * W:(gi + 1) * 3 * W].astype(BF16)
        eg, ef = _headnorm_cols([(qk_g[gi, 0], N_HEADS, HEAD_DIM ** -0.5 * LOG2E),
                                 (qk_g[gi, 1], N_HEADS, 1.0), (None, N_HEADS, 1.0)])
        pg = _proj(x, norm_g, w_g, eg, ef, dil=dil, tn=1024)
        o, lse = _dil_attention(pg.reshape(B * dil, S // dil, 3 * W), _dil_bias(t5_table, dil))
        outs.append(o)
        lses.append(lse)
    mix = _dil_combine(outs, lses, B, S)
    eg, ef = _headnorm_cols([(mem_g, MEM_HEADS, HEAD_DIM ** -0.5)])
    pmem = _proj(x, norm_g, w_in[:, len(DIL_GROUPS) * 3 * W:].astype(BF16), eg, ef)
    return mix, pmem, 0


def _dsa_layer(x, norm_g, w_in, qk_g, t5_table, mem_g):
    B, S, _ = x.shape
    W = MIX_WIDTH
    kvw = DSA_KV_HEADS * HEAD_DIM
    o_k, o_v, o_qi = W, W + kvw, W + 2 * kvw
    o_ki = o_qi + IDX_HEADS * IDX_DIM
    o_wi = o_ki + IDX_DIM
    o_mem = o_wi + IDX_HEADS
    w_qi = jnp.pad(w_in[:, o_qi:o_ki].reshape(D_MODEL, IDX_HEADS, IDX_DIM),
                   ((0, 0), (0, 0), (0, LANE - IDX_DIM))).reshape(D_MODEL, IDX_HEADS * LANE)
    w_main = jnp.concatenate([w_in[:, :W], w_qi, w_in[:, o_k:o_qi], w_in[:, o_mem:]],
                             axis=1).astype(BF16)
    w_aux = _pad_cols(w_in[:, o_ki:o_mem], LANE).astype(BF16)
    eg, ef = _headnorm_cols([(qk_g[0], N_HEADS, HEAD_DIM ** -0.5 * LOG2E), (None, IDX_HEADS, 1.0),
                             (qk_g[1], DSA_KV_HEADS, 1.0), (None, DSA_KV_HEADS, 1.0),
                             (mem_g, MEM_HEADS, HEAD_DIM ** -0.5)])
    pm = _proj(x, norm_g, w_main, eg, ef)
    aux = _proj(x, norm_g, w_aux, out_dtype=F32)
    mask = _dsa_select(pm, aux, B, S, min(TOPK_MAX, S // 4), tq=512)
    tq, tk = _tile(S, 512), _tile(S, 512)
    mix = _dsa_attention(pm, mask, _dsa_bias_table(t5_table, S, tq, tk), B, S, tq, tk)
    return mix, pm, (2 * W + 2 * kvw) // MEM_WIDTH


def kernel(x, mem, positions, t5_table, ffn_norm, ffn_w_gate, ffn_w_up, ffn_w_down, attn_norm,
           mem_norm, mem_w_kv, mem_qk_g, w_out, a_w_in, a_b_f, a_qk_g, b_w_in, b_q_norm, b_w_uq,
           b_kv_norm, b_w_ukv, b_nope_g, b_rope_g, c_w_in, c_qk_g, d_w_in, d_qk_g):
    B, S, D = x.shape
    depth = ffn_norm.shape[0]
    n_mixers = 4

    def ffn(xc, i, k, next_gain=None):
        return _ffn(xc.reshape(B * S, D), ffn_norm[i, k], ffn_w_gate[i, k].astype(BF16),
                    ffn_w_up[i, k].astype(BF16), ffn_w_down[i, k].astype(BF16), next_gain)

    for i in range(depth):
        m, j = i % n_mixers, i // n_mixers
        x, h = ffn(x, i, 0, attn_norm[i])
        x, h = x.reshape(B, S, D), h.reshape(B, S, D)
        mem_g = mem_qk_g[i]
        if m == 0:
            mix, qarr, qblock = _fox_layer(h, None, a_w_in[j], a_b_f[j], a_qk_g[j], mem_g[0])
        elif m == 1:
            mix, qarr, qblock = _mla_layer(h, positions, None, b_w_in[j], b_q_norm[j],
                                           b_w_uq[j], b_kv_norm[j], b_w_ukv[j], b_nope_g[j],
                                           b_rope_g[j], mem_g[0])
        elif m == 2:
            mix, qarr, qblock = _dil_layer(h, None, c_w_in[j], c_qk_g[j], t5_table, mem_g[0])
        else:
            mix, qarr, qblock = _dsa_layer(h, None, d_w_in[j], d_qk_g[j], t5_table, mem_g[0])
        eg, ef = _headnorm_cols([(mem_g[1], MEM_HEADS, 1.0), (None, MEM_HEADS, 1.0)])
        mem_kv = _proj(mem, mem_norm[i], mem_w_kv[i].astype(BF16), eg, ef)
        mo = _mem_attention(qarr, qblock, mem_kv)
        wo = w_out[i].astype(BF16)
        x = _out_proj(x.reshape(B * S, D), mix.reshape(B * S, MIX_WIDTH),
                      mo.reshape(B * S, MEM_WIDTH), wo[:MIX_WIDTH], wo[MIX_WIDTH:]).reshape(B, S, D)
        x = ffn(x, i, 1).reshape(B, S, D)
    return x
```

```python
import functools
import math

import jax
import jax.numpy as jnp
from jax import lax
from jax.experimental import pallas as pl
from jax.experimental.pallas import tpu as pltpu

F32 = jnp.float32
BF16 = jnp.bfloat16

LANE = 128
D_MODEL = 2048
N_HEADS = 16
HEAD_DIM = 128
MIX_WIDTH = N_HEADS * HEAD_DIM
MEM_HEADS = 4
MEM_WIDTH = MEM_HEADS * HEAD_DIM
D_FF = 5632
RMS_EPS = 1e-6
REL_BUCKETS = 32
REL_MAX_DIST = 2048
Q_LORA = 512
KV_LORA = 512
NOPE_DIM = 128
ROPE_DIM = 64
ROPE_THETA = 10000.0
DIL_GROUPS = ((128, 1), (512, 4), (2048, 16))
DIL_BLOCK = 128
PERM_ROWS = 256
DSA_KV_HEADS = 4
IDX_HEADS = 16
IDX_DIM = 64
TOPK_MAX = 256
VMEM_LIMIT = 56 * 1024 * 1024
FFN_VMEM_LIMIT = 62 * 1024 * 1024
NEG_BIG = -1e30
LOG2E = math.log2(math.e)


def _params(*sem):
    return pltpu.CompilerParams(dimension_semantics=sem, vmem_limit_bytes=VMEM_LIMIT)


def _tile(n, pref):
    t = min(n, pref)
    while n % t:
        t //= 2
    return t


def _ffn_kernel(x_ref, g_ref, wg_ref, wu_ref, wd_ref, o_ref, h_ref):
    _ffn_body(x_ref, g_ref, wg_ref, wu_ref, wd_ref, o_ref, h_ref, ())


def _ffn_kernel_with_next(x_ref, g_ref, wg_ref, wu_ref, wd_ref, ng_ref, o_ref, hn_ref, h_ref):
    _ffn_body(x_ref, g_ref, wg_ref, wu_ref, wd_ref, o_ref, h_ref, (ng_ref, hn_ref))


def _ffn_body(x_ref, g_ref, wg_ref, wu_ref, wd_ref, o_ref, h_ref, next_refs):
    j = pl.program_id(1)
    last = pl.num_programs(1) - 1
    tm = x_ref.shape[0]
    halves = [slice(0, tm // 2), slice(tm // 2, tm)]

    def down(h):
        g = jnp.dot(h, wg_ref[...], preferred_element_type=F32)
        u = jnp.dot(h, wu_ref[...], preferred_element_type=F32)
        a = (g * jax.nn.sigmoid(g) * u).astype(BF16)
        return jnp.dot(a, wd_ref[...], preferred_element_type=F32)

    @pl.when(j == 0)
    def _():
        for rows in halves:
            x = x_ref[rows, :]
            ms = jnp.mean(x * x, axis=-1, keepdims=True)
            h = (x * lax.rsqrt(ms + RMS_EPS) * g_ref[...]).astype(BF16)
            h_ref[rows, :] = h
            o_ref[rows, :] = down(h)

    @pl.when(jnp.logical_and(j > 0, j < last))
    def _():
        o_ref[...] += down(h_ref[...])

    @pl.when(jnp.logical_and(j > 0, j == last))
    def _():
        for rows in halves:
            y = x_ref[rows, :] + 0.5 * (o_ref[rows, :] + down(h_ref[rows, :]))
            o_ref[rows, :] = y
            if next_refs:
                ng_ref, hn_ref = next_refs
                ms = jnp.mean(y * y, axis=-1, keepdims=True)
                hn_ref[rows, :] = (y * lax.rsqrt(ms + RMS_EPS) * ng_ref[...]).astype(BF16)


def _ffn(x2, gain, wg, wu, wd, next_gain=None):
    T, D = x2.shape
    F = wg.shape[1]
    tm, tf = _tile(T, 1024), _tile(F, 512)
    assert F // tf >= 2 and tm % 16 == 0
    row_spec = pl.BlockSpec((tm, D), lambda i, j: (i, 0))
    vec_spec = pl.BlockSpec((1, D), lambda i, j: (0, 0))
    in_specs = [row_spec, vec_spec,
                pl.BlockSpec((D, tf), lambda i, j: (0, j)),
                pl.BlockSpec((D, tf), lambda i, j: (0, j)),
                pl.BlockSpec((tf, D), lambda i, j: (j, 0))]
    args = [x2, gain.reshape(1, D), wg, wu, wd]
    out_specs, out_shape = row_spec, jax.ShapeDtypeStruct((T, D), F32)
    kern = _ffn_kernel
    if next_gain is not None:
        in_specs.append(vec_spec)
        args.append(next_gain.reshape(1, D).astype(F32))
        out_specs = [row_spec, row_spec]
        out_shape = [out_shape, jax.ShapeDtypeStruct((T, D), BF16)]
        kern = _ffn_kernel_with_next
    return pl.pallas_call(
        kern,
        grid=(T // tm, F // tf),
        in_specs=in_specs,
        out_specs=out_specs,
        out_shape=out_shape,
        scratch_shapes=[pltpu.VMEM((tm, D), BF16)],
        compiler_params=pltpu.CompilerParams(dimension_semantics=("parallel", "arbitrary"),
                                             vmem_limit_bytes=FFN_VMEM_LIMIT),
        name="ffn",
    )(*args)


def _proj_kernel(x_ref, g_ref, w_ref, eg_ref, ef_ref, o_ref, *scratch, epilogue, dil, prenormed):
    scratch = list(scratch)
    if prenormed:
        h_ref = x_ref
    else:
        h_ref = scratch.pop(0)

        @pl.when(pl.program_id(2) == 0)
        def _():
            x = x_ref[...].astype(F32)
            ms = jnp.mean(x * x, axis=-1, keepdims=True)
            h_ref[...] = (x * lax.rsqrt(ms + RMS_EPS) * g_ref[...]).astype(BF16)

    tm = h_ref.shape[0]
    nb = PERM_ROWS // dil
    if dil > 1:
        hp_ref = scratch.pop(0)

        @pl.when(pl.program_id(2) == 0)
        def _():
            dst = lax.broadcasted_iota(jnp.int32, (PERM_ROWS, PERM_ROWS), 0)
            src = lax.broadcasted_iota(jnp.int32, (PERM_ROWS, PERM_ROWS), 1)
            perm = jnp.where(dst == (src % dil) * nb + src // dil, 1.0, 0.0).astype(BF16)
            for b in range(tm // PERM_ROWS):
                blk = slice(b * PERM_ROWS, (b + 1) * PERM_ROWS)
                hp_ref[blk, :] = jnp.dot(perm, h_ref[blk, :],
                                         preferred_element_type=F32).astype(BF16)

        h_ref = hp_ref
    parts = 2 if tm % (2 * PERM_ROWS) == 0 else 1
    rows_per = tm // parts
    for p in range(parts):
        rows = slice(p * rows_per, (p + 1) * rows_per)
        acc = jnp.dot(h_ref[rows, :], w_ref[...], preferred_element_type=F32)
        for c in range(acc.shape[1] // LANE):
            sl = slice(c * LANE, (c + 1) * LANE)
            y = acc[:, sl]
            if epilogue:
                ms = jnp.mean(y * y, axis=-1, keepdims=True)
                scale = jnp.where(ef_ref[:, sl] > 0.0, lax.rsqrt(ms + RMS_EPS), 1.0)
                y = y * scale * eg_ref[:, sl]
            y = y.astype(o_ref.dtype)
            if dil == 1:
                o_ref[0, rows, sl] = y
            else:
                for b in range(rows_per // PERM_ROWS):
                    a0 = (p * (rows_per // PERM_ROWS) + b) * nb
                    for r in range(dil):
                        o_ref[r, a0:a0 + nb, sl] = y[b * PERM_ROWS + r * nb:b * PERM_ROWS + (r + 1) * nb, :]


def _proj(x, gain, w, eg=None, ef=None, *, dil=1, kblock=0, out_dtype=BF16, tm=1024, tn=512):
    B, S, C = x.shape
    K, N = w.shape
    tm, tn = _tile(S, tm), _tile(N, tn)
    epilogue = eg is not None
    if not epilogue:
        eg = jnp.ones((1, N), F32)
        ef = jnp.zeros((1, N), F32)
    assert dil == 1 or (tm % PERM_ROWS == 0 and PERM_ROWS % (16 * dil) == 0)
    prenormed = gain is None
    if prenormed:
        assert x.dtype == BF16
        gain = jnp.ones((K,), F32)
    out = pl.pallas_call(
        functools.partial(_proj_kernel, epilogue=epilogue, dil=dil, prenormed=prenormed),
        grid=(B, S // tm, N // tn),
        in_specs=[
            pl.BlockSpec((None, tm, K), lambda b, i, j: (b, i, kblock)),
            pl.BlockSpec((1, K), lambda b, i, j: (0, 0)),
            pl.BlockSpec((K, tn), lambda b, i, j: (0, j)),
            pl.BlockSpec((1, tn), lambda b, i, j: (0, j)),
            pl.BlockSpec((1, tn), lambda b, i, j: (0, j)),
        ],
        out_specs=pl.BlockSpec((None, dil, tm // dil, tn), lambda b, i, j: (b, 0, i, j)),
        out_shape=jax.ShapeDtypeStruct((B, dil, S // dil, N), out_dtype),
        scratch_shapes=[pltpu.VMEM((tm, K), BF16)] * ((not prenormed) + (dil > 1)),
        compiler_params=_params("parallel", "parallel", "arbitrary"),
        name="proj",
    )(x, gain.reshape(1, K).astype(F32), w, eg.reshape(1, N), ef.reshape(1, N))
    return out.reshape(B, S, N) if dil == 1 else out


def _out_kernel(x_ref, a_ref, b_ref, wa_ref, wb_ref, o_ref):
    acc = jnp.dot(a_ref[...], wa_ref[...], preferred_element_type=F32)
    acc += jnp.dot(b_ref[...], wb_ref[...], preferred_element_type=F32)
    o_ref[...] = x_ref[...] + acc


def _out_proj(x2, mix2, mo2, wa, wb):
    T, D = x2.shape
    tm, tn = _tile(T, 1024), _tile(D, 1024)
    ka, kb = mix2.shape[1], mo2.shape[1]
    return pl.pallas_call(
        _out_kernel,
        grid=(T // tm, D // tn),
        in_specs=[
            pl.BlockSpec((tm, tn), lambda i, j: (i, j)),
            pl.BlockSpec((tm, ka), lambda i, j: (i, 0)),
            pl.BlockSpec((tm, kb), lambda i, j: (i, 0)),
            pl.BlockSpec((ka, tn), lambda i, j: (0, j)),
            pl.BlockSpec((kb, tn), lambda i, j: (0, j)),
        ],
        out_specs=pl.BlockSpec((tm, tn), lambda i, j: (i, j)),
        out_shape=jax.ShapeDtypeStruct((T, D), F32),
        compiler_params=_params("parallel", "arbitrary"),
        name="out_proj",
    )(x2, mix2, mo2, wa, wb)


def _lanes(x, reps):
    return x if reps == 1 else jnp.concatenate([x] * reps, axis=-1)


def _online_softmax(n_chunks, produce, values, s_ref, m_ref, acc_ref, finish=None):
    G = s_ref.shape[1]

    def consume(j, scores):
        ps, alphas = [], []
        for g, s in enumerate(scores):
            m_old = m_ref[g]
            m_new = jnp.maximum(m_old, jnp.max(s, axis=-1, keepdims=True))
            ps.append(jnp.exp2(s - _lanes(m_new, s.shape[1] // LANE)).astype(BF16))
            alphas.append(jnp.exp2(m_old - m_new))
            m_ref[g] = m_new
        return ps, alphas

    def accumulate(j, ps, alphas):
        for g, (p, alpha, v) in enumerate(zip(ps, alphas, values(j))):
            v1 = jnp.concatenate([v, jnp.ones_like(v)], axis=-1)
            acc_ref[g] = _lanes(alpha, 2) * acc_ref[g] + jnp.dot(p, v1, preferred_element_type=F32)

    m_ref[...] = jnp.full_like(m_ref, NEG_BIG)
    acc_ref[...] = jnp.zeros_like(acc_ref)
    for g, s in enumerate(produce(0)):
        s_ref[0, g] = s

    def step(j, slot):
        cur = [s_ref[slot, g] for g in range(G)]
        nxt = produce(j + 1)
        ps, alphas = consume(j, cur)
        for g, s in enumerate(nxt):
            s_ref[1 - slot, g] = s
        accumulate(j, ps, alphas)

    def body(jj, carry):
        step(2 * jj, 0)
        step(2 * jj + 1, 1)
        return carry

    last = n_chunks - 1
    lax.fori_loop(0, last // 2, body, 0)

    @pl.when(last % 2 == 1)
    def _():
        step(last - 1, 0)

    cur = [s_ref[last % 2, g] for g in range(G)]
    if finish is not None:
        cur = [finish(s) for s in cur]
    accumulate(last, *consume(last, cur))


def _flash_kernel(*refs, nqk, decay, tq, G):
    q_refs, k_refs, v_ref = refs[:nqk], refs[nqk:2 * nqk], refs[2 * nqk]
    pos = 2 * nqk + 1
    if decay:
        ck_ref = refs[pos]
        pos += 1
    o_ref, s_ref, m_ref, acc_ref = refs[pos:pos + 4]
    i = pl.program_id(2)

    def head(ref, g, rows=slice(None)):
        if ref.shape[-1] == LANE:
            return ref[rows, :]
        return ref[rows, g * LANE:(g + 1) * LANE]

    qs = []
    for g in range(G):
        parts = [head(r, g) for r in q_refs]
        qs.append(parts[0] if nqk == 1 else jnp.concatenate(parts, axis=-1))

    def chunk(j):
        return pl.ds(pl.multiple_of(j * tq, tq), tq)

    def produce(j):
        rows = chunk(j)
        scores = []
        for g in range(G):
            parts = [head(r, g, rows) for r in k_refs]
            k = parts[0] if nqk == 1 else jnp.concatenate(parts, axis=-1)
            s = lax.dot_general(qs[g], k, (((1,), (1,)), ((), ())), preferred_element_type=F32)
            scores.append(s - ck_ref[g, :, rows] if decay else s)
        return scores

    def values(j):
        return [head(v_ref, g, chunk(j)) for g in range(G)]

    def causal(s):
        row = lax.broadcasted_iota(jnp.int32, (tq, tq), 0)
        col = lax.broadcasted_iota(jnp.int32, (tq, tq), 1)
        return jnp.where(col <= row, s, -jnp.inf)

    _online_softmax(i + 1, produce, values, s_ref, m_ref, acc_ref, finish=causal)
    for g in range(G):
        acc = acc_ref[g]
        o_ref[:, g * LANE:(g + 1) * LANE] = (acc[:, :LANE] / acc[:, LANE:]).astype(o_ref.dtype)


def _flash(q_parts, k_parts, v_part, B, S, decay_t=None, tq=512, G=4):
    tq = _tile(S, tq)
    nqk = len(q_parts)
    W = G * LANE
    args, specs = [], []
    for arr, off, _ in q_parts:
        args.append(arr)
        specs.append(pl.BlockSpec((None, tq, W), lambda b, h, i, off=off: (b, i, off // W + h)))
    for arr, off, per_head in k_parts + [v_part]:
        args.append(arr)
        if per_head:
            specs.append(pl.BlockSpec((None, S, W), lambda b, h, i, off=off: (b, 0, off // W + h)))
        else:
            specs.append(pl.BlockSpec((None, S, LANE), lambda b, h, i, off=off: (b, 0, off // LANE)))
    if decay_t is not None:
        args.append(decay_t)
        specs.append(pl.BlockSpec((None, G, 1, S), lambda b, h, i: (b, h, 0, 0)))
    return pl.pallas_call(
        functools.partial(_flash_kernel, nqk=nqk, decay=decay_t is not None, tq=tq, G=G),
        grid=(B, N_HEADS // G, S // tq),
        in_specs=specs,
        out_specs=pl.BlockSpec((None, tq, W), lambda b, h, i: (b, i, h)),
        out_shape=jax.ShapeDtypeStruct((B, S, MIX_WIDTH), BF16),
        scratch_shapes=[pltpu.VMEM((2, G, tq, tq), F32), pltpu.VMEM((G, tq, LANE), F32),
                        pltpu.VMEM((G, tq, 2 * HEAD_DIM), F32)],
        compiler_params=_params("parallel", "parallel", "arbitrary"),
        name="flash",
    )(*args)


def _gate_kernel(fg_ref, bf_ref, cumt_ref, carry_ref, *, ts):
    @pl.when(pl.program_id(1) == 0)
    def _():
        carry_ref[...] = jnp.zeros_like(carry_ref)

    z = fg_ref[...] + bf_ref[...]
    lf = jnp.minimum(z, 0.0) - jnp.log1p(jnp.exp(-jnp.abs(z)))
    hi = lf.astype(BF16)
    r1 = lf - hi.astype(F32)
    mid = r1.astype(BF16)
    lo = (r1 - mid.astype(F32)).astype(BF16)
    row = lax.broadcasted_iota(jnp.int32, (ts, ts), 0)
    col = lax.broadcasted_iota(jnp.int32, (ts, ts), 1)
    tri = jnp.where(col <= row, 1.0, 0.0).astype(BF16)
    cum = (jnp.dot(tri, hi, preferred_element_type=F32)
           + jnp.dot(tri, mid, preferred_element_type=F32)
           + jnp.dot(tri, lo, preferred_element_type=F32)) + carry_ref[...]
    carry_ref[...] = cum[ts - 1:ts, :]
    cumt_ref[...] = (cum * LOG2E).T


def _fox_gates(fg, b_f):
    B, S, _ = fg.shape
    ts = _tile(S, 512)
    bf = jnp.zeros((1, LANE), F32).at[0, :N_HEADS].set(b_f.astype(F32))
    return pl.pallas_call(
        functools.partial(_gate_kernel, ts=ts),
        grid=(B, S // ts),
        in_specs=[pl.BlockSpec((None, ts, LANE), lambda b, i: (b, i, 0)),
                  pl.BlockSpec((1, LANE), lambda b, i: (0, 0))],
        out_specs=pl.BlockSpec((None, LANE, ts), lambda b, i: (b, 0, i)),
        out_shape=jax.ShapeDtypeStruct((B, LANE, S), F32),
        scratch_shapes=[pltpu.VMEM((1, LANE), F32)],
        compiler_params=_params("parallel", "arbitrary"),
        name="fox_gates",
    )(fg, bf)


def _rope_kernel(x_ref, cos_ref, sin_ref, g_ref, o_ref):
    cos, sin, g = cos_ref[...], sin_ref[...], g_ref[...]
    for c in range(x_ref.shape[1] // LANE):
        sl = slice(c * LANE, (c + 1) * LANE)
        x = x_ref[:, sl]
        ms = jnp.sum(x * x, axis=-1, keepdims=True) * (1.0 / ROPE_DIM)
        y = x * lax.rsqrt(ms + RMS_EPS) * g
        partner = pltpu.roll(y, ROPE_DIM // 2, 1) + pltpu.roll(y, LANE - ROPE_DIM // 2, 1)
        o_ref[:, sl] = (y * cos + partner * sin).astype(o_ref.dtype)


def _rope(x, cos, sin, gain):
    B, S, N = x.shape
    ts = _tile(S, 512)
    g = jnp.zeros((1, LANE), F32).at[0, :ROPE_DIM].set(gain.astype(F32))
    return pl.pallas_call(
        _rope_kernel,
        grid=(B, S // ts),
        in_specs=[pl.BlockSpec((None, ts, N), lambda b, i: (b, i, 0)),
                  pl.BlockSpec((None, ts, LANE), lambda b, i: (b, i, 0)),
                  pl.BlockSpec((None, ts, LANE), lambda b, i: (b, i, 0)),
                  pl.BlockSpec((1, LANE), lambda b, i: (0, 0))],
        out_specs=pl.BlockSpec((None, ts, N), lambda b, i: (b, i, 0)),
        out_shape=jax.ShapeDtypeStruct((B, S, N), BF16),
        compiler_params=_params("parallel", "parallel"),
        name="rope",
    )(x, cos, sin, g)


def _mem_kernel(q_ref, kv_ref, o_ref):
    for h in range(MEM_HEADS):
        q = q_ref[:, h * LANE:(h + 1) * LANE]
        k = kv_ref[:, h * LANE:(h + 1) * LANE]
        v = kv_ref[:, MEM_WIDTH + h * LANE:MEM_WIDTH + (h + 1) * LANE]
        s = lax.dot_general(q, k, (((1,), (1,)), ((), ())), preferred_element_type=F32)
        p = jnp.exp(s - jnp.max(s, axis=-1, keepdims=True))
        o = jnp.dot(p.astype(BF16), v, preferred_element_type=F32)
        o_ref[:, h * LANE:(h + 1) * LANE] = (o / jnp.sum(p, axis=-1, keepdims=True)).astype(o_ref.dtype)


def _mem_attention(qarr, qblock, mem_kv):
    B, S, _ = qarr.shape
    n_mem = mem_kv.shape[1]
    tq = _tile(S, 1024)
    return pl.pallas_call(
        _mem_kernel,
        grid=(B, S // tq),
        in_specs=[pl.BlockSpec((None, tq, MEM_WIDTH), lambda b, i: (b, i, qblock)),
                  pl.BlockSpec((None, n_mem, 2 * MEM_WIDTH), lambda b, i: (b, 0, 0))],
        out_specs=pl.BlockSpec((None, tq, MEM_WIDTH), lambda b, i: (b, i, 0)),
        out_shape=jax.ShapeDtypeStruct((B, S, MEM_WIDTH), BF16),
        compiler_params=_params("parallel", "parallel"),
        name="mem_attn",
    )(qarr, mem_kv)


def _dil_kernel(q_ref, kp_ref, kc_ref, vp_ref, vc_ref, bias_ref, o_ref, lse_ref):
    n = DIL_BLOCK
    nsub = q_ref.shape[0] // n
    row = lax.broadcasted_iota(jnp.int32, (n, 2 * n), 0)
    col = lax.broadcasted_iota(jnp.int32, (n, 2 * n), 1)
    band = jnp.logical_and(col >= row, col <= row + n)
    first = jnp.logical_and(band, jnp.logical_or(col >= n, pl.program_id(1) > 0))
    lane = lax.broadcasted_iota(jnp.int32, (n, LANE), 1)
    dn = (((1,), (1,)), ((), ()))
    heads = [slice(h * LANE, (h + 1) * LANE) for h in range(N_HEADS)]

    def window(prev_ref, cur_ref, u, sl):
        if u == 0:
            return jnp.concatenate([prev_ref[:, sl], cur_ref[:n, sl]], axis=0)
        return cur_ref[(u - 1) * n:(u + 1) * n, sl]

    for u in range(nsub):
        rows = slice(u * n, (u + 1) * n)
        valid = first if u == 0 else band
        scores = []
        for h, sl in enumerate(heads):
            s = lax.dot_general(q_ref[rows, sl], window(kp_ref, kc_ref, u, sl), dn,
                                preferred_element_type=F32) + bias_ref[h]
            scores.append(jnp.where(valid, s, -jnp.inf))
        ms = [jnp.max(s, axis=-1, keepdims=True) for s in scores]
        ps = [jnp.exp2(s - m).astype(BF16) for s, m in zip(scores, ms)]
        lse_all = jnp.zeros((n, LANE), F32)
        for h, sl in enumerate(heads):
            v2 = window(vp_ref, vc_ref, u, sl)
            pv = jnp.dot(ps[h], jnp.concatenate([v2, jnp.ones_like(v2)], axis=-1),
                         preferred_element_type=F32)
            den = pv[:, LANE:]
            o_ref[h, rows, :] = pv[:, :LANE] / den
            lse_all = jnp.where(lane == h, ms[h] + jnp.log2(den), lse_all)
        lse_ref[rows, :] = lse_all


def _dil_attention(pg, bias):
    N, Ls, _ = pg.shape
    n = DIL_BLOCK
    nsub = 2 if Ls % (2 * n) == 0 else 1
    tr = nsub * n

    def cur(part):
        return pl.BlockSpec((None, tr, MIX_WIDTH), lambda s, i: (s, i, part))

    def prev(part):
        return pl.BlockSpec((None, n, MIX_WIDTH), lambda s, i: (s, jnp.maximum(i * nsub - 1, 0), part))

    return pl.pallas_call(
        _dil_kernel,
        grid=(N, Ls // tr),
        in_specs=[cur(0), prev(1), cur(1), prev(2), cur(2),
                  pl.BlockSpec((N_HEADS, n, 2 * n), lambda s, i: (0, 0, 0))],
        out_specs=[pl.BlockSpec((None, N_HEADS, tr, LANE), lambda s, i: (s, 0, i, 0)),
                   pl.BlockSpec((None, tr, LANE), lambda s, i: (s, i, 0))],
        out_shape=[jax.ShapeDtypeStruct((N, N_HEADS, Ls, LANE), F32),
                   jax.ShapeDtypeStruct((N, Ls, LANE), F32)],
        compiler_params=_params("parallel", "arbitrary"),
        name="dil_attn",
    )(pg, pg, pg, pg, pg, bias)


def _dil_combine_kernel(o1_ref, o2_ref, o3_ref, l1_ref, l2_ref, l3_ref, out_ref, stage_ref, *, d2, d3):
    n = l3_ref.shape[1]
    q = d3 // d2
    for r in range(d3):
        rows = [(pl.ds(r, n, stride=d3),), (r % d2, pl.ds(r // d2, n, stride=q)), (r, slice(None))]
        ls = [l_ref[idx + (slice(None),)]
              for l_ref, idx in zip((l1_ref, l2_ref, l3_ref), rows)]
        m = functools.reduce(jnp.maximum, ls)
        es = [jnp.exp2(l - m) for l in ls]
        inv = 1.0 / functools.reduce(jnp.add, es)
        ws = [e * inv for e in es]
        for h in range(N_HEADS):
            o = (ws[0][:, h:h + 1] * o1_ref[h, pl.ds(r, n, stride=d3), :]
                 + ws[1][:, h:h + 1] * o2_ref[r % d2, h, pl.ds(r // d2, n, stride=q), :]
                 + ws[2][:, h:h + 1] * o3_ref[r, h, :, :])
            stage_ref[h, pl.ds(r, n, stride=d3), :] = o
    for h in range(N_HEADS):
        out_ref[:, h * LANE:(h + 1) * LANE] = stage_ref[h].astype(out_ref.dtype)


def _dil_combine(outs, lses, B, S, tm=256):
    (_, d1), (_, d2), (_, d3) = DIL_GROUPS
    assert d1 == 1 and d3 % d2 == 0
    tm = _tile(S, tm)
    H = N_HEADS
    o_specs = [pl.BlockSpec((None, H, tm, LANE), lambda b, i: (b, 0, i, 0)),
               pl.BlockSpec((None, d2, H, tm // d2, LANE), lambda b, i: (b, 0, 0, i, 0)),
               pl.BlockSpec((None, d3, H, tm // d3, LANE), lambda b, i: (b, 0, 0, i, 0))]
    l_specs = [pl.BlockSpec((None, tm, LANE), lambda b, i: (b, i, 0)),
               pl.BlockSpec((None, d2, tm // d2, LANE), lambda b, i: (b, 0, i, 0)),
               pl.BlockSpec((None, d3, tm // d3, LANE), lambda b, i: (b, 0, i, 0))]
    return pl.pallas_call(
        functools.partial(_dil_combine_kernel, d2=d2, d3=d3),
        grid=(B, S // tm),
        in_specs=o_specs + l_specs,
        out_specs=pl.BlockSpec((None, tm, MIX_WIDTH), lambda b, i: (b, i, 0)),
        out_shape=jax.ShapeDtypeStruct((B, S, MIX_WIDTH), BF16),
        scratch_shapes=[pltpu.VMEM((H, tm, LANE), F32)],
        compiler_params=_params("parallel", "parallel"),
        name="dil_combine",
    )(outs[0].reshape(B, H, S, LANE), outs[1].reshape(B, d2, H, S // d2, LANE),
      outs[2].reshape(B, d3, H, S // d3, LANE),
      lses[0].reshape(B, S, LANE), lses[1].reshape(B, d2, S // d2, LANE),
      lses[2].reshape(B, d3, S // d3, LANE))


def _dsa_topk_kernel(qi_ref, wi_ref, ki_ref, mask_ref, key_ref, jsel_ref, *, tq, tk, S, n_sel):
    i = pl.program_id(1)
    q0 = i * tq
    n_chunks = (q0 + tq + tk - 1) // tk
    dn = (((1,), (1,)), ((), ()))
    krow = lax.broadcasted_iota(jnp.int32, (tk, tq), 0)
    qpos = q0 + lax.broadcasted_iota(jnp.int32, (tk, tq), 1)

    wi_t = (wi_ref[...] * (IDX_HEADS ** -0.5 * IDX_DIM ** -0.5)).T

    def idx_body(c, carry):
        off = pl.multiple_of(c * tk, tk)
        kic = ki_ref[pl.ds(off, tk), :]
        k_lo = jnp.where(lax.broadcasted_iota(jnp.int32, kic.shape, 1) < IDX_DIM, kic, 0.0)
        k_sides = (k_lo.astype(BF16), pltpu.roll(k_lo, IDX_DIM, 1).astype(BF16))
        acc = jnp.zeros((tk, tq), F32)
        for h in range(IDX_HEADS):
            d = lax.dot_general(k_sides[h % 2], qi_ref[:, (h // 2) * LANE:(h // 2 + 1) * LANE], dn,
                                preferred_element_type=F32)
            acc = acc + wi_t[IDX_DIM + h:IDX_DIM + h + 1, :] * jnp.maximum(d, 0.0)
        score = jnp.where(off + krow <= qpos, acc, -jnp.inf)
        bits = pltpu.bitcast(score, jnp.int32)
        key_ref[pl.ds(off, tk), :] = bits ^ ((bits >> 31) & jnp.int32(0x7FFFFFFF))
        return carry

    lax.fori_loop(0, n_chunks, idx_body, 0)

    def count(pred):
        sub = 64
        def body(c, acc):
            off = pl.multiple_of(c * tk, tk)
            ones = jnp.where(pred(key_ref[pl.ds(off, tk), :], off), 1.0, 0.0)
            return acc + jnp.sum(ones.reshape(tk // sub, sub, tq), axis=0)

        acc = lax.fori_loop(0, n_chunks, body, jnp.zeros((sub, tq), F32))
        return jnp.sum(acc, axis=0, keepdims=True)

    k_sel = float(n_sel)
    c0 = count(lambda k, off: k >= 0)
    lo0 = jnp.where(c0 >= k_sel, jnp.int32(0), jnp.int32(-2 ** 31))
    n0 = jnp.where(c0 >= k_sel, c0, (n_chunks * tk).astype(F32))

    def bis_cond(state):
        it, _, n_lo = state
        return jnp.logical_and(it < 31, jnp.max(jnp.abs(n_lo - k_sel)) > 0.0)

    def bis_body(state):
        it, lo, n_lo = state
        cand = lo + (jnp.int32(1) << (30 - it))
        cnt = count(lambda k, off: k >= cand)
        take = cnt >= k_sel
        return it + 1, jnp.where(take, cand, lo), jnp.where(take, cnt, n_lo)

    _, thr, n_ge = lax.while_loop(bis_cond, bis_body, (jnp.int32(0), lo0, n0))
    n_bits = max(1, (S - 1).bit_length())
    jsel_ref[...] = jnp.full_like(jsel_ref, S)

    @pl.when(jnp.max(n_ge) > k_sel)
    def _():
        need = k_sel - count(lambda k, off: k > thr)

        def tie_body(it, jlo):
            cand = jlo + (jnp.int32(1) << (n_bits - 1 - it))
            below = count(lambda k, off: jnp.logical_and(k == thr, off + krow < cand))
            return jnp.where(below < need, cand, jlo)

        jsel = lax.fori_loop(0, n_bits, tie_body, jnp.zeros((1, tq), jnp.int32))
        jsel_ref[...] = jnp.broadcast_to(jsel, jsel_ref.shape)

    jsel = jsel_ref[:1, :]
    mask_ref[...] = jnp.full_like(mask_ref, -jnp.inf)

    def mask_body(c, carry):
        off = pl.multiple_of(c * tk, tk)
        k = key_ref[pl.ds(off, tk), :]
        kpos = off + krow
        tie = jnp.logical_and(k == thr, kpos <= jsel)
        sel = jnp.logical_and(jnp.logical_or(k > thr, tie), kpos <= qpos)
        mask_ref[:, pl.ds(off, tk)] = jnp.where(sel, 0.0, -jnp.inf).T.astype(mask_ref.dtype)
        return carry

    lax.fori_loop(0, n_chunks, mask_body, 0)


def _dsa_select(pm, aux, B, S, n_sel, tq=256, tk=512):
    tq, tk = _tile(S, tq), _tile(S, tk)
    qiw = IDX_HEADS * IDX_DIM
    return pl.pallas_call(
        functools.partial(_dsa_topk_kernel, tq=tq, tk=tk, S=S, n_sel=n_sel),
        grid=(B, S // tq),
        in_specs=[pl.BlockSpec((None, tq, qiw), lambda b, i: (b, i, MIX_WIDTH // qiw)),
                  pl.BlockSpec((None, tq, LANE), lambda b, i: (b, i, 0)),
                  pl.BlockSpec((None, S, LANE), lambda b, i: (b, 0, 0))],
        out_specs=pl.BlockSpec((None, tq, S), lambda b, i: (b, i, 0)),
        out_shape=jax.ShapeDtypeStruct((B, S, S), BF16),
        scratch_shapes=[pltpu.VMEM((S, tq), jnp.int32), pltpu.VMEM((8, tq), jnp.int32)],
        compiler_params=_params("parallel", "arbitrary"),
        name="dsa_select",
    )(pm, aux, aux)


def _dsa_attn_kernel(q_ref, k_ref, v_ref, mask_ref, gt_ref, o_ref, s_ref, m_ref, acc_ref, *, tq, tk, S):
    i = pl.program_id(2)
    q0 = i * tq
    n_chunks = (q0 + tq + tk - 1) // tk
    R = q_ref.shape[1] // LANE
    nblk = tq // LANE
    dn = (((1,), (1,)), ((), ()))
    qs = [q_ref[:, r * LANE:(r + 1) * LANE] for r in range(R)]

    def produce(c):
        off = pl.multiple_of(c * tk, tk)
        kc = k_ref[pl.ds(off, tk), :]
        mask = mask_ref[:, pl.ds(off, tk)].astype(F32)
        st = S - q0 + off
        scores = []
        for r in range(R):
            bias = jnp.concatenate(
                [gt_ref[r, :, pl.ds(pl.multiple_of(st + (nblk - 1 - a) * LANE, LANE), tk)]
                 for a in range(nblk)], axis=0)
            scores.append(lax.dot_general(qs[r], kc, dn, preferred_element_type=F32) + bias + mask)
        return scores

    def values(c):
        return [v_ref[pl.ds(pl.multiple_of(c * tk, tk), tk), :]] * R

    _online_softmax(n_chunks, produce, values, s_ref, m_ref, acc_ref)
    for r in range(R):
        acc = acc_ref[r]
        o_ref[:, r * LANE:(r + 1) * LANE] = (acc[:, :LANE] / acc[:, LANE:]).astype(o_ref.dtype)


def _dsa_attention(pm, mask, gt, B, S, tq=256, tk=512):
    tq, tk = _tile(S, tq), _tile(S, tk)
    R = N_HEADS // DSA_KV_HEADS
    kblk = (MIX_WIDTH + IDX_HEADS * IDX_DIM) // LANE
    return pl.pallas_call(
        functools.partial(_dsa_attn_kernel, tq=tq, tk=tk, S=S),
        grid=(DSA_KV_HEADS, B, S // tq),
        in_specs=[pl.BlockSpec((None, tq, R * LANE), lambda g, b, i: (b, i, g)),
                  pl.BlockSpec((None, S, LANE), lambda g, b, i: (b, 0, kblk + g)),
                  pl.BlockSpec((None, S, LANE), lambda g, b, i: (b, 0, kblk + DSA_KV_HEADS + g)),
                  pl.BlockSpec((None, tq, S), lambda g, b, i: (b, i, 0)),
                  pl.BlockSpec((R, LANE, gt.shape[2]), lambda g, b, i: (g, 0, 0))],
        out_specs=pl.BlockSpec((None, tq, R * LANE), lambda g, b, i: (b, i, g)),
        out_shape=jax.ShapeDtypeStruct((B, S, MIX_WIDTH), BF16),
        scratch_shapes=[pltpu.VMEM((2, R, tq, tk), F32), pltpu.VMEM((R, tq, LANE), F32),
                        pltpu.VMEM((R, tq, 2 * HEAD_DIM), F32)],
        compiler_params=_params("parallel", "parallel", "arbitrary"),
        name="dsa_attn",
    )(pm, pm, pm, mask, gt)


def _t5_bucket(dist):
    n = jnp.maximum(dist, 0)
    exact = REL_BUCKETS // 2
    nf = jnp.maximum(n, 1).astype(F32)
    large = exact + (jnp.log(nf / exact) / math.log(REL_MAX_DIST / exact)
                     * (REL_BUCKETS - exact)).astype(jnp.int32)
    large = jnp.minimum(large, REL_BUCKETS - 1)
    return jnp.where(n < exact, n, large)


def _t5_bias(t5_table, dist):
    onehot = jax.nn.one_hot(_t5_bucket(dist), REL_BUCKETS, dtype=F32)
    table = jnp.einsum('...b,bh->h...', onehot, t5_table.astype(F32), precision=lax.Precision.HIGHEST)
    return table * LOG2E


def _dil_bias(t5_table, dil):
    n = DIL_BLOCK
    rel = jnp.arange(n)[:, None] + n - jnp.arange(2 * n)[None, :]
    return _t5_bias(t5_table, rel * dil)


def _dsa_bias_table(t5_table, S, tq, tk):
    lt = S + tq + tk
    tb = _t5_bias(t5_table, S + tq - 1 - jnp.arange(lt + LANE))
    return jnp.stack([tb[:, LANE - 1 - i:LANE - 1 - i + lt] for i in range(LANE)], axis=1)


def _headnorm_cols(spec):
    eg, ef = [], []
    for gain, count, scale in spec:
        if gain is None:
            eg.append(jnp.ones((count * LANE,), F32))
            ef.append(jnp.zeros((count * LANE,), F32))
        else:
            eg.append(jnp.tile(gain.astype(F32) * scale, count))
            ef.append(jnp.ones((count * LANE,), F32))
    return jnp.concatenate(eg), jnp.concatenate(ef)


def _pad_cols(w, width):
    return jnp.pad(w, ((0, 0), (0, width - w.shape[1])))


def _fox_layer(x, norm_g, w_in, b_f, qk_g, mem_g):
    B, S, _ = x.shape
    W = MIX_WIDTH
    w_main = jnp.concatenate([w_in[:, :3 * W], w_in[:, 3 * W + N_HEADS:]], axis=1).astype(BF16)
    w_gate = _pad_cols(w_in[:, 3 * W:3 * W + N_HEADS], LANE).astype(BF16)
    eg, ef = _headnorm_cols([(qk_g[0], N_HEADS, HEAD_DIM ** -0.5 * LOG2E), (qk_g[1], N_HEADS, 1.0),
                             (None, N_HEADS, 1.0), (mem_g, MEM_HEADS, HEAD_DIM ** -0.5)])
    pm = _proj(x, norm_g, w_main, eg, ef)
    fg = _proj(x, norm_g, w_gate, out_dtype=F32)
    cum_t = _fox_gates(fg, b_f)
    mix = _flash([(pm, 0, True)], [(pm, W, True)], (pm, 2 * W, True), B, S,
                 decay_t=cum_t.reshape(B, LANE, 1, S))
    return mix, pm, 3 * W // MEM_WIDTH


def _mla_layer(x, positions, norm_g, w_in, q_norm, w_uq, kv_norm, w_ukv, nope_g, rope_g, mem_g):
    B, S, _ = x.shape
    scale = (NOPE_DIM + ROPE_DIM) ** -0.5 * LOG2E
    lat = Q_LORA + KV_LORA
    w_lat = jnp.concatenate([w_in[:, :lat], _pad_cols(w_in[:, lat:lat + ROPE_DIM], LANE)],
                            axis=1).astype(BF16)
    w_mem = w_in[:, lat + ROPE_DIM:].astype(BF16)
    pl_ = _proj(x, norm_g, w_lat, out_dtype=F32, tn=lat + LANE)
    eg, ef = _headnorm_cols([(mem_g, MEM_HEADS, HEAD_DIM ** -0.5)])
    pmem = _proj(x, norm_g, w_mem, eg, ef)

    uq = w_uq.reshape(Q_LORA, N_HEADS, NOPE_DIM + ROPE_DIM)
    w_qn = uq[:, :, :NOPE_DIM].reshape(Q_LORA, MIX_WIDTH).astype(BF16)
    w_qr = jnp.pad(uq[:, :, NOPE_DIM:], ((0, 0), (0, 0), (0, LANE - ROPE_DIM))
                   ).reshape(Q_LORA, N_HEADS * LANE).astype(BF16)
    ukv = w_ukv.reshape(KV_LORA, N_HEADS, NOPE_DIM + HEAD_DIM)
    w_kv = jnp.concatenate([ukv[:, :, :NOPE_DIM].reshape(KV_LORA, MIX_WIDTH),
                            ukv[:, :, NOPE_DIM:].reshape(KV_LORA, MIX_WIDTH)], axis=1).astype(BF16)
    eg, ef = _headnorm_cols([(nope_g[0], N_HEADS, scale)])
    qn = _proj(pl_, q_norm, w_qn, eg, ef, kblock=0, tn=2048)
    qr_raw = _proj(pl_, q_norm, w_qr, kblock=0, out_dtype=F32, tn=1024)
    eg, ef = _headnorm_cols([(nope_g[1], N_HEADS, 1.0), (None, N_HEADS, 1.0)])
    kv = _proj(pl_, kv_norm, w_kv, eg, ef, kblock=1, tn=2048)

    half = ROPE_DIM // 2
    inv = ROPE_THETA ** (-jnp.arange(half, dtype=F32) / half)
    ang = positions.astype(F32)[..., None] * inv
    cos, sin = jnp.cos(ang), jnp.sin(ang)
    zero = jnp.zeros((B, S, LANE - ROPE_DIM), F32)
    cos_t = jnp.concatenate([cos, cos, zero], axis=-1)
    sin_t = jnp.concatenate([-sin, sin, zero], axis=-1)
    qr = _rope(qr_raw, cos_t, sin_t, rope_g[0] * scale)
    kr = _rope(pl_[:, :, lat:], cos_t, sin_t, rope_g[1])
    mix = _flash([(qn, 0, True), (qr, 0, True)], [(kv, 0, True), (kr, 0, False)],
                 (kv, MIX_WIDTH, True), B, S)
    return mix, pmem, 0


def _dil_layer(x, norm_g, w_in, qk_g, t5_table, mem_g):
    B, S, _ = x.shape
    W = MIX_WIDTH
    outs, lses = [], []
    for gi, (win, dil) in enumerate(DIL_GROUPS):
        assert win // dil == DIL_BLOCK and (S // dil) % DIL_BLOCK == 0
        w_g = w_in[:, gi * 3 * W:(gi + 1) * 3 * W].astype(BF16)
        eg, ef = _headnorm_cols([(qk_g[gi, 0], N_HEADS, HEAD_DIM ** -0.5 * LOG2E),
                                 (qk_g[gi, 1], N_HEADS, 1.0), (None, N_HEADS, 1.0)])
        pg = _proj(x, norm_g, w_g, eg, ef, dil=dil, tn=1024)
        o, lse = _dil_attention(pg.reshape(B * dil, S // dil, 3 * W), _dil_bias(t5_table, dil))
        outs.append(o)
        lses.append(lse)
    mix = _dil_combine(outs, lses, B, S)
    eg, ef = _headnorm_cols([(mem_g, MEM_HEADS, HEAD_DIM ** -0.5)])
    pmem = _proj(x, norm_g, w_in[:, len(DIL_GROUPS) * 3 * W:].astype(BF16), eg, ef)
    return mix, pmem, 0


def _dsa_layer(x, norm_g, w_in, qk_g, t5_table, mem_g):
    B, S, _ = x.shape
    W = MIX_WIDTH
    kvw = DSA_KV_HEADS * HEAD_DIM
    o_k, o_v, o_qi = W, W + kvw, W + 2 * kvw
    o_ki = o_qi + IDX_HEADS * IDX_DIM
    o_wi = o_ki + IDX_DIM
    o_mem = o_wi + IDX_HEADS
    qiw = IDX_HEADS * IDX_DIM
    w_main = jnp.concatenate([w_in[:, :W], w_in[:, o_qi:o_ki], w_in[:, o_k:o_qi], w_in[:, o_mem:]],
                             axis=1).astype(BF16)
    w_aux = _pad_cols(w_in[:, o_ki:o_mem], LANE).astype(BF16)
    eg, ef = _headnorm_cols([(qk_g[0], N_HEADS, HEAD_DIM ** -0.5 * LOG2E), (None, qiw // LANE, 1.0),
                             (qk_g[1], DSA_KV_HEADS, 1.0), (None, DSA_KV_HEADS, 1.0),
                             (mem_g, MEM_HEADS, HEAD_DIM ** -0.5)])
    pm = _proj(x, norm_g, w_main, eg, ef)
    aux = _proj(x, norm_g, w_aux, out_dtype=F32)
    mask = _dsa_select(pm, aux, B, S, min(TOPK_MAX, S // 4), tq=512)
    tq, tk = _tile(S, 512), _tile(S, 512)
    mix = _dsa_attention(pm, mask, _dsa_bias_table(t5_table, S, tq, tk), B, S, tq, tk)
    return mix, pm, (W + qiw + 2 * kvw) // MEM_WIDTH


def kernel(x, mem, positions, t5_table, ffn_norm, ffn_w_gate, ffn_w_up, ffn_w_down, attn_norm,
           mem_norm, mem_w_kv, mem_qk_g, w_out, a_w_in, a_b_f, a_qk_g, b_w_in, b_q_norm, b_w_uq,
           b_kv_norm, b_w_ukv, b_nope_g, b_rope_g, c_w_in, c_qk_g, d_w_in, d_qk_g):
    B, S, D = x.shape
    depth = ffn_norm.shape[0]
    n_mixers = 4

    def ffn(xc, i, k, next_gain=None):
        return _ffn(xc.reshape(B * S, D), ffn_norm[i, k], ffn_w_gate[i, k].astype(BF16),
                    ffn_w_up[i, k].astype(BF16), ffn_w_down[i, k].astype(BF16), next_gain)

    for i in range(depth):
        m, j = i % n_mixers, i // n_mixers
        x, h = ffn(x, i, 0, attn_norm[i])
        x, h = x.reshape(B, S, D), h.reshape(B, S, D)
        mem_g = mem_qk_g[i]
        if m == 0:
            mix, qarr, qblock = _fox_layer(h, None, a_w_in[j], a_b_f[j], a_qk_g[j], mem_g[0])
        elif m == 1:
            mix, qarr, qblock = _mla_layer(h, positions, None, b_w_in[j], b_q_norm[j],
                                           b_w_uq[j], b_kv_norm[j], b_w_ukv[j], b_nope_g[j],
                                           b_rope_g[j], mem_g[0])
        elif m == 2:
            mix, qarr, qblock = _dil_layer(h, None, c_w_in[j], c_qk_g[j], t5_table, mem_g[0])
        else:
            mix, qarr, qblock = _dsa_layer(h, None, d_w_in[j], d_qk_g[j], t5_table, mem_g[0])
        eg, ef = _headnorm_cols([(mem_g[1], MEM_HEADS, 1.0), (None, MEM_HEADS, 1.0)])
        mem_kv = _proj(mem, mem_norm[i], mem_w_kv[i].astype(BF16), eg, ef)
        mo = _mem_attention(qarr, qblock, mem_kv)
        wo = w_out[i].astype(BF16)
        x = _out_proj(x.reshape(B * S, D), mix.reshape(B * S, MIX_WIDTH),
                      mo.reshape(B * S, MEM_WIDTH), wo[:MIX_WIDTH], wo[MIX_WIDTH:]).reshape(B, S, D)
        x = ffn(x, i, 1).reshape(B, S, D)
    return x
```

```python
import functools
import math

import jax
import jax.numpy as jnp
from jax import lax
from jax.experimental import pallas as pl
from jax.experimental.pallas import tpu as pltpu

F32 = jnp.float32
BF16 = jnp.bfloat16

LANE = 128
D_MODEL = 2048
N_HEADS = 16
HEAD_DIM = 128
MIX_WIDTH = N_HEADS * HEAD_DIM
MEM_HEADS = 4
MEM_WIDTH = MEM_HEADS * HEAD_DIM
D_FF = 5632
RMS_EPS = 1e-6
REL_BUCKETS = 32
REL_MAX_DIST = 2048
Q_LORA = 512
KV_LORA = 512
NOPE_DIM = 128
ROPE_DIM = 64
ROPE_THETA = 10000.0
DIL_GROUPS = ((128, 1), (512, 4), (2048, 16))
DIL_BLOCK = 128
PERM_ROWS = 256
DSA_KV_HEADS = 4
IDX_HEADS = 16
IDX_DIM = 64
TOPK_MAX = 256
VMEM_LIMIT = 56 * 1024 * 1024
FFN_VMEM_LIMIT = 62 * 1024 * 1024
NEG_BIG = -1e30
LOG2E = math.log2(math.e)


def _params(*sem):
    return pltpu.CompilerParams(dimension_semantics=sem, vmem_limit_bytes=VMEM_LIMIT)


def _tile(n, pref):
    t = min(n, pref)
    while n % t:
        t //= 2
    return t


def _ffn_kernel(x_ref, g_ref, wg_ref, wu_ref, wd_ref, o_ref, h_ref):
    _ffn_body(x_ref, g_ref, wg_ref, wu_ref, wd_ref, o_ref, h_ref, ())


def _ffn_kernel_with_next(x_ref, g_ref, wg_ref, wu_ref, wd_ref, ng_ref, o_ref, hn_ref, h_ref):
    _ffn_body(x_ref, g_ref, wg_ref, wu_ref, wd_ref, o_ref, h_ref, (ng_ref, hn_ref))


def _ffn_body(x_ref, g_ref, wg_ref, wu_ref, wd_ref, o_ref, h_ref, next_refs):
    j = pl.program_id(1)
    last = pl.num_programs(1) - 1
    tm = x_ref.shape[0]
    halves = [slice(0, tm // 2), slice(tm // 2, tm)]

    def down(h):
        g = jnp.dot(h, wg_ref[...], preferred_element_type=F32)
        u = jnp.dot(h, wu_ref[...], preferred_element_type=F32)
        a = (g * jax.nn.sigmoid(g) * u).astype(BF16)
        return jnp.dot(a, wd_ref[...], preferred_element_type=F32)

    @pl.when(j == 0)
    def _():
        for rows in halves:
            x = x_ref[rows, :]
            ms = jnp.mean(x * x, axis=-1, keepdims=True)
            h = (x * lax.rsqrt(ms + RMS_EPS) * g_ref[...]).astype(BF16)
            h_ref[rows, :] = h
            o_ref[rows, :] = down(h)

    @pl.when(jnp.logical_and(j > 0, j < last))
    def _():
        o_ref[...] += down(h_ref[...])

    @pl.when(jnp.logical_and(j > 0, j == last))
    def _():
        for rows in halves:
            y = x_ref[rows, :] + 0.5 * (o_ref[rows, :] + down(h_ref[rows, :]))
            o_ref[rows, :] = y
            if next_refs:
                ng_ref, hn_ref = next_refs
                ms = jnp.mean(y * y, axis=-1, keepdims=True)
                hn_ref[rows, :] = (y * lax.rsqrt(ms + RMS_EPS) * ng_ref[...]).astype(BF16)


def _ffn(x2, gain, wg, wu, wd, next_gain=None):
    T, D = x2.shape
    F = wg.shape[1]
    tm, tf = _tile(T, 1024), _tile(F, 512)
    assert F // tf >= 2 and tm % 16 == 0
    row_spec = pl.BlockSpec((tm, D), lambda i, j: (i, 0))
    vec_spec = pl.BlockSpec((1, D), lambda i, j: (0, 0))
    in_specs = [row_spec, vec_spec,
                pl.BlockSpec((D, tf), lambda i, j: (0, j)),
                pl.BlockSpec((D, tf), lambda i, j: (0, j)),
                pl.BlockSpec((tf, D), lambda i, j: (j, 0))]
    args = [x2, gain.reshape(1, D), wg, wu, wd]
    out_specs, out_shape = row_spec, jax.ShapeDtypeStruct((T, D), F32)
    kern = _ffn_kernel
    if next_gain is not None:
        in_specs.append(vec_spec)
        args.append(next_gain.reshape(1, D).astype(F32))
        out_specs = [row_spec, row_spec]
        out_shape = [out_shape, jax.ShapeDtypeStruct((T, D), BF16)]
        kern = _ffn_kernel_with_next
    return pl.pallas_call(
        kern,
        grid=(T // tm, F // tf),
        in_specs=in_specs,
        out_specs=out_specs,
        out_shape=out_shape,
        scratch_shapes=[pltpu.VMEM((tm, D), BF16)],
        compiler_params=pltpu.CompilerParams(dimension_semantics=("parallel", "arbitrary"),
                                             vmem_limit_bytes=FFN_VMEM_LIMIT),
        name="ffn",
    )(*args)


def _proj_kernel(x_ref, g_ref, w_ref, eg_ref, ef_ref, o_ref, *scratch, epilogue, dil, prenormed):
    scratch = list(scratch)
    if prenormed:
        h_ref = x_ref
    else:
        h_ref = scratch.pop(0)

        @pl.when(pl.program_id(2) == 0)
        def _():
            x = x_ref[...].astype(F32)
            ms = jnp.mean(x * x, axis=-1, keepdims=True)
            h_ref[...] = (x * lax.rsqrt(ms + RMS_EPS) * g_ref[...]).astype(BF16)

    tm = h_ref.shape[0]
    nb = PERM_ROWS // dil
    if dil > 1:
        hp_ref = scratch.pop(0)

        @pl.when(pl.program_id(2) == 0)
        def _():
            dst = lax.broadcasted_iota(jnp.int32, (PERM_ROWS, PERM_ROWS), 0)
            src = lax.broadcasted_iota(jnp.int32, (PERM_ROWS, PERM_ROWS), 1)
            perm = jnp.where(dst == (src % dil) * nb + src // dil, 1.0, 0.0).astype(BF16)
            for b in range(tm // PERM_ROWS):
                blk = slice(b * PERM_ROWS, (b + 1) * PERM_ROWS)
                hp_ref[blk, :] = jnp.dot(perm, h_ref[blk, :],
                                         preferred_element_type=F32).astype(BF16)

        h_ref = hp_ref
    parts = 2 if tm % (2 * PERM_ROWS) == 0 else 1
    rows_per = tm // parts
    for p in range(parts):
        rows = slice(p * rows_per, (p + 1) * rows_per)
        acc = jnp.dot(h_ref[rows, :], w_ref[...], preferred_element_type=F32)
        for c in range(acc.shape[1] // LANE):
            sl = slice(c * LANE, (c + 1) * LANE)
            y = acc[:, sl]
            if epilogue:
                ms = jnp.mean(y * y, axis=-1, keepdims=True)
                scale = jnp.where(ef_ref[:, sl] > 0.0, lax.rsqrt(ms + RMS_EPS), 1.0)
                y = y * scale * eg_ref[:, sl]
            y = y.astype(o_ref.dtype)
            if dil == 1:
                o_ref[0, rows, sl] = y
            else:
                for b in range(rows_per // PERM_ROWS):
                    a0 = (p * (rows_per // PERM_ROWS) + b) * nb
                    for r in range(dil):
                        o_ref[r, a0:a0 + nb, sl] = y[b * PERM_ROWS + r * nb:b * PERM_ROWS + (r + 1) * nb, :]


def _proj(x, gain, w, eg=None, ef=None, *, dil=1, kblock=0, out_dtype=BF16, tm=1024, tn=512):
    B, S, C = x.shape
    K, N = w.shape
    tm, tn = _tile(S, tm), _tile(N, tn)
    epilogue = eg is not None
    if not epilogue:
        eg = jnp.ones((1, N), F32)
        ef = jnp.zeros((1, N), F32)
    assert dil == 1 or (tm % PERM_ROWS == 0 and PERM_ROWS % (16 * dil) == 0)
    prenormed = gain is None
    if prenormed:
        assert x.dtype == BF16
        gain = jnp.ones((K,), F32)
    out = pl.pallas_call(
        functools.partial(_proj_kernel, epilogue=epilogue, dil=dil, prenormed=prenormed),
        grid=(B, S // tm, N // tn),
        in_specs=[
            pl.BlockSpec((None, tm, K), lambda b, i, j: (b, i, kblock)),
            pl.BlockSpec((1, K), lambda b, i, j: (0, 0)),
            pl.BlockSpec((K, tn), lambda b, i, j: (0, j)),
            pl.BlockSpec((1, tn), lambda b, i, j: (0, j)),
            pl.BlockSpec((1, tn), lambda b, i, j: (0, j)),
        ],
        out_specs=pl.BlockSpec((None, dil, tm // dil, tn), lambda b, i, j: (b, 0, i, j)),
        out_shape=jax.ShapeDtypeStruct((B, dil, S // dil, N), out_dtype),
        scratch_shapes=[pltpu.VMEM((tm, K), BF16)] * ((not prenormed) + (dil > 1)),
        compiler_params=_params("parallel", "parallel", "arbitrary"),
        name="proj",
    )(x, gain.reshape(1, K).astype(F32), w, eg.reshape(1, N), ef.reshape(1, N))
    return out.reshape(B, S, N) if dil == 1 else out


def _out_kernel(x_ref, a_ref, b_ref, wa_ref, wb_ref, o_ref):
    acc = jnp.dot(a_ref[...], wa_ref[...], preferred_element_type=F32)
    acc += jnp.dot(b_ref[...], wb_ref[...], preferred_element_type=F32)
    o_ref[...] = x_ref[...] + acc


def _out_proj(x2, mix2, mo2, wa, wb):
    T, D = x2.shape
    tm, tn = _tile(T, 1024), _tile(D, 1024)
    ka, kb = mix2.shape[1], mo2.shape[1]
    return pl.pallas_call(
        _out_kernel,
        grid=(T // tm, D // tn),
        in_specs=[
            pl.BlockSpec((tm, tn), lambda i, j: (i, j)),
            pl.BlockSpec((tm, ka), lambda i, j: (i, 0)),
            pl.BlockSpec((tm, kb), lambda i, j: (i, 0)),
            pl.BlockSpec((ka, tn), lambda i, j: (0, j)),
            pl.BlockSpec((kb, tn), lambda i, j: (0, j)),
        ],
        out_specs=pl.BlockSpec((tm, tn), lambda i, j: (i, j)),
        out_shape=jax.ShapeDtypeStruct((T, D), F32),
        compiler_params=_params("parallel", "arbitrary"),
        name="out_proj",
    )(x2, mix2, mo2, wa, wb)


def _lanes(x, reps):
    return x if reps == 1 else jnp.concatenate([x] * reps, axis=-1)


def _softmax_probs(scores, m_ref):
    ps, alphas = [], []
    for g, s in enumerate(scores):
        m_old = m_ref[g]
        m_new = jnp.maximum(m_old, jnp.max(s, axis=-1, keepdims=True))
        ps.append(jnp.exp2(s - _lanes(m_new, s.shape[1] // LANE)).astype(BF16))
        alphas.append(jnp.exp2(m_old - m_new))
        m_ref[g] = m_new
    return ps, alphas


def _accumulate(ps, alphas, values, acc_ref):
    for g, (p, alpha, v) in enumerate(zip(ps, alphas, values)):
        v1 = jnp.concatenate([v, jnp.ones_like(v)], axis=-1)
        acc_ref[g] = _lanes(alpha, 2) * acc_ref[g] + jnp.dot(p, v1, preferred_element_type=F32)


def _online_softmax(n_chunks, produce, values, s_ref, m_ref, acc_ref, finish=None):
    G = s_ref.shape[1]

    def consume(j, scores):
        return _softmax_probs(scores, m_ref)

    def accumulate(j, ps, alphas):
        _accumulate(ps, alphas, values(j), acc_ref)

    m_ref[...] = jnp.full_like(m_ref, NEG_BIG)
    acc_ref[...] = jnp.zeros_like(acc_ref)
    for g, s in enumerate(produce(0)):
        s_ref[0, g] = s

    def step(j, slot):
        cur = [s_ref[slot, g] for g in range(G)]
        nxt = produce(j + 1)
        ps, alphas = consume(j, cur)
        for g, s in enumerate(nxt):
            s_ref[1 - slot, g] = s
        accumulate(j, ps, alphas)

    def body(jj, carry):
        step(2 * jj, 0)
        step(2 * jj + 1, 1)
        return carry

    last = n_chunks - 1
    lax.fori_loop(0, last // 2, body, 0)

    @pl.when(last % 2 == 1)
    def _():
        step(last - 1, 0)

    cur = [s_ref[last % 2, g] for g in range(G)]
    if finish is not None:
        cur = [finish(s) for s in cur]
    accumulate(last, *consume(last, cur))


def _flash_kernel(*refs, nqk, decay, tq, G):
    q_refs, k_refs, v_ref = refs[:nqk], refs[nqk:2 * nqk], refs[2 * nqk]
    pos = 2 * nqk + 1
    if decay:
        ck_ref = refs[pos]
        pos += 1
    o_ref, first_ref, s_ref, m_ref, acc_ref = refs[pos:pos + 5]
    nq = o_ref.shape[0] // tq

    def head(ref, g, rows):
        if ref.shape[-1] == LANE:
            return ref[rows, :]
        return ref[rows, g * LANE:(g + 1) * LANE]

    def block_rows(j):
        return pl.ds(pl.multiple_of(j * tq, tq), tq)

    def joined(refs_, g, rows):
        parts = [head(r, g, rows) for r in refs_]
        return parts[0] if nqk == 1 else jnp.concatenate(parts, axis=-1)

    def produce(i, j):
        krows = block_rows(j)
        scores = []
        for g in range(G):
            s = lax.dot_general(joined(q_refs, g, block_rows(i)), joined(k_refs, g, krows),
                                (((1,), (1,)), ((), ())), preferred_element_type=F32)
            scores.append(s - ck_ref[g, :, krows] if decay else s)
        return scores

    def values(j):
        return [head(v_ref, g, block_rows(j)) for g in range(G)]

    def reset():
        m_ref[...] = jnp.full_like(m_ref, NEG_BIG)
        acc_ref[...] = jnp.zeros_like(acc_ref)

    def step(i, t, src, dst):
        cur = [src[g] for g in range(G)]
        nxt = produce(i, t + 1)
        ps, alphas = _softmax_probs(cur, m_ref)
        for g, s in enumerate(nxt):
            dst[g] = s
        _accumulate(ps, alphas, values(t), acc_ref)

    def last_step(i, src, dst):
        row = lax.broadcasted_iota(jnp.int32, (tq, tq), 0)
        col = lax.broadcasted_iota(jnp.int32, (tq, tq), 1)
        cur = [jnp.where(col <= row, src[g], -jnp.inf) for g in range(G)]
        nxt = produce(jnp.minimum(i + 1, nq - 1), 0)
        ps, alphas = _softmax_probs(cur, m_ref)
        for g, s in enumerate(nxt):
            dst[g] = s
        _accumulate(ps, alphas, values(i), acc_ref)
        for g in range(G):
            acc = acc_ref[g]
            o_ref[block_rows(i), g * LANE:(g + 1) * LANE] = (
                acc[:, :LANE] / acc[:, LANE:]).astype(o_ref.dtype)
        reset()

    def block(i, even):
        step(i, 0, first_ref.at[0 if even else 1], s_ref.at[1])

        def pair(jj, carry):
            step(i, 2 * jj + 1, s_ref.at[1], s_ref.at[0])
            step(i, 2 * jj + 2, s_ref.at[0], s_ref.at[1])
            return carry

        lax.fori_loop(0, (i - 1) // 2, pair, 0)
        if even:
            step(i, i - 1, s_ref.at[1], s_ref.at[0])
            last_step(i, s_ref.at[0], first_ref.at[1])
        else:
            last_step(i, s_ref.at[1], first_ref.at[0])

    reset()
    for g, s in enumerate(produce(0, 0)):
        first_ref[0, g] = s
    last_step(0, first_ref.at[0], first_ref.at[1])
    block(1, even=False)

    def two_blocks(ii, carry):
        block(2 * ii, even=True)
        block(2 * ii + 1, even=False)
        return carry

    lax.fori_loop(1, nq // 2, two_blocks, 0)


def _flash(q_parts, k_parts, v_part, B, S, decay_t=None, tq=512, G=2):
    tq = _tile(S, tq)
    assert (S // tq) % 2 == 0
    nqk = len(q_parts)
    W = G * LANE
    args, specs = [], []
    for arr, off, per_head in q_parts + k_parts + [v_part]:
        args.append(arr)
        if per_head:
            specs.append(pl.BlockSpec((None, S, W), lambda b, h, off=off: (b, 0, off // W + h)))
        else:
            specs.append(pl.BlockSpec((None, S, LANE), lambda b, h, off=off: (b, 0, off // LANE)))
    if decay_t is not None:
        args.append(decay_t)
        specs.append(pl.BlockSpec((None, G, 1, S), lambda b, h: (b, h, 0, 0)))
    tiles = pltpu.VMEM((2, G, tq, tq), F32)
    return pl.pallas_call(
        functools.partial(_flash_kernel, nqk=nqk, decay=decay_t is not None, tq=tq, G=G),
        grid=(B, N_HEADS // G),
        in_specs=specs,
        out_specs=pl.BlockSpec((None, S, W), lambda b, h: (b, 0, h)),
        out_shape=jax.ShapeDtypeStruct((B, S, MIX_WIDTH), BF16),
        scratch_shapes=[tiles, tiles, pltpu.VMEM((G, tq, LANE), F32),
                        pltpu.VMEM((G, tq, 2 * HEAD_DIM), F32)],
        compiler_params=_params("parallel", "parallel"),
        name="flash",
    )(*args)


def _gate_kernel(fg_ref, bf_ref, cumt_ref, carry_ref, *, ts):
    @pl.when(pl.program_id(1) == 0)
    def _():
        carry_ref[...] = jnp.zeros_like(carry_ref)

    z = fg_ref[...] + bf_ref[...]
    lf = jnp.minimum(z, 0.0) - jnp.log1p(jnp.exp(-jnp.abs(z)))
    hi = lf.astype(BF16)
    r1 = lf - hi.astype(F32)
    mid = r1.astype(BF16)
    lo = (r1 - mid.astype(F32)).astype(BF16)
    row = lax.broadcasted_iota(jnp.int32, (ts, ts), 0)
    col = lax.broadcasted_iota(jnp.int32, (ts, ts), 1)
    tri = jnp.where(col <= row, 1.0, 0.0).astype(BF16)
    cum = (jnp.dot(tri, hi, preferred_element_type=F32)
           + jnp.dot(tri, mid, preferred_element_type=F32)
           + jnp.dot(tri, lo, preferred_element_type=F32)) + carry_ref[...]
    carry_ref[...] = cum[ts - 1:ts, :]
    cumt_ref[...] = (cum * LOG2E).T


def _fox_gates(fg, b_f):
    B, S, _ = fg.shape
    ts = _tile(S, 512)
    bf = jnp.zeros((1, LANE), F32).at[0, :N_HEADS].set(b_f.astype(F32))
    return pl.pallas_call(
        functools.partial(_gate_kernel, ts=ts),
        grid=(B, S // ts),
        in_specs=[pl.BlockSpec((None, ts, LANE), lambda b, i: (b, i, 0)),
                  pl.BlockSpec((1, LANE), lambda b, i: (0, 0))],
        out_specs=pl.BlockSpec((None, LANE, ts), lambda b, i: (b, 0, i)),
        out_shape=jax.ShapeDtypeStruct((B, LANE, S), F32),
        scratch_shapes=[pltpu.VMEM((1, LANE), F32)],
        compiler_params=_params("parallel", "arbitrary"),
        name="fox_gates",
    )(fg, bf)


def _rope_kernel(x_ref, cos_ref, sin_ref, g_ref, o_ref):
    cos, sin, g = cos_ref[...], sin_ref[...], g_ref[...]
    for c in range(x_ref.shape[1] // LANE):
        sl = slice(c * LANE, (c + 1) * LANE)
        x = x_ref[:, sl]
        ms = jnp.sum(x * x, axis=-1, keepdims=True) * (1.0 / ROPE_DIM)
        y = x * lax.rsqrt(ms + RMS_EPS) * g
        partner = pltpu.roll(y, ROPE_DIM // 2, 1) + pltpu.roll(y, LANE - ROPE_DIM // 2, 1)
        o_ref[:, sl] = (y * cos + partner * sin).astype(o_ref.dtype)


def _rope(x, cos, sin, gain):
    B, S, N = x.shape
    ts = _tile(S, 512)
    g = jnp.zeros((1, LANE), F32).at[0, :ROPE_DIM].set(gain.astype(F32))
    return pl.pallas_call(
        _rope_kernel,
        grid=(B, S // ts),
        in_specs=[pl.BlockSpec((None, ts, N), lambda b, i: (b, i, 0)),
                  pl.BlockSpec((None, ts, LANE), lambda b, i: (b, i, 0)),
                  pl.BlockSpec((None, ts, LANE), lambda b, i: (b, i, 0)),
                  pl.BlockSpec((1, LANE), lambda b, i: (0, 0))],
        out_specs=pl.BlockSpec((None, ts, N), lambda b, i: (b, i, 0)),
        out_shape=jax.ShapeDtypeStruct((B, S, N), BF16),
        compiler_params=_params("parallel", "parallel"),
        name="rope",
    )(x, cos, sin, g)


def _mem_kernel(q_ref, kv_ref, o_ref):
    for h in range(MEM_HEADS):
        q = q_ref[:, h * LANE:(h + 1) * LANE]
        k = kv_ref[:, h * LANE:(h + 1) * LANE]
        v = kv_ref[:, MEM_WIDTH + h * LANE:MEM_WIDTH + (h + 1) * LANE]
        s = lax.dot_general(q, k, (((1,), (1,)), ((), ())), preferred_element_type=F32)
        p = jnp.exp(s - jnp.max(s, axis=-1, keepdims=True))
        o = jnp.dot(p.astype(BF16), v, preferred_element_type=F32)
        o_ref[:, h * LANE:(h + 1) * LANE] = (o / jnp.sum(p, axis=-1, keepdims=True)).astype(o_ref.dtype)


def _mem_attention(qarr, qblock, mem_kv):
    B, S, _ = qarr.shape
    n_mem = mem_kv.shape[1]
    tq = _tile(S, 1024)
    return pl.pallas_call(
        _mem_kernel,
        grid=(B, S // tq),
        in_specs=[pl.BlockSpec((None, tq, MEM_WIDTH), lambda b, i: (b, i, qblock)),
                  pl.BlockSpec((None, n_mem, 2 * MEM_WIDTH), lambda b, i: (b, 0, 0))],
        out_specs=pl.BlockSpec((None, tq, MEM_WIDTH), lambda b, i: (b, i, 0)),
        out_shape=jax.ShapeDtypeStruct((B, S, MEM_WIDTH), BF16),
        compiler_params=_params("parallel", "parallel"),
        name="mem_attn",
    )(qarr, mem_kv)


def _dil_kernel(q_ref, kp_ref, kc_ref, vp_ref, vc_ref, bias_ref, o_ref, lse_ref):
    n = DIL_BLOCK
    nsub = q_ref.shape[0] // n
    row = lax.broadcasted_iota(jnp.int32, (n, 2 * n), 0)
    col = lax.broadcasted_iota(jnp.int32, (n, 2 * n), 1)
    band = jnp.logical_and(col >= row, col <= row + n)
    first = jnp.logical_and(band, jnp.logical_or(col >= n, pl.program_id(1) > 0))
    lane = lax.broadcasted_iota(jnp.int32, (n, LANE), 1)
    dn = (((1,), (1,)), ((), ()))
    heads = [slice(h * LANE, (h + 1) * LANE) for h in range(N_HEADS)]

    def window(prev_ref, cur_ref, u, sl):
        if u == 0:
            return jnp.concatenate([prev_ref[:, sl], cur_ref[:n, sl]], axis=0)
        return cur_ref[(u - 1) * n:(u + 1) * n, sl]

    for u in range(nsub):
        rows = slice(u * n, (u + 1) * n)
        valid = first if u == 0 else band
        scores = []
        for h, sl in enumerate(heads):
            s = lax.dot_general(q_ref[rows, sl], window(kp_ref, kc_ref, u, sl), dn,
                                preferred_element_type=F32) + bias_ref[h]
            scores.append(jnp.where(valid, s, -jnp.inf))
        ms = [jnp.max(s, axis=-1, keepdims=True) for s in scores]
        ps = [jnp.exp2(s - m).astype(BF16) for s, m in zip(scores, ms)]
        lse_all = jnp.zeros((n, LANE), F32)
        for h, sl in enumerate(heads):
            v2 = window(vp_ref, vc_ref, u, sl)
            pv = jnp.dot(ps[h], jnp.concatenate([v2, jnp.ones_like(v2)], axis=-1),
                         preferred_element_type=F32)
            den = pv[:, LANE:]
            o_ref[h, rows, :] = pv[:, :LANE] / den
            lse_all = jnp.where(lane == h, ms[h] + jnp.log2(den), lse_all)
        lse_ref[rows, :] = lse_all


def _dil_attention(pg, bias):
    N, Ls, _ = pg.shape
    n = DIL_BLOCK
    nsub = 2 if Ls % (2 * n) == 0 else 1
    tr = nsub * n

    def cur(part):
        return pl.BlockSpec((None, tr, MIX_WIDTH), lambda s, i: (s, i, part))

    def prev(part):
        return pl.BlockSpec((None, n, MIX_WIDTH), lambda s, i: (s, jnp.maximum(i * nsub - 1, 0), part))

    return pl.pallas_call(
        _dil_kernel,
        grid=(N, Ls // tr),
        in_specs=[cur(0), prev(1), cur(1), prev(2), cur(2),
                  pl.BlockSpec((N_HEADS, n, 2 * n), lambda s, i: (0, 0, 0))],
        out_specs=[pl.BlockSpec((None, N_HEADS, tr, LANE), lambda s, i: (s, 0, i, 0)),
                   pl.BlockSpec((None, tr, LANE), lambda s, i: (s, i, 0))],
        out_shape=[jax.ShapeDtypeStruct((N, N_HEADS, Ls, LANE), F32),
                   jax.ShapeDtypeStruct((N, Ls, LANE), F32)],
        compiler_params=_params("parallel", "arbitrary"),
        name="dil_attn",
    )(pg, pg, pg, pg, pg, bias)


def _dil_combine_kernel(o1_ref, o2_ref, o3_ref, l1_ref, l2_ref, l3_ref, out_ref, stage_ref, *, d2, d3):
    n = l3_ref.shape[1]
    q = d3 // d2
    for r in range(d3):
        rows = [(pl.ds(r, n, stride=d3),), (r % d2, pl.ds(r // d2, n, stride=q)), (r, slice(None))]
        ls = [l_ref[idx + (slice(None),)]
              for l_ref, idx in zip((l1_ref, l2_ref, l3_ref), rows)]
        m = functools.reduce(jnp.maximum, ls)
        es = [jnp.exp2(l - m) for l in ls]
        inv = 1.0 / functools.reduce(jnp.add, es)
        ws = [e * inv for e in es]
        for h in range(N_HEADS):
            o = (ws[0][:, h:h + 1] * o1_ref[h, pl.ds(r, n, stride=d3), :]
                 + ws[1][:, h:h + 1] * o2_ref[r % d2, h, pl.ds(r // d2, n, stride=q), :]
                 + ws[2][:, h:h + 1] * o3_ref[r, h, :, :])
            stage_ref[h, pl.ds(r, n, stride=d3), :] = o
    for h in range(N_HEADS):
        out_ref[:, h * LANE:(h + 1) * LANE] = stage_ref[h].astype(out_ref.dtype)


def _dil_combine(outs, lses, B, S, tm=256):
    (_, d1), (_, d2), (_, d3) = DIL_GROUPS
    assert d1 == 1 and d3 % d2 == 0
    tm = _tile(S, tm)
    H = N_HEADS
    o_specs = [pl.BlockSpec((None, H, tm, LANE), lambda b, i: (b, 0, i, 0)),
               pl.BlockSpec((None, d2, H, tm // d2, LANE), lambda b, i: (b, 0, 0, i, 0)),
               pl.BlockSpec((None, d3, H, tm // d3, LANE), lambda b, i: (b, 0, 0, i, 0))]
    l_specs = [pl.BlockSpec((None, tm, LANE), lambda b, i: (b, i, 0)),
               pl.BlockSpec((None, d2, tm // d2, LANE), lambda b, i: (b, 0, i, 0)),
               pl.BlockSpec((None, d3, tm // d3, LANE), lambda b, i: (b, 0, i, 0))]
    return pl.pallas_call(
        functools.partial(_dil_combine_kernel, d2=d2, d3=d3),
        grid=(B, S // tm),
        in_specs=o_specs + l_specs,
        out_specs=pl.BlockSpec((None, tm, MIX_WIDTH), lambda b, i: (b, i, 0)),
        out_shape=jax.ShapeDtypeStruct((B, S, MIX_WIDTH), BF16),
        scratch_shapes=[pltpu.VMEM((H, tm, LANE), F32)],
        compiler_params=_params("parallel", "parallel"),
        name="dil_combine",
    )(outs[0].reshape(B, H, S, LANE), outs[1].reshape(B, d2, H, S // d2, LANE),
      outs[2].reshape(B, d3, H, S // d3, LANE),
      lses[0].reshape(B, S, LANE), lses[1].reshape(B, d2, S // d2, LANE),
      lses[2].reshape(B, d3, S // d3, LANE))


def _dsa_topk_kernel(qi_ref, wi_ref, ki_ref, mask_ref, key_ref, jsel_ref, *, tq, tk, S, n_sel):
    i = pl.program_id(1)
    q0 = i * tq
    n_chunks = (q0 + tq + tk - 1) // tk
    dn = (((1,), (1,)), ((), ()))
    krow = lax.broadcasted_iota(jnp.int32, (tk, tq), 0)
    qpos = q0 + lax.broadcasted_iota(jnp.int32, (tk, tq), 1)

    wi_t = (wi_ref[...] * (IDX_HEADS ** -0.5 * IDX_DIM ** -0.5)).T

    def idx_body(c, carry):
        off = pl.multiple_of(c * tk, tk)
        kic = ki_ref[pl.ds(off, tk), :]
        k_lo = jnp.where(lax.broadcasted_iota(jnp.int32, kic.shape, 1) < IDX_DIM, kic, 0.0)
        k_sides = (k_lo.astype(BF16), pltpu.roll(k_lo, IDX_DIM, 1).astype(BF16))
        acc = jnp.zeros((tk, tq), F32)
        for h in range(IDX_HEADS):
            d = lax.dot_general(k_sides[h % 2], qi_ref[:, (h // 2) * LANE:(h // 2 + 1) * LANE], dn,
                                preferred_element_type=F32)
            acc = acc + wi_t[IDX_DIM + h:IDX_DIM + h + 1, :] * jnp.maximum(d, 0.0)
        score = jnp.where(off + krow <= qpos, acc, -jnp.inf)
        bits = pltpu.bitcast(score, jnp.int32)
        key_ref[pl.ds(off, tk), :] = bits ^ ((bits >> 31) & jnp.int32(0x7FFFFFFF))
        return carry

    lax.fori_loop(0, n_chunks, idx_body, 0)

    def count(pred):
        sub = 64
        def body(c, acc):
            off = pl.multiple_of(c * tk, tk)
            ones = jnp.where(pred(key_ref[pl.ds(off, tk), :], off), 1.0, 0.0)
            return acc + jnp.sum(ones.reshape(tk // sub, sub, tq), axis=0)

        acc = lax.fori_loop(0, n_chunks, body, jnp.zeros((sub, tq), F32))
        return jnp.sum(acc, axis=0, keepdims=True)

    k_sel = float(n_sel)
    c0 = count(lambda k, off: k >= 0)
    lo0 = jnp.where(c0 >= k_sel, jnp.int32(0), jnp.int32(-2 ** 31))
    n0 = jnp.where(c0 >= k_sel, c0, (n_chunks * tk).astype(F32))

    def bis_cond(state):
        it, _, n_lo = state
        return jnp.logical_and(it < 31, jnp.max(jnp.abs(n_lo - k_sel)) > 0.0)

    def bis_body(state):
        it, lo, n_lo = state
        cand = lo + (jnp.int32(1) << (30 - it))
        cnt = count(lambda k, off: k >= cand)
        take = cnt >= k_sel
        return it + 1, jnp.where(take, cand, lo), jnp.where(take, cnt, n_lo)

    _, thr, n_ge = lax.while_loop(bis_cond, bis_body, (jnp.int32(0), lo0, n0))
    n_bits = max(1, (S - 1).bit_length())
    jsel_ref[...] = jnp.full_like(jsel_ref, S)

    @pl.when(jnp.max(n_ge) > k_sel)
    def _():
        need = k_sel - count(lambda k, off: k > thr)

        def tie_body(it, jlo):
            cand = jlo + (jnp.int32(1) << (n_bits - 1 - it))
            below = count(lambda k, off: jnp.logical_and(k == thr, off + krow < cand))
            return jnp.where(below < need, cand, jlo)

        jsel = lax.fori_loop(0, n_bits, tie_body, jnp.zeros((1, tq), jnp.int32))
        jsel_ref[...] = jnp.broadcast_to(jsel, jsel_ref.shape)

    jsel = jsel_ref[:1, :]
    mask_ref[...] = jnp.full_like(mask_ref, -jnp.inf)

    def mask_body(c, carry):
        off = pl.multiple_of(c * tk, tk)
        k = key_ref[pl.ds(off, tk), :]
        kpos = off + krow
        tie = jnp.logical_and(k == thr, kpos <= jsel)
        sel = jnp.logical_and(jnp.logical_or(k > thr, tie), kpos <= qpos)
        mask_ref[:, pl.ds(off, tk)] = jnp.where(sel, 0.0, -jnp.inf).T.astype(mask_ref.dtype)
        return carry

    lax.fori_loop(0, n_chunks, mask_body, 0)


def _dsa_select(pm, aux, B, S, n_sel, tq=256, tk=512):
    tq, tk = _tile(S, tq), _tile(S, tk)
    qiw = IDX_HEADS * IDX_DIM
    return pl.pallas_call(
        functools.partial(_dsa_topk_kernel, tq=tq, tk=tk, S=S, n_sel=n_sel),
        grid=(B, S // tq),
        in_specs=[pl.BlockSpec((None, tq, qiw), lambda b, i: (b, i, MIX_WIDTH // qiw)),
                  pl.BlockSpec((None, tq, LANE), lambda b, i: (b, i, 0)),
                  pl.BlockSpec((None, S, LANE), lambda b, i: (b, 0, 0))],
        out_specs=pl.BlockSpec((None, tq, S), lambda b, i: (b, i, 0)),
        out_shape=jax.ShapeDtypeStruct((B, S, S), BF16),
        scratch_shapes=[pltpu.VMEM((S, tq), jnp.int32), pltpu.VMEM((8, tq), jnp.int32)],
        compiler_params=_params("parallel", "arbitrary"),
        name="dsa_select",
    )(pm, aux, aux)


def _dsa_attn_kernel(q_ref, k_ref, v_ref, mask_ref, gt_ref, o_ref, s_ref, m_ref, acc_ref, *, tq, tk, S):
    i = pl.program_id(2)
    q0 = i * tq
    n_chunks = (q0 + tq + tk - 1) // tk
    R = q_ref.shape[1] // LANE
    nblk = tq // LANE
    dn = (((1,), (1,)), ((), ()))
    qs = [q_ref[:, r * LANE:(r + 1) * LANE] for r in range(R)]

    def produce(c):
        off = pl.multiple_of(c * tk, tk)
        kc = k_ref[pl.ds(off, tk), :]
        mask = mask_ref[:, pl.ds(off, tk)].astype(F32)
        st = S - q0 + off
        scores = []
        for r in range(R):
            bias = jnp.concatenate(
                [gt_ref[r, :, pl.ds(pl.multiple_of(st + (nblk - 1 - a) * LANE, LANE), tk)]
                 for a in range(nblk)], axis=0)
            scores.append(lax.dot_general(qs[r], kc, dn, preferred_element_type=F32) + bias + mask)
        return scores

    def values(c):
        return [v_ref[pl.ds(pl.multiple_of(c * tk, tk), tk), :]] * R

    _online_softmax(n_chunks, produce, values, s_ref, m_ref, acc_ref)
    for r in range(R):
        acc = acc_ref[r]
        o_ref[:, r * LANE:(r + 1) * LANE] = (acc[:, :LANE] / acc[:, LANE:]).astype(o_ref.dtype)


def _dsa_attention(pm, mask, gt, B, S, tq=256, tk=512):
    tq, tk = _tile(S, tq), _tile(S, tk)
    R = N_HEADS // DSA_KV_HEADS
    kblk = (MIX_WIDTH + IDX_HEADS * IDX_DIM) // LANE
    return pl.pallas_call(
        functools.partial(_dsa_attn_kernel, tq=tq, tk=tk, S=S),
        grid=(DSA_KV_HEADS, B, S // tq),
        in_specs=[pl.BlockSpec((None, tq, R * LANE), lambda g, b, i: (b, i, g)),
                  pl.BlockSpec((None, S, LANE), lambda g, b, i: (b, 0, kblk + g)),
                  pl.BlockSpec((None, S, LANE), lambda g, b, i: (b, 0, kblk + DSA_KV_HEADS + g)),
                  pl.BlockSpec((None, tq, S), lambda g, b, i: (b, i, 0)),
                  pl.BlockSpec((R, LANE, gt.shape[2]), lambda g, b, i: (g, 0, 0))],
        out_specs=pl.BlockSpec((None, tq, R * LANE), lambda g, b, i: (b, i, g)),
        out_shape=jax.ShapeDtypeStruct((B, S, MIX_WIDTH), BF16),
        scratch_shapes=[pltpu.VMEM((2, R, tq, tk), F32), pltpu.VMEM((R, tq, LANE), F32),
                        pltpu.VMEM((R, tq, 2 * HEAD_DIM), F32)],
        compiler_params=_params("parallel", "parallel", "arbitrary"),
        name="dsa_attn",
    )(pm, pm, pm, mask, gt)


def _t5_bucket(dist):
    n = jnp.maximum(dist, 0)
    exact = REL_BUCKETS // 2
    nf = jnp.maximum(n, 1).astype(F32)
    large = exact + (jnp.log(nf / exact) / math.log(REL_MAX_DIST / exact)
                     * (REL_BUCKETS - exact)).astype(jnp.int32)
    large = jnp.minimum(large, REL_BUCKETS - 1)
    return jnp.where(n < exact, n, large)


def _t5_bias(t5_table, dist):
    onehot = jax.nn.one_hot(_t5_bucket(dist), REL_BUCKETS, dtype=F32)
    table = jnp.einsum('...b,bh->h...', onehot, t5_table.astype(F32), precision=lax.Precision.HIGHEST)
    return table * LOG2E


def _dil_bias(t5_table, dil):
    n = DIL_BLOCK
    rel = jnp.arange(n)[:, None] + n - jnp.arange(2 * n)[None, :]
    return _t5_bias(t5_table, rel * dil)


def _dsa_bias_table(t5_table, S, tq, tk):
    lt = S + tq + tk
    tb = _t5_bias(t5_table, S + tq - 1 - jnp.arange(lt + LANE))
    return jnp.stack([tb[:, LANE - 1 - i:LANE - 1 - i + lt] for i in range(LANE)], axis=1)


def _headnorm_cols(spec):
    eg, ef = [], []
    for gain, count, scale in spec:
        if gain is None:
            eg.append(jnp.ones((count * LANE,), F32))
            ef.append(jnp.zeros((count * LANE,), F32))
        else:
            eg.append(jnp.tile(gain.astype(F32) * scale, count))
            ef.append(jnp.ones((count * LANE,), F32))
    return jnp.concatenate(eg), jnp.concatenate(ef)


def _pad_cols(w, width):
    return jnp.pad(w, ((0, 0), (0, width - w.shape[1])))


def _fox_layer(x, norm_g, w_in, b_f, qk_g, mem_g):
    B, S, _ = x.shape
    W = MIX_WIDTH
    w_main = jnp.concatenate([w_in[:, :3 * W], w_in[:, 3 * W + N_HEADS:]], axis=1).astype(BF16)
    w_gate = _pad_cols(w_in[:, 3 * W:3 * W + N_HEADS], LANE).astype(BF16)
    eg, ef = _headnorm_cols([(qk_g[0], N_HEADS, HEAD_DIM ** -0.5 * LOG2E), (qk_g[1], N_HEADS, 1.0),
                             (None, N_HEADS, 1.0), (mem_g, MEM_HEADS, HEAD_DIM ** -0.5)])
    pm = _proj(x, norm_g, w_main, eg, ef)
    fg = _proj(x, norm_g, w_gate, out_dtype=F32)
    cum_t = _fox_gates(fg, b_f)
    mix = _flash([(pm, 0, True)], [(pm, W, True)], (pm, 2 * W, True), B, S,
                 decay_t=cum_t.reshape(B, LANE, 1, S))
    return mix, pm, 3 * W // MEM_WIDTH


def _mla_layer(x, positions, norm_g, w_in, q_norm, w_uq, kv_norm, w_ukv, nope_g, rope_g, mem_g):
    B, S, _ = x.shape
    scale = (NOPE_DIM + ROPE_DIM) ** -0.5 * LOG2E
    lat = Q_LORA + KV_LORA
    w_lat = jnp.concatenate([w_in[:, :lat], _pad_cols(w_in[:, lat:lat + ROPE_DIM], LANE)],
                            axis=1).astype(BF16)
    w_mem = w_in[:, lat + ROPE_DIM:].astype(BF16)
    pl_ = _proj(x, norm_g, w_lat, out_dtype=F32, tn=lat + LANE)
    eg, ef = _headnorm_cols([(mem_g, MEM_HEADS, HEAD_DIM ** -0.5)])
    pmem = _proj(x, norm_g, w_mem, eg, ef)

    uq = w_uq.reshape(Q_LORA, N_HEADS, NOPE_DIM + ROPE_DIM)
    w_qn = uq[:, :, :NOPE_DIM].reshape(Q_LORA, MIX_WIDTH).astype(BF16)
    w_qr = jnp.pad(uq[:, :, NOPE_DIM:], ((0, 0), (0, 0), (0, LANE - ROPE_DIM))
                   ).reshape(Q_LORA, N_HEADS * LANE).astype(BF16)
    ukv = w_ukv.reshape(KV_LORA, N_HEADS, NOPE_DIM + HEAD_DIM)
    w_kv = jnp.concatenate([ukv[:, :, :NOPE_DIM].reshape(KV_LORA, MIX_WIDTH),
                            ukv[:, :, NOPE_DIM:].reshape(KV_LORA, MIX_WIDTH)], axis=1).astype(BF16)
    eg, ef = _headnorm_cols([(nope_g[0], N_HEADS, scale)])
    qn = _proj(pl_, q_norm, w_qn, eg, ef, kblock=0, tn=2048)
    qr_raw = _proj(pl_, q_norm, w_qr, kblock=0, out_dtype=F32, tn=1024)
    eg, ef = _headnorm_cols([(nope_g[1], N_HEADS, 1.0), (None, N_HEADS, 1.0)])
    kv = _proj(pl_, kv_norm, w_kv, eg, ef, kblock=1, tn=2048)

    half = ROPE_DIM // 2
    inv = ROPE_THETA ** (-jnp.arange(half, dtype=F32) / half)
    ang = positions.astype(F32)[..., None] * inv
    cos, sin = jnp.cos(ang), jnp.sin(ang)
    zero = jnp.zeros((B, S, LANE - ROPE_DIM), F32)
    cos_t = jnp.concatenate([cos, cos, zero], axis=-1)
    sin_t = jnp.concatenate([-sin, sin, zero], axis=-1)
    qr = _rope(qr_raw, cos_t, sin_t, rope_g[0] * scale)
    kr = _rope(pl_[:, :, lat:], cos_t, sin_t, rope_g[1])
    mix = _flash([(qn, 0, True), (qr, 0, True)], [(kv, 0, True), (kr, 0, False)],
                 (kv, MIX_WIDTH, True), B, S)
    return mix, pmem, 0


def _dil_layer(x, norm_g, w_in, qk_g, t5_table, mem_g):
    B, S, _ = x.shape
    W = MIX_WIDTH
    outs, lses = [], []
    for gi, (win, dil) in enumerate(DIL_GROUPS):
        assert win // dil == DIL_BLOCK and (S // dil) % DIL_BLOCK == 0
        w_g = w_in[:, gi * 3 * W:(gi + 1) * 3 * W].astype(BF16)
        eg, ef = _headnorm_cols([(qk_g[gi, 0], N_HEADS, HEAD_DIM ** -0.5 * LOG2E),
                                 (qk_g[gi, 1], N_HEADS, 1.0), (None, N_HEADS, 1.0)])
        pg = _proj(x, norm_g, w_g, eg, ef, dil=dil, tn=1024)
        o, lse = _dil_attention(pg.reshape(B * dil, S // dil, 3 * W), _dil_bias(t5_table, dil))
        outs.append(o)
        lses.append(lse)
    mix = _dil_combine(outs, lses, B, S)
    eg, ef = _headnorm_cols([(mem_g, MEM_HEADS, HEAD_DIM ** -0.5)])
    pmem = _proj(x, norm_g, w_in[:, len(DIL_GROUPS) * 3 * W:].astype(BF16), eg, ef)
    return mix, pmem, 0


def _dsa_layer(x, norm_g, w_in, qk_g, t5_table, mem_g):
    B, S, _ = x.shape
    W = MIX_WIDTH
    kvw = DSA_KV_HEADS * HEAD_DIM
    o_k, o_v, o_qi = W, W + kvw, W + 2 * kvw
    o_ki = o_qi + IDX_HEADS * IDX_DIM
    o_wi = o_ki + IDX_DIM
    o_mem = o_wi + IDX_HEADS
    qiw = IDX_HEADS * IDX_DIM
    w_main = jnp.concatenate([w_in[:, :W], w_in[:, o_qi:o_ki], w_in[:, o_k:o_qi], w_in[:, o_mem:]],
                             axis=1).astype(BF16)
    w_aux = _pad_cols(w_in[:, o_ki:o_mem], LANE).astype(BF16)
    eg, ef = _headnorm_cols([(qk_g[0], N_HEADS, HEAD_DIM ** -0.5 * LOG2E), (None, qiw // LANE, 1.0),
                             (qk_g[1], DSA_KV_HEADS, 1.0), (None, DSA_KV_HEADS, 1.0),
                             (mem_g, MEM_HEADS, HEAD_DIM ** -0.5)])
    pm = _proj(x, norm_g, w_main, eg, ef)
    aux = _proj(x, norm_g, w_aux, out_dtype=F32)
    mask = _dsa_select(pm, aux, B, S, min(TOPK_MAX, S // 4), tq=512)
    tq, tk = _tile(S, 512), _tile(S, 512)
    mix = _dsa_attention(pm, mask, _dsa_bias_table(t5_table, S, tq, tk), B, S, tq, tk)
    return mix, pm, (W + qiw + 2 * kvw) // MEM_WIDTH


def kernel(x, mem, positions, t5_table, ffn_norm, ffn_w_gate, ffn_w_up, ffn_w_down, attn_norm,
           mem_norm, mem_w_kv, mem_qk_g, w_out, a_w_in, a_b_f, a_qk_g, b_w_in, b_q_norm, b_w_uq,
           b_kv_norm, b_w_ukv, b_nope_g, b_rope_g, c_w_in, c_qk_g, d_w_in, d_qk_g):
    B, S, D = x.shape
    depth = ffn_norm.shape[0]
    n_mixers = 4

    def ffn(xc, i, k, next_gain=None):
        return _ffn(xc.reshape(B * S, D), ffn_norm[i, k], ffn_w_gate[i, k].astype(BF16),
                    ffn_w_up[i, k].astype(BF16), ffn_w_down[i, k].astype(BF16), next_gain)

    for i in range(depth):
        m, j = i % n_mixers, i // n_mixers
        x, h = ffn(x, i, 0, attn_norm[i])
        x, h = x.reshape(B, S, D), h.reshape(B, S, D)
        mem_g = mem_qk_g[i]
        if m == 0:
            mix, qarr, qblock = _fox_layer(h, None, a_w_in[j], a_b_f[j], a_qk_g[j], mem_g[0])
        elif m == 1:
            mix, qarr, qblock = _mla_layer(h, positions, None, b_w_in[j], b_q_norm[j],
                                           b_w_uq[j], b_kv_norm[j], b_w_ukv[j], b_nope_g[j],
                                           b_rope_g[j], mem_g[0])
        elif m == 2:
            mix, qarr, qblock = _dil_layer(h, None, c_w_in[j], c_qk_g[j], t5_table, mem_g[0])
        else:
            mix, qarr, qblock = _dsa_layer(h, None, d_w_in[j], d_qk_g[j], t5_table, mem_g[0])
        eg, ef = _headnorm_cols([(mem_g[1], MEM_HEADS, 1.0), (None, MEM_HEADS, 1.0)])
        mem_kv = _proj(mem, mem_norm[i], mem_w_kv[i].astype(BF16), eg, ef)
        mo = _mem_attention(qarr, qblock, mem_kv)
        wo = w_out[i].astype(BF16)
        x = _out_proj(x.reshape(B * S, D), mix.reshape(B * S, MIX_WIDTH),
                      mo.reshape(B * S, MEM_WIDTH), wo[:MIX_WIDTH], wo[MIX_WIDTH:]).reshape(B, S, D)
        x = ffn(x, i, 1).reshape(B, S, D)
    return x
```

```python
import functools
import math

import jax
import jax.numpy as jnp
from jax import lax
from jax.experimental import pallas as pl
from jax.experimental.pallas import tpu as pltpu

F32 = jnp.float32
BF16 = jnp.bfloat16

LANE = 128
D_MODEL = 2048
N_HEADS = 16
HEAD_DIM = 128
MIX_WIDTH = N_HEADS * HEAD_DIM
MEM_HEADS = 4
MEM_WIDTH = MEM_HEADS * HEAD_DIM
D_FF = 5632
RMS_EPS = 1e-6
REL_BUCKETS = 32
REL_MAX_DIST = 2048
Q_LORA = 512
KV_LORA = 512
NOPE_DIM = 128
ROPE_DIM = 64
ROPE_THETA = 10000.0
DIL_GROUPS = ((128, 1), (512, 4), (2048, 16))
DIL_BLOCK = 128
PERM_ROWS = 256
DSA_KV_HEADS = 4
IDX_HEADS = 16
IDX_DIM = 64
TOPK_MAX = 256
VMEM_LIMIT = 56 * 1024 * 1024
FFN_VMEM_LIMIT = 62 * 1024 * 1024
NEG_BIG = -1e30
LOG2E = math.log2(math.e)


def _params(*sem):
    return pltpu.CompilerParams(dimension_semantics=sem, vmem_limit_bytes=VMEM_LIMIT)


def _tile(n, pref):
    t = min(n, pref)
    while n % t:
        t //= 2
    return t


def _ffn_kernel(x_ref, g_ref, wg_ref, wu_ref, wd_ref, o_ref, h_ref):
    _ffn_body(x_ref, g_ref, wg_ref, wu_ref, wd_ref, o_ref, h_ref, ())


def _ffn_kernel_with_next(x_ref, g_ref, wg_ref, wu_ref, wd_ref, ng_ref, o_ref, hn_ref, h_ref):
    _ffn_body(x_ref, g_ref, wg_ref, wu_ref, wd_ref, o_ref, h_ref, (ng_ref, hn_ref))


def _ffn_body(x_ref, g_ref, wg_ref, wu_ref, wd_ref, o_ref, h_ref, next_refs):
    j = pl.program_id(1)
    last = pl.num_programs(1) - 1
    tm = x_ref.shape[0]
    halves = [slice(0, tm // 2), slice(tm // 2, tm)]

    def down(h):
        g = jnp.dot(h, wg_ref[...], preferred_element_type=F32)
        u = jnp.dot(h, wu_ref[...], preferred_element_type=F32)
        a = (g * jax.nn.sigmoid(g) * u).astype(BF16)
        return jnp.dot(a, wd_ref[...], preferred_element_type=F32)

    @pl.when(j == 0)
    def _():
        for rows in halves:
            x = x_ref[rows, :]
            ms = jnp.mean(x * x, axis=-1, keepdims=True)
            h = (x * lax.rsqrt(ms + RMS_EPS) * g_ref[...]).astype(BF16)
            h_ref[rows, :] = h
            o_ref[rows, :] = down(h)

    @pl.when(jnp.logical_and(j > 0, j < last))
    def _():
        o_ref[...] += down(h_ref[...])

    @pl.when(jnp.logical_and(j > 0, j == last))
    def _():
        for rows in halves:
            y = x_ref[rows, :] + 0.5 * (o_ref[rows, :] + down(h_ref[rows, :]))
            o_ref[rows, :] = y
            if next_refs:
                ng_ref, hn_ref = next_refs
                ms = jnp.mean(y * y, axis=-1, keepdims=True)
                hn_ref[rows, :] = (y * lax.rsqrt(ms + RMS_EPS) * ng_ref[...]).astype(BF16)


def _ffn(x2, gain, wg, wu, wd, next_gain=None):
    T, D = x2.shape
    F = wg.shape[1]
    tm, tf = _tile(T, 1024), _tile(F, 512)
    assert F // tf >= 2 and tm % 16 == 0
    row_spec = pl.BlockSpec((tm, D), lambda i, j: (i, 0))
    vec_spec = pl.BlockSpec((1, D), lambda i, j: (0, 0))
    in_specs = [row_spec, vec_spec,
                pl.BlockSpec((D, tf), lambda i, j: (0, j)),
                pl.BlockSpec((D, tf), lambda i, j: (0, j)),
                pl.BlockSpec((tf, D), lambda i, j: (j, 0))]
    args = [x2, gain.reshape(1, D), wg, wu, wd]
    out_specs, out_shape = row_spec, jax.ShapeDtypeStruct((T, D), F32)
    kern = _ffn_kernel
    if next_gain is not None:
        in_specs.append(vec_spec)
        args.append(next_gain.reshape(1, D).astype(F32))
        out_specs = [row_spec, row_spec]
        out_shape = [out_shape, jax.ShapeDtypeStruct((T, D), BF16)]
        kern = _ffn_kernel_with_next
    return pl.pallas_call(
        kern,
        grid=(T // tm, F // tf),
        in_specs=in_specs,
        out_specs=out_specs,
        out_shape=out_shape,
        scratch_shapes=[pltpu.VMEM((tm, D), BF16)],
        compiler_params=pltpu.CompilerParams(dimension_semantics=("parallel", "arbitrary"),
                                             vmem_limit_bytes=FFN_VMEM_LIMIT),
        name="ffn",
    )(*args)


def _proj_kernel(x_ref, g_ref, w_ref, eg_ref, ef_ref, o_ref, *scratch, epilogue, dil, prenormed):
    scratch = list(scratch)
    if prenormed:
        h_ref = x_ref
    else:
        h_ref = scratch.pop(0)

        @pl.when(pl.program_id(2) == 0)
        def _():
            x = x_ref[...].astype(F32)
            ms = jnp.mean(x * x, axis=-1, keepdims=True)
            h_ref[...] = (x * lax.rsqrt(ms + RMS_EPS) * g_ref[...]).astype(BF16)

    tm = h_ref.shape[0]
    nb = PERM_ROWS // dil
    if dil > 1:
        hp_ref = scratch.pop(0)

        @pl.when(pl.program_id(2) == 0)
        def _():
            dst = lax.broadcasted_iota(jnp.int32, (PERM_ROWS, PERM_ROWS), 0)
            src = lax.broadcasted_iota(jnp.int32, (PERM_ROWS, PERM_ROWS), 1)
            perm = jnp.where(dst == (src % dil) * nb + src // dil, 1.0, 0.0).astype(BF16)
            for b in range(tm // PERM_ROWS):
                blk = slice(b * PERM_ROWS, (b + 1) * PERM_ROWS)
                hp_ref[blk, :] = jnp.dot(perm, h_ref[blk, :],
                                         preferred_element_type=F32).astype(BF16)

        h_ref = hp_ref
    parts = 2 if tm % (2 * PERM_ROWS) == 0 else 1
    rows_per = tm // parts
    for p in range(parts):
        rows = slice(p * rows_per, (p + 1) * rows_per)
        acc = jnp.dot(h_ref[rows, :], w_ref[...], preferred_element_type=F32)
        for c in range(acc.shape[1] // LANE):
            sl = slice(c * LANE, (c + 1) * LANE)
            y = acc[:, sl]
            if epilogue:
                ms = jnp.mean(y * y, axis=-1, keepdims=True)
                scale = jnp.where(ef_ref[:, sl] > 0.0, lax.rsqrt(ms + RMS_EPS), 1.0)
                y = y * scale * eg_ref[:, sl]
            y = y.astype(o_ref.dtype)
            if dil == 1:
                o_ref[0, rows, sl] = y
            else:
                for b in range(rows_per // PERM_ROWS):
                    a0 = (p * (rows_per // PERM_ROWS) + b) * nb
                    for r in range(dil):
                        o_ref[r, a0:a0 + nb, sl] = y[b * PERM_ROWS + r * nb:b * PERM_ROWS + (r + 1) * nb, :]


def _proj(x, gain, w, eg=None, ef=None, *, dil=1, kblock=0, out_dtype=BF16, tm=1024, tn=512):
    B, S, C = x.shape
    K, N = w.shape
    tm, tn = _tile(S, tm), _tile(N, tn)
    epilogue = eg is not None
    if not epilogue:
        eg = jnp.ones((1, N), F32)
        ef = jnp.zeros((1, N), F32)
    assert dil == 1 or (tm % PERM_ROWS == 0 and PERM_ROWS % (16 * dil) == 0)
    prenormed = gain is None
    if prenormed:
        assert x.dtype == BF16
        gain = jnp.ones((K,), F32)
    out = pl.pallas_call(
        functools.partial(_proj_kernel, epilogue=epilogue, dil=dil, prenormed=prenormed),
        grid=(B, S // tm, N // tn),
        in_specs=[
            pl.BlockSpec((None, tm, K), lambda b, i, j: (b, i, kblock)),
            pl.BlockSpec((1, K), lambda b, i, j: (0, 0)),
            pl.BlockSpec((K, tn), lambda b, i, j: (0, j)),
            pl.BlockSpec((1, tn), lambda b, i, j: (0, j)),
            pl.BlockSpec((1, tn), lambda b, i, j: (0, j)),
        ],
        out_specs=pl.BlockSpec((None, dil, tm // dil, tn), lambda b, i, j: (b, 0, i, j)),
        out_shape=jax.ShapeDtypeStruct((B, dil, S // dil, N), out_dtype),
        scratch_shapes=[pltpu.VMEM((tm, K), BF16)] * ((not prenormed) + (dil > 1)),
        compiler_params=_params("parallel", "parallel", "arbitrary"),
        name="proj",
    )(x, gain.reshape(1, K).astype(F32), w, eg.reshape(1, N), ef.reshape(1, N))
    return out.reshape(B, S, N) if dil == 1 else out


def _out_kernel(x_ref, a_ref, b_ref, wa_ref, wb_ref, o_ref):
    acc = jnp.dot(a_ref[...], wa_ref[...], preferred_element_type=F32)
    acc += jnp.dot(b_ref[...], wb_ref[...], preferred_element_type=F32)
    o_ref[...] = x_ref[...] + acc


def _out_proj(x2, mix2, mo2, wa, wb):
    T, D = x2.shape
    tm, tn = _tile(T, 1024), _tile(D, 1024)
    ka, kb = mix2.shape[1], mo2.shape[1]
    return pl.pallas_call(
        _out_kernel,
        grid=(T // tm, D // tn),
        in_specs=[
            pl.BlockSpec((tm, tn), lambda i, j: (i, j)),
            pl.BlockSpec((tm, ka), lambda i, j: (i, 0)),
            pl.BlockSpec((tm, kb), lambda i, j: (i, 0)),
            pl.BlockSpec((ka, tn), lambda i, j: (0, j)),
            pl.BlockSpec((kb, tn), lambda i, j: (0, j)),
        ],
        out_specs=pl.BlockSpec((tm, tn), lambda i, j: (i, j)),
        out_shape=jax.ShapeDtypeStruct((T, D), F32),
        compiler_params=_params("parallel", "arbitrary"),
        name="out_proj",
    )(x2, mix2, mo2, wa, wb)


def _lanes(x, reps):
    return x if reps == 1 else jnp.concatenate([x] * reps, axis=-1)


def _softmax_probs(scores, m_ref):
    ps, alphas = [], []
    for g, s in enumerate(scores):
        m_old = m_ref[g]
        m_new = jnp.maximum(m_old, jnp.max(s, axis=-1, keepdims=True))
        ps.append(jnp.exp2(s - _lanes(m_new, s.shape[1] // LANE)).astype(BF16))
        alphas.append(jnp.exp2(m_old - m_new))
        m_ref[g] = m_new
    return ps, alphas


def _accumulate(ps, alphas, values, acc_ref):
    for g, (p, alpha, v) in enumerate(zip(ps, alphas, values)):
        v1 = jnp.concatenate([v, jnp.ones_like(v)], axis=-1)
        acc_ref[g] = _lanes(alpha, 2) * acc_ref[g] + jnp.dot(p, v1, preferred_element_type=F32)


def _online_softmax(n_chunks, produce, values, s_ref, m_ref, acc_ref, finish=None):
    G = s_ref.shape[1]

    def consume(j, scores):
        return _softmax_probs(scores, m_ref)

    def accumulate(j, ps, alphas):
        _accumulate(ps, alphas, values(j), acc_ref)

    m_ref[...] = jnp.full_like(m_ref, NEG_BIG)
    acc_ref[...] = jnp.zeros_like(acc_ref)
    for g, s in enumerate(produce(0)):
        s_ref[0, g] = s

    def step(j, slot):
        cur = [s_ref[slot, g] for g in range(G)]
        nxt = produce(j + 1)
        ps, alphas = consume(j, cur)
        for g, s in enumerate(nxt):
            s_ref[1 - slot, g] = s
        accumulate(j, ps, alphas)

    def body(jj, carry):
        step(2 * jj, 0)
        step(2 * jj + 1, 1)
        return carry

    last = n_chunks - 1
    lax.fori_loop(0, last // 2, body, 0)

    @pl.when(last % 2 == 1)
    def _():
        step(last - 1, 0)

    cur = [s_ref[last % 2, g] for g in range(G)]
    if finish is not None:
        cur = [finish(s) for s in cur]
    accumulate(last, *consume(last, cur))


def _flash_kernel(*refs, nqk, decay, tq, G):
    q_refs, k_refs, v_ref = refs[:nqk], refs[nqk:2 * nqk], refs[2 * nqk]
    pos = 2 * nqk + 1
    if decay:
        ck_ref = refs[pos]
        pos += 1
    o_ref, s_ref, m_ref, acc_ref = refs[pos:pos + 4]
    i = pl.program_id(2)

    def head(ref, g, rows=slice(None)):
        if ref.shape[-1] == LANE:
            return ref[rows, :]
        return ref[rows, g * LANE:(g + 1) * LANE]

    qs = []
    for g in range(G):
        parts = [head(r, g) for r in q_refs]
        qs.append(parts[0] if nqk == 1 else jnp.concatenate(parts, axis=-1))

    def chunk(j):
        return pl.ds(pl.multiple_of(j * tq, tq), tq)

    def produce(j):
        rows = chunk(j)
        scores = []
        for g in range(G):
            parts = [head(r, g, rows) for r in k_refs]
            k = parts[0] if nqk == 1 else jnp.concatenate(parts, axis=-1)
            s = lax.dot_general(qs[g], k, (((1,), (1,)), ((), ())), preferred_element_type=F32)
            scores.append(s - ck_ref[g, :, rows] if decay else s)
        return scores

    def values(j):
        return [head(v_ref, g, chunk(j)) for g in range(G)]

    def causal(s):
        row = lax.broadcasted_iota(jnp.int32, (tq, tq), 0)
        col = lax.broadcasted_iota(jnp.int32, (tq, tq), 1)
        return jnp.where(col <= row, s, -jnp.inf)

    _online_softmax(i + 1, produce, values, s_ref, m_ref, acc_ref, finish=causal)
    for g in range(G):
        acc = acc_ref[g]
        o_ref[:, g * LANE:(g + 1) * LANE] = (acc[:, :LANE] / acc[:, LANE:]).astype(o_ref.dtype)


def _flash(q_parts, k_parts, v_part, B, S, decay_t=None, tq=512, G=4):
    tq = _tile(S, tq)
    nqk = len(q_parts)
    W = G * LANE
    args, specs = [], []
    for arr, off, _ in q_parts:
        args.append(arr)
        specs.append(pl.BlockSpec((None, tq, W), lambda b, h, i, off=off: (b, i, off // W + h)))
    for arr, off, per_head in k_parts + [v_part]:
        args.append(arr)
        if per_head:
            specs.append(pl.BlockSpec((None, S, W), lambda b, h, i, off=off: (b, 0, off // W + h)))
        else:
            specs.append(pl.BlockSpec((None, S, LANE), lambda b, h, i, off=off: (b, 0, off // LANE)))
    if decay_t is not None:
        args.append(decay_t)
        specs.append(pl.BlockSpec((None, G, 1, S), lambda b, h, i: (b, h, 0, 0)))
    return pl.pallas_call(
        functools.partial(_flash_kernel, nqk=nqk, decay=decay_t is not None, tq=tq, G=G),
        grid=(B, N_HEADS // G, S // tq),
        in_specs=specs,
        out_specs=pl.BlockSpec((None, tq, W), lambda b, h, i: (b, i, h)),
        out_shape=jax.ShapeDtypeStruct((B, S, MIX_WIDTH), BF16),
        scratch_shapes=[pltpu.VMEM((2, G, tq, tq), F32), pltpu.VMEM((G, tq, LANE), F32),
                        pltpu.VMEM((G, tq, 2 * HEAD_DIM), F32)],
        compiler_params=_params("parallel", "parallel", "arbitrary"),
        name="flash",
    )(*args)


def _gate_kernel(fg_ref, bf_ref, cumt_ref, carry_ref, *, ts):
    @pl.when(pl.program_id(1) == 0)
    def _():
        carry_ref[...] = jnp.zeros_like(carry_ref)

    z = fg_ref[...] + bf_ref[...]
    lf = jnp.minimum(z, 0.0) - jnp.log1p(jnp.exp(-jnp.abs(z)))
    hi = lf.astype(BF16)
    r1 = lf - hi.astype(F32)
    mid = r1.astype(BF16)
    lo = (r1 - mid.astype(F32)).astype(BF16)
    row = lax.broadcasted_iota(jnp.int32, (ts, ts), 0)
    col = lax.broadcasted_iota(jnp.int32, (ts, ts), 1)
    tri = jnp.where(col <= row, 1.0, 0.0).astype(BF16)
    cum = (jnp.dot(tri, hi, preferred_element_type=F32)
           + jnp.dot(tri, mid, preferred_element_type=F32)
           + jnp.dot(tri, lo, preferred_element_type=F32)) + carry_ref[...]
    carry_ref[...] = cum[ts - 1:ts, :]
    cumt_ref[...] = (cum * LOG2E).T


def _fox_gates(fg, b_f):
    B, S, _ = fg.shape
    ts = _tile(S, 512)
    bf = jnp.zeros((1, LANE), F32).at[0, :N_HEADS].set(b_f.astype(F32))
    return pl.pallas_call(
        functools.partial(_gate_kernel, ts=ts),
        grid=(B, S // ts),
        in_specs=[pl.BlockSpec((None, ts, LANE), lambda b, i: (b, i, 0)),
                  pl.BlockSpec((1, LANE), lambda b, i: (0, 0))],
        out_specs=pl.BlockSpec((None, LANE, ts), lambda b, i: (b, 0, i)),
        out_shape=jax.ShapeDtypeStruct((B, LANE, S), F32),
        scratch_shapes=[pltpu.VMEM((1, LANE), F32)],
        compiler_params=_params("parallel", "arbitrary"),
        name="fox_gates",
    )(fg, bf)


def _rope_kernel(x_ref, cos_ref, sin_ref, g_ref, o_ref):
    cos, sin, g = cos_ref[...], sin_ref[...], g_ref[...]
    for c in range(x_ref.shape[1] // LANE):
        sl = slice(c * LANE, (c + 1) * LANE)
        x = x_ref[:, sl]
        ms = jnp.sum(x * x, axis=-1, keepdims=True) * (1.0 / ROPE_DIM)
        y = x * lax.rsqrt(ms + RMS_EPS) * g
        partner = pltpu.roll(y, ROPE_DIM // 2, 1) + pltpu.roll(y, LANE - ROPE_DIM // 2, 1)
        o_ref[:, sl] = (y * cos + partner * sin).astype(o_ref.dtype)


def _rope(x, cos, sin, gain):
    B, S, N = x.shape
    ts = _tile(S, 512)
    g = jnp.zeros((1, LANE), F32).at[0, :ROPE_DIM].set(gain.astype(F32))
    return pl.pallas_call(
        _rope_kernel,
        grid=(B, S // ts),
        in_specs=[pl.BlockSpec((None, ts, N), lambda b, i: (b, i, 0)),
                  pl.BlockSpec((None, ts, LANE), lambda b, i: (b, i, 0)),
                  pl.BlockSpec((None, ts, LANE), lambda b, i: (b, i, 0)),
                  pl.BlockSpec((1, LANE), lambda b, i: (0, 0))],
        out_specs=pl.BlockSpec((None, ts, N), lambda b, i: (b, i, 0)),
        out_shape=jax.ShapeDtypeStruct((B, S, N), BF16),
        compiler_params=_params("parallel", "parallel"),
        name="rope",
    )(x, cos, sin, g)


def _mem_kernel(q_ref, kv_ref, o_ref):
    for h in range(MEM_HEADS):
        q = q_ref[:, h * LANE:(h + 1) * LANE]
        k = kv_ref[:, h * LANE:(h + 1) * LANE]
        v = kv_ref[:, MEM_WIDTH + h * LANE:MEM_WIDTH + (h + 1) * LANE]
        s = lax.dot_general(q, k, (((1,), (1,)), ((), ())), preferred_element_type=F32)
        p = jnp.exp(s - jnp.max(s, axis=-1, keepdims=True))
        o = jnp.dot(p.astype(BF16), v, preferred_element_type=F32)
        o_ref[:, h * LANE:(h + 1) * LANE] = (o / jnp.sum(p, axis=-1, keepdims=True)).astype(o_ref.dtype)


def _mem_attention(qarr, qblock, mem_kv):
    B, S, _ = qarr.shape
    n_mem = mem_kv.shape[1]
    tq = _tile(S, 1024)
    return pl.pallas_call(
        _mem_kernel,
        grid=(B, S // tq),
        in_specs=[pl.BlockSpec((None, tq, MEM_WIDTH), lambda b, i: (b, i, qblock)),
                  pl.BlockSpec((None, n_mem, 2 * MEM_WIDTH), lambda b, i: (b, 0, 0))],
        out_specs=pl.BlockSpec((None, tq, MEM_WIDTH), lambda b, i: (b, i, 0)),
        out_shape=jax.ShapeDtypeStruct((B, S, MEM_WIDTH), BF16),
        compiler_params=_params("parallel", "parallel"),
        name="mem_attn",
    )(qarr, mem_kv)


def _dil_kernel(q_ref, kp_ref, kc_ref, vp_ref, vc_ref, bias_ref, o_ref, lse_ref):
    n = DIL_BLOCK
    nsub = q_ref.shape[0] // n
    row = lax.broadcasted_iota(jnp.int32, (n, 2 * n), 0)
    col = lax.broadcasted_iota(jnp.int32, (n, 2 * n), 1)
    band = jnp.logical_and(col >= row, col <= row + n)
    first = jnp.logical_and(band, jnp.logical_or(col >= n, pl.program_id(1) > 0))
    lane = lax.broadcasted_iota(jnp.int32, (n, LANE), 1)
    dn = (((1,), (1,)), ((), ()))
    heads = [slice(h * LANE, (h + 1) * LANE) for h in range(N_HEADS)]

    def window(prev_ref, cur_ref, u, sl):
        if u == 0:
            return jnp.concatenate([prev_ref[:, sl], cur_ref[:n, sl]], axis=0)
        return cur_ref[(u - 1) * n:(u + 1) * n, sl]

    for u in range(nsub):
        rows = slice(u * n, (u + 1) * n)
        valid = first if u == 0 else band
        scores = []
        for h, sl in enumerate(heads):
            s = lax.dot_general(q_ref[rows, sl], window(kp_ref, kc_ref, u, sl), dn,
                                preferred_element_type=F32) + bias_ref[h]
            scores.append(jnp.where(valid, s, -jnp.inf))
        ms = [jnp.max(s, axis=-1, keepdims=True) for s in scores]
        ps = [jnp.exp2(s - m).astype(BF16) for s, m in zip(scores, ms)]
        lse_all = jnp.zeros((n, LANE), F32)
        for h, sl in enumerate(heads):
            v2 = window(vp_ref, vc_ref, u, sl)
            pv = jnp.dot(ps[h], jnp.concatenate([v2, jnp.ones_like(v2)], axis=-1),
                         preferred_element_type=F32)
            den = pv[:, LANE:]
            o_ref[h, rows, :] = pv[:, :LANE] / den
            lse_all = jnp.where(lane == h, ms[h] + jnp.log2(den), lse_all)
        lse_ref[rows, :] = lse_all


def _dil_attention(pg, bias):
    N, Ls, _ = pg.shape
    n = DIL_BLOCK
    nsub = 2 if Ls % (2 * n) == 0 else 1
    tr = nsub * n

    def cur(part):
        return pl.BlockSpec((None, tr, MIX_WIDTH), lambda s, i: (s, i, part))

    def prev(part):
        return pl.BlockSpec((None, n, MIX_WIDTH), lambda s, i: (s, jnp.maximum(i * nsub - 1, 0), part))

    return pl.pallas_call(
        _dil_kernel,
        grid=(N, Ls // tr),
        in_specs=[cur(0), prev(1), cur(1), prev(2), cur(2),
                  pl.BlockSpec((N_HEADS, n, 2 * n), lambda s, i: (0, 0, 0))],
        out_specs=[pl.BlockSpec((None, N_HEADS, tr, LANE), lambda s, i: (s, 0, i, 0)),
                   pl.BlockSpec((None, tr, LANE), lambda s, i: (s, i, 0))],
        out_shape=[jax.ShapeDtypeStruct((N, N_HEADS, Ls, LANE), F32),
                   jax.ShapeDtypeStruct((N, Ls, LANE), F32)],
        compiler_params=_params("parallel", "arbitrary"),
        name="dil_attn",
    )(pg, pg, pg, pg, pg, bias)


def _dil_combine_kernel(o1_ref, o2_ref, o3_ref, l1_ref, l2_ref, l3_ref, out_ref, stage_ref, *, d2, d3):
    n = l3_ref.shape[1]
    q = d3 // d2
    for r in range(d3):
        rows = [(pl.ds(r, n, stride=d3),), (r % d2, pl.ds(r // d2, n, stride=q)), (r, slice(None))]
        ls = [l_ref[idx + (slice(None),)]
              for l_ref, idx in zip((l1_ref, l2_ref, l3_ref), rows)]
        m = functools.reduce(jnp.maximum, ls)
        es = [jnp.exp2(l - m) for l in ls]
        inv = 1.0 / functools.reduce(jnp.add, es)
        ws = [e * inv for e in es]
        for h in range(N_HEADS):
            o = (ws[0][:, h:h + 1] * o1_ref[h, pl.ds(r, n, stride=d3), :]
                 + ws[1][:, h:h + 1] * o2_ref[r % d2, h, pl.ds(r // d2, n, stride=q), :]
                 + ws[2][:, h:h + 1] * o3_ref[r, h, :, :])
            stage_ref[h, pl.ds(r, n, stride=d3), :] = o
    for h in range(N_HEADS):
        out_ref[:, h * LANE:(h + 1) * LANE] = stage_ref[h].astype(out_ref.dtype)


def _dil_combine(outs, lses, B, S, tm=256):
    (_, d1), (_, d2), (_, d3) = DIL_GROUPS
    assert d1 == 1 and d3 % d2 == 0
    tm = _tile(S, tm)
    H = N_HEADS
    o_specs = [pl.BlockSpec((None, H, tm, LANE), lambda b, i: (b, 0, i, 0)),
               pl.BlockSpec((None, d2, H, tm // d2, LANE), lambda b, i: (b, 0, 0, i, 0)),
               pl.BlockSpec((None, d3, H, tm // d3, LANE), lambda b, i: (b, 0, 0, i, 0))]
    l_specs = [pl.BlockSpec((None, tm, LANE), lambda b, i: (b, i, 0)),
               pl.BlockSpec((None, d2, tm // d2, LANE), lambda b, i: (b, 0, i, 0)),
               pl.BlockSpec((None, d3, tm // d3, LANE), lambda b, i: (b, 0, i, 0))]
    return pl.pallas_call(
        functools.partial(_dil_combine_kernel, d2=d2, d3=d3),
        grid=(B, S // tm),
        in_specs=o_specs + l_specs,
        out_specs=pl.BlockSpec((None, tm, MIX_WIDTH), lambda b, i: (b, i, 0)),
        out_shape=jax.ShapeDtypeStruct((B, S, MIX_WIDTH), BF16),
        scratch_shapes=[pltpu.VMEM((H, tm, LANE), F32)],
        compiler_params=_params("parallel", "parallel"),
        name="dil_combine",
    )(outs[0].reshape(B, H, S, LANE), outs[1].reshape(B, d2, H, S // d2, LANE),
      outs[2].reshape(B, d3, H, S // d3, LANE),
      lses[0].reshape(B, S, LANE), lses[1].reshape(B, d2, S // d2, LANE),
      lses[2].reshape(B, d3, S // d3, LANE))


def _dsa_topk_kernel(qi_ref, wi_ref, ki_ref, mask_ref, key_ref, jsel_ref, *, tq, tk, S, n_sel):
    i = pl.program_id(1)
    q0 = i * tq
    n_chunks = (q0 + tq + tk - 1) // tk
    dn = (((1,), (1,)), ((), ()))
    krow = lax.broadcasted_iota(jnp.int32, (tk, tq), 0)
    qpos = q0 + lax.broadcasted_iota(jnp.int32, (tk, tq), 1)

    wi_t = (wi_ref[...] * (IDX_HEADS ** -0.5 * IDX_DIM ** -0.5)).T

    def idx_body(c, carry):
        off = pl.multiple_of(c * tk, tk)
        kic = ki_ref[pl.ds(off, tk), :]
        k_lo = jnp.where(lax.broadcasted_iota(jnp.int32, kic.shape, 1) < IDX_DIM, kic, 0.0)
        k_sides = (k_lo.astype(BF16), pltpu.roll(k_lo, IDX_DIM, 1).astype(BF16))
        acc = jnp.zeros((tk, tq), F32)
        for h in range(IDX_HEADS):
            d = lax.dot_general(k_sides[h % 2], qi_ref[:, (h // 2) * LANE:(h // 2 + 1) * LANE], dn,
                                preferred_element_type=F32)
            acc = acc + wi_t[IDX_DIM + h:IDX_DIM + h + 1, :] * jnp.maximum(d, 0.0)
        score = jnp.where(off + krow <= qpos, acc, -jnp.inf)
        bits = pltpu.bitcast(score, jnp.int32)
        key_ref[pl.ds(off, tk), :] = bits ^ ((bits >> 31) & jnp.int32(0x7FFFFFFF))
        return carry

    lax.fori_loop(0, n_chunks, idx_body, 0)

    def count(pred):
        sub = 64
        def body(c, acc):
            off = pl.multiple_of(c * tk, tk)
            ones = jnp.where(pred(key_ref[pl.ds(off, tk), :], off), 1.0, 0.0)
            return acc + jnp.sum(ones.reshape(tk // sub, sub, tq), axis=0)

        acc = lax.fori_loop(0, n_chunks, body, jnp.zeros((sub, tq), F32))
        return jnp.sum(acc, axis=0, keepdims=True)

    k_sel = float(n_sel)
    c0 = count(lambda k, off: k >= 0)
    lo0 = jnp.where(c0 >= k_sel, jnp.int32(0), jnp.int32(-2 ** 31))
    n0 = jnp.where(c0 >= k_sel, c0, (n_chunks * tk).astype(F32))

    def bis_cond(state):
        it, _, n_lo = state
        return jnp.logical_and(it < 31, jnp.max(jnp.abs(n_lo - k_sel)) > 0.0)

    def bis_body(state):
        it, lo, n_lo = state
        cand = lo + (jnp.int32(1) << (30 - it))
        cnt = count(lambda k, off: k >= cand)
        take = cnt >= k_sel
        return it + 1, jnp.where(take, cand, lo), jnp.where(take, cnt, n_lo)

    _, thr, n_ge = lax.while_loop(bis_cond, bis_body, (jnp.int32(0), lo0, n0))
    n_bits = max(1, (S - 1).bit_length())
    jsel_ref[...] = jnp.full_like(jsel_ref, S)

    @pl.when(jnp.max(n_ge) > k_sel)
    def _():
        need = k_sel - count(lambda k, off: k > thr)

        def tie_body(it, jlo):
            cand = jlo + (jnp.int32(1) << (n_bits - 1 - it))
            below = count(lambda k, off: jnp.logical_and(k == thr, off + krow < cand))
            return jnp.where(below < need, cand, jlo)

        jsel = lax.fori_loop(0, n_bits, tie_body, jnp.zeros((1, tq), jnp.int32))
        jsel_ref[...] = jnp.broadcast_to(jsel, jsel_ref.shape)

    jsel = jsel_ref[:1, :]
    mask_ref[...] = jnp.full_like(mask_ref, -jnp.inf)

    def mask_body(c, carry):
        off = pl.multiple_of(c * tk, tk)
        k = key_ref[pl.ds(off, tk), :]
        kpos = off + krow
        tie = jnp.logical_and(k == thr, kpos <= jsel)
        sel = jnp.logical_and(jnp.logical_or(k > thr, tie), kpos <= qpos)
        mask_ref[:, pl.ds(off, tk)] = jnp.where(sel, 0.0, -jnp.inf).T.astype(mask_ref.dtype)
        return carry

    lax.fori_loop(0, n_chunks, mask_body, 0)


def _dsa_select(pm, aux, B, S, n_sel, tq=256, tk=512):
    tq, tk = _tile(S, tq), _tile(S, tk)
    qiw = IDX_HEADS * IDX_DIM
    return pl.pallas_call(
        functools.partial(_dsa_topk_kernel, tq=tq, tk=tk, S=S, n_sel=n_sel),
        grid=(B, S // tq),
        in_specs=[pl.BlockSpec((None, tq, qiw), lambda b, i: (b, i, MIX_WIDTH // qiw)),
                  pl.BlockSpec((None, tq, LANE), lambda b, i: (b, i, 0)),
                  pl.BlockSpec((None, S, LANE), lambda b, i: (b, 0, 0))],
        out_specs=pl.BlockSpec((None, tq, S), lambda b, i: (b, i, 0)),
        out_shape=jax.ShapeDtypeStruct((B, S, S), BF16),
        scratch_shapes=[pltpu.VMEM((S, tq), jnp.int32), pltpu.VMEM((8, tq), jnp.int32)],
        compiler_params=_params("parallel", "arbitrary"),
        name="dsa_select",
    )(pm, aux, aux)


def _dsa_attn_kernel(q_ref, k_ref, v_ref, mask_ref, gt_ref, o_ref, s_ref, m_ref, acc_ref, *, tq, tk, S):
    i = pl.program_id(2)
    q0 = i * tq
    n_chunks = (q0 + tq + tk - 1) // tk
    R = q_ref.shape[1] // LANE
    nblk = tq // LANE
    dn = (((1,), (1,)), ((), ()))
    qs = [q_ref[:, r * LANE:(r + 1) * LANE] for r in range(R)]

    def produce(c):
        off = pl.multiple_of(c * tk, tk)
        kc = k_ref[pl.ds(off, tk), :]
        mask = mask_ref[:, pl.ds(off, tk)].astype(F32)
        st = S - q0 + off
        scores = []
        for r in range(R):
            bias = jnp.concatenate(
                [gt_ref[r, :, pl.ds(pl.multiple_of(st + (nblk - 1 - a) * LANE, LANE), tk)]
                 for a in range(nblk)], axis=0)
            scores.append(lax.dot_general(qs[r], kc, dn, preferred_element_type=F32) + bias + mask)
        return scores

    def values(c):
        return [v_ref[pl.ds(pl.multiple_of(c * tk, tk), tk), :]] * R

    _online_softmax(n_chunks, produce, values, s_ref, m_ref, acc_ref)
    for r in range(R):
        acc = acc_ref[r]
        o_ref[:, r * LANE:(r + 1) * LANE] = (acc[:, :LANE] / acc[:, LANE:]).astype(o_ref.dtype)


def _dsa_attention(pm, mask, gt, B, S, tq=256, tk=512):
    tq, tk = _tile(S, tq), _tile(S, tk)
    R = N_HEADS // DSA_KV_HEADS
    kblk = (MIX_WIDTH + IDX_HEADS * IDX_DIM) // LANE
    return pl.pallas_call(
        functools.partial(_dsa_attn_kernel, tq=tq, tk=tk, S=S),
        grid=(DSA_KV_HEADS, B, S // tq),
        in_specs=[pl.BlockSpec((None, tq, R * LANE), lambda g, b, i: (b, i, g)),
                  pl.BlockSpec((None, S, LANE), lambda g, b, i: (b, 0, kblk + g)),
                  pl.BlockSpec((None, S, LANE), lambda g, b, i: (b, 0, kblk + DSA_KV_HEADS + g)),
                  pl.BlockSpec((None, tq, S), lambda g, b, i: (b, i, 0)),
                  pl.BlockSpec((R, LANE, gt.shape[2]), lambda g, b, i: (g, 0, 0))],
        out_specs=pl.BlockSpec((None, tq, R * LANE), lambda g, b, i: (b, i, g)),
        out_shape=jax.ShapeDtypeStruct((B, S, MIX_WIDTH), BF16),
        scratch_shapes=[pltpu.VMEM((2, R, tq, tk), F32), pltpu.VMEM((R, tq, LANE), F32),
                        pltpu.VMEM((R, tq, 2 * HEAD_DIM), F32)],
        compiler_params=_params("parallel", "parallel", "arbitrary"),
        name="dsa_attn",
    )(pm, pm, pm, mask, gt)


def _t5_bucket(dist):
    n = jnp.maximum(dist, 0)
    exact = REL_BUCKETS // 2
    nf = jnp.maximum(n, 1).astype(F32)
    large = exact + (jnp.log(nf / exact) / math.log(REL_MAX_DIST / exact)
                     * (REL_BUCKETS - exact)).astype(jnp.int32)
    large = jnp.minimum(large, REL_BUCKETS - 1)
    return jnp.where(n < exact, n, large)


def _t5_bias(t5_table, dist):
    onehot = jax.nn.one_hot(_t5_bucket(dist), REL_BUCKETS, dtype=F32)
    table = jnp.einsum('...b,bh->h...', onehot, t5_table.astype(F32), precision=lax.Precision.HIGHEST)
    return table * LOG2E


def _dil_bias(t5_table, dil):
    n = DIL_BLOCK
    rel = jnp.arange(n)[:, None] + n - jnp.arange(2 * n)[None, :]
    return _t5_bias(t5_table, rel * dil)


def _dsa_bias_table(t5_table, S, tq, tk):
    m = jnp.arange(S + tq + tk)[None, :]
    i = jnp.arange(LANE)[:, None]
    return _t5_bias(t5_table, S + tq - LANE - m + i)


def _headnorm_cols(spec):
    eg, ef = [], []
    for gain, count, scale in spec:
        if gain is None:
            eg.append(jnp.ones((count * LANE,), F32))
            ef.append(jnp.zeros((count * LANE,), F32))
        else:
            eg.append(jnp.tile(gain.astype(F32) * scale, count))
            ef.append(jnp.ones((count * LANE,), F32))
    return jnp.concatenate(eg), jnp.concatenate(ef)


def _pad_cols(w, width):
    return jnp.pad(w, ((0, 0), (0, width - w.shape[1])))


def _fox_layer(x, norm_g, w_in, b_f, qk_g, mem_g):
    B, S, _ = x.shape
    W = MIX_WIDTH
    w_main = jnp.concatenate([w_in[:, :3 * W], w_in[:, 3 * W + N_HEADS:]], axis=1).astype(BF16)
    w_gate = _pad_cols(w_in[:, 3 * W:3 * W + N_HEADS], LANE).astype(BF16)
    eg, ef = _headnorm_cols([(qk_g[0], N_HEADS, HEAD_DIM ** -0.5 * LOG2E), (qk_g[1], N_HEADS, 1.0),
                             (None, N_HEADS, 1.0), (mem_g, MEM_HEADS, HEAD_DIM ** -0.5)])
    pm = _proj(x, norm_g, w_main, eg, ef)
    fg = _proj(x, norm_g, w_gate, out_dtype=F32)
    cum_t = _fox_gates(fg, b_f)
    mix = _flash([(pm, 0, True)], [(pm, W, True)], (pm, 2 * W, True), B, S,
                 decay_t=cum_t.reshape(B, LANE, 1, S))
    return mix, pm, 3 * W // MEM_WIDTH


def _mla_layer(x, positions, norm_g, w_in, q_norm, w_uq, kv_norm, w_ukv, nope_g, rope_g, mem_g):
    B, S, _ = x.shape
    scale = (NOPE_DIM + ROPE_DIM) ** -0.5 * LOG2E
    lat = Q_LORA + KV_LORA
    w_lat = jnp.concatenate([w_in[:, :lat], _pad_cols(w_in[:, lat:lat + ROPE_DIM], LANE)],
                            axis=1).astype(BF16)
    w_mem = w_in[:, lat + ROPE_DIM:].astype(BF16)
    pl_ = _proj(x, norm_g, w_lat, out_dtype=F32, tn=lat + LANE)
    eg, ef = _headnorm_cols([(mem_g, MEM_HEADS, HEAD_DIM ** -0.5)])
    pmem = _proj(x, norm_g, w_mem, eg, ef)

    uq = w_uq.reshape(Q_LORA, N_HEADS, NOPE_DIM + ROPE_DIM)
    w_qn = uq[:, :, :NOPE_DIM].reshape(Q_LORA, MIX_WIDTH).astype(BF16)
    w_qr = jnp.pad(uq[:, :, NOPE_DIM:], ((0, 0), (0, 0), (0, LANE - ROPE_DIM))
                   ).reshape(Q_LORA, N_HEADS * LANE).astype(BF16)
    ukv = w_ukv.reshape(KV_LORA, N_HEADS, NOPE_DIM + HEAD_DIM)
    w_kv = jnp.concatenate([ukv[:, :, :NOPE_DIM].reshape(KV_LORA, MIX_WIDTH),
                            ukv[:, :, NOPE_DIM:].reshape(KV_LORA, MIX_WIDTH)], axis=1).astype(BF16)
    eg, ef = _headnorm_cols([(nope_g[0], N_HEADS, scale)])
    qn = _proj(pl_, q_norm, w_qn, eg, ef, kblock=0, tn=2048)
    qr_raw = _proj(pl_, q_norm, w_qr, kblock=0, out_dtype=F32, tn=1024)
    eg, ef = _headnorm_cols([(nope_g[1], N_HEADS, 1.0), (None, N_HEADS, 1.0)])
    kv = _proj(pl_, kv_norm, w_kv, eg, ef, kblock=1, tn=2048)

    half = ROPE_DIM // 2
    inv = ROPE_THETA ** (-jnp.arange(half, dtype=F32) / half)
    ang = positions.astype(F32)[..., None] * inv
    cos, sin = jnp.cos(ang), jnp.sin(ang)
    zero = jnp.zeros((B, S, LANE - ROPE_DIM), F32)
    cos_t = jnp.concatenate([cos, cos, zero], axis=-1)
    sin_t = jnp.concatenate([-sin, sin, zero], axis=-1)
    qr = _rope(qr_raw, cos_t, sin_t, rope_g[0] * scale)
    kr = _rope(pl_[:, :, lat:], cos_t, sin_t, rope_g[1])
    mix = _flash([(qn, 0, True), (qr, 0, True)], [(kv, 0, True), (kr, 0, False)],
                 (kv, MIX_WIDTH, True), B, S)
    return mix, pmem, 0


def _dil_layer(x, norm_g, w_in, qk_g, t5_table, mem_g):
    B, S, _ = x.shape
    W = MIX_WIDTH
    outs, lses = [], []
    for gi, (win, dil) in enumerate(DIL_GROUPS):
        assert win // dil == DIL_BLOCK and (S // dil) % DIL_BLOCK == 0
        w_g = w_in[:, gi * 3 * W:(gi + 1) * 3 * W].astype(BF16)
        eg, ef = _headnorm_cols([(qk_g[gi, 0], N_HEADS, HEAD_DIM ** -0.5 * LOG2E),
                                 (qk_g[gi, 1], N_HEADS, 1.0), (None, N_HEADS, 1.0)])
        pg = _proj(x, norm_g, w_g, eg, ef, dil=dil, tn=1024)
        o, lse = _dil_attention(pg.reshape(B * dil, S // dil, 3 * W), _dil_bias(t5_table, dil))
        outs.append(o)
        lses.append(lse)
    mix = _dil_combine(outs, lses, B, S)
    eg, ef = _headnorm_cols([(mem_g, MEM_HEADS, HEAD_DIM ** -0.5)])
    pmem = _proj(x, norm_g, w_in[:, len(DIL_GROUPS) * 3 * W:].astype(BF16), eg, ef)
    return mix, pmem, 0


def _dsa_layer(x, norm_g, w_in, qk_g, t5_table, mem_g):
    B, S, _ = x.shape
    W = MIX_WIDTH
    kvw = DSA_KV_HEADS * HEAD_DIM
    o_k, o_v, o_qi = W, W + kvw, W + 2 * kvw
    o_ki = o_qi + IDX_HEADS * IDX_DIM
    o_wi = o_ki + IDX_DIM
    o_mem = o_wi + IDX_HEADS
    qiw = IDX_HEADS * IDX_DIM
    w_main = jnp.concatenate([w_in[:, :W], w_in[:, o_qi:o_ki], w_in[:, o_k:o_qi], w_in[:, o_mem:]],
                             axis=1).astype(BF16)
    w_aux = _pad_cols(w_in[:, o_ki:o_mem], LANE).astype(BF16)
    eg, ef = _headnorm_cols([(qk_g[0], N_HEADS, HEAD_DIM ** -0.5 * LOG2E), (None, qiw // LANE, 1.0),
                             (qk_g[1], DSA_KV_HEADS, 1.0), (None, DSA_KV_HEADS, 1.0),
                             (mem_g, MEM_HEADS, HEAD_DIM ** -0.5)])
    pm = _proj(x, norm_g, w_main, eg, ef)
    aux = _proj(x, norm_g, w_aux, out_dtype=F32)
    mask = _dsa_select(pm, aux, B, S, min(TOPK_MAX, S // 4), tq=512)
    tq, tk = _tile(S, 512), _tile(S, 512)
    mix = _dsa_attention(pm, mask, _dsa_bias_table(t5_table, S, tq, tk), B, S, tq, tk)
    return mix, pm, (W + qiw + 2 * kvw) // MEM_WIDTH


def kernel(x, mem, positions, t5_table, ffn_norm, ffn_w_gate, ffn_w_up, ffn_w_down, attn_norm,
           mem_norm, mem_w_kv, mem_qk_g, w_out, a_w_in, a_b_f, a_qk_g, b_w_in, b_q_norm, b_w_uq,
           b_kv_norm, b_w_ukv, b_nope_g, b_rope_g, c_w_in, c_qk_g, d_w_in, d_qk_g):
    B, S, D = x.shape
    depth = ffn_norm.shape[0]
    n_mixers = 4

    def ffn(xc, i, k, next_gain=None):
        return _ffn(xc.reshape(B * S, D), ffn_norm[i, k], ffn_w_gate[i, k].astype(BF16),
                    ffn_w_up[i, k].astype(BF16), ffn_w_down[i, k].astype(BF16), next_gain)

    for i in range(depth):
        m, j = i % n_mixers, i // n_mixers
        x, h = ffn(x, i, 0, attn_norm[i])
        x, h = x.reshape(B, S, D), h.reshape(B, S, D)
        mem_g = mem_qk_g[i]
        if m == 0:
            mix, qarr, qblock = _fox_layer(h, None, a_w_in[j], a_b_f[j], a_qk_g[j], mem_g[0])
        elif m == 1:
            mix, qarr, qblock = _mla_layer(h, positions, None, b_w_in[j], b_q_norm[j],
                                           b_w_uq[j], b_kv_norm[j], b_w_ukv[j], b_nope_g[j],
                                           b_rope_g[j], mem_g[0])
        elif m == 2:
            mix, qarr, qblock = _dil_layer(h, None, c_w_in[j], c_qk_g[j], t5_table, mem_g[0])
        else:
            mix, qarr, qblock = _dsa_layer(h, None, d_w_in[j], d_qk_g[j], t5_table, mem_g[0])
        eg, ef = _headnorm_cols([(mem_g[1], MEM_HEADS, 1.0), (None, MEM_HEADS, 1.0)])
        mem_kv = _proj(mem, mem_norm[i], mem_w_kv[i].astype(BF16), eg, ef)
        mo = _mem_attention(qarr, qblock, mem_kv)
        wo = w_out[i].astype(BF16)
        x = _out_proj(x.reshape(B * S, D), mix.reshape(B * S, MIX_WIDTH),
                      mo.reshape(B * S, MEM_WIDTH), wo[:MIX_WIDTH], wo[MIX_WIDTH:]).reshape(B, S, D)
        x = ffn(x, i, 1).reshape(B, S, D)
    return x
```

```python
import functools
import math

import jax
import jax.numpy as jnp
from jax import lax
from jax.experimental import pallas as pl
from jax.experimental.pallas import tpu as pltpu

F32 = jnp.float32
BF16 = jnp.bfloat16

LANE = 128
D_MODEL = 2048
N_HEADS = 16
HEAD_DIM = 128
MIX_WIDTH = N_HEADS * HEAD_DIM
MEM_HEADS = 4
MEM_WIDTH = MEM_HEADS * HEAD_DIM
D_FF = 5632
RMS_EPS = 1e-6
REL_BUCKETS = 32
REL_MAX_DIST = 2048
Q_LORA = 512
KV_LORA = 512
NOPE_DIM = 128
ROPE_DIM = 64
ROPE_THETA = 10000.0
DIL_GROUPS = ((128, 1), (512, 4), (2048, 16))
DIL_BLOCK = 128
PERM_ROWS = 256
DSA_KV_HEADS = 4
IDX_HEADS = 16
IDX_DIM = 64
TOPK_MAX = 256
VMEM_LIMIT = 56 * 1024 * 1024
FFN_VMEM_LIMIT = 62 * 1024 * 1024
NEG_BIG = -1e30
LOG2E = math.log2(math.e)


def _params(*sem):
    return pltpu.CompilerParams(dimension_semantics=sem, vmem_limit_bytes=VMEM_LIMIT)


def _tile(n, pref):
    t = min(n, pref)
    while n % t:
        t //= 2
    return t


def _ffn_kernel(x_ref, g_ref, wg_ref, wu_ref, wd_ref, o_ref, h_ref):
    _ffn_body(x_ref, g_ref, wg_ref, wu_ref, wd_ref, o_ref, h_ref, ())


def _ffn_kernel_with_next(x_ref, g_ref, wg_ref, wu_ref, wd_ref, ng_ref, o_ref, hn_ref, h_ref):
    _ffn_body(x_ref, g_ref, wg_ref, wu_ref, wd_ref, o_ref, h_ref, (ng_ref, hn_ref))


def _ffn_body(x_ref, g_ref, wg_ref, wu_ref, wd_ref, o_ref, h_ref, next_refs):
    j = pl.program_id(1)
    last = pl.num_programs(1) - 1
    tm = x_ref.shape[0]
    halves = [slice(0, tm // 2), slice(tm // 2, tm)]

    def down(h):
        g = jnp.dot(h, wg_ref[...], preferred_element_type=F32)
        u = jnp.dot(h, wu_ref[...], preferred_element_type=F32)
        a = (g * jax.nn.sigmoid(g) * u).astype(BF16)
        return jnp.dot(a, wd_ref[...], preferred_element_type=F32)

    @pl.when(j == 0)
    def _():
        for rows in halves:
            x = x_ref[rows, :]
            ms = jnp.mean(x * x, axis=-1, keepdims=True)
            h = (x * lax.rsqrt(ms + RMS_EPS) * g_ref[...]).astype(BF16)
            h_ref[rows, :] = h
            o_ref[rows, :] = down(h)

    @pl.when(jnp.logical_and(j > 0, j < last))
    def _():
        o_ref[...] += down(h_ref[...])

    @pl.when(jnp.logical_and(j > 0, j == last))
    def _():
        for rows in halves:
            y = x_ref[rows, :] + 0.5 * (o_ref[rows, :] + down(h_ref[rows, :]))
            o_ref[rows, :] = y
            if next_refs:
                ng_ref, hn_ref = next_refs
                ms = jnp.mean(y * y, axis=-1, keepdims=True)
                hn_ref[rows, :] = (y * lax.rsqrt(ms + RMS_EPS) * ng_ref[...]).astype(BF16)


def _ffn(x2, gain, wg, wu, wd, next_gain=None):
    T, D = x2.shape
    F = wg.shape[1]
    tm, tf = _tile(T, 1024), _tile(F, 512)
    assert F // tf >= 2 and tm % 16 == 0
    row_spec = pl.BlockSpec((tm, D), lambda i, j: (i, 0))
    vec_spec = pl.BlockSpec((1, D), lambda i, j: (0, 0))
    in_specs = [row_spec, vec_spec,
                pl.BlockSpec((D, tf), lambda i, j: (0, j)),
                pl.BlockSpec((D, tf), lambda i, j: (0, j)),
                pl.BlockSpec((tf, D), lambda i, j: (j, 0))]
    args = [x2, gain.reshape(1, D), wg, wu, wd]
    out_specs, out_shape = row_spec, jax.ShapeDtypeStruct((T, D), F32)
    kern = _ffn_kernel
    if next_gain is not None:
        in_specs.append(vec_spec)
        args.append(next_gain.reshape(1, D).astype(F32))
        out_specs = [row_spec, row_spec]
        out_shape = [out_shape, jax.ShapeDtypeStruct((T, D), BF16)]
        kern = _ffn_kernel_with_next
    return pl.pallas_call(
        kern,
        grid=(T // tm, F // tf),
        in_specs=in_specs,
        out_specs=out_specs,
        out_shape=out_shape,
        scratch_shapes=[pltpu.VMEM((tm, D), BF16)],
        compiler_params=pltpu.CompilerParams(dimension_semantics=("parallel", "arbitrary"),
                                             vmem_limit_bytes=FFN_VMEM_LIMIT),
        name="ffn",
    )(*args)


def _proj_kernel(x_ref, g_ref, w_ref, eg_ref, ef_ref, *rest, epilogue, dil, prenormed, rope):
    rest = list(rest)
    if rope:
        cos_ref, sin_ref = rest.pop(0), rest.pop(0)
    o_ref, scratch = rest[0], rest[1:]
    if prenormed:
        h_ref = x_ref
    else:
        h_ref = scratch.pop(0)

        @pl.when(pl.program_id(2) == 0)
        def _():
            x = x_ref[...].astype(F32)
            ms = jnp.mean(x * x, axis=-1, keepdims=True)
            h_ref[...] = (x * lax.rsqrt(ms + RMS_EPS) * g_ref[...]).astype(BF16)

    tm = h_ref.shape[0]
    nb = PERM_ROWS // dil
    if dil > 1:
        hp_ref = scratch.pop(0)

        @pl.when(pl.program_id(2) == 0)
        def _():
            dst = lax.broadcasted_iota(jnp.int32, (PERM_ROWS, PERM_ROWS), 0)
            src = lax.broadcasted_iota(jnp.int32, (PERM_ROWS, PERM_ROWS), 1)
            perm = jnp.where(dst == (src % dil) * nb + src // dil, 1.0, 0.0).astype(BF16)
            for b in range(tm // PERM_ROWS):
                blk = slice(b * PERM_ROWS, (b + 1) * PERM_ROWS)
                hp_ref[blk, :] = jnp.dot(perm, h_ref[blk, :],
                                         preferred_element_type=F32).astype(BF16)

        h_ref = hp_ref
    parts = 2 if tm % (2 * PERM_ROWS) == 0 else 1
    rows_per = tm // parts
    for p in range(parts):
        rows = slice(p * rows_per, (p + 1) * rows_per)
        acc = jnp.dot(h_ref[rows, :], w_ref[...], preferred_element_type=F32)
        for c in range(acc.shape[1] // LANE):
            sl = slice(c * LANE, (c + 1) * LANE)
            y = acc[:, sl]
            if rope:
                ms = jnp.sum(y * y, axis=-1, keepdims=True) * (1.0 / ROPE_DIM)
                y = y * lax.rsqrt(ms + RMS_EPS) * eg_ref[:, sl]
                partner = pltpu.roll(y, ROPE_DIM // 2, 1) + pltpu.roll(y, LANE - ROPE_DIM // 2, 1)
                y = y * cos_ref[rows, :] + partner * sin_ref[rows, :]
            elif epilogue:
                ms = jnp.mean(y * y, axis=-1, keepdims=True)
                scale = jnp.where(ef_ref[:, sl] > 0.0, lax.rsqrt(ms + RMS_EPS), 1.0)
                y = y * scale * eg_ref[:, sl]
            y = y.astype(o_ref.dtype)
            if dil == 1:
                o_ref[0, rows, sl] = y
            else:
                for b in range(rows_per // PERM_ROWS):
                    a0 = (p * (rows_per // PERM_ROWS) + b) * nb
                    for r in range(dil):
                        o_ref[r, a0:a0 + nb, sl] = y[b * PERM_ROWS + r * nb:b * PERM_ROWS + (r + 1) * nb, :]


def _proj(x, gain, w, eg=None, ef=None, *, dil=1, kblock=0, out_dtype=BF16, tm=1024, tn=512,
          rope=None):
    B, S, C = x.shape
    K, N = w.shape
    tm, tn = _tile(S, tm), _tile(N, tn)
    epilogue = eg is not None
    if not epilogue:
        eg = jnp.ones((1, N), F32)
    if ef is None:
        ef = jnp.zeros((1, N), F32)
    row_tables = [] if rope is None else list(rope)
    assert dil == 1 or (tm % PERM_ROWS == 0 and PERM_ROWS % (16 * dil) == 0)
    prenormed = gain is None
    if prenormed:
        assert x.dtype == BF16
        gain = jnp.ones((K,), F32)
    out = pl.pallas_call(
        functools.partial(_proj_kernel, epilogue=epilogue, dil=dil, prenormed=prenormed,
                          rope=rope is not None),
        grid=(B, S // tm, N // tn),
        in_specs=[
            pl.BlockSpec((None, tm, K), lambda b, i, j: (b, i, kblock)),
            pl.BlockSpec((1, K), lambda b, i, j: (0, 0)),
            pl.BlockSpec((K, tn), lambda b, i, j: (0, j)),
            pl.BlockSpec((1, tn), lambda b, i, j: (0, j)),
            pl.BlockSpec((1, tn), lambda b, i, j: (0, j)),
        ] + [pl.BlockSpec((None, tm, LANE), lambda b, i, j: (b, i, 0)) for _ in row_tables],
        out_specs=pl.BlockSpec((None, dil, tm // dil, tn), lambda b, i, j: (b, 0, i, j)),
        out_shape=jax.ShapeDtypeStruct((B, dil, S // dil, N), out_dtype),
        scratch_shapes=[pltpu.VMEM((tm, K), BF16)] * ((not prenormed) + (dil > 1)),
        compiler_params=_params("parallel", "parallel", "arbitrary"),
        name="proj",
    )(x, gain.reshape(1, K).astype(F32), w, eg.reshape(1, N), ef.reshape(1, N), *row_tables)
    return out.reshape(B, S, N) if dil == 1 else out


def _out_kernel(x_ref, a_ref, b_ref, wa_ref, wb_ref, o_ref):
    acc = jnp.dot(a_ref[...], wa_ref[...], preferred_element_type=F32)
    acc += jnp.dot(b_ref[...], wb_ref[...], preferred_element_type=F32)
    o_ref[...] = x_ref[...] + acc


def _out_proj(x2, mix2, mo2, wa, wb):
    T, D = x2.shape
    tm, tn = _tile(T, 1024), _tile(D, 1024)
    ka, kb = mix2.shape[1], mo2.shape[1]
    return pl.pallas_call(
        _out_kernel,
        grid=(T // tm, D // tn),
        in_specs=[
            pl.BlockSpec((tm, tn), lambda i, j: (i, j)),
            pl.BlockSpec((tm, ka), lambda i, j: (i, 0)),
            pl.BlockSpec((tm, kb), lambda i, j: (i, 0)),
            pl.BlockSpec((ka, tn), lambda i, j: (0, j)),
            pl.BlockSpec((kb, tn), lambda i, j: (0, j)),
        ],
        out_specs=pl.BlockSpec((tm, tn), lambda i, j: (i, j)),
        out_shape=jax.ShapeDtypeStruct((T, D), F32),
        compiler_params=_params("parallel", "arbitrary"),
        name="out_proj",
    )(x2, mix2, mo2, wa, wb)


def _lanes(x, reps):
    return x if reps == 1 else jnp.concatenate([x] * reps, axis=-1)


def _softmax_probs(scores, m_ref):
    ps, alphas = [], []
    for g, s in enumerate(scores):
        m_old = m_ref[g]
        m_new = jnp.maximum(m_old, jnp.max(s, axis=-1, keepdims=True))
        ps.append(jnp.exp2(s - _lanes(m_new, s.shape[1] // LANE)).astype(BF16))
        alphas.append(jnp.exp2(m_old - m_new))
        m_ref[g] = m_new
    return ps, alphas


def _accumulate(ps, alphas, values, acc_ref):
    for g, (p, alpha, v) in enumerate(zip(ps, alphas, values)):
        v1 = jnp.concatenate([v, jnp.ones_like(v)], axis=-1)
        acc_ref[g] = _lanes(alpha, 2) * acc_ref[g] + jnp.dot(p, v1, preferred_element_type=F32)


def _online_softmax(n_chunks, produce, values, s_ref, m_ref, acc_ref, finish=None):
    G = s_ref.shape[1]

    def consume(j, scores):
        return _softmax_probs(scores, m_ref)

    def accumulate(j, ps, alphas):
        _accumulate(ps, alphas, values(j), acc_ref)

    m_ref[...] = jnp.full_like(m_ref, NEG_BIG)
    acc_ref[...] = jnp.zeros_like(acc_ref)
    for g, s in enumerate(produce(0)):
        s_ref[0, g] = s

    def step(j, slot):
        cur = [s_ref[slot, g] for g in range(G)]
        nxt = produce(j + 1)
        ps, alphas = consume(j, cur)
        for g, s in enumerate(nxt):
            s_ref[1 - slot, g] = s
        accumulate(j, ps, alphas)

    def body(jj, carry):
        step(2 * jj, 0)
        step(2 * jj + 1, 1)
        return carry

    last = n_chunks - 1
    lax.fori_loop(0, last // 2, body, 0)

    @pl.when(last % 2 == 1)
    def _():
        step(last - 1, 0)

    cur = [s_ref[last % 2, g] for g in range(G)]
    if finish is not None:
        cur = [finish(s) for s in cur]
    accumulate(last, *consume(last, cur))


def _flash_kernel(*refs, nqk, decay, tq, G):
    q_refs, k_refs, v_ref = refs[:nqk], refs[nqk:2 * nqk], refs[2 * nqk]
    pos = 2 * nqk + 1
    if decay:
        ck_ref = refs[pos]
        pos += 1
    o_ref, s_ref, m_ref, acc_ref = refs[pos:pos + 4]
    i = pl.program_id(2)

    def head(ref, g, rows=slice(None)):
        if ref.shape[-1] == LANE:
            return ref[rows, :]
        return ref[rows, g * LANE:(g + 1) * LANE]

    qs = []
    for g in range(G):
        parts = [head(r, g) for r in q_refs]
        qs.append(parts[0] if nqk == 1 else jnp.concatenate(parts, axis=-1))

    def chunk(j):
        return pl.ds(pl.multiple_of(j * tq, tq), tq)

    def produce(j):
        rows = chunk(j)
        scores = []
        for g in range(G):
            parts = [head(r, g, rows) for r in k_refs]
            k = parts[0] if nqk == 1 else jnp.concatenate(parts, axis=-1)
            s = lax.dot_general(qs[g], k, (((1,), (1,)), ((), ())), preferred_element_type=F32)
            scores.append(s - ck_ref[g, :, rows] if decay else s)
        return scores

    def values(j):
        return [head(v_ref, g, chunk(j)) for g in range(G)]

    def causal(s):
        row = lax.broadcasted_iota(jnp.int32, (tq, tq), 0)
        col = lax.broadcasted_iota(jnp.int32, (tq, tq), 1)
        return jnp.where(col <= row, s, -jnp.inf)

    _online_softmax(i + 1, produce, values, s_ref, m_ref, acc_ref, finish=causal)
    for g in range(G):
        acc = acc_ref[g]
        o_ref[:, g * LANE:(g + 1) * LANE] = (acc[:, :LANE] / acc[:, LANE:]).astype(o_ref.dtype)


def _flash(q_parts, k_parts, v_part, B, S, decay_t=None, tq=512, G=4):
    tq = _tile(S, tq)
    nqk = len(q_parts)
    W = G * LANE
    args, specs = [], []
    for arr, off, _ in q_parts:
        args.append(arr)
        specs.append(pl.BlockSpec((None, tq, W), lambda b, h, i, off=off: (b, i, off // W + h)))
    for arr, off, per_head in k_parts + [v_part]:
        args.append(arr)
        if per_head:
            specs.append(pl.BlockSpec((None, S, W), lambda b, h, i, off=off: (b, 0, off // W + h)))
        else:
            specs.append(pl.BlockSpec((None, S, LANE), lambda b, h, i, off=off: (b, 0, off // LANE)))
    if decay_t is not None:
        args.append(decay_t)
        specs.append(pl.BlockSpec((None, G, 1, S), lambda b, h, i: (b, h, 0, 0)))
    return pl.pallas_call(
        functools.partial(_flash_kernel, nqk=nqk, decay=decay_t is not None, tq=tq, G=G),
        grid=(B, N_HEADS // G, S // tq),
        in_specs=specs,
        out_specs=pl.BlockSpec((None, tq, W), lambda b, h, i: (b, i, h)),
        out_shape=jax.ShapeDtypeStruct((B, S, MIX_WIDTH), BF16),
        scratch_shapes=[pltpu.VMEM((2, G, tq, tq), F32), pltpu.VMEM((G, tq, LANE), F32),
                        pltpu.VMEM((G, tq, 2 * HEAD_DIM), F32)],
        compiler_params=_params("parallel", "parallel", "arbitrary"),
        name="flash",
    )(*args)


def _gate_kernel(fg_ref, bf_ref, cumt_ref, carry_ref, *, ts):
    @pl.when(pl.program_id(1) == 0)
    def _():
        carry_ref[...] = jnp.zeros_like(carry_ref)

    z = fg_ref[...] + bf_ref[...]
    lf = jnp.minimum(z, 0.0) - jnp.log1p(jnp.exp(-jnp.abs(z)))
    hi = lf.astype(BF16)
    r1 = lf - hi.astype(F32)
    mid = r1.astype(BF16)
    lo = (r1 - mid.astype(F32)).astype(BF16)
    row = lax.broadcasted_iota(jnp.int32, (ts, ts), 0)
    col = lax.broadcasted_iota(jnp.int32, (ts, ts), 1)
    tri = jnp.where(col <= row, 1.0, 0.0).astype(BF16)
    cum = (jnp.dot(tri, hi, preferred_element_type=F32)
           + jnp.dot(tri, mid, preferred_element_type=F32)
           + jnp.dot(tri, lo, preferred_element_type=F32)) + carry_ref[...]
    carry_ref[...] = cum[ts - 1:ts, :]
    cumt_ref[...] = (cum * LOG2E).T


def _fox_gates(fg, b_f):
    B, S, _ = fg.shape
    ts = _tile(S, 512)
    bf = jnp.zeros((1, LANE), F32).at[0, :N_HEADS].set(b_f.astype(F32))
    return pl.pallas_call(
        functools.partial(_gate_kernel, ts=ts),
        grid=(B, S // ts),
        in_specs=[pl.BlockSpec((None, ts, LANE), lambda b, i: (b, i, 0)),
                  pl.BlockSpec((1, LANE), lambda b, i: (0, 0))],
        out_specs=pl.BlockSpec((None, LANE, ts), lambda b, i: (b, 0, i)),
        out_shape=jax.ShapeDtypeStruct((B, LANE, S), F32),
        scratch_shapes=[pltpu.VMEM((1, LANE), F32)],
        compiler_params=_params("parallel", "arbitrary"),
        name="fox_gates",
    )(fg, bf)


def _rope_kernel(x_ref, cos_ref, sin_ref, g_ref, o_ref):
    cos, sin, g = cos_ref[...], sin_ref[...], g_ref[...]
    for c in range(x_ref.shape[1] // LANE):
        sl = slice(c * LANE, (c + 1) * LANE)
        x = x_ref[:, sl]
        ms = jnp.sum(x * x, axis=-1, keepdims=True) * (1.0 / ROPE_DIM)
        y = x * lax.rsqrt(ms + RMS_EPS) * g
        partner = pltpu.roll(y, ROPE_DIM // 2, 1) + pltpu.roll(y, LANE - ROPE_DIM // 2, 1)
        o_ref[:, sl] = (y * cos + partner * sin).astype(o_ref.dtype)


def _rope(x, cos, sin, gain):
    B, S, N = x.shape
    ts = _tile(S, 512)
    g = jnp.zeros((1, LANE), F32).at[0, :ROPE_DIM].set(gain.astype(F32))
    return pl.pallas_call(
        _rope_kernel,
        grid=(B, S // ts),
        in_specs=[pl.BlockSpec((None, ts, N), lambda b, i: (b, i, 0)),
                  pl.BlockSpec((None, ts, LANE), lambda b, i: (b, i, 0)),
                  pl.BlockSpec((None, ts, LANE), lambda b, i: (b, i, 0)),
                  pl.BlockSpec((1, LANE), lambda b, i: (0, 0))],
        out_specs=pl.BlockSpec((None, ts, N), lambda b, i: (b, i, 0)),
        out_shape=jax.ShapeDtypeStruct((B, S, N), BF16),
        compiler_params=_params("parallel", "parallel"),
        name="rope",
    )(x, cos, sin, g)


def _mem_kernel(q_ref, kv_ref, o_ref):
    for h in range(MEM_HEADS):
        q = q_ref[:, h * LANE:(h + 1) * LANE]
        k = kv_ref[:, h * LANE:(h + 1) * LANE]
        v = kv_ref[:, MEM_WIDTH + h * LANE:MEM_WIDTH + (h + 1) * LANE]
        s = lax.dot_general(q, k, (((1,), (1,)), ((), ())), preferred_element_type=F32)
        p = jnp.exp(s - jnp.max(s, axis=-1, keepdims=True))
        o = jnp.dot(p.astype(BF16), v, preferred_element_type=F32)
        o_ref[:, h * LANE:(h + 1) * LANE] = (o / jnp.sum(p, axis=-1, keepdims=True)).astype(o_ref.dtype)


def _mem_attention(qarr, qblock, mem_kv):
    B, S, _ = qarr.shape
    n_mem = mem_kv.shape[1]
    tq = _tile(S, 1024)
    return pl.pallas_call(
        _mem_kernel,
        grid=(B, S // tq),
        in_specs=[pl.BlockSpec((None, tq, MEM_WIDTH), lambda b, i: (b, i, qblock)),
                  pl.BlockSpec((None, n_mem, 2 * MEM_WIDTH), lambda b, i: (b, 0, 0))],
        out_specs=pl.BlockSpec((None, tq, MEM_WIDTH), lambda b, i: (b, i, 0)),
        out_shape=jax.ShapeDtypeStruct((B, S, MEM_WIDTH), BF16),
        compiler_params=_params("parallel", "parallel"),
        name="mem_attn",
    )(qarr, mem_kv)


def _dil_kernel(q_ref, kp_ref, kc_ref, vp_ref, vc_ref, bias_ref, o_ref, lse_ref):
    n = DIL_BLOCK
    nsub = q_ref.shape[0] // n
    row = lax.broadcasted_iota(jnp.int32, (n, 2 * n), 0)
    col = lax.broadcasted_iota(jnp.int32, (n, 2 * n), 1)
    band = jnp.logical_and(col >= row, col <= row + n)
    first = jnp.logical_and(band, jnp.logical_or(col >= n, pl.program_id(1) > 0))
    lane = lax.broadcasted_iota(jnp.int32, (n, LANE), 1)
    dn = (((1,), (1,)), ((), ()))
    heads = [slice(h * LANE, (h + 1) * LANE) for h in range(N_HEADS)]

    def window(prev_ref, cur_ref, u, sl):
        if u == 0:
            return jnp.concatenate([prev_ref[:, sl], cur_ref[:n, sl]], axis=0)
        return cur_ref[(u - 1) * n:(u + 1) * n, sl]

    for u in range(nsub):
        rows = slice(u * n, (u + 1) * n)
        valid = first if u == 0 else band
        scores = []
        for h, sl in enumerate(heads):
            s = lax.dot_general(q_ref[rows, sl], window(kp_ref, kc_ref, u, sl), dn,
                                preferred_element_type=F32) + bias_ref[h]
            scores.append(jnp.where(valid, s, -jnp.inf))
        ms = [jnp.max(s, axis=-1, keepdims=True) for s in scores]
        ps = [jnp.exp2(s - m).astype(BF16) for s, m in zip(scores, ms)]
        lse_all = jnp.zeros((n, LANE), F32)
        for h, sl in enumerate(heads):
            v2 = window(vp_ref, vc_ref, u, sl)
            pv = jnp.dot(ps[h], jnp.concatenate([v2, jnp.ones_like(v2)], axis=-1),
                         preferred_element_type=F32)
            den = pv[:, LANE:]
            o_ref[h, rows, :] = pv[:, :LANE] / den
            lse_all = jnp.where(lane == h, ms[h] + jnp.log2(den), lse_all)
        lse_ref[rows, :] = lse_all


def _dil_attention(pg, bias):
    N, Ls, _ = pg.shape
    n = DIL_BLOCK
    nsub = 2 if Ls % (2 * n) == 0 else 1
    tr = nsub * n

    def cur(part):
        return pl.BlockSpec((None, tr, MIX_WIDTH), lambda s, i: (s, i, part))

    def prev(part):
        return pl.BlockSpec((None, n, MIX_WIDTH), lambda s, i: (s, jnp.maximum(i * nsub - 1, 0), part))

    return pl.pallas_call(
        _dil_kernel,
        grid=(N, Ls // tr),
        in_specs=[cur(0), prev(1), cur(1), prev(2), cur(2),
                  pl.BlockSpec((N_HEADS, n, 2 * n), lambda s, i: (0, 0, 0))],
        out_specs=[pl.BlockSpec((None, N_HEADS, tr, LANE), lambda s, i: (s, 0, i, 0)),
                   pl.BlockSpec((None, tr, LANE), lambda s, i: (s, i, 0))],
        out_shape=[jax.ShapeDtypeStruct((N, N_HEADS, Ls, LANE), F32),
                   jax.ShapeDtypeStruct((N, Ls, LANE), F32)],
        compiler_params=_params("parallel", "arbitrary"),
        name="dil_attn",
    )(pg, pg, pg, pg, pg, bias)


def _dil_combine_kernel(o1_ref, o2_ref, o3_ref, l1_ref, l2_ref, l3_ref, out_ref, stage_ref, *, d2, d3):
    n = l3_ref.shape[1]
    q = d3 // d2
    for r in range(d3):
        rows = [(pl.ds(r, n, stride=d3),), (r % d2, pl.ds(r // d2, n, stride=q)), (r, slice(None))]
        ls = [l_ref[idx + (slice(None),)]
              for l_ref, idx in zip((l1_ref, l2_ref, l3_ref), rows)]
        m = functools.reduce(jnp.maximum, ls)
        es = [jnp.exp2(l - m) for l in ls]
        inv = 1.0 / functools.reduce(jnp.add, es)
        ws = [e * inv for e in es]
        for h in range(N_HEADS):
            o = (ws[0][:, h:h + 1] * o1_ref[h, pl.ds(r, n, stride=d3), :]
                 + ws[1][:, h:h + 1] * o2_ref[r % d2, h, pl.ds(r // d2, n, stride=q), :]
                 + ws[2][:, h:h + 1] * o3_ref[r, h, :, :])
            stage_ref[h, pl.ds(r, n, stride=d3), :] = o
    for h in range(N_HEADS):
        out_ref[:, h * LANE:(h + 1) * LANE] = stage_ref[h].astype(out_ref.dtype)


def _dil_combine(outs, lses, B, S, tm=256):
    (_, d1), (_, d2), (_, d3) = DIL_GROUPS
    assert d1 == 1 and d3 % d2 == 0
    tm = _tile(S, tm)
    H = N_HEADS
    o_specs = [pl.BlockSpec((None, H, tm, LANE), lambda b, i: (b, 0, i, 0)),
               pl.BlockSpec((None, d2, H, tm // d2, LANE), lambda b, i: (b, 0, 0, i, 0)),
               pl.BlockSpec((None, d3, H, tm // d3, LANE), lambda b, i: (b, 0, 0, i, 0))]
    l_specs = [pl.BlockSpec((None, tm, LANE), lambda b, i: (b, i, 0)),
               pl.BlockSpec((None, d2, tm // d2, LANE), lambda b, i: (b, 0, i, 0)),
               pl.BlockSpec((None, d3, tm // d3, LANE), lambda b, i: (b, 0, i, 0))]
    return pl.pallas_call(
        functools.partial(_dil_combine_kernel, d2=d2, d3=d3),
        grid=(B, S // tm),
        in_specs=o_specs + l_specs,
        out_specs=pl.BlockSpec((None, tm, MIX_WIDTH), lambda b, i: (b, i, 0)),
        out_shape=jax.ShapeDtypeStruct((B, S, MIX_WIDTH), BF16),
        scratch_shapes=[pltpu.VMEM((H, tm, LANE), F32)],
        compiler_params=_params("parallel", "parallel"),
        name="dil_combine",
    )(outs[0].reshape(B, H, S, LANE), outs[1].reshape(B, d2, H, S // d2, LANE),
      outs[2].reshape(B, d3, H, S // d3, LANE),
      lses[0].reshape(B, S, LANE), lses[1].reshape(B, d2, S // d2, LANE),
      lses[2].reshape(B, d3, S // d3, LANE))


def _dsa_topk_kernel(qi_ref, wi_ref, ki_ref, mask_ref, key_ref, jsel_ref, *, tq, tk, S, n_sel):
    i = pl.program_id(1)
    q0 = i * tq
    n_chunks = (q0 + tq + tk - 1) // tk
    dn = (((1,), (1,)), ((), ()))
    krow = lax.broadcasted_iota(jnp.int32, (tk, tq), 0)
    qpos = q0 + lax.broadcasted_iota(jnp.int32, (tk, tq), 1)

    wi_t = (wi_ref[...] * (IDX_HEADS ** -0.5 * IDX_DIM ** -0.5)).T

    def idx_body(c, carry):
        off = pl.multiple_of(c * tk, tk)
        kic = ki_ref[pl.ds(off, tk), :]
        k_lo = jnp.where(lax.broadcasted_iota(jnp.int32, kic.shape, 1) < IDX_DIM, kic, 0.0)
        k_sides = (k_lo.astype(BF16), pltpu.roll(k_lo, IDX_DIM, 1).astype(BF16))
        acc = jnp.zeros((tk, tq), F32)
        for h in range(IDX_HEADS):
            d = lax.dot_general(k_sides[h % 2], qi_ref[:, (h // 2) * LANE:(h // 2 + 1) * LANE], dn,
                                preferred_element_type=F32)
            acc = acc + wi_t[IDX_DIM + h:IDX_DIM + h + 1, :] * jnp.maximum(d, 0.0)
        score = jnp.where(off + krow <= qpos, acc, -jnp.inf)
        bits = pltpu.bitcast(score, jnp.int32)
        key_ref[pl.ds(off, tk), :] = bits ^ ((bits >> 31) & jnp.int32(0x7FFFFFFF))
        return carry

    lax.fori_loop(0, n_chunks, idx_body, 0)

    def count(pred):
        sub = 64
        def body(c, acc):
            off = pl.multiple_of(c * tk, tk)
            ones = jnp.where(pred(key_ref[pl.ds(off, tk), :], off), 1.0, 0.0)
            return acc + jnp.sum(ones.reshape(tk // sub, sub, tq), axis=0)

        acc = lax.fori_loop(0, n_chunks, body, jnp.zeros((sub, tq), F32))
        return jnp.sum(acc, axis=0, keepdims=True)

    k_sel = float(n_sel)
    c0 = count(lambda k, off: k >= 0)
    lo0 = jnp.where(c0 >= k_sel, jnp.int32(0), jnp.int32(-2 ** 31))
    n0 = jnp.where(c0 >= k_sel, c0, (n_chunks * tk).astype(F32))

    def bis_cond(state):
        it, _, n_lo = state
        return jnp.logical_and(it < 31, jnp.max(jnp.abs(n_lo - k_sel)) > 0.0)

    def bis_body(state):
        it, lo, n_lo = state
        cand = lo + (jnp.int32(1) << (30 - it))
        cnt = count(lambda k, off: k >= cand)
        take = cnt >= k_sel
        return it + 1, jnp.where(take, cand, lo), jnp.where(take, cnt, n_lo)

    _, thr, n_ge = lax.while_loop(bis_cond, bis_body, (jnp.int32(0), lo0, n0))
    n_bits = max(1, (S - 1).bit_length())
    jsel_ref[...] = jnp.full_like(jsel_ref, S)

    @pl.when(jnp.max(n_ge) > k_sel)
    def _():
        need = k_sel - count(lambda k, off: k > thr)

        def tie_body(it, jlo):
            cand = jlo + (jnp.int32(1) << (n_bits - 1 - it))
            below = count(lambda k, off: jnp.logical_and(k == thr, off + krow < cand))
            return jnp.where(below < need, cand, jlo)

        jsel = lax.fori_loop(0, n_bits, tie_body, jnp.zeros((1, tq), jnp.int32))
        jsel_ref[...] = jnp.broadcast_to(jsel, jsel_ref.shape)

    jsel = jsel_ref[:1, :]
    mask_ref[...] = jnp.full_like(mask_ref, -jnp.inf)

    def mask_body(c, carry):
        off = pl.multiple_of(c * tk, tk)
        k = key_ref[pl.ds(off, tk), :]
        kpos = off + krow
        tie = jnp.logical_and(k == thr, kpos <= jsel)
        sel = jnp.logical_and(jnp.logical_or(k > thr, tie), kpos <= qpos)
        mask_ref[:, pl.ds(off, tk)] = jnp.where(sel, 0.0, -jnp.inf).T.astype(mask_ref.dtype)
        return carry

    lax.fori_loop(0, n_chunks, mask_body, 0)


def _dsa_select(pm, aux, B, S, n_sel, tq=256, tk=512):
    tq, tk = _tile(S, tq), _tile(S, tk)
    qiw = IDX_HEADS * IDX_DIM
    return pl.pallas_call(
        functools.partial(_dsa_topk_kernel, tq=tq, tk=tk, S=S, n_sel=n_sel),
        grid=(B, S // tq),
        in_specs=[pl.BlockSpec((None, tq, qiw), lambda b, i: (b, i, MIX_WIDTH // qiw)),
                  pl.BlockSpec((None, tq, LANE), lambda b, i: (b, i, 0)),
                  pl.BlockSpec((None, S, LANE), lambda b, i: (b, 0, 0))],
        out_specs=pl.BlockSpec((None, tq, S), lambda b, i: (b, i, 0)),
        out_shape=jax.ShapeDtypeStruct((B, S, S), BF16),
        scratch_shapes=[pltpu.VMEM((S, tq), jnp.int32), pltpu.VMEM((8, tq), jnp.int32)],
        compiler_params=_params("parallel", "arbitrary"),
        name="dsa_select",
    )(pm, aux, aux)


def _dsa_attn_kernel(q_ref, k_ref, v_ref, mask_ref, gt_ref, o_ref, s_ref, m_ref, acc_ref, *, tq, tk, S):
    i = pl.program_id(2)
    q0 = i * tq
    n_chunks = (q0 + tq + tk - 1) // tk
    R = q_ref.shape[1] // LANE
    nblk = tq // LANE
    dn = (((1,), (1,)), ((), ()))
    qs = [q_ref[:, r * LANE:(r + 1) * LANE] for r in range(R)]

    def produce(c):
        off = pl.multiple_of(c * tk, tk)
        kc = k_ref[pl.ds(off, tk), :]
        mask = mask_ref[:, pl.ds(off, tk)].astype(F32)
        st = S - q0 + off
        scores = []
        for r in range(R):
            bias = jnp.concatenate(
                [gt_ref[r, :, pl.ds(pl.multiple_of(st + (nblk - 1 - a) * LANE, LANE), tk)]
                 for a in range(nblk)], axis=0)
            scores.append(lax.dot_general(qs[r], kc, dn, preferred_element_type=F32) + bias + mask)
        return scores

    def values(c):
        return [v_ref[pl.ds(pl.multiple_of(c * tk, tk), tk), :]] * R

    _online_softmax(n_chunks, produce, values, s_ref, m_ref, acc_ref)
    for r in range(R):
        acc = acc_ref[r]
        o_ref[:, r * LANE:(r + 1) * LANE] = (acc[:, :LANE] / acc[:, LANE:]).astype(o_ref.dtype)


def _dsa_attention(pm, mask, gt, B, S, tq=256, tk=512):
    tq, tk = _tile(S, tq), _tile(S, tk)
    R = N_HEADS // DSA_KV_HEADS
    kblk = (MIX_WIDTH + IDX_HEADS * IDX_DIM) // LANE
    return pl.pallas_call(
        functools.partial(_dsa_attn_kernel, tq=tq, tk=tk, S=S),
        grid=(DSA_KV_HEADS, B, S // tq),
        in_specs=[pl.BlockSpec((None, tq, R * LANE), lambda g, b, i: (b, i, g)),
                  pl.BlockSpec((None, S, LANE), lambda g, b, i: (b, 0, kblk + g)),
                  pl.BlockSpec((None, S, LANE), lambda g, b, i: (b, 0, kblk + DSA_KV_HEADS + g)),
                  pl.BlockSpec((None, tq, S), lambda g, b, i: (b, i, 0)),
                  pl.BlockSpec((R, LANE, gt.shape[2]), lambda g, b, i: (g, 0, 0))],
        out_specs=pl.BlockSpec((None, tq, R * LANE), lambda g, b, i: (b, i, g)),
        out_shape=jax.ShapeDtypeStruct((B, S, MIX_WIDTH), BF16),
        scratch_shapes=[pltpu.VMEM((2, R, tq, tk), F32), pltpu.VMEM((R, tq, LANE), F32),
                        pltpu.VMEM((R, tq, 2 * HEAD_DIM), F32)],
        compiler_params=_params("parallel", "parallel", "arbitrary"),
        name="dsa_attn",
    )(pm, pm, pm, mask, gt)


def _t5_bucket(dist):
    n = jnp.maximum(dist, 0)
    exact = REL_BUCKETS // 2
    nf = jnp.maximum(n, 1).astype(F32)
    large = exact + (jnp.log(nf / exact) / math.log(REL_MAX_DIST / exact)
                     * (REL_BUCKETS - exact)).astype(jnp.int32)
    large = jnp.minimum(large, REL_BUCKETS - 1)
    return jnp.where(n < exact, n, large)


def _t5_bias(t5_table, dist):
    onehot = jax.nn.one_hot(_t5_bucket(dist), REL_BUCKETS, dtype=F32)
    table = jnp.einsum('...b,bh->h...', onehot, t5_table.astype(F32), precision=lax.Precision.HIGHEST)
    return table * LOG2E


def _dil_bias(t5_table, dil):
    n = DIL_BLOCK
    rel = jnp.arange(n)[:, None] + n - jnp.arange(2 * n)[None, :]
    return _t5_bias(t5_table, rel * dil)


def _dsa_bias_table(t5_table, S, tq, tk):
    m = jnp.arange(S + tq + tk)[None, :]
    i = jnp.arange(LANE)[:, None]
    return _t5_bias(t5_table, S + tq - LANE - m + i)


def _headnorm_cols(spec):
    eg, ef = [], []
    for gain, count, scale in spec:
        if gain is None:
            eg.append(jnp.ones((count * LANE,), F32))
            ef.append(jnp.zeros((count * LANE,), F32))
        else:
            eg.append(jnp.tile(gain.astype(F32) * scale, count))
            ef.append(jnp.ones((count * LANE,), F32))
    return jnp.concatenate(eg), jnp.concatenate(ef)


def _pad_cols(w, width):
    return jnp.pad(w, ((0, 0), (0, width - w.shape[1])))


def _fox_layer(x, norm_g, w_in, b_f, qk_g, mem_g):
    B, S, _ = x.shape
    W = MIX_WIDTH
    w_main = jnp.concatenate([w_in[:, :3 * W], w_in[:, 3 * W + N_HEADS:]], axis=1).astype(BF16)
    w_gate = _pad_cols(w_in[:, 3 * W:3 * W + N_HEADS], LANE).astype(BF16)
    eg, ef = _headnorm_cols([(qk_g[0], N_HEADS, HEAD_DIM ** -0.5 * LOG2E), (qk_g[1], N_HEADS, 1.0),
                             (None, N_HEADS, 1.0), (mem_g, MEM_HEADS, HEAD_DIM ** -0.5)])
    pm = _proj(x, norm_g, w_main, eg, ef)
    fg = _proj(x, norm_g, w_gate, out_dtype=F32)
    cum_t = _fox_gates(fg, b_f)
    mix = _flash([(pm, 0, True)], [(pm, W, True)], (pm, 2 * W, True), B, S,
                 decay_t=cum_t.reshape(B, LANE, 1, S))
    return mix, pm, 3 * W // MEM_WIDTH


def _mla_layer(x, positions, norm_g, w_in, q_norm, w_uq, kv_norm, w_ukv, nope_g, rope_g, mem_g):
    B, S, _ = x.shape
    scale = (NOPE_DIM + ROPE_DIM) ** -0.5 * LOG2E
    lat = Q_LORA + KV_LORA
    w_lat = jnp.concatenate([w_in[:, :lat], _pad_cols(w_in[:, lat:lat + ROPE_DIM], LANE)],
                            axis=1).astype(BF16)
    w_mem = w_in[:, lat + ROPE_DIM:].astype(BF16)
    pl_ = _proj(x, norm_g, w_lat, out_dtype=F32, tn=lat + LANE)
    eg, ef = _headnorm_cols([(mem_g, MEM_HEADS, HEAD_DIM ** -0.5)])
    pmem = _proj(x, norm_g, w_mem, eg, ef)

    uq = w_uq.reshape(Q_LORA, N_HEADS, NOPE_DIM + ROPE_DIM)
    w_qn = uq[:, :, :NOPE_DIM].reshape(Q_LORA, MIX_WIDTH).astype(BF16)
    w_qr = jnp.pad(uq[:, :, NOPE_DIM:], ((0, 0), (0, 0), (0, LANE - ROPE_DIM))
                   ).reshape(Q_LORA, N_HEADS * LANE).astype(BF16)
    ukv = w_ukv.reshape(KV_LORA, N_HEADS, NOPE_DIM + HEAD_DIM)
    w_kv = jnp.concatenate([ukv[:, :, :NOPE_DIM].reshape(KV_LORA, MIX_WIDTH),
                            ukv[:, :, NOPE_DIM:].reshape(KV_LORA, MIX_WIDTH)], axis=1).astype(BF16)
    eg, ef = _headnorm_cols([(nope_g[0], N_HEADS, scale)])
    qn = _proj(pl_, q_norm, w_qn, eg, ef, kblock=0, tn=2048)
    eg, ef = _headnorm_cols([(nope_g[1], N_HEADS, 1.0), (None, N_HEADS, 1.0)])
    kv = _proj(pl_, kv_norm, w_kv, eg, ef, kblock=1, tn=2048)

    half = ROPE_DIM // 2
    inv = ROPE_THETA ** (-jnp.arange(half, dtype=F32) / half)
    ang = positions.astype(F32)[..., None] * inv
    cos, sin = jnp.cos(ang), jnp.sin(ang)
    zero = jnp.zeros((B, S, LANE - ROPE_DIM), F32)
    cos_t = jnp.concatenate([cos, cos, zero], axis=-1)
    sin_t = jnp.concatenate([-sin, sin, zero], axis=-1)
    g_rope = jnp.tile(_pad_cols((rope_g[0].astype(F32) * scale)[None, :], LANE)[0], N_HEADS)
    qr = _proj(pl_, q_norm, w_qr, g_rope, kblock=0, tn=1024, rope=(cos_t, sin_t))
    kr = _rope(pl_[:, :, lat:], cos_t, sin_t, rope_g[1])
    mix = _flash([(qn, 0, True), (qr, 0, True)], [(kv, 0, True), (kr, 0, False)],
                 (kv, MIX_WIDTH, True), B, S)
    return mix, pmem, 0


def _dil_layer(x, norm_g, w_in, qk_g, t5_table, mem_g):
    B, S, _ = x.shape
    W = MIX_WIDTH
    outs, lses = [], []
    for gi, (win, dil) in enumerate(DIL_GROUPS):
        assert win // dil == DIL_BLOCK and (S // dil) % DIL_BLOCK == 0
        w_g = w_in[:, gi * 3 * W:(gi + 1) * 3 * W].astype(BF16)
        eg, ef = _headnorm_cols([(qk_g[gi, 0], N_HEADS, HEAD_DIM ** -0.5 * LOG2E),
                                 (qk_g[gi, 1], N_HEADS, 1.0), (None, N_HEADS, 1.0)])
        pg = _proj(x, norm_g, w_g, eg, ef, dil=dil, tn=1024)
        o, lse = _dil_attention(pg.reshape(B * dil, S // dil, 3 * W), _dil_bias(t5_table, dil))
        outs.append(o)
        lses.append(lse)
    mix = _dil_combine(outs, lses, B, S)
    eg, ef = _headnorm_cols([(mem_g, MEM_HEADS, HEAD_DIM ** -0.5)])
    pmem = _proj(x, norm_g, w_in[:, len(DIL_GROUPS) * 3 * W:].astype(BF16), eg, ef)
    return mix, pmem, 0


def _dsa_layer(x, norm_g, w_in, qk_g, t5_table, mem_g):
    B, S, _ = x.shape
    W = MIX_WIDTH
    kvw = DSA_KV_HEADS * HEAD_DIM
    o_k, o_v, o_qi = W, W + kvw, W + 2 * kvw
    o_ki = o_qi + IDX_HEADS * IDX_DIM
    o_wi = o_ki + IDX_DIM
    o_mem = o_wi + IDX_HEADS
    qiw = IDX_HEADS * IDX_DIM
    w_main = jnp.concatenate([w_in[:, :W], w_in[:, o_qi:o_ki], w_in[:, o_k:o_qi], w_in[:, o_mem:]],
                             axis=1).astype(BF16)
    w_aux = _pad_cols(w_in[:, o_ki:o_mem], LANE).astype(BF16)
    eg, ef = _headnorm_cols([(qk_g[0], N_HEADS, HEAD_DIM ** -0.5 * LOG2E), (None, qiw // LANE, 1.0),
                             (qk_g[1], DSA_KV_HEADS, 1.0), (None, DSA_KV_HEADS, 1.0),
                             (mem_g, MEM_HEADS, HEAD_DIM ** -0.5)])
    pm = _proj(x, norm_g, w_main, eg, ef)
    aux = _proj(x, norm_g, w_aux, out_dtype=F32)
    mask = _dsa_select(pm, aux, B, S, min(TOPK_MAX, S // 4), tq=512)
    tq, tk = _tile(S, 512), _tile(S, 512)
    mix = _dsa_attention(pm, mask, _dsa_bias_table(t5_table, S, tq, tk), B, S, tq, tk)
    return mix, pm, (W + qiw + 2 * kvw) // MEM_WIDTH


def kernel(x, mem, positions, t5_table, ffn_norm, ffn_w_gate, ffn_w_up, ffn_w_down, attn_norm,
           mem_norm, mem_w_kv, mem_qk_g, w_out, a_w_in, a_b_f, a_qk_g, b_w_in, b_q_norm, b_w_uq,
           b_kv_norm, b_w_ukv, b_nope_g, b_rope_g, c_w_in, c_qk_g, d_w_in, d_qk_g):
    B, S, D = x.shape
    depth = ffn_norm.shape[0]
    n_mixers = 4

    def ffn(xc, i, k, next_gain=None):
        return _ffn(xc.reshape(B * S, D), ffn_norm[i, k], ffn_w_gate[i, k].astype(BF16),
                    ffn_w_up[i, k].astype(BF16), ffn_w_down[i, k].astype(BF16), next_gain)

    for i in range(depth):
        m, j = i % n_mixers, i // n_mixers
        x, h = ffn(x, i, 0, attn_norm[i])
        x, h = x.reshape(B, S, D), h.reshape(B, S, D)
        mem_g = mem_qk_g[i]
        if m == 0:
            mix, qarr, qblock = _fox_layer(h, None, a_w_in[j], a_b_f[j], a_qk_g[j], mem_g[0])
        elif m == 1:
            mix, qarr, qblock = _mla_layer(h, positions, None, b_w_in[j], b_q_norm[j],
                                           b_w_uq[j], b_kv_norm[j], b_w_ukv[j], b_nope_g[j],
                                           b_rope_g[j], mem_g[0])
        elif m == 2:
            mix, qarr, qblock = _dil_layer(h, None, c_w_in[j], c_qk_g[j], t5_table, mem_g[0])
        else:
            mix, qarr, qblock = _dsa_layer(h, None, d_w_in[j], d_qk_g[j], t5_table, mem_g[0])
        eg, ef = _headnorm_cols([(mem_g[1], MEM_HEADS, 1.0), (None, MEM_HEADS, 1.0)])
        mem_kv = _proj(mem, mem_norm[i], mem_w_kv[i].astype(BF16), eg, ef)
        mo = _mem_attention(qarr, qblock, mem_kv)
        wo = w_out[i].astype(BF16)
        x = _out_proj(x.reshape(B * S, D), mix.reshape(B * S, MIX_WIDTH),
                      mo.reshape(B * S, MEM_WIDTH), wo[:MIX_WIDTH], wo[MIX_WIDTH:]).reshape(B, S, D)
        x = ffn(x, i, 1).reshape(B, S, D)
    return x
```

```python
import functools
import math

import jax
import jax.numpy as jnp
from jax import lax
from jax.experimental import pallas as pl
from jax.experimental.pallas import tpu as pltpu

F32 = jnp.float32
BF16 = jnp.bfloat16

LANE = 128
D_MODEL = 2048
N_HEADS = 16
HEAD_DIM = 128
MIX_WIDTH = N_HEADS * HEAD_DIM
MEM_HEADS = 4
MEM_WIDTH = MEM_HEADS * HEAD_DIM
D_FF = 5632
RMS_EPS = 1e-6
REL_BUCKETS = 32
REL_MAX_DIST = 2048
Q_LORA = 512
KV_LORA = 512
NOPE_DIM = 128
ROPE_DIM = 64
ROPE_THETA = 10000.0
DIL_GROUPS = ((128, 1), (512, 4), (2048, 16))
DIL_BLOCK = 128
PERM_ROWS = 256
DSA_KV_HEADS = 4
IDX_HEADS = 16
IDX_DIM = 64
TOPK_MAX = 256
VMEM_LIMIT = 56 * 1024 * 1024
FFN_VMEM_LIMIT = 62 * 1024 * 1024
NEG_BIG = -1e30
LOG2E = math.log2(math.e)


def _params(*sem):
    return pltpu.CompilerParams(dimension_semantics=sem, vmem_limit_bytes=VMEM_LIMIT)


def _tile(n, pref):
    t = min(n, pref)
    while n % t:
        t //= 2
    return t


def _ffn_kernel(x_ref, g_ref, wg_ref, wu_ref, wd_ref, o_ref, h_ref):
    _ffn_body(x_ref, g_ref, wg_ref, wu_ref, wd_ref, o_ref, h_ref, ())


def _ffn_kernel_with_next(x_ref, g_ref, wg_ref, wu_ref, wd_ref, ng_ref, o_ref, hn_ref, h_ref):
    _ffn_body(x_ref, g_ref, wg_ref, wu_ref, wd_ref, o_ref, h_ref, (ng_ref, hn_ref))


def _ffn_body(x_ref, g_ref, wg_ref, wu_ref, wd_ref, o_ref, h_ref, next_refs):
    j = pl.program_id(1)
    last = pl.num_programs(1) - 1
    tm = x_ref.shape[0]
    halves = [slice(0, tm // 2), slice(tm // 2, tm)]

    def down(h):
        g = jnp.dot(h, wg_ref[...], preferred_element_type=F32)
        u = jnp.dot(h, wu_ref[...], preferred_element_type=F32)
        a = (g * jax.nn.sigmoid(g) * u).astype(BF16)
        return jnp.dot(a, wd_ref[...], preferred_element_type=F32)

    @pl.when(j == 0)
    def _():
        for rows in halves:
            x = x_ref[rows, :]
            ms = jnp.mean(x * x, axis=-1, keepdims=True)
            h = (x * lax.rsqrt(ms + RMS_EPS) * g_ref[...]).astype(BF16)
            h_ref[rows, :] = h
            o_ref[rows, :] = down(h)

    @pl.when(jnp.logical_and(j > 0, j < last))
    def _():
        o_ref[...] += down(h_ref[...])

    @pl.when(jnp.logical_and(j > 0, j == last))
    def _():
        for rows in halves:
            y = x_ref[rows, :] + 0.5 * (o_ref[rows, :] + down(h_ref[rows, :]))
            o_ref[rows, :] = y
            if next_refs:
                ng_ref, hn_ref = next_refs
                ms = jnp.mean(y * y, axis=-1, keepdims=True)
                hn_ref[rows, :] = (y * lax.rsqrt(ms + RMS_EPS) * ng_ref[...]).astype(BF16)


def _ffn(x2, gain, wg, wu, wd, next_gain=None):
    T, D = x2.shape
    F = wg.shape[1]
    tm, tf = _tile(T, 1024), _tile(F, 512)
    assert F // tf >= 2 and tm % 16 == 0
    row_spec = pl.BlockSpec((tm, D), lambda i, j: (i, 0))
    vec_spec = pl.BlockSpec((1, D), lambda i, j: (0, 0))
    in_specs = [row_spec, vec_spec,
                pl.BlockSpec((D, tf), lambda i, j: (0, j)),
                pl.BlockSpec((D, tf), lambda i, j: (0, j)),
                pl.BlockSpec((tf, D), lambda i, j: (j, 0))]
    args = [x2, gain.reshape(1, D), wg, wu, wd]
    out_specs, out_shape = row_spec, jax.ShapeDtypeStruct((T, D), F32)
    kern = _ffn_kernel
    if next_gain is not None:
        in_specs.append(vec_spec)
        args.append(next_gain.reshape(1, D).astype(F32))
        out_specs = [row_spec, row_spec]
        out_shape = [out_shape, jax.ShapeDtypeStruct((T, D), BF16)]
        kern = _ffn_kernel_with_next
    return pl.pallas_call(
        kern,
        grid=(T // tm, F // tf),
        in_specs=in_specs,
        out_specs=out_specs,
        out_shape=out_shape,
        scratch_shapes=[pltpu.VMEM((tm, D), BF16)],
        compiler_params=pltpu.CompilerParams(dimension_semantics=("parallel", "arbitrary"),
                                             vmem_limit_bytes=FFN_VMEM_LIMIT),
        name="ffn",
    )(*args)


def _proj_kernel(x_ref, g_ref, w_ref, eg_ref, ef_ref, o_ref, *scratch, epilogue, dil, prenormed):
    scratch = list(scratch)
    if prenormed:
        h_ref = x_ref
    else:
        h_ref = scratch.pop(0)

        @pl.when(pl.program_id(2) == 0)
        def _():
            x = x_ref[...].astype(F32)
            ms = jnp.mean(x * x, axis=-1, keepdims=True)
            h_ref[...] = (x * lax.rsqrt(ms + RMS_EPS) * g_ref[...]).astype(BF16)

    tm = h_ref.shape[0]
    nb = PERM_ROWS // dil
    if dil > 1:
        hp_ref = scratch.pop(0)

        @pl.when(pl.program_id(2) == 0)
        def _():
            dst = lax.broadcasted_iota(jnp.int32, (PERM_ROWS, PERM_ROWS), 0)
            src = lax.broadcasted_iota(jnp.int32, (PERM_ROWS, PERM_ROWS), 1)
            perm = jnp.where(dst == (src % dil) * nb + src // dil, 1.0, 0.0).astype(BF16)
            for b in range(tm // PERM_ROWS):
                blk = slice(b * PERM_ROWS, (b + 1) * PERM_ROWS)
                hp_ref[blk, :] = jnp.dot(perm, h_ref[blk, :],
                                         preferred_element_type=F32).astype(BF16)

        h_ref = hp_ref
    parts = 2 if tm % (2 * PERM_ROWS) == 0 else 1
    rows_per = tm // parts
    for p in range(parts):
        rows = slice(p * rows_per, (p + 1) * rows_per)
        acc = jnp.dot(h_ref[rows, :], w_ref[...], preferred_element_type=F32)
        for c in range(acc.shape[1] // LANE):
            sl = slice(c * LANE, (c + 1) * LANE)
            y = acc[:, sl]
            if epilogue:
                ms = jnp.mean(y * y, axis=-1, keepdims=True)
                scale = jnp.where(ef_ref[:, sl] > 0.0, lax.rsqrt(ms + RMS_EPS), 1.0)
                y = y * scale * eg_ref[:, sl]
            y = y.astype(o_ref.dtype)
            if dil == 1:
                o_ref[0, rows, sl] = y
            else:
                for b in range(rows_per // PERM_ROWS):
                    a0 = (p * (rows_per // PERM_ROWS) + b) * nb
                    for r in range(dil):
                        o_ref[r, a0:a0 + nb, sl] = y[b * PERM_ROWS + r * nb:b * PERM_ROWS + (r + 1) * nb, :]


def _proj(x, gain, w, eg=None, ef=None, *, dil=1, kblock=0, out_dtype=BF16, tm=1024, tn=512):
    B, S, C = x.shape
    K, N = w.shape
    tm, tn = _tile(S, tm), _tile(N, tn)
    epilogue = eg is not None
    if not epilogue:
        eg = jnp.ones((1, N), F32)
        ef = jnp.zeros((1, N), F32)
    assert dil == 1 or (tm % PERM_ROWS == 0 and PERM_ROWS % (16 * dil) == 0)
    prenormed = gain is None
    if prenormed:
        assert x.dtype == BF16
        gain = jnp.ones((K,), F32)
    out = pl.pallas_call(
        functools.partial(_proj_kernel, epilogue=epilogue, dil=dil, prenormed=prenormed),
        grid=(B, S // tm, N // tn),
        in_specs=[
            pl.BlockSpec((None, tm, K), lambda b, i, j: (b, i, kblock)),
            pl.BlockSpec((1, K), lambda b, i, j: (0, 0)),
            pl.BlockSpec((K, tn), lambda b, i, j: (0, j)),
            pl.BlockSpec((1, tn), lambda b, i, j: (0, j)),
            pl.BlockSpec((1, tn), lambda b, i, j: (0, j)),
        ],
        out_specs=pl.BlockSpec((None, dil, tm // dil, tn), lambda b, i, j: (b, 0, i, j)),
        out_shape=jax.ShapeDtypeStruct((B, dil, S // dil, N), out_dtype),
        scratch_shapes=[pltpu.VMEM((tm, K), BF16)] * ((not prenormed) + (dil > 1)),
        compiler_params=_params("parallel", "parallel", "arbitrary"),
        name="proj",
    )(x, gain.reshape(1, K).astype(F32), w, eg.reshape(1, N), ef.reshape(1, N))
    return out.reshape(B, S, N) if dil == 1 else out


def _out_kernel(x_ref, a_ref, b_ref, wa_ref, wb_ref, o_ref):
    acc = jnp.dot(a_ref[...], wa_ref[...], preferred_element_type=F32)
    acc += jnp.dot(b_ref[...], wb_ref[...], preferred_element_type=F32)
    o_ref[...] = x_ref[...] + acc


def _out_proj(x2, mix2, mo2, wa, wb):
    T, D = x2.shape
    tm, tn = _tile(T, 1024), _tile(D, 1024)
    ka, kb = mix2.shape[1], mo2.shape[1]
    return pl.pallas_call(
        _out_kernel,
        grid=(T // tm, D // tn),
        in_specs=[
            pl.BlockSpec((tm, tn), lambda i, j: (i, j)),
            pl.BlockSpec((tm, ka), lambda i, j: (i, 0)),
            pl.BlockSpec((tm, kb), lambda i, j: (i, 0)),
            pl.BlockSpec((ka, tn), lambda i, j: (0, j)),
            pl.BlockSpec((kb, tn), lambda i, j: (0, j)),
        ],
        out_specs=pl.BlockSpec((tm, tn), lambda i, j: (i, j)),
        out_shape=jax.ShapeDtypeStruct((T, D), F32),
        compiler_params=_params("parallel", "arbitrary"),
        name="out_proj",
    )(x2, mix2, mo2, wa, wb)


def _lanes(x, reps):
    return x if reps == 1 else jnp.concatenate([x] * reps, axis=-1)


def _softmax_probs(scores, m_ref):
    ps, alphas = [], []
    for g, s in enumerate(scores):
        m_old = m_ref[g]
        m_new = jnp.maximum(m_old, jnp.max(s, axis=-1, keepdims=True))
        ps.append(jnp.exp2(s - _lanes(m_new, s.shape[1] // LANE)).astype(BF16))
        alphas.append(jnp.exp2(m_old - m_new))
        m_ref[g] = m_new
    return ps, alphas


def _accumulate(ps, alphas, values, acc_ref):
    for g, (p, alpha, v) in enumerate(zip(ps, alphas, values)):
        v1 = jnp.concatenate([v, jnp.ones_like(v)], axis=-1)
        acc_ref[g] = _lanes(alpha, 2) * acc_ref[g] + jnp.dot(p, v1, preferred_element_type=F32)


def _online_softmax(n_chunks, produce, values, s_ref, m_ref, acc_ref, finish=None):
    G = s_ref.shape[1]

    def consume(j, scores):
        return _softmax_probs(scores, m_ref)

    def accumulate(j, ps, alphas):
        _accumulate(ps, alphas, values(j), acc_ref)

    m_ref[...] = jnp.full_like(m_ref, NEG_BIG)
    acc_ref[...] = jnp.zeros_like(acc_ref)
    for g, s in enumerate(produce(0)):
        s_ref[0, g] = s

    def step(j, slot):
        cur = [s_ref[slot, g] for g in range(G)]
        nxt = produce(j + 1)
        ps, alphas = consume(j, cur)
        for g, s in enumerate(nxt):
            s_ref[1 - slot, g] = s
        accumulate(j, ps, alphas)

    def body(jj, carry):
        step(2 * jj, 0)
        step(2 * jj + 1, 1)
        return carry

    last = n_chunks - 1
    lax.fori_loop(0, last // 2, body, 0)

    @pl.when(last % 2 == 1)
    def _():
        step(last - 1, 0)

    cur = [s_ref[last % 2, g] for g in range(G)]
    if finish is not None:
        cur = [finish(s) for s in cur]
    accumulate(last, *consume(last, cur))


def _flash_kernel(*refs, nqk, decay, tq, G):
    q_refs, k_refs, v_ref = refs[:nqk], refs[nqk:2 * nqk], refs[2 * nqk]
    pos = 2 * nqk + 1
    if decay:
        ck_ref = refs[pos]
        pos += 1
    o_ref, s_ref, m_ref, acc_ref = refs[pos:pos + 4]
    i = pl.program_id(2)

    def head(ref, g, rows=slice(None)):
        if ref.shape[-1] == LANE:
            return ref[rows, :]
        return ref[rows, g * LANE:(g + 1) * LANE]

    qs = []
    for g in range(G):
        parts = [head(r, g) for r in q_refs]
        qs.append(parts[0] if nqk == 1 else jnp.concatenate(parts, axis=-1))

    def chunk(j):
        return pl.ds(pl.multiple_of(j * tq, tq), tq)

    def produce(j):
        rows = chunk(j)
        scores = []
        for g in range(G):
            parts = [head(r, g, rows) for r in k_refs]
            k = parts[0] if nqk == 1 else jnp.concatenate(parts, axis=-1)
            s = lax.dot_general(qs[g], k, (((1,), (1,)), ((), ())), preferred_element_type=F32)
            scores.append(s - ck_ref[g, :, rows] if decay else s)
        return scores

    def values(j):
        return [head(v_ref, g, chunk(j)) for g in range(G)]

    def causal(s):
        row = lax.broadcasted_iota(jnp.int32, (tq, tq), 0)
        col = lax.broadcasted_iota(jnp.int32, (tq, tq), 1)
        return jnp.where(col <= row, s, -jnp.inf)

    _online_softmax(i + 1, produce, values, s_ref, m_ref, acc_ref, finish=causal)
    for g in range(G):
        acc = acc_ref[g]
        o_ref[:, g * LANE:(g + 1) * LANE] = (acc[:, :LANE] / acc[:, LANE:]).astype(o_ref.dtype)


def _flash(q_parts, k_parts, v_part, B, S, decay_t=None, tq=512, G=4):
    tq = _tile(S, tq)
    nqk = len(q_parts)
    W = G * LANE
    args, specs = [], []
    for arr, off, _ in q_parts:
        args.append(arr)
        specs.append(pl.BlockSpec((None, tq, W), lambda b, h, i, off=off: (b, i, off // W + h)))
    for arr, off, per_head in k_parts + [v_part]:
        args.append(arr)
        if per_head:
            specs.append(pl.BlockSpec((None, S, W), lambda b, h, i, off=off: (b, 0, off // W + h)))
        else:
            specs.append(pl.BlockSpec((None, S, LANE), lambda b, h, i, off=off: (b, 0, off // LANE)))
    if decay_t is not None:
        args.append(decay_t)
        specs.append(pl.BlockSpec((None, G, 1, S), lambda b, h, i: (b, h, 0, 0)))
    return pl.pallas_call(
        functools.partial(_flash_kernel, nqk=nqk, decay=decay_t is not None, tq=tq, G=G),
        grid=(B, N_HEADS // G, S // tq),
        in_specs=specs,
        out_specs=pl.BlockSpec((None, tq, W), lambda b, h, i: (b, i, h)),
        out_shape=jax.ShapeDtypeStruct((B, S, MIX_WIDTH), BF16),
        scratch_shapes=[pltpu.VMEM((2, G, tq, tq), F32), pltpu.VMEM((G, tq, LANE), F32),
                        pltpu.VMEM((G, tq, 2 * HEAD_DIM), F32)],
        compiler_params=_params("parallel", "parallel", "arbitrary"),
        name="flash",
    )(*args)


def _gate_kernel(fg_ref, bf_ref, cumt_ref, carry_ref, *, ts):
    @pl.when(pl.program_id(1) == 0)
    def _():
        carry_ref[...] = jnp.zeros_like(carry_ref)

    z = fg_ref[...] + bf_ref[...]
    lf = jnp.minimum(z, 0.0) - jnp.log1p(jnp.exp(-jnp.abs(z)))
    hi = lf.astype(BF16)
    r1 = lf - hi.astype(F32)
    mid = r1.astype(BF16)
    lo = (r1 - mid.astype(F32)).astype(BF16)
    row = lax.broadcasted_iota(jnp.int32, (ts, ts), 0)
    col = lax.broadcasted_iota(jnp.int32, (ts, ts), 1)
    tri = jnp.where(col <= row, 1.0, 0.0).astype(BF16)
    cum = (jnp.dot(tri, hi, preferred_element_type=F32)
           + jnp.dot(tri, mid, preferred_element_type=F32)
           + jnp.dot(tri, lo, preferred_element_type=F32)) + carry_ref[...]
    carry_ref[...] = cum[ts - 1:ts, :]
    cumt_ref[...] = (cum * LOG2E).T


def _fox_gates(fg, b_f):
    B, S, _ = fg.shape
    ts = _tile(S, 512)
    bf = jnp.zeros((1, LANE), F32).at[0, :N_HEADS].set(b_f.astype(F32))
    return pl.pallas_call(
        functools.partial(_gate_kernel, ts=ts),
        grid=(B, S // ts),
        in_specs=[pl.BlockSpec((None, ts, LANE), lambda b, i: (b, i, 0)),
                  pl.BlockSpec((1, LANE), lambda b, i: (0, 0))],
        out_specs=pl.BlockSpec((None, LANE, ts), lambda b, i: (b, 0, i)),
        out_shape=jax.ShapeDtypeStruct((B, LANE, S), F32),
        scratch_shapes=[pltpu.VMEM((1, LANE), F32)],
        compiler_params=_params("parallel", "arbitrary"),
        name="fox_gates",
    )(fg, bf)


def _rope_kernel(x_ref, cos_ref, sin_ref, g_ref, o_ref):
    cos, sin, g = cos_ref[...], sin_ref[...], g_ref[...]
    for c in range(x_ref.shape[1] // LANE):
        sl = slice(c * LANE, (c + 1) * LANE)
        x = x_ref[:, sl]
        ms = jnp.sum(x * x, axis=-1, keepdims=True) * (1.0 / ROPE_DIM)
        y = x * lax.rsqrt(ms + RMS_EPS) * g
        partner = pltpu.roll(y, ROPE_DIM // 2, 1) + pltpu.roll(y, LANE - ROPE_DIM // 2, 1)
        o_ref[:, sl] = (y * cos + partner * sin).astype(o_ref.dtype)


def _rope(x, cos, sin, gain):
    B, S, N = x.shape
    ts = _tile(S, 512)
    g = jnp.zeros((1, LANE), F32).at[0, :ROPE_DIM].set(gain.astype(F32))
    return pl.pallas_call(
        _rope_kernel,
        grid=(B, S // ts),
        in_specs=[pl.BlockSpec((None, ts, N), lambda b, i: (b, i, 0)),
                  pl.BlockSpec((None, ts, LANE), lambda b, i: (b, i, 0)),
                  pl.BlockSpec((None, ts, LANE), lambda b, i: (b, i, 0)),
                  pl.BlockSpec((1, LANE), lambda b, i: (0, 0))],
        out_specs=pl.BlockSpec((None, ts, N), lambda b, i: (b, i, 0)),
        out_shape=jax.ShapeDtypeStruct((B, S, N), BF16),
        compiler_params=_params("parallel", "parallel"),
        name="rope",
    )(x, cos, sin, g)


def _mem_kernel(q_ref, kv_ref, o_ref):
    for h in range(MEM_HEADS):
        q = q_ref[:, h * LANE:(h + 1) * LANE]
        k = kv_ref[:, h * LANE:(h + 1) * LANE]
        v = kv_ref[:, MEM_WIDTH + h * LANE:MEM_WIDTH + (h + 1) * LANE]
        s = lax.dot_general(q, k, (((1,), (1,)), ((), ())), preferred_element_type=F32)
        p = jnp.exp(s - jnp.max(s, axis=-1, keepdims=True))
        o = jnp.dot(p.astype(BF16), v, preferred_element_type=F32)
        o_ref[:, h * LANE:(h + 1) * LANE] = (o / jnp.sum(p, axis=-1, keepdims=True)).astype(o_ref.dtype)


def _mem_attention(qarr, qblock, mem_kv):
    B, S, _ = qarr.shape
    n_mem = mem_kv.shape[1]
    tq = _tile(S, 1024)
    return pl.pallas_call(
        _mem_kernel,
        grid=(B, S // tq),
        in_specs=[pl.BlockSpec((None, tq, MEM_WIDTH), lambda b, i: (b, i, qblock)),
                  pl.BlockSpec((None, n_mem, 2 * MEM_WIDTH), lambda b, i: (b, 0, 0))],
        out_specs=pl.BlockSpec((None, tq, MEM_WIDTH), lambda b, i: (b, i, 0)),
        out_shape=jax.ShapeDtypeStruct((B, S, MEM_WIDTH), BF16),
        compiler_params=_params("parallel", "parallel"),
        name="mem_attn",
    )(qarr, mem_kv)


def _dil_kernel(q_ref, kp_ref, kc_ref, vp_ref, vc_ref, bias_ref, o_ref, lse_ref):
    n = DIL_BLOCK
    nsub = q_ref.shape[0] // n
    row = lax.broadcasted_iota(jnp.int32, (n, 2 * n), 0)
    col = lax.broadcasted_iota(jnp.int32, (n, 2 * n), 1)
    band = jnp.logical_and(col >= row, col <= row + n)
    first = jnp.logical_and(band, jnp.logical_or(col >= n, pl.program_id(1) > 0))
    lane = lax.broadcasted_iota(jnp.int32, (n, LANE), 1)
    dn = (((1,), (1,)), ((), ()))
    heads = [slice(h * LANE, (h + 1) * LANE) for h in range(N_HEADS)]

    def window(prev_ref, cur_ref, u, sl):
        if u == 0:
            return jnp.concatenate([prev_ref[:, sl], cur_ref[:n, sl]], axis=0)
        return cur_ref[(u - 1) * n:(u + 1) * n, sl]

    for u in range(nsub):
        rows = slice(u * n, (u + 1) * n)
        valid = first if u == 0 else band
        scores = []
        for h, sl in enumerate(heads):
            s = lax.dot_general(q_ref[rows, sl], window(kp_ref, kc_ref, u, sl), dn,
                                preferred_element_type=F32) + bias_ref[h]
            scores.append(jnp.where(valid, s, -jnp.inf))
        ms = [jnp.max(s, axis=-1, keepdims=True) for s in scores]
        ps = [jnp.exp2(s - m).astype(BF16) for s, m in zip(scores, ms)]
        lse_all = jnp.zeros((n, LANE), F32)
        for h, sl in enumerate(heads):
            v2 = window(vp_ref, vc_ref, u, sl)
            pv = jnp.dot(ps[h], jnp.concatenate([v2, jnp.ones_like(v2)], axis=-1),
                         preferred_element_type=F32)
            den = pv[:, LANE:]
            o_ref[h, rows, :] = pv[:, :LANE] / den
            lse_all = jnp.where(lane == h, ms[h] + jnp.log2(den), lse_all)
        lse_ref[rows, :] = lse_all


def _dil_attention(pg, bias):
    N, Ls, _ = pg.shape
    n = DIL_BLOCK
    nsub = max(u for u in (1, 2, 4) if Ls % (u * n) == 0)
    tr = nsub * n

    def cur(part):
        return pl.BlockSpec((None, tr, MIX_WIDTH), lambda s, i: (s, i, part))

    def prev(part):
        return pl.BlockSpec((None, n, MIX_WIDTH), lambda s, i: (s, jnp.maximum(i * nsub - 1, 0), part))

    return pl.pallas_call(
        _dil_kernel,
        grid=(N, Ls // tr),
        in_specs=[cur(0), prev(1), cur(1), prev(2), cur(2),
                  pl.BlockSpec((N_HEADS, n, 2 * n), lambda s, i: (0, 0, 0))],
        out_specs=[pl.BlockSpec((None, N_HEADS, tr, LANE), lambda s, i: (s, 0, i, 0)),
                   pl.BlockSpec((None, tr, LANE), lambda s, i: (s, i, 0))],
        out_shape=[jax.ShapeDtypeStruct((N, N_HEADS, Ls, LANE), F32),
                   jax.ShapeDtypeStruct((N, Ls, LANE), F32)],
        compiler_params=_params("parallel", "arbitrary"),
        name="dil_attn",
    )(pg, pg, pg, pg, pg, bias)


def _dil_combine_kernel(o1_ref, o2_ref, o3_ref, l1_ref, l2_ref, l3_ref, out_ref, stage_ref, *, d2, d3):
    n = l3_ref.shape[1]
    q = d3 // d2
    for r in range(d3):
        rows = [(pl.ds(r, n, stride=d3),), (r % d2, pl.ds(r // d2, n, stride=q)), (r, slice(None))]
        ls = [l_ref[idx + (slice(None),)]
              for l_ref, idx in zip((l1_ref, l2_ref, l3_ref), rows)]
        m = functools.reduce(jnp.maximum, ls)
        es = [jnp.exp2(l - m) for l in ls]
        inv = 1.0 / functools.reduce(jnp.add, es)
        ws = [e * inv for e in es]
        for h in range(N_HEADS):
            o = (ws[0][:, h:h + 1] * o1_ref[h, pl.ds(r, n, stride=d3), :]
                 + ws[1][:, h:h + 1] * o2_ref[r % d2, h, pl.ds(r // d2, n, stride=q), :]
                 + ws[2][:, h:h + 1] * o3_ref[r, h, :, :])
            stage_ref[h, pl.ds(r, n, stride=d3), :] = o
    for h in range(N_HEADS):
        out_ref[:, h * LANE:(h + 1) * LANE] = stage_ref[h].astype(out_ref.dtype)


def _dil_combine(outs, lses, B, S, tm=512):
    (_, d1), (_, d2), (_, d3) = DIL_GROUPS
    assert d1 == 1 and d3 % d2 == 0
    tm = _tile(S, tm)
    H = N_HEADS
    o_specs = [pl.BlockSpec((None, H, tm, LANE), lambda b, i: (b, 0, i, 0)),
               pl.BlockSpec((None, d2, H, tm // d2, LANE), lambda b, i: (b, 0, 0, i, 0)),
               pl.BlockSpec((None, d3, H, tm // d3, LANE), lambda b, i: (b, 0, 0, i, 0))]
    l_specs = [pl.BlockSpec((None, tm, LANE), lambda b, i: (b, i, 0)),
               pl.BlockSpec((None, d2, tm // d2, LANE), lambda b, i: (b, 0, i, 0)),
               pl.BlockSpec((None, d3, tm // d3, LANE), lambda b, i: (b, 0, i, 0))]
    return pl.pallas_call(
        functools.partial(_dil_combine_kernel, d2=d2, d3=d3),
        grid=(B, S // tm),
        in_specs=o_specs + l_specs,
        out_specs=pl.BlockSpec((None, tm, MIX_WIDTH), lambda b, i: (b, i, 0)),
        out_shape=jax.ShapeDtypeStruct((B, S, MIX_WIDTH), BF16),
        scratch_shapes=[pltpu.VMEM((H, tm, LANE), F32)],
        compiler_params=_params("parallel", "parallel"),
        name="dil_combine",
    )(outs[0].reshape(B, H, S, LANE), outs[1].reshape(B, d2, H, S // d2, LANE),
      outs[2].reshape(B, d3, H, S // d3, LANE),
      lses[0].reshape(B, S, LANE), lses[1].reshape(B, d2, S // d2, LANE),
      lses[2].reshape(B, d3, S // d3, LANE))


def _dsa_topk_kernel(qi_ref, wi_ref, ki_ref, mask_ref, key_ref, jsel_ref, *, tq, tk, S, n_sel):
    i = pl.program_id(1)
    q0 = i * tq
    n_chunks = (q0 + tq + tk - 1) // tk
    dn = (((1,), (1,)), ((), ()))
    krow = lax.broadcasted_iota(jnp.int32, (tk, tq), 0)
    qpos = q0 + lax.broadcasted_iota(jnp.int32, (tk, tq), 1)

    wi_t = (wi_ref[...] * (IDX_HEADS ** -0.5 * IDX_DIM ** -0.5)).T

    def idx_body(c, carry):
        off = pl.multiple_of(c * tk, tk)
        kic = ki_ref[pl.ds(off, tk), :]
        k_lo = jnp.where(lax.broadcasted_iota(jnp.int32, kic.shape, 1) < IDX_DIM, kic, 0.0)
        k_sides = (k_lo.astype(BF16), pltpu.roll(k_lo, IDX_DIM, 1).astype(BF16))
        acc = jnp.zeros((tk, tq), F32)
        for h in range(IDX_HEADS):
            d = lax.dot_general(k_sides[h % 2], qi_ref[:, (h // 2) * LANE:(h // 2 + 1) * LANE], dn,
                                preferred_element_type=F32)
            acc = acc + wi_t[IDX_DIM + h:IDX_DIM + h + 1, :] * jnp.maximum(d, 0.0)
        score = jnp.where(off + krow <= qpos, acc, -jnp.inf)
        bits = pltpu.bitcast(score, jnp.int32)
        key_ref[pl.ds(off, tk), :] = bits ^ ((bits >> 31) & jnp.int32(0x7FFFFFFF))
        return carry

    lax.fori_loop(0, n_chunks, idx_body, 0)

    def count(pred):
        sub = 64
        def body(c, acc):
            off = pl.multiple_of(c * tk, tk)
            ones = jnp.where(pred(key_ref[pl.ds(off, tk), :], off), 1.0, 0.0)
            return acc + jnp.sum(ones.reshape(tk // sub, sub, tq), axis=0)

        acc = lax.fori_loop(0, n_chunks, body, jnp.zeros((sub, tq), F32))
        return jnp.sum(acc, axis=0, keepdims=True)

    k_sel = float(n_sel)
    c0 = count(lambda k, off: k >= 0)
    lo0 = jnp.where(c0 >= k_sel, jnp.int32(0), jnp.int32(-2 ** 31))
    n0 = jnp.where(c0 >= k_sel, c0, (n_chunks * tk).astype(F32))

    def bis_cond(state):
        it, _, n_lo = state
        return jnp.logical_and(it < 31, jnp.max(jnp.abs(n_lo - k_sel)) > 0.0)

    def bis_body(state):
        it, lo, n_lo = state
        cand = lo + (jnp.int32(1) << (30 - it))
        cnt = count(lambda k, off: k >= cand)
        take = cnt >= k_sel
        return it + 1, jnp.where(take, cand, lo), jnp.where(take, cnt, n_lo)

    _, thr, n_ge = lax.while_loop(bis_cond, bis_body, (jnp.int32(0), lo0, n0))
    n_bits = max(1, (S - 1).bit_length())
    jsel_ref[...] = jnp.full_like(jsel_ref, S)

    @pl.when(jnp.max(n_ge) > k_sel)
    def _():
        need = k_sel - count(lambda k, off: k > thr)

        def tie_body(it, jlo):
            cand = jlo + (jnp.int32(1) << (n_bits - 1 - it))
            below = count(lambda k, off: jnp.logical_and(k == thr, off + krow < cand))
            return jnp.where(below < need, cand, jlo)

        jsel = lax.fori_loop(0, n_bits, tie_body, jnp.zeros((1, tq), jnp.int32))
        jsel_ref[...] = jnp.broadcast_to(jsel, jsel_ref.shape)

    jsel = jsel_ref[:1, :]
    mask_ref[...] = jnp.full_like(mask_ref, -jnp.inf)

    def mask_body(c, carry):
        off = pl.multiple_of(c * tk, tk)
        k = key_ref[pl.ds(off, tk), :]
        kpos = off + krow
        tie = jnp.logical_and(k == thr, kpos <= jsel)
        sel = jnp.logical_and(jnp.logical_or(k > thr, tie), kpos <= qpos)
        mask_ref[:, pl.ds(off, tk)] = jnp.where(sel, 0.0, -jnp.inf).T.astype(mask_ref.dtype)
        return carry

    lax.fori_loop(0, n_chunks, mask_body, 0)


def _dsa_select(pm, aux, B, S, n_sel, tq=256, tk=512):
    tq, tk = _tile(S, tq), _tile(S, tk)
    qiw = IDX_HEADS * IDX_DIM
    return pl.pallas_call(
        functools.partial(_dsa_topk_kernel, tq=tq, tk=tk, S=S, n_sel=n_sel),
        grid=(B, S // tq),
        in_specs=[pl.BlockSpec((None, tq, qiw), lambda b, i: (b, i, MIX_WIDTH // qiw)),
                  pl.BlockSpec((None, tq, LANE), lambda b, i: (b, i, 0)),
                  pl.BlockSpec((None, S, LANE), lambda b, i: (b, 0, 0))],
        out_specs=pl.BlockSpec((None, tq, S), lambda b, i: (b, i, 0)),
        out_shape=jax.ShapeDtypeStruct((B, S, S), BF16),
        scratch_shapes=[pltpu.VMEM((S, tq), jnp.int32), pltpu.VMEM((8, tq), jnp.int32)],
        compiler_params=_params("parallel", "arbitrary"),
        name="dsa_select",
    )(pm, aux, aux)


def _dsa_attn_kernel(q_ref, k_ref, v_ref, mask_ref, gt_ref, o_ref, s_ref, m_ref, acc_ref, *, tq, tk, S):
    i = pl.program_id(2)
    q0 = i * tq
    n_chunks = (q0 + tq + tk - 1) // tk
    R = q_ref.shape[1] // LANE
    nblk = tq // LANE
    dn = (((1,), (1,)), ((), ()))
    qs = [q_ref[:, r * LANE:(r + 1) * LANE] for r in range(R)]

    def produce(c):
        off = pl.multiple_of(c * tk, tk)
        kc = k_ref[pl.ds(off, tk), :]
        mask = mask_ref[:, pl.ds(off, tk)].astype(F32)
        st = S - q0 + off
        scores = []
        for r in range(R):
            bias = jnp.concatenate(
                [gt_ref[r, :, pl.ds(pl.multiple_of(st + (nblk - 1 - a) * LANE, LANE), tk)]
                 for a in range(nblk)], axis=0)
            scores.append(lax.dot_general(qs[r], kc, dn, preferred_element_type=F32) + bias + mask)
        return scores

    def values(c):
        return [v_ref[pl.ds(pl.multiple_of(c * tk, tk), tk), :]] * R

    _online_softmax(n_chunks, produce, values, s_ref, m_ref, acc_ref)
    for r in range(R):
        acc = acc_ref[r]
        o_ref[:, r * LANE:(r + 1) * LANE] = (acc[:, :LANE] / acc[:, LANE:]).astype(o_ref.dtype)


def _dsa_attention(pm, mask, gt, B, S, tq=256, tk=512):
    tq, tk = _tile(S, tq), _tile(S, tk)
    R = N_HEADS // DSA_KV_HEADS
    kblk = (MIX_WIDTH + IDX_HEADS * IDX_DIM) // LANE
    return pl.pallas_call(
        functools.partial(_dsa_attn_kernel, tq=tq, tk=tk, S=S),
        grid=(DSA_KV_HEADS, B, S // tq),
        in_specs=[pl.BlockSpec((None, tq, R * LANE), lambda g, b, i: (b, i, g)),
                  pl.BlockSpec((None, S, LANE), lambda g, b, i: (b, 0, kblk + g)),
                  pl.BlockSpec((None, S, LANE), lambda g, b, i: (b, 0, kblk + DSA_KV_HEADS + g)),
                  pl.BlockSpec((None, tq, S), lambda g, b, i: (b, i, 0)),
                  pl.BlockSpec((R, LANE, gt.shape[2]), lambda g, b, i: (g, 0, 0))],
        out_specs=pl.BlockSpec((None, tq, R * LANE), lambda g, b, i: (b, i, g)),
        out_shape=jax.ShapeDtypeStruct((B, S, MIX_WIDTH), BF16),
        scratch_shapes=[pltpu.VMEM((2, R, tq, tk), F32), pltpu.VMEM((R, tq, LANE), F32),
                        pltpu.VMEM((R, tq, 2 * HEAD_DIM), F32)],
        compiler_params=_params("parallel", "parallel", "arbitrary"),
        name="dsa_attn",
    )(pm, pm, pm, mask, gt)


def _t5_bucket(dist):
    n = jnp.maximum(dist, 0)
    exact = REL_BUCKETS // 2
    nf = jnp.maximum(n, 1).astype(F32)
    large = exact + (jnp.log(nf / exact) / math.log(REL_MAX_DIST / exact)
                     * (REL_BUCKETS - exact)).astype(jnp.int32)
    large = jnp.minimum(large, REL_BUCKETS - 1)
    return jnp.where(n < exact, n, large)


def _t5_bias(t5_table, dist):
    onehot = jax.nn.one_hot(_t5_bucket(dist), REL_BUCKETS, dtype=F32)
    table = jnp.einsum('...b,bh->h...', onehot, t5_table.astype(F32), precision=lax.Precision.HIGHEST)
    return table * LOG2E


def _dil_bias(t5_table, dil):
    n = DIL_BLOCK
    rel = jnp.arange(n)[:, None] + n - jnp.arange(2 * n)[None, :]
    return _t5_bias(t5_table, rel * dil)


def _dsa_bias_table(t5_table, S, tq, tk):
    m = jnp.arange(S + tq + tk)[None, :]
    i = jnp.arange(LANE)[:, None]
    return _t5_bias(t5_table, S + tq - LANE - m + i)


def _headnorm_cols(spec):
    eg, ef = [], []
    for gain, count, scale in spec:
        if gain is None:
            eg.append(jnp.ones((count * LANE,), F32))
            ef.append(jnp.zeros((count * LANE,), F32))
        else:
            eg.append(jnp.tile(gain.astype(F32) * scale, count))
            ef.append(jnp.ones((count * LANE,), F32))
    return jnp.concatenate(eg), jnp.concatenate(ef)


def _pad_cols(w, width):
    return jnp.pad(w, ((0, 0), (0, width - w.shape[1])))


def _fox_layer(x, norm_g, w_in, b_f, qk_g, mem_g):
    B, S, _ = x.shape
    W = MIX_WIDTH
    w_main = jnp.concatenate([w_in[:, :3 * W], w_in[:, 3 * W + N_HEADS:]], axis=1).astype(BF16)
    w_gate = _pad_cols(w_in[:, 3 * W:3 * W + N_HEADS], LANE).astype(BF16)
    eg, ef = _headnorm_cols([(qk_g[0], N_HEADS, HEAD_DIM ** -0.5 * LOG2E), (qk_g[1], N_HEADS, 1.0),
                             (None, N_HEADS, 1.0), (mem_g, MEM_HEADS, HEAD_DIM ** -0.5)])
    pm = _proj(x, norm_g, w_main, eg, ef)
    fg = _proj(x, norm_g, w_gate, out_dtype=F32)
    cum_t = _fox_gates(fg, b_f)
    mix = _flash([(pm, 0, True)], [(pm, W, True)], (pm, 2 * W, True), B, S,
                 decay_t=cum_t.reshape(B, LANE, 1, S))
    return mix, pm, 3 * W // MEM_WIDTH


def _mla_layer(x, positions, norm_g, w_in, q_norm, w_uq, kv_norm, w_ukv, nope_g, rope_g, mem_g):
    B, S, _ = x.shape
    scale = (NOPE_DIM + ROPE_DIM) ** -0.5 * LOG2E
    lat = Q_LORA + KV_LORA
    w_lat = jnp.concatenate([w_in[:, :lat], _pad_cols(w_in[:, lat:lat + ROPE_DIM], LANE)],
                            axis=1).astype(BF16)
    w_mem = w_in[:, lat + ROPE_DIM:].astype(BF16)
    pl_ = _proj(x, norm_g, w_lat, out_dtype=F32, tn=lat + LANE)
    eg, ef = _headnorm_cols([(mem_g, MEM_HEADS, HEAD_DIM ** -0.5)])
    pmem = _proj(x, norm_g, w_mem, eg, ef)

    uq = w_uq.reshape(Q_LORA, N_HEADS, NOPE_DIM + ROPE_DIM)
    w_qn = uq[:, :, :NOPE_DIM].reshape(Q_LORA, MIX_WIDTH).astype(BF16)
    w_qr = jnp.pad(uq[:, :, NOPE_DIM:], ((0, 0), (0, 0), (0, LANE - ROPE_DIM))
                   ).reshape(Q_LORA, N_HEADS * LANE).astype(BF16)
    ukv = w_ukv.reshape(KV_LORA, N_HEADS, NOPE_DIM + HEAD_DIM)
    w_kv = jnp.concatenate([ukv[:, :, :NOPE_DIM].reshape(KV_LORA, MIX_WIDTH),
                            ukv[:, :, NOPE_DIM:].reshape(KV_LORA, MIX_WIDTH)], axis=1).astype(BF16)
    eg, ef = _headnorm_cols([(nope_g[0], N_HEADS, scale)])
    qn = _proj(pl_, q_norm, w_qn, eg, ef, kblock=0, tn=2048)
    qr_raw = _proj(pl_, q_norm, w_qr, kblock=0, out_dtype=F32, tn=1024)
    eg, ef = _headnorm_cols([(nope_g[1], N_HEADS, 1.0), (None, N_HEADS, 1.0)])
    kv = _proj(pl_, kv_norm, w_kv, eg, ef, kblock=1, tn=2048)

    half = ROPE_DIM // 2
    inv = ROPE_THETA ** (-jnp.arange(half, dtype=F32) / half)
    ang = positions.astype(F32)[..., None] * inv
    cos, sin = jnp.cos(ang), jnp.sin(ang)
    zero = jnp.zeros((B, S, LANE - ROPE_DIM), F32)
    cos_t = jnp.concatenate([cos, cos, zero], axis=-1)
    sin_t = jnp.concatenate([-sin, sin, zero], axis=-1)
    qr = _rope(qr_raw, cos_t, sin_t, rope_g[0] * scale)
    kr = _rope(pl_[:, :, lat:], cos_t, sin_t, rope_g[1])
    mix = _flash([(qn, 0, True), (qr, 0, True)], [(kv, 0, True), (kr, 0, False)],
                 (kv, MIX_WIDTH, True), B, S)
    return mix, pmem, 0


def _dil_layer(x, norm_g, w_in, qk_g, t5_table, mem_g):
    B, S, _ = x.shape
    W = MIX_WIDTH
    outs, lses = [], []
    for gi, (win, dil) in enumerate(DIL_GROUPS):
        assert win // dil == DIL_BLOCK and (S // dil) % DIL_BLOCK == 0
        w_g = w_in[:, gi * 3 * W:(gi + 1) * 3 * W].astype(BF16)
        eg, ef = _headnorm_cols([(qk_g[gi, 0], N_HEADS, HEAD_DIM ** -0.5 * LOG2E),
                                 (qk_g[gi, 1], N_HEADS, 1.0), (None, N_HEADS, 1.0)])
        pg = _proj(x, norm_g, w_g, eg, ef, dil=dil, tn=1024)
        o, lse = _dil_attention(pg.reshape(B * dil, S // dil, 3 * W), _dil_bias(t5_table, dil))
        outs.append(o)
        lses.append(lse)
    mix = _dil_combine(outs, lses, B, S)
    eg, ef = _headnorm_cols([(mem_g, MEM_HEADS, HEAD_DIM ** -0.5)])
    pmem = _proj(x, norm_g, w_in[:, len(DIL_GROUPS) * 3 * W:].astype(BF16), eg, ef)
    return mix, pmem, 0


def _dsa_layer(x, norm_g, w_in, qk_g, t5_table, mem_g):
    B, S, _ = x.shape
    W = MIX_WIDTH
    kvw = DSA_KV_HEADS * HEAD_DIM
    o_k, o_v, o_qi = W, W + kvw, W + 2 * kvw
    o_ki = o_qi + IDX_HEADS * IDX_DIM
    o_wi = o_ki + IDX_DIM
    o_mem = o_wi + IDX_HEADS
    qiw = IDX_HEADS * IDX_DIM
    w_main = jnp.concatenate([w_in[:, :W], w_in[:, o_qi:o_ki], w_in[:, o_k:o_qi], w_in[:, o_mem:]],
                             axis=1).astype(BF16)
    w_aux = _pad_cols(w_in[:, o_ki:o_mem], LANE).astype(BF16)
    eg, ef = _headnorm_cols([(qk_g[0], N_HEADS, HEAD_DIM ** -0.5 * LOG2E), (None, qiw // LANE, 1.0),
                             (qk_g[1], DSA_KV_HEADS, 1.0), (None, DSA_KV_HEADS, 1.0),
                             (mem_g, MEM_HEADS, HEAD_DIM ** -0.5)])
    pm = _proj(x, norm_g, w_main, eg, ef)
    aux = _proj(x, norm_g, w_aux, out_dtype=F32)
    mask = _dsa_select(pm, aux, B, S, min(TOPK_MAX, S // 4), tq=512)
    tq, tk = _tile(S, 512), _tile(S, 512)
    mix = _dsa_attention(pm, mask, _dsa_bias_table(t5_table, S, tq, tk), B, S, tq, tk)
    return mix, pm, (W + qiw + 2 * kvw) // MEM_WIDTH


def kernel(x, mem, positions, t5_table, ffn_norm, ffn_w_gate, ffn_w_up, ffn_w_down, attn_norm,
           mem_norm, mem_w_kv, mem_qk_g, w_out, a_w_in, a_b_f, a_qk_g, b_w_in, b_q_norm, b_w_uq,
           b_kv_norm, b_w_ukv, b_nope_g, b_rope_g, c_w_in, c_qk_g, d_w_in, d_qk_g):
    B, S, D = x.shape
    depth = ffn_norm.shape[0]
    n_mixers = 4

    def ffn(xc, i, k, next_gain=None):
        return _ffn(xc.reshape(B * S, D), ffn_norm[i, k], ffn_w_gate[i, k].astype(BF16),
                    ffn_w_up[i, k].astype(BF16), ffn_w_down[i, k].astype(BF16), next_gain)

    for i in range(depth):
        m, j = i % n_mixers, i // n_mixers
        x, h = ffn(x, i, 0, attn_norm[i])
        x, h = x.reshape(B, S, D), h.reshape(B, S, D)
        mem_g = mem_qk_g[i]
        if m == 0:
            mix, qarr, qblock = _fox_layer(h, None, a_w_in[j], a_b_f[j], a_qk_g[j], mem_g[0])
        elif m == 1:
            mix, qarr, qblock = _mla_layer(h, positions, None, b_w_in[j], b_q_norm[j],
                                           b_w_uq[j], b_kv_norm[j], b_w_ukv[j], b_nope_g[j],
                                           b_rope_g[j], mem_g[0])
        elif m == 2:
            mix, qarr, qblock = _dil_layer(h, None, c_w_in[j], c_qk_g[j], t5_table, mem_g[0])
        else:
            mix, qarr, qblock = _dsa_layer(h, None, d_w_in[j], d_qk_g[j], t5_table, mem_g[0])
        eg, ef = _headnorm_cols([(mem_g[1], MEM_HEADS, 1.0), (None, MEM_HEADS, 1.0)])
        mem_kv = _proj(mem, mem_norm[i], mem_w_kv[i].astype(BF16), eg, ef)
        mo = _mem_attention(qarr, qblock, mem_kv)
        wo = w_out[i].astype(BF16)
        x = _out_proj(x.reshape(B * S, D), mix.reshape(B * S, MIX_WIDTH),
                      mo.reshape(B * S, MEM_WIDTH), wo[:MIX_WIDTH], wo[MIX_WIDTH:]).reshape(B, S, D)
        x = ffn(x, i, 1).reshape(B, S, D)
    return x
```

```python
import functools
import math

import jax
import jax.numpy as jnp
from jax import lax
from jax.experimental import pallas as pl
from jax.experimental.pallas import tpu as pltpu

F32 = jnp.float32
BF16 = jnp.bfloat16

LANE = 128
D_MODEL = 2048
N_HEADS = 16
HEAD_DIM = 128
MIX_WIDTH = N_HEADS * HEAD_DIM
MEM_HEADS = 4
MEM_WIDTH = MEM_HEADS * HEAD_DIM
D_FF = 5632
RMS_EPS = 1e-6
REL_BUCKETS = 32
REL_MAX_DIST = 2048
Q_LORA = 512
KV_LORA = 512
NOPE_DIM = 128
ROPE_DIM = 64
ROPE_THETA = 10000.0
DIL_GROUPS = ((128, 1), (512, 4), (2048, 16))
DIL_BLOCK = 128
PERM_ROWS = 256
DSA_KV_HEADS = 4
IDX_HEADS = 16
IDX_DIM = 64
TOPK_MAX = 256
VMEM_LIMIT = 56 * 1024 * 1024
FFN_VMEM_LIMIT = 62 * 1024 * 1024
NEG_BIG = -1e30
LOG2E = math.log2(math.e)


def _params(*sem):
    return pltpu.CompilerParams(dimension_semantics=sem, vmem_limit_bytes=VMEM_LIMIT)


def _tile(n, pref):
    t = min(n, pref)
    while n % t:
        t //= 2
    return t


def _ffn_kernel(x_ref, g_ref, wg_ref, wu_ref, wd_ref, o_ref, h_ref):
    _ffn_body(x_ref, g_ref, wg_ref, wu_ref, wd_ref, o_ref, h_ref, ())


def _ffn_kernel_with_next(x_ref, g_ref, wg_ref, wu_ref, wd_ref, ng_ref, o_ref, hn_ref, h_ref):
    _ffn_body(x_ref, g_ref, wg_ref, wu_ref, wd_ref, o_ref, h_ref, (ng_ref, hn_ref))


def _ffn_body(x_ref, g_ref, wg_ref, wu_ref, wd_ref, o_ref, h_ref, next_refs):
    j = pl.program_id(1)
    last = pl.num_programs(1) - 1
    tm = x_ref.shape[0]
    halves = [slice(0, tm // 2), slice(tm // 2, tm)]

    def down(h):
        g = jnp.dot(h, wg_ref[...], preferred_element_type=F32)
        u = jnp.dot(h, wu_ref[...], preferred_element_type=F32)
        a = (g * jax.nn.sigmoid(g) * u).astype(BF16)
        return jnp.dot(a, wd_ref[...], preferred_element_type=F32)

    @pl.when(j == 0)
    def _():
        for rows in halves:
            x = x_ref[rows, :]
            ms = jnp.mean(x * x, axis=-1, keepdims=True)
            h = (x * lax.rsqrt(ms + RMS_EPS) * g_ref[...]).astype(BF16)
            h_ref[rows, :] = h
            o_ref[rows, :] = down(h)

    @pl.when(jnp.logical_and(j > 0, j < last))
    def _():
        o_ref[...] += down(h_ref[...])

    @pl.when(jnp.logical_and(j > 0, j == last))
    def _():
        for rows in halves:
            y = x_ref[rows, :] + 0.5 * (o_ref[rows, :] + down(h_ref[rows, :]))
            o_ref[rows, :] = y
            if next_refs:
                ng_ref, hn_ref = next_refs
                ms = jnp.mean(y * y, axis=-1, keepdims=True)
                hn_ref[rows, :] = (y * lax.rsqrt(ms + RMS_EPS) * ng_ref[...]).astype(BF16)


def _ffn(x2, gain, wg, wu, wd, next_gain=None):
    T, D = x2.shape
    F = wg.shape[1]
    tm, tf = _tile(T, 1024), _tile(F, 512)
    assert F // tf >= 2 and tm % 16 == 0
    row_spec = pl.BlockSpec((tm, D), lambda i, j: (i, 0))
    vec_spec = pl.BlockSpec((1, D), lambda i, j: (0, 0))
    in_specs = [row_spec, vec_spec,
                pl.BlockSpec((D, tf), lambda i, j: (0, j)),
                pl.BlockSpec((D, tf), lambda i, j: (0, j)),
                pl.BlockSpec((tf, D), lambda i, j: (j, 0))]
    args = [x2, gain.reshape(1, D), wg, wu, wd]
    out_specs, out_shape = row_spec, jax.ShapeDtypeStruct((T, D), F32)
    kern = _ffn_kernel
    if next_gain is not None:
        in_specs.append(vec_spec)
        args.append(next_gain.reshape(1, D).astype(F32))
        out_specs = [row_spec, row_spec]
        out_shape = [out_shape, jax.ShapeDtypeStruct((T, D), BF16)]
        kern = _ffn_kernel_with_next
    return pl.pallas_call(
        kern,
        grid=(T // tm, F // tf),
        in_specs=in_specs,
        out_specs=out_specs,
        out_shape=out_shape,
        scratch_shapes=[pltpu.VMEM((tm, D), BF16)],
        compiler_params=pltpu.CompilerParams(dimension_semantics=("parallel", "arbitrary"),
                                             vmem_limit_bytes=FFN_VMEM_LIMIT),
        name="ffn",
    )(*args)


def _proj_kernel(x_ref, g_ref, w_ref, eg_ref, ef_ref, o_ref, *scratch, epilogue, dil, prenormed):
    scratch = list(scratch)
    if prenormed:
        h_ref = x_ref
    else:
        h_ref = scratch.pop(0)

        @pl.when(pl.program_id(2) == 0)
        def _():
            x = x_ref[...].astype(F32)
            ms = jnp.mean(x * x, axis=-1, keepdims=True)
            h_ref[...] = (x * lax.rsqrt(ms + RMS_EPS) * g_ref[...]).astype(BF16)

    tm = h_ref.shape[0]
    nb = PERM_ROWS // dil
    if dil > 1:
        hp_ref = scratch.pop(0)

        @pl.when(pl.program_id(2) == 0)
        def _():
            dst = lax.broadcasted_iota(jnp.int32, (PERM_ROWS, PERM_ROWS), 0)
            src = lax.broadcasted_iota(jnp.int32, (PERM_ROWS, PERM_ROWS), 1)
            perm = jnp.where(dst == (src % dil) * nb + src // dil, 1.0, 0.0).astype(BF16)
            for b in range(tm // PERM_ROWS):
                blk = slice(b * PERM_ROWS, (b + 1) * PERM_ROWS)
                hp_ref[blk, :] = jnp.dot(perm, h_ref[blk, :],
                                         preferred_element_type=F32).astype(BF16)

        h_ref = hp_ref
    parts = 2 if tm % (2 * PERM_ROWS) == 0 else 1
    rows_per = tm // parts
    for p in range(parts):
        rows = slice(p * rows_per, (p + 1) * rows_per)
        acc = jnp.dot(h_ref[rows, :], w_ref[...], preferred_element_type=F32)
        for c in range(acc.shape[1] // LANE):
            sl = slice(c * LANE, (c + 1) * LANE)
            y = acc[:, sl]
            if epilogue:
                ms = jnp.mean(y * y, axis=-1, keepdims=True)
                scale = jnp.where(ef_ref[:, sl] > 0.0, lax.rsqrt(ms + RMS_EPS), 1.0)
                y = y * scale * eg_ref[:, sl]
            y = y.astype(o_ref.dtype)
            if dil == 1:
                o_ref[0, rows, sl] = y
            else:
                for b in range(rows_per // PERM_ROWS):
                    a0 = (p * (rows_per // PERM_ROWS) + b) * nb
                    for r in range(dil):
                        o_ref[r, a0:a0 + nb, sl] = y[b * PERM_ROWS + r * nb:b * PERM_ROWS + (r + 1) * nb, :]


def _proj(x, gain, w, eg=None, ef=None, *, dil=1, kblock=0, out_dtype=BF16, tm=1024, tn=512):
    B, S, C = x.shape
    K, N = w.shape
    tm, tn = _tile(S, tm), _tile(N, tn)
    epilogue = eg is not None
    if not epilogue:
        eg = jnp.ones((1, N), F32)
        ef = jnp.zeros((1, N), F32)
    assert dil == 1 or (tm % PERM_ROWS == 0 and PERM_ROWS % (16 * dil) == 0)
    prenormed = gain is None
    if prenormed:
        assert x.dtype == BF16
        gain = jnp.ones((K,), F32)
    out = pl.pallas_call(
        functools.partial(_proj_kernel, epilogue=epilogue, dil=dil, prenormed=prenormed),
        grid=(B, S // tm, N // tn),
        in_specs=[
            pl.BlockSpec((None, tm, K), lambda b, i, j: (b, i, kblock)),
            pl.BlockSpec((1, K), lambda b, i, j: (0, 0)),
            pl.BlockSpec((K, tn), lambda b, i, j: (0, j)),
            pl.BlockSpec((1, tn), lambda b, i, j: (0, j)),
            pl.BlockSpec((1, tn), lambda b, i, j: (0, j)),
        ],
        out_specs=pl.BlockSpec((None, dil, tm // dil, tn), lambda b, i, j: (b, 0, i, j)),
        out_shape=jax.ShapeDtypeStruct((B, dil, S // dil, N), out_dtype),
        scratch_shapes=[pltpu.VMEM((tm, K), BF16)] * ((not prenormed) + (dil > 1)),
        compiler_params=_params("parallel", "parallel", "arbitrary"),
        name="proj",
    )(x, gain.reshape(1, K).astype(F32), w, eg.reshape(1, N), ef.reshape(1, N))
    return out.reshape(B, S, N) if dil == 1 else out


def _out_kernel(x_ref, a_ref, b_ref, wa_ref, wb_ref, o_ref):
    acc = jnp.dot(a_ref[...], wa_ref[...], preferred_element_type=F32)
    acc += jnp.dot(b_ref[...], wb_ref[...], preferred_element_type=F32)
    o_ref[...] = x_ref[...] + acc


def _out_proj(x2, mix2, mo2, wa, wb):
    T, D = x2.shape
    tm, tn = _tile(T, 1024), _tile(D, 1024)
    ka, kb = mix2.shape[1], mo2.shape[1]
    return pl.pallas_call(
        _out_kernel,
        grid=(T // tm, D // tn),
        in_specs=[
            pl.BlockSpec((tm, tn), lambda i, j: (i, j)),
            pl.BlockSpec((tm, ka), lambda i, j: (i, 0)),
            pl.BlockSpec((tm, kb), lambda i, j: (i, 0)),
            pl.BlockSpec((ka, tn), lambda i, j: (0, j)),
            pl.BlockSpec((kb, tn), lambda i, j: (0, j)),
        ],
        out_specs=pl.BlockSpec((tm, tn), lambda i, j: (i, j)),
        out_shape=jax.ShapeDtypeStruct((T, D), F32),
        compiler_params=_params("parallel", "arbitrary"),
        name="out_proj",
    )(x2, mix2, mo2, wa, wb)


def _lanes(x, reps):
    return x if reps == 1 else jnp.concatenate([x] * reps, axis=-1)


def _softmax_probs(scores, m_ref):
    ps, alphas = [], []
    for g, s in enumerate(scores):
        m_old = m_ref[g]
        m_new = jnp.maximum(m_old, jnp.max(s, axis=-1, keepdims=True))
        ps.append(jnp.exp2(s - _lanes(m_new, s.shape[1] // LANE)).astype(BF16))
        alphas.append(jnp.exp2(m_old - m_new))
        m_ref[g] = m_new
    return ps, alphas


def _accumulate(ps, alphas, values, acc_ref):
    for g, (p, alpha, v) in enumerate(zip(ps, alphas, values)):
        v1 = jnp.concatenate([v, jnp.ones_like(v)], axis=-1)
        acc_ref[g] = _lanes(alpha, 2) * acc_ref[g] + jnp.dot(p, v1, preferred_element_type=F32)


def _online_softmax(n_chunks, produce, values, s_ref, m_ref, acc_ref, finish=None):
    G = s_ref.shape[1]

    def consume(j, scores):
        return _softmax_probs(scores, m_ref)

    def accumulate(j, ps, alphas):
        _accumulate(ps, alphas, values(j), acc_ref)

    m_ref[...] = jnp.full_like(m_ref, NEG_BIG)
    acc_ref[...] = jnp.zeros_like(acc_ref)
    for g, s in enumerate(produce(0)):
        s_ref[0, g] = s

    def step(j, slot):
        cur = [s_ref[slot, g] for g in range(G)]
        nxt = produce(j + 1)
        ps, alphas = consume(j, cur)
        for g, s in enumerate(nxt):
            s_ref[1 - slot, g] = s
        accumulate(j, ps, alphas)

    def body(jj, carry):
        step(2 * jj, 0)
        step(2 * jj + 1, 1)
        return carry

    last = n_chunks - 1
    lax.fori_loop(0, last // 2, body, 0)

    @pl.when(last % 2 == 1)
    def _():
        step(last - 1, 0)

    cur = [s_ref[last % 2, g] for g in range(G)]
    if finish is not None:
        cur = [finish(s) for s in cur]
    accumulate(last, *consume(last, cur))


def _flash_kernel(*refs, nqk, decay, tq, G):
    q_refs, k_refs, v_ref = refs[:nqk], refs[nqk:2 * nqk], refs[2 * nqk]
    pos = 2 * nqk + 1
    if decay:
        ck_ref = refs[pos]
        pos += 1
    o_ref, s_ref, m_ref, acc_ref = refs[pos:pos + 4]
    i = pl.program_id(2)

    def head(ref, g, rows=slice(None)):
        if ref.shape[-1] == LANE:
            return ref[rows, :]
        return ref[rows, g * LANE:(g + 1) * LANE]

    qs = []
    for g in range(G):
        parts = [head(r, g) for r in q_refs]
        qs.append(parts[0] if nqk == 1 else jnp.concatenate(parts, axis=-1))

    def chunk(j):
        return pl.ds(pl.multiple_of(j * tq, tq), tq)

    def produce(j):
        rows = chunk(j)
        scores = []
        for g in range(G):
            parts = [head(r, g, rows) for r in k_refs]
            k = parts[0] if nqk == 1 else jnp.concatenate(parts, axis=-1)
            s = lax.dot_general(qs[g], k, (((1,), (1,)), ((), ())), preferred_element_type=F32)
            scores.append(s - ck_ref[g, :, rows] if decay else s)
        return scores

    def values(j):
        return [head(v_ref, g, chunk(j)) for g in range(G)]

    def causal(s):
        row = lax.broadcasted_iota(jnp.int32, (tq, tq), 0)
        col = lax.broadcasted_iota(jnp.int32, (tq, tq), 1)
        return jnp.where(col <= row, s, -jnp.inf)

    _online_softmax(i + 1, produce, values, s_ref, m_ref, acc_ref, finish=causal)
    for g in range(G):
        acc = acc_ref[g]
        o_ref[:, g * LANE:(g + 1) * LANE] = (acc[:, :LANE] / acc[:, LANE:]).astype(o_ref.dtype)


def _flash(q_parts, k_parts, v_part, B, S, decay_t=None, tq=512, G=4):
    tq = _tile(S, tq)
    nqk = len(q_parts)
    W = G * LANE
    args, specs = [], []
    for arr, off, _ in q_parts:
        args.append(arr)
        specs.append(pl.BlockSpec((None, tq, W), lambda b, h, i, off=off: (b, i, off // W + h)))
    for arr, off, per_head in k_parts + [v_part]:
        args.append(arr)
        if per_head:
            specs.append(pl.BlockSpec((None, S, W), lambda b, h, i, off=off: (b, 0, off // W + h)))
        else:
            specs.append(pl.BlockSpec((None, S, LANE), lambda b, h, i, off=off: (b, 0, off // LANE)))
    if decay_t is not None:
        args.append(decay_t)
        specs.append(pl.BlockSpec((None, G, 1, S), lambda b, h, i: (b, h, 0, 0)))
    return pl.pallas_call(
        functools.partial(_flash_kernel, nqk=nqk, decay=decay_t is not None, tq=tq, G=G),
        grid=(B, N_HEADS // G, S // tq),
        in_specs=specs,
        out_specs=pl.BlockSpec((None, tq, W), lambda b, h, i: (b, i, h)),
        out_shape=jax.ShapeDtypeStruct((B, S, MIX_WIDTH), BF16),
        scratch_shapes=[pltpu.VMEM((2, G, tq, tq), F32), pltpu.VMEM((G, tq, LANE), F32),
                        pltpu.VMEM((G, tq, 2 * HEAD_DIM), F32)],
        compiler_params=_params("parallel", "parallel", "arbitrary"),
        name="flash",
    )(*args)


def _gate_kernel(fg_ref, bf_ref, cumt_ref, carry_ref, *, ts):
    @pl.when(pl.program_id(1) == 0)
    def _():
        carry_ref[...] = jnp.zeros_like(carry_ref)

    z = fg_ref[...] + bf_ref[...]
    lf = jnp.minimum(z, 0.0) - jnp.log1p(jnp.exp(-jnp.abs(z)))
    hi = lf.astype(BF16)
    r1 = lf - hi.astype(F32)
    mid = r1.astype(BF16)
    lo = (r1 - mid.astype(F32)).astype(BF16)
    row = lax.broadcasted_iota(jnp.int32, (ts, ts), 0)
    col = lax.broadcasted_iota(jnp.int32, (ts, ts), 1)
    tri = jnp.where(col <= row, 1.0, 0.0).astype(BF16)
    cum = (jnp.dot(tri, hi, preferred_element_type=F32)
           + jnp.dot(tri, mid, preferred_element_type=F32)
           + jnp.dot(tri, lo, preferred_element_type=F32)) + carry_ref[...]
    carry_ref[...] = cum[ts - 1:ts, :]
    cumt_ref[...] = (cum * LOG2E).T


def _fox_gates(fg, b_f):
    B, S, _ = fg.shape
    ts = _tile(S, 512)
    bf = jnp.zeros((1, LANE), F32).at[0, :N_HEADS].set(b_f.astype(F32))
    return pl.pallas_call(
        functools.partial(_gate_kernel, ts=ts),
        grid=(B, S // ts),
        in_specs=[pl.BlockSpec((None, ts, LANE), lambda b, i: (b, i, 0)),
                  pl.BlockSpec((1, LANE), lambda b, i: (0, 0))],
        out_specs=pl.BlockSpec((None, LANE, ts), lambda b, i: (b, 0, i)),
        out_shape=jax.ShapeDtypeStruct((B, LANE, S), F32),
        scratch_shapes=[pltpu.VMEM((1, LANE), F32)],
        compiler_params=_params("parallel", "arbitrary"),
        name="fox_gates",
    )(fg, bf)


def _rope_kernel(x_ref, cos_ref, sin_ref, g_ref, o_ref):
    cos, sin, g = cos_ref[...], sin_ref[...], g_ref[...]
    for c in range(x_ref.shape[1] // LANE):
        sl = slice(c * LANE, (c + 1) * LANE)
        x = x_ref[:, sl]
        ms = jnp.sum(x * x, axis=-1, keepdims=True) * (1.0 / ROPE_DIM)
        y = x * lax.rsqrt(ms + RMS_EPS) * g
        partner = pltpu.roll(y, ROPE_DIM // 2, 1) + pltpu.roll(y, LANE - ROPE_DIM // 2, 1)
        o_ref[:, sl] = (y * cos + partner * sin).astype(o_ref.dtype)


def _rope(x, cos, sin, gain):
    B, S, N = x.shape
    ts = _tile(S, 512)
    g = jnp.zeros((1, LANE), F32).at[0, :ROPE_DIM].set(gain.astype(F32))
    return pl.pallas_call(
        _rope_kernel,
        grid=(B, S // ts),
        in_specs=[pl.BlockSpec((None, ts, N), lambda b, i: (b, i, 0)),
                  pl.BlockSpec((None, ts, LANE), lambda b, i: (b, i, 0)),
                  pl.BlockSpec((None, ts, LANE), lambda b, i: (b, i, 0)),
                  pl.BlockSpec((1, LANE), lambda b, i: (0, 0))],
        out_specs=pl.BlockSpec((None, ts, N), lambda b, i: (b, i, 0)),
        out_shape=jax.ShapeDtypeStruct((B, S, N), BF16),
        compiler_params=_params("parallel", "parallel"),
        name="rope",
    )(x, cos, sin, g)


def _mem_kernel(q_ref, kv_ref, o_ref):
    for h in range(MEM_HEADS):
        q = q_ref[:, h * LANE:(h + 1) * LANE]
        k = kv_ref[:, h * LANE:(h + 1) * LANE]
        v = kv_ref[:, MEM_WIDTH + h * LANE:MEM_WIDTH + (h + 1) * LANE]
        s = lax.dot_general(q, k, (((1,), (1,)), ((), ())), preferred_element_type=F32)
        p = jnp.exp(s - jnp.max(s, axis=-1, keepdims=True))
        o = jnp.dot(p.astype(BF16), v, preferred_element_type=F32)
        o_ref[:, h * LANE:(h + 1) * LANE] = (o / jnp.sum(p, axis=-1, keepdims=True)).astype(o_ref.dtype)


def _mem_attention(qarr, qblock, mem_kv):
    B, S, _ = qarr.shape
    n_mem = mem_kv.shape[1]
    tq = _tile(S, 1024)
    return pl.pallas_call(
        _mem_kernel,
        grid=(B, S // tq),
        in_specs=[pl.BlockSpec((None, tq, MEM_WIDTH), lambda b, i: (b, i, qblock)),
                  pl.BlockSpec((None, n_mem, 2 * MEM_WIDTH), lambda b, i: (b, 0, 0))],
        out_specs=pl.BlockSpec((None, tq, MEM_WIDTH), lambda b, i: (b, i, 0)),
        out_shape=jax.ShapeDtypeStruct((B, S, MEM_WIDTH), BF16),
        compiler_params=_params("parallel", "parallel"),
        name="mem_attn",
    )(qarr, mem_kv)


def _dil_kernel(q_ref, kp_ref, kc_ref, vp_ref, vc_ref, bias_ref, o_ref, lse_ref):
    n = DIL_BLOCK
    nsub = q_ref.shape[0] // n
    row = lax.broadcasted_iota(jnp.int32, (n, 2 * n), 0)
    col = lax.broadcasted_iota(jnp.int32, (n, 2 * n), 1)
    band = jnp.logical_and(col >= row, col <= row + n)
    first = jnp.logical_and(band, jnp.logical_or(col >= n, pl.program_id(1) > 0))
    lane = lax.broadcasted_iota(jnp.int32, (n, LANE), 1)
    dn = (((1,), (1,)), ((), ()))
    heads = [slice(h * LANE, (h + 1) * LANE) for h in range(N_HEADS)]

    def window(prev_ref, cur_ref, u, sl):
        if u == 0:
            return jnp.concatenate([prev_ref[:, sl], cur_ref[:n, sl]], axis=0)
        return cur_ref[(u - 1) * n:(u + 1) * n, sl]

    for u in range(nsub):
        rows = slice(u * n, (u + 1) * n)
        valid = first if u == 0 else band
        scores = []
        for h, sl in enumerate(heads):
            s = lax.dot_general(q_ref[rows, sl], window(kp_ref, kc_ref, u, sl), dn,
                                preferred_element_type=F32) + bias_ref[h]
            scores.append(jnp.where(valid, s, -jnp.inf))
        ms = [jnp.max(s, axis=-1, keepdims=True) for s in scores]
        ps = [jnp.exp2(s - m).astype(BF16) for s, m in zip(scores, ms)]
        lse_all = jnp.zeros((n, LANE), F32)
        for h, sl in enumerate(heads):
            v2 = window(vp_ref, vc_ref, u, sl)
            pv = jnp.dot(ps[h], jnp.concatenate([v2, jnp.ones_like(v2)], axis=-1),
                         preferred_element_type=F32)
            den = pv[:, LANE:]
            o_ref[h, rows, :] = pv[:, :LANE] / den
            lse_all = jnp.where(lane == h, ms[h] + jnp.log2(den), lse_all)
        lse_ref[rows, :] = lse_all


def _dil_attention(pg, bias):
    N, Ls, _ = pg.shape
    n = DIL_BLOCK
    nsub = max(u for u in (1, 2, 4) if Ls % (u * n) == 0)
    tr = nsub * n

    def cur(part):
        return pl.BlockSpec((None, tr, MIX_WIDTH), lambda s, i: (s, i, part))

    def prev(part):
        return pl.BlockSpec((None, n, MIX_WIDTH), lambda s, i: (s, jnp.maximum(i * nsub - 1, 0), part))

    return pl.pallas_call(
        _dil_kernel,
        grid=(N, Ls // tr),
        in_specs=[cur(0), prev(1), cur(1), prev(2), cur(2),
                  pl.BlockSpec((N_HEADS, n, 2 * n), lambda s, i: (0, 0, 0))],
        out_specs=[pl.BlockSpec((None, N_HEADS, tr, LANE), lambda s, i: (s, 0, i, 0)),
                   pl.BlockSpec((None, tr, LANE), lambda s, i: (s, i, 0))],
        out_shape=[jax.ShapeDtypeStruct((N, N_HEADS, Ls, LANE), F32),
                   jax.ShapeDtypeStruct((N, Ls, LANE), F32)],
        compiler_params=_params("parallel", "arbitrary"),
        name="dil_attn",
    )(pg, pg, pg, pg, pg, bias)


def _dil_combine_kernel(o1_ref, o2_ref, o3_ref, l1_ref, l2_ref, l3_ref, out_ref, stage_ref, *, d2, d3):
    n = l3_ref.shape[1]
    q = d3 // d2
    for r in range(d3):
        rows = [(pl.ds(r, n, stride=d3),), (r % d2, pl.ds(r // d2, n, stride=q)), (r, slice(None))]
        ls = [l_ref[idx + (slice(None),)]
              for l_ref, idx in zip((l1_ref, l2_ref, l3_ref), rows)]
        m = functools.reduce(jnp.maximum, ls)
        es = [jnp.exp2(l - m) for l in ls]
        inv = 1.0 / functools.reduce(jnp.add, es)
        ws = [e * inv for e in es]
        for h in range(N_HEADS):
            o = (ws[0][:, h:h + 1] * o1_ref[h, pl.ds(r, n, stride=d3), :]
                 + ws[1][:, h:h + 1] * o2_ref[r % d2, h, pl.ds(r // d2, n, stride=q), :]
                 + ws[2][:, h:h + 1] * o3_ref[r, h, :, :])
            stage_ref[h, pl.ds(r, n, stride=d3), :] = o
    for h in range(N_HEADS):
        out_ref[:, h * LANE:(h + 1) * LANE] = stage_ref[h].astype(out_ref.dtype)


def _dil_combine(outs, lses, B, S, tm=512):
    (_, d1), (_, d2), (_, d3) = DIL_GROUPS
    assert d1 == 1 and d3 % d2 == 0
    tm = _tile(S, tm)
    H = N_HEADS
    o_specs = [pl.BlockSpec((None, H, tm, LANE), lambda b, i: (b, 0, i, 0)),
               pl.BlockSpec((None, d2, H, tm // d2, LANE), lambda b, i: (b, 0, 0, i, 0)),
               pl.BlockSpec((None, d3, H, tm // d3, LANE), lambda b, i: (b, 0, 0, i, 0))]
    l_specs = [pl.BlockSpec((None, tm, LANE), lambda b, i: (b, i, 0)),
               pl.BlockSpec((None, d2, tm // d2, LANE), lambda b, i: (b, 0, i, 0)),
               pl.BlockSpec((None, d3, tm // d3, LANE), lambda b, i: (b, 0, i, 0))]
    return pl.pallas_call(
        functools.partial(_dil_combine_kernel, d2=d2, d3=d3),
        grid=(B, S // tm),
        in_specs=o_specs + l_specs,
        out_specs=pl.BlockSpec((None, tm, MIX_WIDTH), lambda b, i: (b, i, 0)),
        out_shape=jax.ShapeDtypeStruct((B, S, MIX_WIDTH), BF16),
        scratch_shapes=[pltpu.VMEM((H, tm, LANE), F32)],
        compiler_params=_params("parallel", "parallel"),
        name="dil_combine",
    )(outs[0].reshape(B, H, S, LANE), outs[1].reshape(B, d2, H, S // d2, LANE),
      outs[2].reshape(B, d3, H, S // d3, LANE),
      lses[0].reshape(B, S, LANE), lses[1].reshape(B, d2, S // d2, LANE),
      lses[2].reshape(B, d3, S // d3, LANE))


def _dsa_topk_kernel(qi_ref, wi_ref, ki_ref, mask_ref, key_ref, jsel_ref, *, tq, tk, S, n_sel):
    i = pl.program_id(1)
    q0 = i * tq
    n_chunks = (q0 + tq + tk - 1) // tk
    dn = (((1,), (1,)), ((), ()))
    krow = lax.broadcasted_iota(jnp.int32, (tk, tq), 0)
    qpos = q0 + lax.broadcasted_iota(jnp.int32, (tk, tq), 1)

    wi_t = (wi_ref[...] * (IDX_HEADS ** -0.5 * IDX_DIM ** -0.5)).T

    def idx_body(c, carry):
        off = pl.multiple_of(c * tk, tk)
        kic = ki_ref[pl.ds(off, tk), :]
        k_lo = jnp.where(lax.broadcasted_iota(jnp.int32, kic.shape, 1) < IDX_DIM, kic, 0.0)
        k_sides = (k_lo.astype(BF16), pltpu.roll(k_lo, IDX_DIM, 1).astype(BF16))
        acc = jnp.zeros((tk, tq), F32)
        for h in range(IDX_HEADS):
            d = lax.dot_general(k_sides[h % 2], qi_ref[:, (h // 2) * LANE:(h // 2 + 1) * LANE], dn,
                                preferred_element_type=F32)
            acc = acc + wi_t[IDX_DIM + h:IDX_DIM + h + 1, :] * jnp.maximum(d, 0.0)
        score = jnp.where(off + krow <= qpos, acc, -jnp.inf)
        bits = pltpu.bitcast(score, jnp.int32)
        key_ref[pl.ds(off, tk), :] = bits ^ ((bits >> 31) & jnp.int32(0x7FFFFFFF))
        return carry

    lax.fori_loop(0, n_chunks, idx_body, 0)

    def count(pred):
        sub = 64
        def body(c, acc):
            off = pl.multiple_of(c * tk, tk)
            ones = jnp.where(pred(key_ref[pl.ds(off, tk), :], off), 1.0, 0.0)
            return acc + jnp.sum(ones.reshape(tk // sub, sub, tq), axis=0)

        acc = lax.fori_loop(0, n_chunks, body, jnp.zeros((sub, tq), F32))
        return jnp.sum(acc, axis=0, keepdims=True)

    k_sel = float(n_sel)
    c0 = count(lambda k, off: k >= 0)
    lo0 = jnp.where(c0 >= k_sel, jnp.int32(0), jnp.int32(-2 ** 31))
    n0 = jnp.where(c0 >= k_sel, c0, (n_chunks * tk).astype(F32))

    def bis_cond(state):
        it, _, n_lo = state
        return jnp.logical_and(it < 31, jnp.max(jnp.abs(n_lo - k_sel)) > 0.0)

    def bis_body(state):
        it, lo, n_lo = state
        cand = lo + (jnp.int32(1) << (30 - it))
        cnt = count(lambda k, off: k >= cand)
        take = cnt >= k_sel
        return it + 1, jnp.where(take, cand, lo), jnp.where(take, cnt, n_lo)

    _, thr, n_ge = lax.while_loop(bis_cond, bis_body, (jnp.int32(0), lo0, n0))
    n_bits = max(1, (S - 1).bit_length())
    jsel_ref[...] = jnp.full_like(jsel_ref, S)

    @pl.when(jnp.max(n_ge) > k_sel)
    def _():
        need = k_sel - count(lambda k, off: k > thr)

        def tie_body(it, jlo):
            cand = jlo + (jnp.int32(1) << (n_bits - 1 - it))
            below = count(lambda k, off: jnp.logical_and(k == thr, off + krow < cand))
            return jnp.where(below < need, cand, jlo)

        jsel = lax.fori_loop(0, n_bits, tie_body, jnp.zeros((1, tq), jnp.int32))
        jsel_ref[...] = jnp.broadcast_to(jsel, jsel_ref.shape)

    jsel = jsel_ref[:1, :]
    mask_ref[...] = jnp.full_like(mask_ref, -jnp.inf)

    def mask_body(c, carry):
        off = pl.multiple_of(c * tk, tk)
        k = key_ref[pl.ds(off, tk), :]
        kpos = off + krow
        tie = jnp.logical_and(k == thr, kpos <= jsel)
        sel = jnp.logical_and(jnp.logical_or(k > thr, tie), kpos <= qpos)
        mask_ref[:, pl.ds(off, tk)] = jnp.where(sel, 0.0, -jnp.inf).T.astype(mask_ref.dtype)
        return carry

    lax.fori_loop(0, n_chunks, mask_body, 0)


def _dsa_select(pm, aux, B, S, n_sel, tq=256, tk=512):
    tq, tk = _tile(S, tq), _tile(S, tk)
    qiw = IDX_HEADS * IDX_DIM
    return pl.pallas_call(
        functools.partial(_dsa_topk_kernel, tq=tq, tk=tk, S=S, n_sel=n_sel),
        grid=(B, S // tq),
        in_specs=[pl.BlockSpec((None, tq, qiw), lambda b, i: (b, i, MIX_WIDTH // qiw)),
                  pl.BlockSpec((None, tq, LANE), lambda b, i: (b, i, 0)),
                  pl.BlockSpec((None, S, LANE), lambda b, i: (b, 0, 0))],
        out_specs=pl.BlockSpec((None, tq, S), lambda b, i: (b, i, 0)),
        out_shape=jax.ShapeDtypeStruct((B, S, S), BF16),
        scratch_shapes=[pltpu.VMEM((S, tq), jnp.int32), pltpu.VMEM((8, tq), jnp.int32)],
        compiler_params=_params("parallel", "arbitrary"),
        name="dsa_select",
    )(pm, aux, aux)


def _dsa_attn_kernel(q_ref, k_ref, v_ref, mask_ref, gt_ref, o_ref, s_ref, m_ref, acc_ref, *, tq, tk, S):
    i = pl.program_id(2)
    q0 = i * tq
    n_chunks = (q0 + tq + tk - 1) // tk
    R = q_ref.shape[1] // LANE
    nblk = tq // LANE
    dn = (((1,), (1,)), ((), ()))
    qs = [q_ref[:, r * LANE:(r + 1) * LANE] for r in range(R)]

    def produce(c):
        off = pl.multiple_of(c * tk, tk)
        kc = k_ref[pl.ds(off, tk), :]
        mask = mask_ref[:, pl.ds(off, tk)].astype(F32)
        st = S - q0 + off
        scores = []
        for r in range(R):
            bias = jnp.concatenate(
                [gt_ref[r, :, pl.ds(pl.multiple_of(st + (nblk - 1 - a) * LANE, LANE), tk)]
                 for a in range(nblk)], axis=0)
            scores.append(lax.dot_general(qs[r], kc, dn, preferred_element_type=F32) + bias + mask)
        return scores

    def values(c):
        return [v_ref[pl.ds(pl.multiple_of(c * tk, tk), tk), :]] * R

    _online_softmax(n_chunks, produce, values, s_ref, m_ref, acc_ref)
    for r in range(R):
        acc = acc_ref[r]
        o_ref[:, r * LANE:(r + 1) * LANE] = (acc[:, :LANE] / acc[:, LANE:]).astype(o_ref.dtype)


def _dsa_attention(pm, mask, gt, B, S, tq=256, tk=512):
    tq, tk = _tile(S, tq), _tile(S, tk)
    R = N_HEADS // DSA_KV_HEADS
    kblk = (MIX_WIDTH + IDX_HEADS * IDX_DIM) // LANE
    return pl.pallas_call(
        functools.partial(_dsa_attn_kernel, tq=tq, tk=tk, S=S),
        grid=(DSA_KV_HEADS, B, S // tq),
        in_specs=[pl.BlockSpec((None, tq, R * LANE), lambda g, b, i: (b, i, g)),
                  pl.BlockSpec((None, S, LANE), lambda g, b, i: (b, 0, kblk + g)),
                  pl.BlockSpec((None, S, LANE), lambda g, b, i: (b, 0, kblk + DSA_KV_HEADS + g)),
                  pl.BlockSpec((None, tq, S), lambda g, b, i: (b, i, 0)),
                  pl.BlockSpec((R, LANE, gt.shape[2]), lambda g, b, i: (g, 0, 0))],
        out_specs=pl.BlockSpec((None, tq, R * LANE), lambda g, b, i: (b, i, g)),
        out_shape=jax.ShapeDtypeStruct((B, S, MIX_WIDTH), BF16),
        scratch_shapes=[pltpu.VMEM((2, R, tq, tk), F32), pltpu.VMEM((R, tq, LANE), F32),
                        pltpu.VMEM((R, tq, 2 * HEAD_DIM), F32)],
        compiler_params=_params("parallel", "parallel", "arbitrary"),
        name="dsa_attn",
    )(pm, pm, pm, mask, gt)


def _t5_bucket(dist):
    n = jnp.maximum(dist, 0)
    exact = REL_BUCKETS // 2
    nf = jnp.maximum(n, 1).astype(F32)
    large = exact + (jnp.log(nf / exact) / math.log(REL_MAX_DIST / exact)
                     * (REL_BUCKETS - exact)).astype(jnp.int32)
    large = jnp.minimum(large, REL_BUCKETS - 1)
    return jnp.where(n < exact, n, large)


def _t5_bias(t5_table, dist):
    onehot = jax.nn.one_hot(_t5_bucket(dist), REL_BUCKETS, dtype=F32)
    table = jnp.einsum('...b,bh->h...', onehot, t5_table.astype(F32), precision=lax.Precision.HIGHEST)
    return table * LOG2E


def _dil_bias(t5_table, dil):
    n = DIL_BLOCK
    rel = jnp.arange(n)[:, None] + n - jnp.arange(2 * n)[None, :]
    return _t5_bias(t5_table, rel * dil)


def _dsa_bias_table(t5_table, S, tq, tk):
    m = jnp.arange(S + tq + tk)[None, :]
    i = jnp.arange(LANE)[:, None]
    return _t5_bias(t5_table, S + tq - LANE - m + i)


def _headnorm_cols(spec):
    eg, ef = [], []
    for gain, count, scale in spec:
        if gain is None:
            eg.append(jnp.ones((count * LANE,), F32))
            ef.append(jnp.zeros((count * LANE,), F32))
        else:
            eg.append(jnp.tile(gain.astype(F32) * scale, count))
            ef.append(jnp.ones((count * LANE,), F32))
    return jnp.concatenate(eg), jnp.concatenate(ef)


def _pad_cols(w, width):
    return jnp.pad(w, ((0, 0), (0, width - w.shape[1])))


def _fox_layer(x, norm_g, w_in, b_f, qk_g, mem_g):
    B, S, _ = x.shape
    W = MIX_WIDTH
    w_main = jnp.concatenate([w_in[:, :3 * W], w_in[:, 3 * W + N_HEADS:]], axis=1).astype(BF16)
    w_gate = _pad_cols(w_in[:, 3 * W:3 * W + N_HEADS], LANE).astype(BF16)
    eg, ef = _headnorm_cols([(qk_g[0], N_HEADS, HEAD_DIM ** -0.5 * LOG2E), (qk_g[1], N_HEADS, 1.0),
                             (None, N_HEADS, 1.0), (mem_g, MEM_HEADS, HEAD_DIM ** -0.5)])
    pm = _proj(x, norm_g, w_main, eg, ef)
    fg = _proj(x, norm_g, w_gate, out_dtype=F32)
    cum_t = _fox_gates(fg, b_f)
    mix = _flash([(pm, 0, True)], [(pm, W, True)], (pm, 2 * W, True), B, S,
                 decay_t=cum_t.reshape(B, LANE, 1, S))
    return mix, pm, 3 * W // MEM_WIDTH


def _mla_layer(x, positions, norm_g, w_in, q_norm, w_uq, kv_norm, w_ukv, nope_g, rope_g, mem_g):
    B, S, _ = x.shape
    scale = (NOPE_DIM + ROPE_DIM) ** -0.5 * LOG2E
    lat = Q_LORA + KV_LORA
    w_lat = jnp.concatenate([w_in[:, :lat], _pad_cols(w_in[:, lat:lat + ROPE_DIM], LANE)],
                            axis=1).astype(BF16)
    w_mem = w_in[:, lat + ROPE_DIM:].astype(BF16)
    pl_ = _proj(x, norm_g, w_lat, out_dtype=F32, tn=lat + LANE)
    eg, ef = _headnorm_cols([(mem_g, MEM_HEADS, HEAD_DIM ** -0.5)])
    pmem = _proj(x, norm_g, w_mem, eg, ef)

    uq = w_uq.reshape(Q_LORA, N_HEADS, NOPE_DIM + ROPE_DIM)
    w_qn = uq[:, :, :NOPE_DIM].reshape(Q_LORA, MIX_WIDTH).astype(BF16)
    w_qr = jnp.pad(uq[:, :, NOPE_DIM:], ((0, 0), (0, 0), (0, LANE - ROPE_DIM))
                   ).reshape(Q_LORA, N_HEADS * LANE).astype(BF16)
    ukv = w_ukv.reshape(KV_LORA, N_HEADS, NOPE_DIM + HEAD_DIM)
    w_kv = jnp.concatenate([ukv[:, :, :NOPE_DIM].reshape(KV_LORA, MIX_WIDTH),
                            ukv[:, :, NOPE_DIM:].reshape(KV_LORA, MIX_WIDTH)], axis=1).astype(BF16)
    eg, ef = _headnorm_cols([(nope_g[0], N_HEADS, scale)])
    qn = _proj(pl_, q_norm, w_qn, eg, ef, kblock=0, tn=2048)
    qr_raw = _proj(pl_, q_norm, w_qr, kblock=0, out_dtype=F32, tn=1024)
    eg, ef = _headnorm_cols([(nope_g[1], N_HEADS, 1.0), (None, N_HEADS, 1.0)])
    kv = _proj(pl_, kv_norm, w_kv, eg, ef, kblock=1, tn=2048)

    half = ROPE_DIM // 2
    inv = ROPE_THETA ** (-jnp.arange(half, dtype=F32) / half)
    ang = positions.astype(F32)[..., None] * inv
    cos, sin = jnp.cos(ang), jnp.sin(ang)
    zero = jnp.zeros((B, S, LANE - ROPE_DIM), F32)
    cos_t = jnp.concatenate([cos, cos, zero], axis=-1)
    sin_t = jnp.concatenate([-sin, sin, zero], axis=-1)
    qr = _rope(qr_raw, cos_t, sin_t, rope_g[0] * scale)
    kr = _rope(pl_[:, :, lat:], cos_t, sin_t, rope_g[1])
    mix = _flash([(qn, 0, True), (qr, 0, True)], [(kv, 0, True), (kr, 0, False)],
                 (kv, MIX_WIDTH, True), B, S)
    return mix, pmem, 0


def _dil_layer(x, norm_g, w_in, qk_g, t5_table, mem_g):
    B, S, _ = x.shape
    W = MIX_WIDTH
    outs, lses = [], []
    for gi, (win, dil) in enumerate(DIL_GROUPS):
        assert win // dil == DIL_BLOCK and (S // dil) % DIL_BLOCK == 0
        w_g = w_in[:, gi * 3 * W:(gi + 1) * 3 * W].astype(BF16)
        eg, ef = _headnorm_cols([(qk_g[gi, 0], N_HEADS, HEAD_DIM ** -0.5 * LOG2E),
                                 (qk_g[gi, 1], N_HEADS, 1.0), (None, N_HEADS, 1.0)])
        pg = _proj(x, norm_g, w_g, eg, ef, dil=dil, tn=2048)
        o, lse = _dil_attention(pg.reshape(B * dil, S // dil, 3 * W), _dil_bias(t5_table, dil))
        outs.append(o)
        lses.append(lse)
    mix = _dil_combine(outs, lses, B, S)
    eg, ef = _headnorm_cols([(mem_g, MEM_HEADS, HEAD_DIM ** -0.5)])
    pmem = _proj(x, norm_g, w_in[:, len(DIL_GROUPS) * 3 * W:].astype(BF16), eg, ef)
    return mix, pmem, 0


def _dsa_layer(x, norm_g, w_in, qk_g, t5_table, mem_g):
    B, S, _ = x.shape
    W = MIX_WIDTH
    kvw = DSA_KV_HEADS * HEAD_DIM
    o_k, o_v, o_qi = W, W + kvw, W + 2 * kvw
    o_ki = o_qi + IDX_HEADS * IDX_DIM
    o_wi = o_ki + IDX_DIM
    o_mem = o_wi + IDX_HEADS
    qiw = IDX_HEADS * IDX_DIM
    w_main = jnp.concatenate([w_in[:, :W], w_in[:, o_qi:o_ki], w_in[:, o_k:o_qi], w_in[:, o_mem:]],
                             axis=1).astype(BF16)
    w_aux = _pad_cols(w_in[:, o_ki:o_mem], LANE).astype(BF16)
    eg, ef = _headnorm_cols([(qk_g[0], N_HEADS, HEAD_DIM ** -0.5 * LOG2E), (None, qiw // LANE, 1.0),
                             (qk_g[1], DSA_KV_HEADS, 1.0), (None, DSA_KV_HEADS, 1.0),
                             (mem_g, MEM_HEADS, HEAD_DIM ** -0.5)])
    pm = _proj(x, norm_g, w_main, eg, ef, tn=1536)
    aux = _proj(x, norm_g, w_aux, out_dtype=F32)
    mask = _dsa_select(pm, aux, B, S, min(TOPK_MAX, S // 4), tq=512)
    tq, tk = _tile(S, 512), _tile(S, 512)
    mix = _dsa_attention(pm, mask, _dsa_bias_table(t5_table, S, tq, tk), B, S, tq, tk)
    return mix, pm, (W + qiw + 2 * kvw) // MEM_WIDTH


def kernel(x, mem, positions, t5_table, ffn_norm, ffn_w_gate, ffn_w_up, ffn_w_down, attn_norm,
           mem_norm, mem_w_kv, mem_qk_g, w_out, a_w_in, a_b_f, a_qk_g, b_w_in, b_q_norm, b_w_uq,
           b_kv_norm, b_w_ukv, b_nope_g, b_rope_g, c_w_in, c_qk_g, d_w_in, d_qk_g):
    B, S, D = x.shape
    depth = ffn_norm.shape[0]
    n_mixers = 4

    def ffn(xc, i, k, next_gain=None):
        return _ffn(xc.reshape(B * S, D), ffn_norm[i, k], ffn_w_gate[i, k].astype(BF16),
                    ffn_w_up[i, k].astype(BF16), ffn_w_down[i, k].astype(BF16), next_gain)

    for i in range(depth):
        m, j = i % n_mixers, i // n_mixers
        x, h = ffn(x, i, 0, attn_norm[i])
        x, h = x.reshape(B, S, D), h.reshape(B, S, D)
        mem_g = mem_qk_g[i]
        if m == 0:
            mix, qarr, qblock = _fox_layer(h, None, a_w_in[j], a_b_f[j], a_qk_g[j], mem_g[0])
        elif m == 1:
            mix, qarr, qblock = _mla_layer(h, positions, None, b_w_in[j], b_q_norm[j],
                                           b_w_uq[j], b_kv_norm[j], b_w_ukv[j], b_nope_g[j],
                                           b_rope_g[j], mem_g[0])
        elif m == 2:
            mix, qarr, qblock = _dil_layer(h, None, c_w_in[j], c_qk_g[j], t5_table, mem_g[0])
        else:
            mix, qarr, qblock = _dsa_layer(h, None, d_w_in[j], d_qk_g[j], t5_table, mem_g[0])
        eg, ef = _headnorm_cols([(mem_g[1], MEM_HEADS, 1.0), (None, MEM_HEADS, 1.0)])
        mem_kv = _proj(mem, mem_norm[i], mem_w_kv[i].astype(BF16), eg, ef)
        mo = _mem_attention(qarr, qblock, mem_kv)
        wo = w_out[i].astype(BF16)
        x = _out_proj(x.reshape(B * S, D), mix.reshape(B * S, MIX_WIDTH),
                      mo.reshape(B * S, MEM_WIDTH), wo[:MIX_WIDTH], wo[MIX_WIDTH:]).reshape(B, S, D)
        x = ffn(x, i, 1).reshape(B, S, D)
    return x
```

```python
import functools
import math

import jax
import jax.numpy as jnp
from jax import lax
from jax.experimental import pallas as pl
from jax.experimental.pallas import tpu as pltpu

F32 = jnp.float32
BF16 = jnp.bfloat16

LANE = 128
D_MODEL = 2048
N_HEADS = 16
HEAD_DIM = 128
MIX_WIDTH = N_HEADS * HEAD_DIM
MEM_HEADS = 4
MEM_WIDTH = MEM_HEADS * HEAD_DIM
D_FF = 5632
RMS_EPS = 1e-6
REL_BUCKETS = 32
REL_MAX_DIST = 2048
Q_LORA = 512
KV_LORA = 512
NOPE_DIM = 128
ROPE_DIM = 64
ROPE_THETA = 10000.0
DIL_GROUPS = ((128, 1), (512, 4), (2048, 16))
DIL_BLOCK = 128
PERM_ROWS = 256
DSA_KV_HEADS = 4
IDX_HEADS = 16
IDX_DIM = 64
TOPK_MAX = 256
VMEM_LIMIT = 56 * 1024 * 1024
FFN_VMEM_LIMIT = 62 * 1024 * 1024
NEG_BIG = -1e30
LOG2E = math.log2(math.e)


def _params(*sem):
    return pltpu.CompilerParams(dimension_semantics=sem, vmem_limit_bytes=VMEM_LIMIT)


def _tile(n, pref):
    t = min(n, pref)
    while n % t:
        t //= 2
    return t


def _ffn_kernel(x_ref, g_ref, wg_ref, wu_ref, wd_ref, o_ref, h_ref):
    _ffn_body(x_ref, g_ref, wg_ref, wu_ref, wd_ref, o_ref, h_ref, ())


def _ffn_kernel_with_next(x_ref, g_ref, wg_ref, wu_ref, wd_ref, ng_ref, o_ref, hn_ref, h_ref):
    _ffn_body(x_ref, g_ref, wg_ref, wu_ref, wd_ref, o_ref, h_ref, (ng_ref, hn_ref))


def _ffn_body(x_ref, g_ref, wg_ref, wu_ref, wd_ref, o_ref, h_ref, next_refs):
    j = pl.program_id(1)
    last = pl.num_programs(1) - 1
    tm = x_ref.shape[0]
    halves = [slice(0, tm // 2), slice(tm // 2, tm)]

    def down(h):
        g = jnp.dot(h, wg_ref[...], preferred_element_type=F32)
        u = jnp.dot(h, wu_ref[...], preferred_element_type=F32)
        a = (g * jax.nn.sigmoid(g) * u).astype(BF16)
        return jnp.dot(a, wd_ref[...], preferred_element_type=F32)

    @pl.when(j == 0)
    def _():
        for rows in halves:
            x = x_ref[rows, :]
            ms = jnp.mean(x * x, axis=-1, keepdims=True)
            h = (x * lax.rsqrt(ms + RMS_EPS) * g_ref[...]).astype(BF16)
            h_ref[rows, :] = h
            o_ref[rows, :] = down(h)

    @pl.when(jnp.logical_and(j > 0, j < last))
    def _():
        o_ref[...] += down(h_ref[...])

    @pl.when(jnp.logical_and(j > 0, j == last))
    def _():
        for rows in halves:
            y = x_ref[rows, :] + 0.5 * (o_ref[rows, :] + down(h_ref[rows, :]))
            o_ref[rows, :] = y
            if next_refs:
                ng_ref, hn_ref = next_refs
                ms = jnp.mean(y * y, axis=-1, keepdims=True)
                hn_ref[rows, :] = (y * lax.rsqrt(ms + RMS_EPS) * ng_ref[...]).astype(BF16)


def _ffn(x2, gain, wg, wu, wd, next_gain=None):
    T, D = x2.shape
    F = wg.shape[1]
    tm, tf = _tile(T, 1024), _tile(F, 512)
    assert F // tf >= 2 and tm % 16 == 0
    row_spec = pl.BlockSpec((tm, D), lambda i, j: (i, 0))
    vec_spec = pl.BlockSpec((1, D), lambda i, j: (0, 0))
    in_specs = [row_spec, vec_spec,
                pl.BlockSpec((D, tf), lambda i, j: (0, j)),
                pl.BlockSpec((D, tf), lambda i, j: (0, j)),
                pl.BlockSpec((tf, D), lambda i, j: (j, 0))]
    args = [x2, gain.reshape(1, D), wg, wu, wd]
    out_specs, out_shape = row_spec, jax.ShapeDtypeStruct((T, D), F32)
    kern = _ffn_kernel
    if next_gain is not None:
        in_specs.append(vec_spec)
        args.append(next_gain.reshape(1, D).astype(F32))
        out_specs = [row_spec, row_spec]
        out_shape = [out_shape, jax.ShapeDtypeStruct((T, D), BF16)]
        kern = _ffn_kernel_with_next
    return pl.pallas_call(
        kern,
        grid=(T // tm, F // tf),
        in_specs=in_specs,
        out_specs=out_specs,
        out_shape=out_shape,
        scratch_shapes=[pltpu.VMEM((tm, D), BF16)],
        compiler_params=pltpu.CompilerParams(dimension_semantics=("parallel", "arbitrary"),
                                             vmem_limit_bytes=FFN_VMEM_LIMIT),
        name="ffn",
    )(*args)


def _proj_kernel(x_ref, g_ref, w_ref, eg_ref, ef_ref, o_ref, *scratch, epilogue, dil, prenormed):
    scratch = list(scratch)
    if prenormed:
        h_ref = x_ref
    else:
        h_ref = scratch.pop(0)

        @pl.when(pl.program_id(2) == 0)
        def _():
            x = x_ref[...].astype(F32)
            ms = jnp.mean(x * x, axis=-1, keepdims=True)
            h_ref[...] = (x * lax.rsqrt(ms + RMS_EPS) * g_ref[...]).astype(BF16)

    tm = h_ref.shape[0]
    nb = PERM_ROWS // dil
    if dil > 1:
        hp_ref = scratch.pop(0)

        @pl.when(pl.program_id(2) == 0)
        def _():
            dst = lax.broadcasted_iota(jnp.int32, (PERM_ROWS, PERM_ROWS), 0)
            src = lax.broadcasted_iota(jnp.int32, (PERM_ROWS, PERM_ROWS), 1)
            perm = jnp.where(dst == (src % dil) * nb + src // dil, 1.0, 0.0).astype(BF16)
            for b in range(tm // PERM_ROWS):
                blk = slice(b * PERM_ROWS, (b + 1) * PERM_ROWS)
                hp_ref[blk, :] = jnp.dot(perm, h_ref[blk, :],
                                         preferred_element_type=F32).astype(BF16)

        h_ref = hp_ref
    parts = 2 if tm % (2 * PERM_ROWS) == 0 else 1
    rows_per = tm // parts
    for p in range(parts):
        rows = slice(p * rows_per, (p + 1) * rows_per)
        acc = jnp.dot(h_ref[rows, :], w_ref[...], preferred_element_type=F32)
        for c in range(acc.shape[1] // LANE):
            sl = slice(c * LANE, (c + 1) * LANE)
            y = acc[:, sl]
            if epilogue:
                ms = jnp.mean(y * y, axis=-1, keepdims=True)
                scale = jnp.where(ef_ref[:, sl] > 0.0, lax.rsqrt(ms + RMS_EPS), 1.0)
                y = y * scale * eg_ref[:, sl]
            y = y.astype(o_ref.dtype)
            if dil == 1:
                o_ref[0, rows, sl] = y
            else:
                for b in range(rows_per // PERM_ROWS):
                    a0 = (p * (rows_per // PERM_ROWS) + b) * nb
                    for r in range(dil):
                        o_ref[r, a0:a0 + nb, sl] = y[b * PERM_ROWS + r * nb:b * PERM_ROWS + (r + 1) * nb, :]


def _proj(x, gain, w, eg=None, ef=None, *, dil=1, kblock=0, out_dtype=BF16, tm=1024, tn=512):
    B, S, C = x.shape
    K, N = w.shape
    tm, tn = _tile(S, tm), _tile(N, tn)
    epilogue = eg is not None
    if not epilogue:
        eg = jnp.ones((1, N), F32)
        ef = jnp.zeros((1, N), F32)
    assert dil == 1 or (tm % PERM_ROWS == 0 and PERM_ROWS % (16 * dil) == 0)
    prenormed = gain is None
    if prenormed:
        assert x.dtype == BF16
        gain = jnp.ones((K,), F32)
    out = pl.pallas_call(
        functools.partial(_proj_kernel, epilogue=epilogue, dil=dil, prenormed=prenormed),
        grid=(B, S // tm, N // tn),
        in_specs=[
            pl.BlockSpec((None, tm, K), lambda b, i, j: (b, i, kblock)),
            pl.BlockSpec((1, K), lambda b, i, j: (0, 0)),
            pl.BlockSpec((K, tn), lambda b, i, j: (0, j)),
            pl.BlockSpec((1, tn), lambda b, i, j: (0, j)),
            pl.BlockSpec((1, tn), lambda b, i, j: (0, j)),
        ],
        out_specs=pl.BlockSpec((None, dil, tm // dil, tn), lambda b, i, j: (b, 0, i, j)),
        out_shape=jax.ShapeDtypeStruct((B, dil, S // dil, N), out_dtype),
        scratch_shapes=[pltpu.VMEM((tm, K), BF16)] * ((not prenormed) + (dil > 1)),
        compiler_params=_params("parallel", "parallel", "arbitrary"),
        name="proj",
    )(x, gain.reshape(1, K).astype(F32), w, eg.reshape(1, N), ef.reshape(1, N))
    return out.reshape(B, S, N) if dil == 1 else out


def _out_kernel(x_ref, a_ref, b_ref, wa_ref, wb_ref, o_ref):
    acc = jnp.dot(a_ref[...], wa_ref[...], preferred_element_type=F32)
    acc += jnp.dot(b_ref[...], wb_ref[...], preferred_element_type=F32)
    o_ref[...] = x_ref[...] + acc


def _out_proj(x2, mix2, mo2, wa, wb):
    T, D = x2.shape
    tm, tn = _tile(T, 1024), _tile(D, 1024)
    ka, kb = mix2.shape[1], mo2.shape[1]
    return pl.pallas_call(
        _out_kernel,
        grid=(T // tm, D // tn),
        in_specs=[
            pl.BlockSpec((tm, tn), lambda i, j: (i, j)),
            pl.BlockSpec((tm, ka), lambda i, j: (i, 0)),
            pl.BlockSpec((tm, kb), lambda i, j: (i, 0)),
            pl.BlockSpec((ka, tn), lambda i, j: (0, j)),
            pl.BlockSpec((kb, tn), lambda i, j: (0, j)),
        ],
        out_specs=pl.BlockSpec((tm, tn), lambda i, j: (i, j)),
        out_shape=jax.ShapeDtypeStruct((T, D), F32),
        compiler_params=_params("parallel", "arbitrary"),
        name="out_proj",
    )(x2, mix2, mo2, wa, wb)


def _lanes(x, reps):
    return x if reps == 1 else jnp.concatenate([x] * reps, axis=-1)


def _softmax_probs(scores, m_ref):
    ps, alphas = [], []
    for g, s in enumerate(scores):
        m_old = m_ref[g]
        m_new = jnp.maximum(m_old, jnp.max(s, axis=-1, keepdims=True))
        ps.append(jnp.exp2(s - _lanes(m_new, s.shape[1] // LANE)).astype(BF16))
        alphas.append(jnp.exp2(m_old - m_new))
        m_ref[g] = m_new
    return ps, alphas


def _accumulate(ps, alphas, values, acc_ref):
    for g, (p, alpha, v) in enumerate(zip(ps, alphas, values)):
        v1 = jnp.concatenate([v, jnp.ones_like(v)], axis=-1)
        acc_ref[g] = _lanes(alpha, 2) * acc_ref[g] + jnp.dot(p, v1, preferred_element_type=F32)


def _online_softmax(n_chunks, produce, values, s_ref, m_ref, acc_ref, finish=None):
    G = s_ref.shape[1]

    def consume(j, scores):
        return _softmax_probs(scores, m_ref)

    def accumulate(j, ps, alphas):
        _accumulate(ps, alphas, values(j), acc_ref)

    m_ref[...] = jnp.full_like(m_ref, NEG_BIG)
    acc_ref[...] = jnp.zeros_like(acc_ref)
    for g, s in enumerate(produce(0)):
        s_ref[0, g] = s

    def step(j, slot):
        cur = [s_ref[slot, g] for g in range(G)]
        nxt = produce(j + 1)
        ps, alphas = consume(j, cur)
        for g, s in enumerate(nxt):
            s_ref[1 - slot, g] = s
        accumulate(j, ps, alphas)

    def body(jj, carry):
        step(2 * jj, 0)
        step(2 * jj + 1, 1)
        return carry

    last = n_chunks - 1
    lax.fori_loop(0, last // 2, body, 0)

    @pl.when(last % 2 == 1)
    def _():
        step(last - 1, 0)

    cur = [s_ref[last % 2, g] for g in range(G)]
    if finish is not None:
        cur = [finish(s) for s in cur]
    accumulate(last, *consume(last, cur))


def _flash_kernel(*refs, nqk, decay, tq, G):
    q_refs, k_refs, v_ref = refs[:nqk], refs[nqk:2 * nqk], refs[2 * nqk]
    pos = 2 * nqk + 1
    if decay:
        ck_ref = refs[pos]
        pos += 1
    o_ref, s_ref, m_ref, acc_ref = refs[pos:pos + 4]
    i = pl.program_id(2)

    def head(ref, g, rows=slice(None)):
        if ref.shape[-1] == LANE:
            return ref[rows, :]
        return ref[rows, g * LANE:(g + 1) * LANE]

    qs = []
    for g in range(G):
        parts = [head(r, g) for r in q_refs]
        qs.append(parts[0] if nqk == 1 else jnp.concatenate(parts, axis=-1))

    def chunk(j):
        return pl.ds(pl.multiple_of(j * tq, tq), tq)

    def produce(j):
        rows = chunk(j)
        scores = []
        for g in range(G):
            parts = [head(r, g, rows) for r in k_refs]
            k = parts[0] if nqk == 1 else jnp.concatenate(parts, axis=-1)
            s = lax.dot_general(qs[g], k, (((1,), (1,)), ((), ())), preferred_element_type=F32)
            scores.append(s - ck_ref[g, :, rows] if decay else s)
        return scores

    def values(j):
        return [head(v_ref, g, chunk(j)) for g in range(G)]

    def causal(s):
        row = lax.broadcasted_iota(jnp.int32, (tq, tq), 0)
        col = lax.broadcasted_iota(jnp.int32, (tq, tq), 1)
        return jnp.where(col <= row, s, -jnp.inf)

    _online_softmax(i + 1, produce, values, s_ref, m_ref, acc_ref, finish=causal)
    for g in range(G):
        acc = acc_ref[g]
        o_ref[:, g * LANE:(g + 1) * LANE] = (acc[:, :LANE] / acc[:, LANE:]).astype(o_ref.dtype)


def _flash(q_parts, k_parts, v_part, B, S, decay_t=None, tq=512, G=4):
    tq = _tile(S, tq)
    nqk = len(q_parts)
    W = G * LANE
    args, specs = [], []
    for arr, off, _ in q_parts:
        args.append(arr)
        specs.append(pl.BlockSpec((None, tq, W), lambda b, h, i, off=off: (b, i, off // W + h)))
    for arr, off, per_head in k_parts + [v_part]:
        args.append(arr)
        if per_head:
            specs.append(pl.BlockSpec((None, S, W), lambda b, h, i, off=off: (b, 0, off // W + h)))
        else:
            specs.append(pl.BlockSpec((None, S, LANE), lambda b, h, i, off=off: (b, 0, off // LANE)))
    if decay_t is not None:
        args.append(decay_t)
        specs.append(pl.BlockSpec((None, G, 1, S), lambda b, h, i: (b, h, 0, 0)))
    return pl.pallas_call(
        functools.partial(_flash_kernel, nqk=nqk, decay=decay_t is not None, tq=tq, G=G),
        grid=(B, N_HEADS // G, S // tq),
        in_specs=specs,
        out_specs=pl.BlockSpec((None, tq, W), lambda b, h, i: (b, i, h)),
        out_shape=jax.ShapeDtypeStruct((B, S, MIX_WIDTH), BF16),
        scratch_shapes=[pltpu.VMEM((2, G, tq, tq), F32), pltpu.VMEM((G, tq, LANE), F32),
                        pltpu.VMEM((G, tq, 2 * HEAD_DIM), F32)],
        compiler_params=_params("parallel", "parallel", "arbitrary"),
        name="flash",
    )(*args)


def _gate_kernel(fg_ref, bf_ref, cumt_ref, carry_ref, *, ts):
    @pl.when(pl.program_id(1) == 0)
    def _():
        carry_ref[...] = jnp.zeros_like(carry_ref)

    z = fg_ref[...] + bf_ref[...]
    lf = jnp.minimum(z, 0.0) - jnp.log1p(jnp.exp(-jnp.abs(z)))
    hi = lf.astype(BF16)
    r1 = lf - hi.astype(F32)
    mid = r1.astype(BF16)
    lo = (r1 - mid.astype(F32)).astype(BF16)
    row = lax.broadcasted_iota(jnp.int32, (ts, ts), 0)
    col = lax.broadcasted_iota(jnp.int32, (ts, ts), 1)
    tri = jnp.where(col <= row, 1.0, 0.0).astype(BF16)
    cum = (jnp.dot(tri, hi, preferred_element_type=F32)
           + jnp.dot(tri, mid, preferred_element_type=F32)
           + jnp.dot(tri, lo, preferred_element_type=F32)) + carry_ref[...]
    carry_ref[...] = cum[ts - 1:ts, :]
    cumt_ref[...] = (cum * LOG2E).T


def _fox_gates(fg, b_f):
    B, S, _ = fg.shape
    ts = _tile(S, 512)
    bf = jnp.zeros((1, LANE), F32).at[0, :N_HEADS].set(b_f.astype(F32))
    return pl.pallas_call(
        functools.partial(_gate_kernel, ts=ts),
        grid=(B, S // ts),
        in_specs=[pl.BlockSpec((None, ts, LANE), lambda b, i: (b, i, 0)),
                  pl.BlockSpec((1, LANE), lambda b, i: (0, 0))],
        out_specs=pl.BlockSpec((None, LANE, ts), lambda b, i: (b, 0, i)),
        out_shape=jax.ShapeDtypeStruct((B, LANE, S), F32),
        scratch_shapes=[pltpu.VMEM((1, LANE), F32)],
        compiler_params=_params("parallel", "arbitrary"),
        name="fox_gates",
    )(fg, bf)


def _rope_kernel(x_ref, cos_ref, sin_ref, g_ref, o_ref):
    cos, sin, g = cos_ref[...], sin_ref[...], g_ref[...]
    for c in range(x_ref.shape[1] // LANE):
        sl = slice(c * LANE, (c + 1) * LANE)
        x = x_ref[:, sl]
        ms = jnp.sum(x * x, axis=-1, keepdims=True) * (1.0 / ROPE_DIM)
        y = x * lax.rsqrt(ms + RMS_EPS) * g
        partner = pltpu.roll(y, ROPE_DIM // 2, 1) + pltpu.roll(y, LANE - ROPE_DIM // 2, 1)
        o_ref[:, sl] = (y * cos + partner * sin).astype(o_ref.dtype)


def _rope(x, cos, sin, gain):
    B, S, N = x.shape
    ts = _tile(S, 512)
    g = jnp.zeros((1, LANE), F32).at[0, :ROPE_DIM].set(gain.astype(F32))
    return pl.pallas_call(
        _rope_kernel,
        grid=(B, S // ts),
        in_specs=[pl.BlockSpec((None, ts, N), lambda b, i: (b, i, 0)),
                  pl.BlockSpec((None, ts, LANE), lambda b, i: (b, i, 0)),
                  pl.BlockSpec((None, ts, LANE), lambda b, i: (b, i, 0)),
                  pl.BlockSpec((1, LANE), lambda b, i: (0, 0))],
        out_specs=pl.BlockSpec((None, ts, N), lambda b, i: (b, i, 0)),
        out_shape=jax.ShapeDtypeStruct((B, S, N), BF16),
        compiler_params=_params("parallel", "parallel"),
        name="rope",
    )(x, cos, sin, g)


def _mem_kernel(q_ref, kv_ref, o_ref):
    for h in range(MEM_HEADS):
        q = q_ref[:, h * LANE:(h + 1) * LANE]
        k = kv_ref[:, h * LANE:(h + 1) * LANE]
        v = kv_ref[:, MEM_WIDTH + h * LANE:MEM_WIDTH + (h + 1) * LANE]
        s = lax.dot_general(q, k, (((1,), (1,)), ((), ())), preferred_element_type=F32)
        p = jnp.exp(s - jnp.max(s, axis=-1, keepdims=True))
        o = jnp.dot(p.astype(BF16), v, preferred_element_type=F32)
        o_ref[:, h * LANE:(h + 1) * LANE] = (o / jnp.sum(p, axis=-1, keepdims=True)).astype(o_ref.dtype)


def _mem_attention(qarr, qblock, mem_kv):
    B, S, _ = qarr.shape
    n_mem = mem_kv.shape[1]
    tq = _tile(S, 1024)
    return pl.pallas_call(
        _mem_kernel,
        grid=(B, S // tq),
        in_specs=[pl.BlockSpec((None, tq, MEM_WIDTH), lambda b, i: (b, i, qblock)),
                  pl.BlockSpec((None, n_mem, 2 * MEM_WIDTH), lambda b, i: (b, 0, 0))],
        out_specs=pl.BlockSpec((None, tq, MEM_WIDTH), lambda b, i: (b, i, 0)),
        out_shape=jax.ShapeDtypeStruct((B, S, MEM_WIDTH), BF16),
        compiler_params=_params("parallel", "parallel"),
        name="mem_attn",
    )(qarr, mem_kv)


def _dil_kernel(q_ref, kp_ref, kc_ref, vp_ref, vc_ref, bias_ref, o_ref, lse_ref):
    n = DIL_BLOCK
    nsub = q_ref.shape[0] // n
    row = lax.broadcasted_iota(jnp.int32, (n, 2 * n), 0)
    col = lax.broadcasted_iota(jnp.int32, (n, 2 * n), 1)
    band = jnp.logical_and(col >= row, col <= row + n)
    first = jnp.logical_and(band, jnp.logical_or(col >= n, pl.program_id(1) > 0))
    lane = lax.broadcasted_iota(jnp.int32, (n, LANE), 1)
    dn = (((1,), (1,)), ((), ()))
    heads = [slice(h * LANE, (h + 1) * LANE) for h in range(N_HEADS)]

    def window(prev_ref, cur_ref, u, sl):
        if u == 0:
            return jnp.concatenate([prev_ref[:, sl], cur_ref[:n, sl]], axis=0)
        return cur_ref[(u - 1) * n:(u + 1) * n, sl]

    for u in range(nsub):
        rows = slice(u * n, (u + 1) * n)
        valid = first if u == 0 else band
        scores = []
        for h, sl in enumerate(heads):
            s = lax.dot_general(q_ref[rows, sl], window(kp_ref, kc_ref, u, sl), dn,
                                preferred_element_type=F32) + bias_ref[h]
            scores.append(jnp.where(valid, s, -jnp.inf))
        ms = [jnp.max(s, axis=-1, keepdims=True) for s in scores]
        ps = [jnp.exp2(s - m).astype(BF16) for s, m in zip(scores, ms)]
        lse_all = jnp.zeros((n, LANE), F32)
        for h, sl in enumerate(heads):
            v2 = window(vp_ref, vc_ref, u, sl)
            pv = jnp.dot(ps[h], jnp.concatenate([v2, jnp.ones_like(v2)], axis=-1),
                         preferred_element_type=F32)
            den = pv[:, LANE:]
            o_ref[h, rows, :] = pv[:, :LANE] / den
            lse_all = jnp.where(lane == h, ms[h] + jnp.log2(den), lse_all)
        lse_ref[rows, :] = lse_all


def _dil_attention(pg, bias):
    N, Ls, _ = pg.shape
    n = DIL_BLOCK
    nsub = max(u for u in (1, 2, 4) if Ls % (u * n) == 0)
    tr = nsub * n

    def cur(part):
        return pl.BlockSpec((None, tr, MIX_WIDTH), lambda s, i: (s, i, part))

    def prev(part):
        return pl.BlockSpec((None, n, MIX_WIDTH), lambda s, i: (s, jnp.maximum(i * nsub - 1, 0), part))

    return pl.pallas_call(
        _dil_kernel,
        grid=(N, Ls // tr),
        in_specs=[cur(0), prev(1), cur(1), prev(2), cur(2),
                  pl.BlockSpec((N_HEADS, n, 2 * n), lambda s, i: (0, 0, 0))],
        out_specs=[pl.BlockSpec((None, N_HEADS, tr, LANE), lambda s, i: (s, 0, i, 0)),
                   pl.BlockSpec((None, tr, LANE), lambda s, i: (s, i, 0))],
        out_shape=[jax.ShapeDtypeStruct((N, N_HEADS, Ls, LANE), F32),
                   jax.ShapeDtypeStruct((N, Ls, LANE), F32)],
        compiler_params=_params("parallel", "arbitrary"),
        name="dil_attn",
    )(pg, pg, pg, pg, pg, bias)


def _dil_combine_kernel(o1_ref, o2_ref, o3_ref, l1_ref, l2_ref, l3_ref, out_ref, stage_ref, *, d2, d3):
    n = l3_ref.shape[1]
    q = d3 // d2
    for r in range(d3):
        rows = [(pl.ds(r, n, stride=d3),), (r % d2, pl.ds(r // d2, n, stride=q)), (r, slice(None))]
        ls = [l_ref[idx + (slice(None),)]
              for l_ref, idx in zip((l1_ref, l2_ref, l3_ref), rows)]
        m = functools.reduce(jnp.maximum, ls)
        es = [jnp.exp2(l - m) for l in ls]
        inv = 1.0 / functools.reduce(jnp.add, es)
        ws = [e * inv for e in es]
        for h in range(N_HEADS):
            o = (ws[0][:, h:h + 1] * o1_ref[h, pl.ds(r, n, stride=d3), :]
                 + ws[1][:, h:h + 1] * o2_ref[r % d2, h, pl.ds(r // d2, n, stride=q), :]
                 + ws[2][:, h:h + 1] * o3_ref[r, h, :, :])
            stage_ref[h, pl.ds(r, n, stride=d3), :] = o
    for h in range(N_HEADS):
        out_ref[:, h * LANE:(h + 1) * LANE] = stage_ref[h].astype(out_ref.dtype)


def _dil_combine(outs, lses, B, S, tm=512):
    (_, d1), (_, d2), (_, d3) = DIL_GROUPS
    assert d1 == 1 and d3 % d2 == 0
    tm = _tile(S, tm)
    H = N_HEADS
    o_specs = [pl.BlockSpec((None, H, tm, LANE), lambda b, i: (b, 0, i, 0)),
               pl.BlockSpec((None, d2, H, tm // d2, LANE), lambda b, i: (b, 0, 0, i, 0)),
               pl.BlockSpec((None, d3, H, tm // d3, LANE), lambda b, i: (b, 0, 0, i, 0))]
    l_specs = [pl.BlockSpec((None, tm, LANE), lambda b, i: (b, i, 0)),
               pl.BlockSpec((None, d2, tm // d2, LANE), lambda b, i: (b, 0, i, 0)),
               pl.BlockSpec((None, d3, tm // d3, LANE), lambda b, i: (b, 0, i, 0))]
    return pl.pallas_call(
        functools.partial(_dil_combine_kernel, d2=d2, d3=d3),
        grid=(B, S // tm),
        in_specs=o_specs + l_specs,
        out_specs=pl.BlockSpec((None, tm, MIX_WIDTH), lambda b, i: (b, i, 0)),
        out_shape=jax.ShapeDtypeStruct((B, S, MIX_WIDTH), BF16),
        scratch_shapes=[pltpu.VMEM((H, tm, LANE), F32)],
        compiler_params=_params("parallel", "parallel"),
        name="dil_combine",
    )(outs[0].reshape(B, H, S, LANE), outs[1].reshape(B, d2, H, S // d2, LANE),
      outs[2].reshape(B, d3, H, S // d3, LANE),
      lses[0].reshape(B, S, LANE), lses[1].reshape(B, d2, S // d2, LANE),
      lses[2].reshape(B, d3, S // d3, LANE))


def _dsa_topk_kernel(qi_ref, wi_ref, ki_ref, mask_ref, key_ref, jsel_ref, *, tq, tk, S, n_sel):
    i = pl.program_id(1)
    q0 = i * tq
    n_chunks = (q0 + tq + tk - 1) // tk
    dn = (((1,), (1,)), ((), ()))
    krow = lax.broadcasted_iota(jnp.int32, (tk, tq), 0)
    qpos = q0 + lax.broadcasted_iota(jnp.int32, (tk, tq), 1)

    wi_t = (wi_ref[...] * (IDX_HEADS ** -0.5 * IDX_DIM ** -0.5)).T

    def idx_body(c, carry):
        off = pl.multiple_of(c * tk, tk)
        kic = ki_ref[pl.ds(off, tk), :]
        k_lo = jnp.where(lax.broadcasted_iota(jnp.int32, kic.shape, 1) < IDX_DIM, kic, 0.0)
        k_sides = (k_lo.astype(BF16), pltpu.roll(k_lo, IDX_DIM, 1).astype(BF16))
        acc = jnp.zeros((tk, tq), F32)
        for h in range(IDX_HEADS):
            d = lax.dot_general(k_sides[h % 2], qi_ref[:, (h // 2) * LANE:(h // 2 + 1) * LANE], dn,
                                preferred_element_type=F32)
            acc = acc + wi_t[IDX_DIM + h:IDX_DIM + h + 1, :] * jnp.maximum(d, 0.0)
        score = jnp.where(off + krow <= qpos, acc, -jnp.inf)
        bits = pltpu.bitcast(score, jnp.int32)
        key_ref[pl.ds(off, tk), :] = bits ^ ((bits >> 31) & jnp.int32(0x7FFFFFFF))
        return carry

    lax.fori_loop(0, n_chunks, idx_body, 0)

    def count(pred):
        sub = 64
        def body(c, acc):
            off = pl.multiple_of(c * tk, tk)
            ones = jnp.where(pred(key_ref[pl.ds(off, tk), :], off), 1.0, 0.0)
            return acc + jnp.sum(ones.reshape(tk // sub, sub, tq), axis=0)

        acc = lax.fori_loop(0, n_chunks, body, jnp.zeros((sub, tq), F32))
        return jnp.sum(acc, axis=0, keepdims=True)

    k_sel = float(n_sel)
    c0 = count(lambda k, off: k >= 0)
    lo0 = jnp.where(c0 >= k_sel, jnp.int32(0), jnp.int32(-2 ** 31))
    n0 = jnp.where(c0 >= k_sel, c0, (n_chunks * tk).astype(F32))

    def bis_cond(state):
        it, _, n_lo = state
        return jnp.logical_and(it < 31, jnp.max(jnp.abs(n_lo - k_sel)) > 0.0)

    def bis_body(state):
        it, lo, n_lo = state
        cand = lo + (jnp.int32(1) << (30 - it))
        cnt = count(lambda k, off: k >= cand)
        take = cnt >= k_sel
        return it + 1, jnp.where(take, cand, lo), jnp.where(take, cnt, n_lo)

    _, thr, n_ge = lax.while_loop(bis_cond, bis_body, (jnp.int32(0), lo0, n0))
    n_bits = max(1, (S - 1).bit_length())
    jsel_ref[...] = jnp.full_like(jsel_ref, S)

    @pl.when(jnp.max(n_ge) > k_sel)
    def _():
        need = k_sel - count(lambda k, off: k > thr)

        def tie_body(it, jlo):
            cand = jlo + (jnp.int32(1) << (n_bits - 1 - it))
            below = count(lambda k, off: jnp.logical_and(k == thr, off + krow < cand))
            return jnp.where(below < need, cand, jlo)

        jsel = lax.fori_loop(0, n_bits, tie_body, jnp.zeros((1, tq), jnp.int32))
        jsel_ref[...] = jnp.broadcast_to(jsel, jsel_ref.shape)

    jsel = jsel_ref[:1, :]
    mask_ref[...] = jnp.full_like(mask_ref, -jnp.inf)

    def mask_body(c, carry):
        off = pl.multiple_of(c * tk, tk)
        k = key_ref[pl.ds(off, tk), :]
        kpos = off + krow
        tie = jnp.logical_and(k == thr, kpos <= jsel)
        sel = jnp.logical_and(jnp.logical_or(k > thr, tie), kpos <= qpos)
        mask_ref[:, pl.ds(off, tk)] = jnp.where(sel, 0.0, -jnp.inf).T.astype(mask_ref.dtype)
        return carry

    lax.fori_loop(0, n_chunks, mask_body, 0)


def _dsa_select(pm, aux, B, S, n_sel, tq=256, tk=512):
    tq, tk = _tile(S, tq), _tile(S, tk)
    qiw = IDX_HEADS * IDX_DIM
    return pl.pallas_call(
        functools.partial(_dsa_topk_kernel, tq=tq, tk=tk, S=S, n_sel=n_sel),
        grid=(B, S // tq),
        in_specs=[pl.BlockSpec((None, tq, qiw), lambda b, i: (b, i, MIX_WIDTH // qiw)),
                  pl.BlockSpec((None, tq, LANE), lambda b, i: (b, i, 0)),
                  pl.BlockSpec((None, S, LANE), lambda b, i: (b, 0, 0))],
        out_specs=pl.BlockSpec((None, tq, S), lambda b, i: (b, i, 0)),
        out_shape=jax.ShapeDtypeStruct((B, S, S), BF16),
        scratch_shapes=[pltpu.VMEM((S, tq), jnp.int32), pltpu.VMEM((8, tq), jnp.int32)],
        compiler_params=_params("parallel", "arbitrary"),
        name="dsa_select",
    )(pm, aux, aux)


def _dsa_attn_kernel(q_ref, k_ref, v_ref, mask_ref, gt_ref, o_ref, s_ref, m_ref, acc_ref, *, tq, tk, S):
    i = pl.program_id(2)
    q0 = i * tq
    n_chunks = (q0 + tq + tk - 1) // tk
    R = q_ref.shape[1] // LANE
    nblk = tq // LANE
    dn = (((1,), (1,)), ((), ()))
    qs = [q_ref[:, r * LANE:(r + 1) * LANE] for r in range(R)]

    def produce(c):
        off = pl.multiple_of(c * tk, tk)
        kc = k_ref[pl.ds(off, tk), :]
        mask = mask_ref[:, pl.ds(off, tk)].astype(F32)
        st = S - q0 + off
        scores = []
        for r in range(R):
            bias = jnp.concatenate(
                [gt_ref[r, :, pl.ds(pl.multiple_of(st + (nblk - 1 - a) * LANE, LANE), tk)]
                 for a in range(nblk)], axis=0)
            scores.append(lax.dot_general(qs[r], kc, dn, preferred_element_type=F32) + bias + mask)
        return scores

    def values(c):
        return [v_ref[pl.ds(pl.multiple_of(c * tk, tk), tk), :]] * R

    _online_softmax(n_chunks, produce, values, s_ref, m_ref, acc_ref)
    for r in range(R):
        acc = acc_ref[r]
        o_ref[:, r * LANE:(r + 1) * LANE] = (acc[:, :LANE] / acc[:, LANE:]).astype(o_ref.dtype)


def _dsa_attention(pm, mask, gt, B, S, tq=256, tk=512):
    tq, tk = _tile(S, tq), _tile(S, tk)
    R = N_HEADS // DSA_KV_HEADS
    kblk = (MIX_WIDTH + IDX_HEADS * IDX_DIM) // LANE
    return pl.pallas_call(
        functools.partial(_dsa_attn_kernel, tq=tq, tk=tk, S=S),
        grid=(DSA_KV_HEADS, B, S // tq),
        in_specs=[pl.BlockSpec((None, tq, R * LANE), lambda g, b, i: (b, i, g)),
                  pl.BlockSpec((None, S, LANE), lambda g, b, i: (b, 0, kblk + g)),
                  pl.BlockSpec((None, S, LANE), lambda g, b, i: (b, 0, kblk + DSA_KV_HEADS + g)),
                  pl.BlockSpec((None, tq, S), lambda g, b, i: (b, i, 0)),
                  pl.BlockSpec((R, LANE, gt.shape[2]), lambda g, b, i: (g, 0, 0))],
        out_specs=pl.BlockSpec((None, tq, R * LANE), lambda g, b, i: (b, i, g)),
        out_shape=jax.ShapeDtypeStruct((B, S, MIX_WIDTH), BF16),
        scratch_shapes=[pltpu.VMEM((2, R, tq, tk), F32), pltpu.VMEM((R, tq, LANE), F32),
                        pltpu.VMEM((R, tq, 2 * HEAD_DIM), F32)],
        compiler_params=_params("parallel", "parallel", "arbitrary"),
        name="dsa_attn",
    )(pm, pm, pm, mask, gt)


def _t5_bucket(dist):
    n = jnp.maximum(dist, 0)
    exact = REL_BUCKETS // 2
    nf = jnp.maximum(n, 1).astype(F32)
    large = exact + (jnp.log(nf / exact) / math.log(REL_MAX_DIST / exact)
                     * (REL_BUCKETS - exact)).astype(jnp.int32)
    large = jnp.minimum(large, REL_BUCKETS - 1)
    return jnp.where(n < exact, n, large)


def _t5_bias(t5_table, dist):
    onehot = jax.nn.one_hot(_t5_bucket(dist), REL_BUCKETS, dtype=F32)
    table = jnp.einsum('...b,bh->h...', onehot, t5_table.astype(F32), precision=lax.Precision.HIGHEST)
    return table * LOG2E


def _dil_bias(t5_table, dil):
    n = DIL_BLOCK
    rel = jnp.arange(n)[:, None] + n - jnp.arange(2 * n)[None, :]
    return _t5_bias(t5_table, rel * dil)


def _dsa_bias_table(t5_table, S, tq, tk):
    m = jnp.arange(S + tq + tk)[None, :]
    i = jnp.arange(LANE)[:, None]
    return _t5_bias(t5_table, S + tq - LANE - m + i)


def _headnorm_cols(spec):
    eg, ef = [], []
    for gain, count, scale in spec:
        if gain is None:
            eg.append(jnp.ones((count * LANE,), F32))
            ef.append(jnp.zeros((count * LANE,), F32))
        else:
            eg.append(jnp.tile(gain.astype(F32) * scale, count))
            ef.append(jnp.ones((count * LANE,), F32))
    return jnp.concatenate(eg), jnp.concatenate(ef)


def _pad_cols(w, width):
    return jnp.pad(w, ((0, 0), (0, width - w.shape[1])))


def _fox_layer(x, norm_g, w_in, b_f, qk_g, mem_g):
    B, S, _ = x.shape
    W = MIX_WIDTH
    w_main = jnp.concatenate([w_in[:, :3 * W], w_in[:, 3 * W + N_HEADS:]], axis=1).astype(BF16)
    w_gate = _pad_cols(w_in[:, 3 * W:3 * W + N_HEADS], LANE).astype(BF16)
    eg, ef = _headnorm_cols([(qk_g[0], N_HEADS, HEAD_DIM ** -0.5 * LOG2E), (qk_g[1], N_HEADS, 1.0),
                             (None, N_HEADS, 1.0), (mem_g, MEM_HEADS, HEAD_DIM ** -0.5)])
    pm = _proj(x, norm_g, w_main, eg, ef, tm=512, tn=3328)
    fg = _proj(x, norm_g, w_gate, out_dtype=F32)
    cum_t = _fox_gates(fg, b_f)
    mix = _flash([(pm, 0, True)], [(pm, W, True)], (pm, 2 * W, True), B, S,
                 decay_t=cum_t.reshape(B, LANE, 1, S))
    return mix, pm, 3 * W // MEM_WIDTH


def _mla_layer(x, positions, norm_g, w_in, q_norm, w_uq, kv_norm, w_ukv, nope_g, rope_g, mem_g):
    B, S, _ = x.shape
    scale = (NOPE_DIM + ROPE_DIM) ** -0.5 * LOG2E
    lat = Q_LORA + KV_LORA
    w_lat = jnp.concatenate([w_in[:, :lat], _pad_cols(w_in[:, lat:lat + ROPE_DIM], LANE)],
                            axis=1).astype(BF16)
    w_mem = w_in[:, lat + ROPE_DIM:].astype(BF16)
    pl_ = _proj(x, norm_g, w_lat, out_dtype=F32, tn=lat + LANE)
    eg, ef = _headnorm_cols([(mem_g, MEM_HEADS, HEAD_DIM ** -0.5)])
    pmem = _proj(x, norm_g, w_mem, eg, ef)

    uq = w_uq.reshape(Q_LORA, N_HEADS, NOPE_DIM + ROPE_DIM)
    w_qn = uq[:, :, :NOPE_DIM].reshape(Q_LORA, MIX_WIDTH).astype(BF16)
    w_qr = jnp.pad(uq[:, :, NOPE_DIM:], ((0, 0), (0, 0), (0, LANE - ROPE_DIM))
                   ).reshape(Q_LORA, N_HEADS * LANE).astype(BF16)
    ukv = w_ukv.reshape(KV_LORA, N_HEADS, NOPE_DIM + HEAD_DIM)
    w_kv = jnp.concatenate([ukv[:, :, :NOPE_DIM].reshape(KV_LORA, MIX_WIDTH),
                            ukv[:, :, NOPE_DIM:].reshape(KV_LORA, MIX_WIDTH)], axis=1).astype(BF16)
    eg, ef = _headnorm_cols([(nope_g[0], N_HEADS, scale)])
    qn = _proj(pl_, q_norm, w_qn, eg, ef, kblock=0, tn=2048)
    qr_raw = _proj(pl_, q_norm, w_qr, kblock=0, out_dtype=F32, tn=1024)
    eg, ef = _headnorm_cols([(nope_g[1], N_HEADS, 1.0), (None, N_HEADS, 1.0)])
    kv = _proj(pl_, kv_norm, w_kv, eg, ef, kblock=1, tn=2048)

    half = ROPE_DIM // 2
    inv = ROPE_THETA ** (-jnp.arange(half, dtype=F32) / half)
    ang = positions.astype(F32)[..., None] * inv
    cos, sin = jnp.cos(ang), jnp.sin(ang)
    zero = jnp.zeros((B, S, LANE - ROPE_DIM), F32)
    cos_t = jnp.concatenate([cos, cos, zero], axis=-1)
    sin_t = jnp.concatenate([-sin, sin, zero], axis=-1)
    qr = _rope(qr_raw, cos_t, sin_t, rope_g[0] * scale)
    kr = _rope(pl_[:, :, lat:], cos_t, sin_t, rope_g[1])
    mix = _flash([(qn, 0, True), (qr, 0, True)], [(kv, 0, True), (kr, 0, False)],
                 (kv, MIX_WIDTH, True), B, S)
    return mix, pmem, 0


def _dil_layer(x, norm_g, w_in, qk_g, t5_table, mem_g):
    B, S, _ = x.shape
    W = MIX_WIDTH
    outs, lses = [], []
    for gi, (win, dil) in enumerate(DIL_GROUPS):
        assert win // dil == DIL_BLOCK and (S // dil) % DIL_BLOCK == 0
        w_g = w_in[:, gi * 3 * W:(gi + 1) * 3 * W].astype(BF16)
        eg, ef = _headnorm_cols([(qk_g[gi, 0], N_HEADS, HEAD_DIM ** -0.5 * LOG2E),
                                 (qk_g[gi, 1], N_HEADS, 1.0), (None, N_HEADS, 1.0)])
        pg = _proj(x, norm_g, w_g, eg, ef, dil=dil, tn=2048)
        o, lse = _dil_attention(pg.reshape(B * dil, S // dil, 3 * W), _dil_bias(t5_table, dil))
        outs.append(o)
        lses.append(lse)
    mix = _dil_combine(outs, lses, B, S)
    eg, ef = _headnorm_cols([(mem_g, MEM_HEADS, HEAD_DIM ** -0.5)])
    pmem = _proj(x, norm_g, w_in[:, len(DIL_GROUPS) * 3 * W:].astype(BF16), eg, ef)
    return mix, pmem, 0


def _dsa_layer(x, norm_g, w_in, qk_g, t5_table, mem_g):
    B, S, _ = x.shape
    W = MIX_WIDTH
    kvw = DSA_KV_HEADS * HEAD_DIM
    o_k, o_v, o_qi = W, W + kvw, W + 2 * kvw
    o_ki = o_qi + IDX_HEADS * IDX_DIM
    o_wi = o_ki + IDX_DIM
    o_mem = o_wi + IDX_HEADS
    qiw = IDX_HEADS * IDX_DIM
    w_main = jnp.concatenate([w_in[:, :W], w_in[:, o_qi:o_ki], w_in[:, o_k:o_qi], w_in[:, o_mem:]],
                             axis=1).astype(BF16)
    w_aux = _pad_cols(w_in[:, o_ki:o_mem], LANE).astype(BF16)
    eg, ef = _headnorm_cols([(qk_g[0], N_HEADS, HEAD_DIM ** -0.5 * LOG2E), (None, qiw // LANE, 1.0),
                             (qk_g[1], DSA_KV_HEADS, 1.0), (None, DSA_KV_HEADS, 1.0),
                             (mem_g, MEM_HEADS, HEAD_DIM ** -0.5)])
    pm = _proj(x, norm_g, w_main, eg, ef, tn=1536)
    aux = _proj(x, norm_g, w_aux, out_dtype=F32)
    mask = _dsa_select(pm, aux, B, S, min(TOPK_MAX, S // 4), tq=512)
    tq, tk = _tile(S, 512), _tile(S, 512)
    mix = _dsa_attention(pm, mask, _dsa_bias_table(t5_table, S, tq, tk), B, S, tq, tk)
    return mix, pm, (W + qiw + 2 * kvw) // MEM_WIDTH


def kernel(x, mem, positions, t5_table, ffn_norm, ffn_w_gate, ffn_w_up, ffn_w_down, attn_norm,
           mem_norm, mem_w_kv, mem_qk_g, w_out, a_w_in, a_b_f, a_qk_g, b_w_in, b_q_norm, b_w_uq,
           b_kv_norm, b_w_ukv, b_nope_g, b_rope_g, c_w_in, c_qk_g, d_w_in, d_qk_g):
    B, S, D = x.shape
    depth = ffn_norm.shape[0]
    n_mixers = 4

    def ffn(xc, i, k, next_gain=None):
        return _ffn(xc.reshape(B * S, D), ffn_norm[i, k], ffn_w_gate[i, k].astype(BF16),
                    ffn_w_up[i, k].astype(BF16), ffn_w_down[i, k].astype(BF16), next_gain)

    for i in range(depth):
        m, j = i % n_mixers, i // n_mixers
        x, h = ffn(x, i, 0, attn_norm[i])
        x, h = x.reshape(B, S, D), h.reshape(B, S, D)
        mem_g = mem_qk_g[i]
        if m == 0:
            mix, qarr, qblock = _fox_layer(h, None, a_w_in[j], a_b_f[j], a_qk_g[j], mem_g[0])
        elif m == 1:
            mix, qarr, qblock = _mla_layer(h, positions, None, b_w_in[j], b_q_norm[j],
                                           b_w_uq[j], b_kv_norm[j], b_w_ukv[j], b_nope_g[j],
                                           b_rope_g[j], mem_g[0])
        elif m == 2:
            mix, qarr, qblock = _dil_layer(h, None, c_w_in[j], c_qk_g[j], t5_table, mem_g[0])
        else:
            mix, qarr, qblock = _dsa_layer(h, None, d_w_in[j], d_qk_g[j], t5_table, mem_g[0])
        eg, ef = _headnorm_cols([(mem_g[1], MEM_HEADS, 1.0), (None, MEM_HEADS, 1.0)])
        mem_kv = _proj(mem, mem_norm[i], mem_w_kv[i].astype(BF16), eg, ef)
        mo = _mem_attention(qarr, qblock, mem_kv)
        wo = w_out[i].astype(BF16)
        x = _out_proj(x.reshape(B * S, D), mix.reshape(B * S, MIX_WIDTH),
                      mo.reshape(B * S, MEM_WIDTH), wo[:MIX_WIDTH], wo[MIX_WIDTH:]).reshape(B, S, D)
        x = ffn(x, i, 1).reshape(B, S, D)
    return x
```
